```python
import math
import jax, jax.numpy as jnp
from jax import lax
import numpy as np

D_MODEL = 2048
BATCH = 1
SEQ = 8192
DEPTH = 4

MEM_LEN = 256
DA_HEADS = 8
DA_HEAD_DIM = 64
DA_V_DIM = 2 * DA_HEAD_DIM
DA_WIDTH = DA_HEADS * DA_V_DIM
Q_BLOCK = 128
ROPE_THETA = 10000.0
CV_WIDTH = 512
CONV_TAPS = 31
SSM_WIDTH = 512
SSM_GROUP = 16
SSM_GROUPS = SSM_WIDTH // SSM_GROUP
SSM_STATE = 64
XA_HEADS = 4
XA_HEAD_DIM = 128
XA_WIDTH = XA_HEADS * XA_HEAD_DIM
MOE_GROUPS = 4
MOE_PER_GROUP = 4
MOE_EXPERTS = MOE_GROUPS * MOE_PER_GROUP
MOE_TOPK = 2
MOE_FF = 512
RMS_EPS = 1e-6
HEAD_NORM_EPS = 1e-5
LN_EPS = 1e-5

IN_Q = 2 * DA_HEADS * DA_HEAD_DIM
IN_K = 2 * DA_HEADS * DA_HEAD_DIM
IN_V = DA_WIDTH
IN_CV = 2 * CV_WIDTH
IN_SSM = SSM_WIDTH
IN_GATE = 3 * D_MODEL
IN_TOTAL = IN_Q + IN_K + IN_V + IN_CV + IN_SSM + IN_GATE
IN_SPLITS = (IN_Q, IN_Q + IN_K, IN_Q + IN_K + IN_V, IN_Q + IN_K + IN_V + IN_CV,
             IN_Q + IN_K + IN_V + IN_CV + IN_SSM)

kernel_name = "hybrid_gated_diffattn_conv_s5_hmoe"


def rms_norm(x, g, eps=RMS_EPS):
    xf = x.astype(jnp.float32)
    y = xf * lax.rsqrt(jnp.mean(xf * xf, axis=-1, keepdims=True) + eps)
    return (y * g.astype(jnp.float32)).astype(x.dtype)


def layer_norm(x, g, b, eps=LN_EPS):
    xf = x.astype(jnp.float32)
    mu = jnp.mean(xf, axis=-1, keepdims=True)
    xc = xf - mu
    var = jnp.mean(xc * xc, axis=-1, keepdims=True)
    y = xc * lax.rsqrt(var + eps) * g.astype(jnp.float32) + b.astype(jnp.float32)
    return y.astype(x.dtype)


def rope_tables(positions, dim):
    inv_freq = ROPE_THETA ** (-jnp.arange(0, dim, 2, dtype=jnp.float32) / dim)
    ang = positions.astype(jnp.float32)[..., None] * inv_freq
    return jnp.cos(ang), jnp.sin(ang)


def apply_rope(t, cos, sin):
    half = t.shape[-1] // 2
    t1, t2 = t[..., :half], t[..., half:]
    cos = cos.astype(t.dtype)
    sin = sin.astype(t.dtype)
    return jnp.concatenate([t1 * cos - t2 * sin, t2 * cos + t1 * sin], axis=-1)


def diff_attention(q, k, v, lam):
    bsz, seq = q.shape[0], q.shape[1]
    nblk = seq // Q_BLOCK
    qb = jnp.moveaxis(q.reshape(bsz, nblk, Q_BLOCK, DA_HEADS, 2, DA_HEAD_DIM), 1, 0)
    key_pos = jnp.arange(seq)
    scale = DA_HEAD_DIM ** -0.5

    def block(args):
        q_blk, start = args
        s = jnp.einsum('bqhcd,bkhcd->bhcqk', q_blk, k).astype(jnp.float32) * scale
        q_pos = start + jnp.arange(Q_BLOCK)
        mask = key_pos[None, :] <= q_pos[:, None]
        s = jnp.where(mask, s, -jnp.inf)
        p = jax.nn.softmax(s, axis=-1)
        a = p[:, :, 0] - lam * p[:, :, 1]
        return jnp.einsum('bhqk,bkhe->bqhe', a.astype(v.dtype), v)

    starts = jnp.arange(nblk) * Q_BLOCK
    out = lax.map(block, (qb, starts))
    return jnp.moveaxis(out, 0, 1).reshape(bsz, seq, DA_HEADS, DA_V_DIM)


def conformer_conv(u, dw_w, dw_b, ln_g, ln_b, w_out):
    a, b = jnp.split(u, 2, axis=-1)
    z = a * jax.nn.sigmoid(b)
    z = lax.conv_general_dilated(
        z, dw_w[:, None, :], window_strides=(1,), padding=((CONV_TAPS - 1, 0),),
        dimension_numbers=('NWC', 'WIO', 'NWC'), feature_group_count=CV_WIDTH) + dw_b
    z = jax.nn.silu(layer_norm(z, ln_g, ln_b))
    return z @ w_out


def s5_ssm(u, lam_re, lam_im, log_dt, b_re, b_im, c_re, c_im, d_skip, w_glu, b_glu, w_out):
    bsz, seq, _ = u.shape
    f32 = jnp.float32
    uf = u.astype(f32).reshape(bsz, seq, SSM_GROUPS, SSM_GROUP)
    lr = lam_re.astype(f32)
    li = lam_im.astype(f32)
    dt = jnp.exp(log_dt.astype(f32))[:, None]
    mag = jnp.exp(lr * dt)
    ab_re = mag * jnp.cos(li * dt)
    ab_im = mag * jnp.sin(li * dt)
    den = lr * lr + li * li
    f_re = ((ab_re - 1.0) * lr + ab_im * li) / den
    f_im = (ab_im * lr - (ab_re - 1.0) * li) / den
    br = b_re.astype(f32)
    bi = b_im.astype(f32)
    bb_re = f_re[..., None] * br - f_im[..., None] * bi
    bb_im = f_re[..., None] * bi + f_im[..., None] * br
    bu_re = jnp.einsum('gph,blgh->blgp', bb_re, uf)
    bu_im = jnp.einsum('gph,blgh->blgp', bb_im, uf)
    a_re = jnp.broadcast_to(ab_re, bu_re.shape)
    a_im = jnp.broadcast_to(ab_im, bu_re.shape)

    def combine(e1, e2):
        a1r, a1i, b1r, b1i = e1
        a2r, a2i, b2r, b2i = e2
        return (a2r * a1r - a2i * a1i,
                a2r * a1i + a2i * a1r,
                a2r * b1r - a2i * b1i + b2r,
                a2r * b1i + a2i * b1r + b2i)

    _, _, x_re, x_im = lax.associative_scan(combine, (a_re, a_im, bu_re, bu_im), axis=1)
    y = (jnp.einsum('ghp,blgp->blgh', c_re.astype(f32), x_re)
         - jnp.einsum('ghp,blgp->blgh', c_im.astype(f32), x_im)
         + d_skip.astype(f32) * uf)
    y = y.reshape(bsz, seq, SSM_WIDTH).astype(u.dtype)
    ga, gb = jnp.split(y @ w_glu + b_glu, 2, axis=-1)
    return (ga * jax.nn.sigmoid(gb)) @ w_out


def mem_cross_attention(h, m, w_q, w_kv, w_out):
    bsz, seq, _ = h.shape
    q = (h @ w_q).reshape(bsz, seq, XA_HEADS, XA_HEAD_DIM)
    k, v = jnp.split(m @ w_kv, 2, axis=-1)
    k = k.reshape(bsz, -1, XA_HEADS, XA_HEAD_DIM)
    v = v.reshape(bsz, -1, XA_HEADS, XA_HEAD_DIM)
    s = jnp.einsum('bqhd,bkhd->bhqk', q, k).astype(jnp.float32) * (XA_HEAD_DIM ** -0.5)
    p = jax.nn.softmax(s, axis=-1)
    o = jnp.einsum('bhqk,bkhd->bqhd', p.astype(v.dtype), v).reshape(bsz, seq, XA_WIDTH)
    return o @ w_out


def hier_moe(h, w_rg, b_rg, w_re, b_re, w_gate, w_up, w_down):
    bsz, seq, dm = h.shape
    t = h.reshape(-1, dm)
    g_logits = (t @ w_rg).astype(jnp.float32) + b_rg.astype(jnp.float32)
    g_prob = jax.nn.softmax(g_logits, axis=-1)
    g_idx = jnp.argmax(g_logits, axis=-1)
    g_w = jnp.take_along_axis(g_prob, g_idx[:, None], axis=1)
    e_logits = ((t @ w_re).astype(jnp.float32) + b_re.astype(jnp.float32)).reshape(
        -1, MOE_GROUPS, MOE_PER_GROUP)
    e_sel = jnp.take_along_axis(e_logits, g_idx[:, None, None], axis=1)[:, 0]
    top_v, top_i = lax.top_k(e_sel, MOE_TOPK)
    w = jax.nn.softmax(top_v, axis=-1) * g_w
    e_idx = g_idx[:, None] * MOE_PER_GROUP + top_i
    comb = jnp.sum(jax.nn.one_hot(e_idx, MOE_EXPERTS, dtype=jnp.float32) * w[..., None], axis=1)
    comb = comb.astype(t.dtype)
    out = jnp.zeros_like(t)
    for e in range(MOE_EXPERTS):
        he = jax.nn.silu(t @ w_gate[e]) * (t @ w_up[e])
        out = out + comb[:, e:e + 1] * (he @ w_down[e])
    return out.reshape(bsz, seq, dm)


def setup_inputs(seed: int = 0) -> dict:
    key = jax.random.key(seed)
    ks = iter(jax.random.split(key, 48))
    f32 = jnp.float32

    def nrm(shape, scale):
        return jax.random.normal(next(ks), shape, f32) * scale

    def gain(shape):
        return 1.0 + 0.02 * jax.random.normal(next(ks), shape, f32)

    L = DEPTH
    n = jnp.arange(SSM_STATE, dtype=f32)
    return {
        "x": nrm((BATCH, SEQ, D_MODEL), 1.0),
        "mem": nrm((BATCH, MEM_LEN, D_MODEL), 1.0),
        "positions": jnp.broadcast_to(jnp.arange(SEQ, dtype=jnp.int32), (BATCH, SEQ)),
        "norm_mix": gain((L, D_MODEL)),
        "w_in": nrm((L, D_MODEL, IN_TOTAL), D_MODEL ** -0.5),
        "da_lam_q1": nrm((L, DA_HEAD_DIM), 0.1),
        "da_lam_k1": nrm((L, DA_HEAD_DIM), 0.1),
        "da_lam_q2": nrm((L, DA_HEAD_DIM), 0.1),
        "da_lam_k2": nrm((L, DA_HEAD_DIM), 0.1),
        "da_head_norm": gain((L, DA_V_DIM)),
        "w_da_out": nrm((L, DA_WIDTH, D_MODEL), DA_WIDTH ** -0.5),
        "cv_dw_w": nrm((L, CONV_TAPS, CV_WIDTH), CONV_TAPS ** -0.5),
        "cv_dw_b": nrm((L, CV_WIDTH), 0.02),
        "cv_ln_g": gain((L, CV_WIDTH)),
        "cv_ln_b": nrm((L, CV_WIDTH), 0.02),
        "w_cv_out": nrm((L, CV_WIDTH, D_MODEL), CV_WIDTH ** -0.5),
        "ssm_lam_re": -0.5 + nrm((L, SSM_GROUPS, SSM_STATE), 0.01),
        "ssm_lam_im": math.pi * n + nrm((L, SSM_GROUPS, SSM_STATE), 0.01),
        "ssm_log_dt": jax.random.uniform(next(ks), (L, SSM_GROUPS), f32,
                                         math.log(1e-3), math.log(1e-1)),
        "ssm_b_re": nrm((L, SSM_GROUPS, SSM_STATE, SSM_GROUP), (2 * SSM_GROUP) ** -0.5),
        "ssm_b_im": nrm((L, SSM_GROUPS, SSM_STATE, SSM_GROUP), (2 * SSM_GROUP) ** -0.5),
        "ssm_c_re": nrm((L, SSM_GROUPS, SSM_GROUP, SSM_STATE), (2 * SSM_STATE) ** -0.5),
        "ssm_c_im": nrm((L, SSM_GROUPS, SSM_GROUP, SSM_STATE), (2 * SSM_STATE) ** -0.5),
        "ssm_d": nrm((L, SSM_GROUPS, SSM_GROUP), 1.0),
        "w_ssm_glu": nrm((L, SSM_WIDTH, 2 * SSM_WIDTH), SSM_WIDTH ** -0.5),
        "b_ssm_glu": nrm((L, 2 * SSM_WIDTH), 0.02),
        "w_ssm_out": nrm((L, SSM_WIDTH, D_MODEL), SSM_WIDTH ** -0.5),
        "w_mix_out": nrm((L, D_MODEL, D_MODEL), D_MODEL ** -0.5),
        "norm_xa": gain((L, D_MODEL)),
        "norm_mem": gain((L, D_MODEL)),
        "w_xa_q": nrm((L, D_MODEL, XA_WIDTH), D_MODEL ** -0.5),
        "w_xa_kv": nrm((L, D_MODEL, 2 * XA_WIDTH), D_MODEL ** -0.5),
        "w_xa_out": nrm((L, XA_WIDTH, D_MODEL), XA_WIDTH ** -0.5),
        "norm_ffn": gain((L, D_MODEL)),
        "w_router_group": nrm((L, D_MODEL, MOE_GROUPS), D_MODEL ** -0.5),
        "b_router_group": nrm((L, MOE_GROUPS), 0.01),
        "w_router_expert": nrm((L, D_MODEL, MOE_EXPERTS), D_MODEL ** -0.5),
        "b_router_expert": nrm((L, MOE_EXPERTS), 0.01),
        "w_exp_gate": nrm((L, MOE_EXPERTS, D_MODEL, MOE_FF), D_MODEL ** -0.5),
        "w_exp_up": nrm((L, MOE_EXPERTS, D_MODEL, MOE_FF), D_MODEL ** -0.5),
        "w_exp_down": nrm((L, MOE_EXPERTS, MOE_FF, D_MODEL), MOE_FF ** -0.5),
        "norm_final": gain((D_MODEL,)),
    }


def reference(x, mem, positions, norm_mix, w_in, da_lam_q1, da_lam_k1, da_lam_q2, da_lam_k2,
              da_head_norm, w_da_out, cv_dw_w, cv_dw_b, cv_ln_g, cv_ln_b, w_cv_out,
              ssm_lam_re, ssm_lam_im, ssm_log_dt, ssm_b_re, ssm_b_im, ssm_c_re, ssm_c_im, ssm_d,
              w_ssm_glu, b_ssm_glu, w_ssm_out, w_mix_out, norm_xa, norm_mem, w_xa_q, w_xa_kv,
              w_xa_out, norm_ffn, w_router_group, b_router_group, w_router_expert,
              b_router_expert, w_exp_gate, w_exp_up, w_exp_down, norm_final):
    bsz, seq, _ = x.shape
    cos, sin = rope_tables(positions, DA_HEAD_DIM)
    cos = cos[:, :, None, None, :]
    sin = sin[:, :, None, None, :]
    for l in range(DEPTH):
        lam_init = 0.8 - 0.6 * math.exp(-0.3 * l)
        h = rms_norm(x, norm_mix[l])
        proj = h @ w_in[l]
        q, k, v, u_cv, u_ssm, gates = jnp.split(proj, IN_SPLITS, axis=-1)
        q = apply_rope(q.reshape(bsz, seq, DA_HEADS, 2, DA_HEAD_DIM), cos, sin)
        k = apply_rope(k.reshape(bsz, seq, DA_HEADS, 2, DA_HEAD_DIM), cos, sin)
        v = v.reshape(bsz, seq, DA_HEADS, DA_V_DIM)
        lam = (jnp.exp(jnp.sum(da_lam_q1[l].astype(jnp.float32) * da_lam_k1[l].astype(jnp.float32)))
               - jnp.exp(jnp.sum(da_lam_q2[l].astype(jnp.float32) * da_lam_k2[l].astype(jnp.float32)))
               + lam_init)
        o_a = diff_attention(q, k, v, lam)
        o_a = rms_norm(o_a, da_head_norm[l], HEAD_NORM_EPS) * (1.0 - lam_init)
        y_a = o_a.reshape(bsz, seq, DA_WIDTH) @ w_da_out[l]
        y_b = conformer_conv(u_cv, cv_dw_w[l], cv_dw_b[l], cv_ln_g[l], cv_ln_b[l], w_cv_out[l])
        y_c = s5_ssm(u_ssm, ssm_lam_re[l], ssm_lam_im[l], ssm_log_dt[l], ssm_b_re[l], ssm_b_im[l],
                     ssm_c_re[l], ssm_c_im[l], ssm_d[l], w_ssm_glu[l], b_ssm_glu[l], w_ssm_out[l])
        g_a, g_b, g_c = jnp.split(gates, 3, axis=-1)
        merged = (jax.nn.sigmoid(g_a) * y_a + jax.nn.sigmoid(g_b) * y_b
                  + jax.nn.sigmoid(g_c) * y_c)
        x = x + merged @ w_mix_out[l]
        h = rms_norm(x, norm_xa[l])
        m = rms_norm(mem, norm_mem[l])
        x = x + mem_cross_attention(h, m, w_xa_q[l], w_xa_kv[l], w_xa_out[l])
        h = rms_norm(x, norm_ffn[l])
        x = x + hier_moe(h, w_router_group[l], b_router_group[l], w_router_expert[l],
                         b_router_expert[l], w_exp_gate[l], w_exp_up[l], w_exp_down[l])
    return rms_norm(x, norm_final)
```

```python
import functools
import math

import jax
import jax.numpy as jnp
from jax import lax
from jax.experimental import pallas as pl
from jax.experimental.pallas import tpu as pltpu

F32 = jnp.float32
BF16 = jnp.bfloat16

D_MODEL = 2048
SEQ = 8192
DEPTH = 4
MEM_LEN = 256
DA_HEADS = 8
DA_HEAD_DIM = 64
DA_V_DIM = 128
DA_WIDTH = 1024
ROPE_THETA = 10000.0
CV_WIDTH = 512
CONV_TAPS = 31
SSM_WIDTH = 512
SSM_GROUP = 16
SSM_GROUPS = 32
SSM_STATE = 64
XA_HEADS = 4
XA_HEAD_DIM = 128
XA_WIDTH = 512
MOE_GROUPS = 4
MOE_PER_GROUP = 4
MOE_EXPERTS = 16
MOE_FF = 512
RMS_EPS = 1e-6
HEAD_NORM_EPS = 1e-5
LN_EPS = 1e-5

IN_MAIN = 4608
IN_GATE = 3 * D_MODEL
IN_TOTAL = IN_MAIN + IN_GATE

COL_Q = IN_GATE
COL_K = COL_Q + 1024
COL_V = COL_K + 1024
COL_CVA = COL_V + 1024
COL_CVB = COL_CVA + CV_WIDTH
COL_SSM = COL_CVB + CV_WIDTH

LANES = 128
VMEM_LIMIT = 56 * 1024 * 1024

SSM_CHUNK = 16
SSM_NCHUNK = SEQ // SSM_CHUNK
SSM_PAIRS = SSM_GROUPS // 2
SSM_PAIR_W = 2 * SSM_CHUNK * SSM_GROUP


def _params(sem, vmem=VMEM_LIMIT):
    return pltpu.CompilerParams(dimension_semantics=sem, vmem_limit_bytes=vmem)


def _vec_spec(width, layer, ngrid):
    if ngrid == 1:
        return pl.BlockSpec((None, 1, width), lambda i: (layer, 0, 0))
    return pl.BlockSpec((None, 1, width), lambda i, j: (layer, 0, 0))


def _rms(xf, g, eps):
    ms = jnp.mean(xf * xf, axis=-1, keepdims=True)
    return xf * lax.rsqrt(ms + eps) * g


INP_TM = 1024
INP_TN = 512
_Q_TILE0 = COL_Q // INP_TN
_K_TILE0 = COL_K // INP_TN
_V_TILE0 = COL_V // INP_TN


def _inproj_kernel(x_ref, g_ref, w_ref, cos_ref, sin_ref, o_ref, h_ref):
    j = pl.program_id(1)

    @pl.when(j == 0)
    def _():
        h_ref[...] = _rms(x_ref[...], g_ref[...], RMS_EPS).astype(BF16)

    acc = jnp.dot(h_ref[...], w_ref[...], preferred_element_type=F32)
    is_rope = jnp.logical_and(j >= _Q_TILE0, j < _V_TILE0)

    @pl.when(is_rope)
    def _():
        scale = jnp.where(j < _K_TILE0, DA_HEAD_DIM ** -0.5, 1.0).astype(F32)
        cos = cos_ref[...] * scale
        sin = sin_ref[...] * scale
        lane = lax.broadcasted_iota(jnp.int32, (INP_TM, LANES), 1)
        first_half = (lane % DA_HEAD_DIM) < (DA_HEAD_DIM // 2)
        for c in range(INP_TN // LANES):
            t = acc[:, c * LANES:(c + 1) * LANES]
            swapped = jnp.where(first_half, pltpu.roll(t, LANES - 32, 1), pltpu.roll(t, 32, 1))
            o_ref[:, c * LANES:(c + 1) * LANES] = (t * cos + swapped * sin).astype(BF16)

    @pl.when(jnp.logical_not(is_rope))
    def _():
        o_ref[...] = acc.astype(BF16)


def _inproj(x, g, w_all, layer, cos_t, sin_t):
    t = x.shape[0]
    return pl.pallas_call(
        _inproj_kernel,
        grid=(t // INP_TM, IN_TOTAL // INP_TN),
        in_specs=[
            pl.BlockSpec((INP_TM, D_MODEL), lambda i, j: (i, 0)),
            _vec_spec(D_MODEL, layer, 2),
            pl.BlockSpec((None, D_MODEL, INP_TN), lambda i, j: (layer, 0, j)),
            pl.BlockSpec((INP_TM, LANES), lambda i, j: (i, 0)),
            pl.BlockSpec((INP_TM, LANES), lambda i, j: (i, 0)),
        ],
        out_specs=pl.BlockSpec((INP_TM, INP_TN), lambda i, j: (i, j)),
        out_shape=jax.ShapeDtypeStruct((t, IN_TOTAL), BF16),
        scratch_shapes=[pltpu.VMEM((INP_TM, D_MODEL), BF16)],
        compiler_params=_params(("arbitrary", "arbitrary")),
        name="inproj",
    )(x, g, w_all, cos_t, sin_t)


ATT_TQ = 512
ATT_TK = 512


def _attn_kernel(lam_ref, g_ref, q_ref, k_ref, v_ref, o_ref, qs_ref, m_ref, l_ref, acc_ref):
    i = pl.program_id(1)
    tq = ATT_TQ
    q = q_ref[...]
    lane = lax.broadcasted_iota(jnp.int32, q.shape, 1)
    zero = jnp.zeros_like(q)
    qs_ref[0:tq, :] = jnp.where(lane < DA_HEAD_DIM, q, zero)
    qs_ref[tq:2 * tq, :] = jnp.where(lane >= DA_HEAD_DIM, q, zero)
    m_ref[...] = jnp.full(m_ref.shape, -jnp.inf, F32)
    l_ref[...] = jnp.zeros(l_ref.shape, F32)
    acc_ref[...] = jnp.zeros(acc_ref.shape, F32)

    def update(k, v, mask):
        s = lax.dot_general(qs_ref[...], k, (((1,), (1,)), ((), ())), preferred_element_type=F32)
        if mask is not None:
            s = jnp.where(mask, s, -jnp.inf)
        m_old = m_ref[...]
        m_new = jnp.maximum(m_old, jnp.max(s, axis=1, keepdims=True))
        alpha = jnp.exp(m_old - m_new)
        p = jnp.exp(s - m_new)
        l_ref[...] = alpha * l_ref[...] + jnp.sum(p, axis=1, keepdims=True)
        acc_ref[...] = alpha * acc_ref[...] + jnp.dot(p.astype(BF16), v, preferred_element_type=F32)
        m_ref[...] = m_new

    def body(c, carry):
        off = pl.multiple_of(c * ATT_TK, ATT_TK)
        update(k_ref[pl.ds(off, ATT_TK), :], v_ref[pl.ds(off, ATT_TK), :], None)
        return carry

    lax.fori_loop(0, i * (tq // ATT_TK), body, 0)

    off = pl.multiple_of(i * tq, tq)
    row = lax.broadcasted_iota(jnp.int32, (2 * tq, tq), 0)
    col = lax.broadcasted_iota(jnp.int32, (2 * tq, tq), 1)
    qpos = jnp.where(row >= tq, row - tq, row)
    update(k_ref[pl.ds(off, tq), :], v_ref[pl.ds(off, tq), :], col <= qpos)

    lam_init = lam_ref[4:5, 0:1]
    lam = (jnp.exp(jnp.sum(lam_ref[0:1, :] * lam_ref[1:2, :], axis=1, keepdims=True))
           - jnp.exp(jnp.sum(lam_ref[2:3, :] * lam_ref[3:4, :], axis=1, keepdims=True))
           + lam_init)
    acc = acc_ref[...]
    inv_l = 1.0 / l_ref[...]
    o = acc[0:tq] * inv_l[0:tq] - lam * (acc[tq:2 * tq] * inv_l[tq:2 * tq])
    o = _rms(o, g_ref[...], HEAD_NORM_EPS) * (1.0 - lam_init)
    o_ref[...] = o.astype(BF16)


def _attention(proj, lam_pack, head_g, layer):
    t = proj.shape[0]
    qb, kb, vb = COL_Q // LANES, COL_K // LANES, COL_V // LANES
    return pl.pallas_call(
        _attn_kernel,
        grid=(DA_HEADS, t // ATT_TQ),
        in_specs=[
            pl.BlockSpec((None, 8, DA_HEAD_DIM), lambda h, i: (layer, 0, 0)),
            _vec_spec(DA_V_DIM, layer, 2),
            pl.BlockSpec((ATT_TQ, LANES), lambda h, i: (i, qb + h)),
            pl.BlockSpec((t, LANES), lambda h, i: (0, kb + h)),
            pl.BlockSpec((t, LANES), lambda h, i: (0, vb + h)),
        ],
        out_specs=pl.BlockSpec((ATT_TQ, LANES), lambda h, i: (i, h)),
        out_shape=jax.ShapeDtypeStruct((t, DA_WIDTH), BF16),
        scratch_shapes=[
            pltpu.VMEM((2 * ATT_TQ, LANES), BF16),
            pltpu.VMEM((2 * ATT_TQ, 1), F32),
            pltpu.VMEM((2 * ATT_TQ, 1), F32),
            pltpu.VMEM((2 * ATT_TQ, LANES), F32),
        ],
        compiler_params=_params(("arbitrary", "arbitrary")),
        name="diff_attn",
    )(lam_pack, head_g, proj, proj, proj)


CV_TM = 512
CV_HALO = 32
CV_ROWS = 64


def _conv_kernel(a_ref, b_ref, w_ref, bias_ref, g_ref, beta_ref, o_ref, z_ref):
    i = pl.program_id(0)

    @pl.when(i == 0)
    def _():
        z_ref[0:CV_HALO, :] = jnp.zeros((CV_HALO, CV_WIDTH), F32)

    @pl.when(i > 0)
    def _():
        z_ref[0:CV_HALO, :] = z_ref[CV_TM:CV_TM + CV_HALO, :]

    a = a_ref[...].astype(F32)
    b = b_ref[...].astype(F32)
    z_ref[CV_HALO:CV_HALO + CV_TM, :] = a * jax.nn.sigmoid(b)

    base = CV_HALO - (CONV_TAPS - 1)
    for r in range(0, CV_TM, CV_ROWS):
        acc = jnp.zeros((CV_ROWS, CV_WIDTH), F32) + bias_ref[...]
        for j in range(CONV_TAPS):
            acc = acc + w_ref[j:j + 1, :] * z_ref[base + r + j:base + r + j + CV_ROWS, :]
        mu = jnp.mean(acc, axis=-1, keepdims=True)
        xc = acc - mu
        var = jnp.mean(xc * xc, axis=-1, keepdims=True)
        y = xc * lax.rsqrt(var + LN_EPS) * g_ref[...] + beta_ref[...]
        o_ref[r:r + CV_ROWS, :] = (y * jax.nn.sigmoid(y)).astype(BF16)


def _conv(proj, dw_w, dw_b, ln_g, ln_b, layer):
    t = proj.shape[0]
    ab, bb = COL_CVA // CV_WIDTH, COL_CVB // CV_WIDTH
    vec = _vec_spec(CV_WIDTH, layer, 1)
    return pl.pallas_call(
        _conv_kernel,
        grid=(t // CV_TM,),
        in_specs=[
            pl.BlockSpec((CV_TM, CV_WIDTH), lambda i: (i, ab)),
            pl.BlockSpec((CV_TM, CV_WIDTH), lambda i: (i, bb)),
            pl.BlockSpec((None, CONV_TAPS, CV_WIDTH), lambda i: (layer, 0, 0)),
            vec, vec, vec,
        ],
        out_specs=pl.BlockSpec((CV_TM, CV_WIDTH), lambda i: (i, 0)),
        out_shape=jax.ShapeDtypeStruct((t, CV_WIDTH), BF16),
        scratch_shapes=[pltpu.VMEM((CV_HALO + CV_TM, CV_WIDTH), F32)],
        compiler_params=_params(("arbitrary",)),
        name="conformer_conv",
    )(proj, proj, dw_w, dw_b, ln_g, ln_b)


def _ssm_kernel(u_ref, t_ref, wre_ref, wim_ref, vre_ref, vim_ref, ar_ref, ai_ref, d_ref,
                y_ref, sre_ref, sim_ref):
    npair, nchunk = SSM_PAIRS, SSM_NCHUNK
    for p in range(npair):
        u = u_ref[p]
        sre_ref[pl.ds(p, nchunk, stride=npair), :] = jnp.dot(u, wre_ref[p], preferred_element_type=F32)
        sim_ref[pl.ds(p, nchunk, stride=npair), :] = jnp.dot(u, wim_ref[p], preferred_element_type=F32)

    ar = ar_ref[...]
    ai = ai_ref[...]

    def body(c, carry):
        xr, xi = carry
        off = pl.multiple_of(c * npair, npair)
        sr = sre_ref[pl.ds(off, npair), :]
        si = sim_ref[pl.ds(off, npair), :]
        sre_ref[pl.ds(off, npair), :] = xr
        sim_ref[pl.ds(off, npair), :] = xi
        return ar * xr - ai * xi + sr, ar * xi + ai * xr + si

    zero = jnp.zeros((npair, LANES), F32)
    lax.fori_loop(0, nchunk, body, (zero, zero), unroll=8)

    for p in range(npair):
        u = u_ref[p]
        xr = sre_ref[pl.ds(p, nchunk, stride=npair), :].astype(BF16)
        xi = sim_ref[pl.ds(p, nchunk, stride=npair), :].astype(BF16)
        y = jnp.dot(u, t_ref[p], preferred_element_type=F32)
        y = y + jnp.dot(xr, vre_ref[p], preferred_element_type=F32)
        y = y + jnp.dot(xi, vim_ref[p], preferred_element_type=F32)
        y = y + u.astype(F32) * d_ref[p:p + 1, :]
        y_ref[p] = y.astype(BF16)


def _ssm(u_pairs, mats):
    vm = pl.BlockSpec(memory_space=pltpu.VMEM)
    return pl.pallas_call(
        _ssm_kernel,
        in_specs=[vm] * 9,
        out_specs=vm,
        out_shape=jax.ShapeDtypeStruct((SSM_PAIRS, SSM_NCHUNK, SSM_PAIR_W), BF16),
        scratch_shapes=[pltpu.VMEM((SSM_NCHUNK * SSM_PAIRS, LANES), F32),
                        pltpu.VMEM((SSM_NCHUNK * SSM_PAIRS, LANES), F32)],
        compiler_params=pltpu.CompilerParams(vmem_limit_bytes=VMEM_LIMIT),
        name="s5_scan",
    )(u_pairs, *mats)


def _blockdiag2(m):
    l, _, r, c = m.shape
    m = m.reshape(l, SSM_PAIRS, 2, r, c)
    z = jnp.zeros_like(m[:, :, 0])
    top = jnp.concatenate([m[:, :, 0], z], axis=-1)
    bot = jnp.concatenate([z, m[:, :, 1]], axis=-1)
    return jnp.concatenate([top, bot], axis=-2)


def _ssm_matrices(lam_re, lam_im, log_dt, b_re, b_im, c_re, c_im, d_skip):
    hp = lax.Precision.HIGHEST
    lr = lam_re.astype(F32)
    li = lam_im.astype(F32)
    dt = jnp.exp(log_dt.astype(F32))[..., None]
    mag = jnp.exp(lr * dt)
    ab_re = mag * jnp.cos(li * dt)
    ab_im = mag * jnp.sin(li * dt)
    den = lr * lr + li * li
    f_re = ((ab_re - 1.0) * lr + ab_im * li) / den
    f_im = (ab_im * lr - (ab_re - 1.0) * li) / den
    br = b_re.astype(F32)
    bi = b_im.astype(F32)
    bb_re = f_re[..., None] * br - f_im[..., None] * bi
    bb_im = f_re[..., None] * bi + f_im[..., None] * br
    tau = jnp.arange(SSM_CHUNK + 1, dtype=F32)[:, None, None, None]
    pmag = jnp.exp(tau * (lr * dt)[None])
    pw_re = pmag * jnp.cos(tau * (li * dt)[None])
    pw_im = pmag * jnp.sin(tau * (li * dt)[None])
    cr = c_re.astype(F32)
    ci = c_im.astype(F32)
    ca_re = cr[None] * pw_re[:, :, :, None, :] - ci[None] * pw_im[:, :, :, None, :]
    ca_im = cr[None] * pw_im[:, :, :, None, :] + ci[None] * pw_re[:, :, :, None, :]
    kk = (jnp.einsum('tlghp,lgpk->lgthk', ca_re[:SSM_CHUNK], bb_re, precision=hp)
          - jnp.einsum('tlghp,lgpk->lgthk', ca_im[:SSM_CHUNK], bb_im, precision=hp))
    s_idx = jnp.arange(SSM_CHUNK)[:, None]
    t_idx = jnp.arange(SSM_CHUNK)[None, :]
    lag = t_idx - s_idx
    tm = kk[:, :, jnp.clip(lag, 0, SSM_CHUNK - 1)]
    tm = jnp.where((lag >= 0)[None, None, :, :, None, None], tm, 0.0)
    nl = lr.shape[0]
    tm = tm.transpose(0, 1, 2, 5, 3, 4).reshape(nl, SSM_GROUPS, 256, 256)
    rev_re = pw_re[:SSM_CHUNK][::-1]
    rev_im = pw_im[:SSM_CHUNK][::-1]
    w_re = rev_re[..., None] * bb_re[None] - rev_im[..., None] * bb_im[None]
    w_im = rev_re[..., None] * bb_im[None] + rev_im[..., None] * bb_re[None]
    w_re = w_re.transpose(1, 2, 0, 4, 3).reshape(nl, SSM_GROUPS, 256, SSM_STATE)
    w_im = w_im.transpose(1, 2, 0, 4, 3).reshape(nl, SSM_GROUPS, 256, SSM_STATE)
    v_re = ca_re[1:].transpose(1, 2, 4, 0, 3).reshape(nl, SSM_GROUPS, SSM_STATE, 256)
    v_im = (-ca_im[1:]).transpose(1, 2, 4, 0, 3).reshape(nl, SSM_GROUPS, SSM_STATE, 256)
    a_re = pw_re[SSM_CHUNK].reshape(nl, SSM_PAIRS, LANES)
    a_im = pw_im[SSM_CHUNK].reshape(nl, SSM_PAIRS, LANES)
    d_t = jnp.broadcast_to(d_skip.astype(F32)[:, :, None, :], (nl, SSM_GROUPS, SSM_CHUNK, SSM_GROUP))
    d_t = d_t.reshape(nl, SSM_PAIRS, SSM_PAIR_W)
    return (_blockdiag2(tm).astype(BF16), _blockdiag2(w_re).astype(BF16), _blockdiag2(w_im).astype(BF16),
            _blockdiag2(v_re).astype(BF16), _blockdiag2(v_im).astype(BF16), a_re, a_im, d_t)


def _to_pairs(u):
    t = u.shape[0]
    u = u.reshape(t // SSM_CHUNK, SSM_CHUNK, SSM_PAIRS, 2, SSM_GROUP)
    return u.transpose(2, 0, 3, 1, 4).reshape(SSM_PAIRS, t // SSM_CHUNK, SSM_PAIR_W)


def _from_pairs(y):
    nchunk = y.shape[1]
    y = y.reshape(SSM_PAIRS, nchunk, 2, SSM_CHUNK, SSM_GROUP)
    return y.transpose(1, 3, 0, 2, 4).reshape(nchunk * SSM_CHUNK, SSM_WIDTH)


MG_TM = 256


def _merge_kernel(oa_ref, zb_ref, yc_ref, ga_ref, gb_ref, gc_ref, wda_ref, wcv_ref, wglu_ref,
                  bglu_ref, wso_ref, o_ref):
    y_a = jnp.dot(oa_ref[...], wda_ref[...], preferred_element_type=F32)
    y_b = jnp.dot(zb_ref[...], wcv_ref[...], preferred_element_type=F32)
    glu = jnp.dot(yc_ref[...], wglu_ref[...], preferred_element_type=F32) + bglu_ref[...]
    sc = (glu[:, :SSM_WIDTH] * jax.nn.sigmoid(glu[:, SSM_WIDTH:])).astype(BF16)
    y_c = jnp.dot(sc, wso_ref[...], preferred_element_type=F32)
    merged = (jax.nn.sigmoid(ga_ref[...].astype(F32)) * y_a
              + jax.nn.sigmoid(gb_ref[...].astype(F32)) * y_b
              + jax.nn.sigmoid(gc_ref[...].astype(F32)) * y_c)
    o_ref[...] = merged.astype(BF16)


def _merge(o_a, z_b, y_c, proj, w_da, w_cv, w_glu, b_glu, w_so, layer):
    t = o_a.shape[0]

    def wspec(k, n):
        return pl.BlockSpec((None, k, n), lambda i: (layer, 0, 0))

    return pl.pallas_call(
        _merge_kernel,
        grid=(t // MG_TM,),
        in_specs=[
            pl.BlockSpec((MG_TM, DA_WIDTH), lambda i: (i, 0)),
            pl.BlockSpec((MG_TM, CV_WIDTH), lambda i: (i, 0)),
            pl.BlockSpec((MG_TM, SSM_WIDTH), lambda i: (i, 0)),
            pl.BlockSpec((MG_TM, D_MODEL), lambda i: (i, 0)),
            pl.BlockSpec((MG_TM, D_MODEL), lambda i: (i, 1)),
            pl.BlockSpec((MG_TM, D_MODEL), lambda i: (i, 2)),
            wspec(DA_WIDTH, D_MODEL), wspec(CV_WIDTH, D_MODEL), wspec(SSM_WIDTH, 2 * SSM_WIDTH),
            _vec_spec(2 * SSM_WIDTH, layer, 1),
            wspec(SSM_WIDTH, D_MODEL),
        ],
        out_specs=pl.BlockSpec((MG_TM, D_MODEL), lambda i: (i, 0)),
        out_shape=jax.ShapeDtypeStruct((t, D_MODEL), BF16),
        compiler_params=_params(("arbitrary",)),
        name="gated_merge",
    )(o_a, z_b, y_c, proj, proj, proj, w_da, w_cv, w_glu, b_glu, w_so)


MR_TM = 512


def _matmul_res_kernel(a_ref, w_ref, x_ref, o_ref):
    o_ref[...] = x_ref[...] + jnp.dot(a_ref[...], w_ref[...], preferred_element_type=F32)


def _matmul_residual(a, w_all, x, layer):
    t, k = a.shape
    n = x.shape[1]
    return pl.pallas_call(
        _matmul_res_kernel,
        grid=(t // MR_TM,),
        in_specs=[
            pl.BlockSpec((MR_TM, k), lambda i: (i, 0)),
            pl.BlockSpec((None, k, n), lambda i: (layer, 0, 0)),
            pl.BlockSpec((MR_TM, n), lambda i: (i, 0)),
        ],
        out_specs=pl.BlockSpec((MR_TM, n), lambda i: (i, 0)),
        out_shape=jax.ShapeDtypeStruct((t, n), F32),
        compiler_params=_params(("arbitrary",)),
        name="matmul_residual",
    )(a, w_all, x)


def _norm_matmul_kernel(x_ref, g_ref, w_ref, o_ref):
    h = _rms(x_ref[...], g_ref[...], RMS_EPS).astype(BF16)
    o_ref[...] = jnp.dot(h, w_ref[...], preferred_element_type=F32).astype(o_ref.dtype)


def _mem_kv(mem, g_all, w_all, layer):
    m = mem.shape[0]
    n = 2 * XA_WIDTH
    return pl.pallas_call(
        _norm_matmul_kernel,
        grid=(1,),
        in_specs=[
            pl.BlockSpec((m, D_MODEL), lambda i: (0, 0)),
            _vec_spec(D_MODEL, layer, 1),
            pl.BlockSpec((None, D_MODEL, n), lambda i: (layer, 0, 0)),
        ],
        out_specs=pl.BlockSpec((m, n), lambda i: (0, 0)),
        out_shape=jax.ShapeDtypeStruct((m, n), BF16),
        compiler_params=_params(("arbitrary",)),
        name="mem_kv",
    )(mem, g_all, w_all)


XA_TM = 512


def _xattn_kernel(x_ref, g_ref, wq_ref, kv_ref, wo_ref, o_ref):
    x = x_ref[...]
    h = _rms(x, g_ref[...], RMS_EPS).astype(BF16)
    q = jnp.dot(h, wq_ref[...], preferred_element_type=F32).astype(BF16)
    heads = []
    for hd in range(XA_HEADS):
        lo = hd * XA_HEAD_DIM
        k = kv_ref[:, lo:lo + XA_HEAD_DIM]
        v = kv_ref[:, XA_WIDTH + lo:XA_WIDTH + lo + XA_HEAD_DIM]
        s = lax.dot_general(q[:, lo:lo + XA_HEAD_DIM], k, (((1,), (1,)), ((), ())),
                            preferred_element_type=F32) * (XA_HEAD_DIM ** -0.5)
        m = jnp.max(s, axis=-1, keepdims=True)
        e = jnp.exp(s - m)
        p = e / jnp.sum(e, axis=-1, keepdims=True)
        heads.append(jnp.dot(p.astype(BF16), v, preferred_element_type=F32).astype(BF16))
    o = jnp.concatenate(heads, axis=-1)
    o_ref[...] = x + jnp.dot(o, wo_ref[...], preferred_element_type=F32)


def _xattn(x, g_all, wq_all, kv, wo_all, layer):
    t = x.shape[0]
    return pl.pallas_call(
        _xattn_kernel,
        grid=(t // XA_TM,),
        in_specs=[
            pl.BlockSpec((XA_TM, D_MODEL), lambda i: (i, 0)),
            _vec_spec(D_MODEL, layer, 1),
            pl.BlockSpec((None, D_MODEL, XA_WIDTH), lambda i: (layer, 0, 0)),
            pl.BlockSpec((MEM_LEN, 2 * XA_WIDTH), lambda i: (0, 0)),
            pl.BlockSpec((None, XA_WIDTH, D_MODEL), lambda i: (layer, 0, 0)),
        ],
        out_specs=pl.BlockSpec((XA_TM, D_MODEL), lambda i: (i, 0)),
        out_shape=jax.ShapeDtypeStruct((t, D_MODEL), F32),
        compiler_params=_params(("arbitrary",)),
        name="mem_xattn",
    )(x, g_all, wq_all, kv, wo_all)


RT_TM = 512
_E_LANE0 = MOE_GROUPS


def _router_kernel(x_ref, g_ref, w_ref, b_ref, h_ref, comb_ref):
    h = _rms(x_ref[...], g_ref[...], RMS_EPS).astype(BF16)
    h_ref[...] = h
    logits = jnp.dot(h, w_ref[...], preferred_element_type=F32) + b_ref[...]
    lane = lax.broadcasted_iota(jnp.int32, logits.shape, 1).astype(F32)
    neg = jnp.float32(-jnp.inf)
    big = jnp.float32(LANES)

    def first_argmax(vals):
        top = jnp.max(vals, axis=-1, keepdims=True)
        idx = jnp.min(jnp.where(vals == top, lane, big), axis=-1, keepdims=True)
        return top, idx

    gl = jnp.where(lane < MOE_GROUPS, logits, neg)
    g_top, g_idx = first_argmax(gl)
    g_w = 1.0 / jnp.sum(jnp.exp(gl - g_top), axis=-1, keepdims=True)
    e_lane = lane - _E_LANE0
    in_group = jnp.logical_and(e_lane >= g_idx * MOE_PER_GROUP, e_lane < (g_idx + 1) * MOE_PER_GROUP)
    el = jnp.where(in_group, logits, neg)
    v1, i1 = first_argmax(el)
    el2 = jnp.where(lane == i1, neg, el)
    v2, i2 = first_argmax(el2)
    e2 = jnp.exp(v2 - v1)
    w1 = 1.0 / (1.0 + e2)
    w2 = e2 / (1.0 + e2)
    comb_ref[...] = (jnp.where(lane == i1, w1 * g_w, 0.0) + jnp.where(lane == i2, w2 * g_w, 0.0))


def _router(x, g_all, w_r, b_r, layer):
    t = x.shape[0]
    return pl.pallas_call(
        _router_kernel,
        grid=(t // RT_TM,),
        in_specs=[
            pl.BlockSpec((RT_TM, D_MODEL), lambda i: (i, 0)),
            _vec_spec(D_MODEL, layer, 1),
            pl.BlockSpec((None, D_MODEL, LANES), lambda i: (layer, 0, 0)),
            _vec_spec(LANES, layer, 1),
        ],
        out_specs=[
            pl.BlockSpec((RT_TM, D_MODEL), lambda i: (i, 0)),
            pl.BlockSpec((RT_TM, LANES), lambda i: (i, 0)),
        ],
        out_shape=[jax.ShapeDtypeStruct((t, D_MODEL), BF16), jax.ShapeDtypeStruct((t, LANES), F32)],
        compiler_params=_params(("arbitrary",)),
        name="moe_router",
    )(x, g_all, w_r, b_r)


EX_TM = 512


def _experts_kernel(h_ref, comb_ref, x_ref, wg_ref, wu_ref, wd_ref, o_ref):
    e = pl.program_id(1)

    @pl.when(e == 0)
    def _():
        o_ref[...] = x_ref[...]

    h = h_ref[...]
    gate = jnp.dot(h, wg_ref[...], preferred_element_type=F32)
    up = jnp.dot(h, wu_ref[...], preferred_element_type=F32)
    act = (gate * jax.nn.sigmoid(gate) * up).astype(BF16)
    lane = lax.broadcasted_iota(jnp.int32, comb_ref.shape, 1)
    w = jnp.sum(jnp.where(lane == e + _E_LANE0, comb_ref[...], 0.0), axis=-1, keepdims=True)
    o_ref[...] += w * jnp.dot(act, wd_ref[...], preferred_element_type=F32)


def _experts(h, comb, x, wg_all, wu_all, wd_all, layer):
    t = h.shape[0]
    return pl.pallas_call(
        _experts_kernel,
        grid=(t // EX_TM, MOE_EXPERTS),
        in_specs=[
            pl.BlockSpec((EX_TM, D_MODEL), lambda i, e: (i, 0)),
            pl.BlockSpec((EX_TM, LANES), lambda i, e: (i, 0)),
            pl.BlockSpec((EX_TM, D_MODEL), lambda i, e: (i, 0)),
            pl.BlockSpec((None, None, D_MODEL, MOE_FF), lambda i, e: (layer, e, 0, 0)),
            pl.BlockSpec((None, None, D_MODEL, MOE_FF), lambda i, e: (layer, e, 0, 0)),
            pl.BlockSpec((None, None, MOE_FF, D_MODEL), lambda i, e: (layer, e, 0, 0)),
        ],
        out_specs=pl.BlockSpec((EX_TM, D_MODEL), lambda i, e: (i, 0)),
        out_shape=jax.ShapeDtypeStruct((t, D_MODEL), F32),
        compiler_params=_params(("arbitrary", "arbitrary")),
        name="moe_experts",
    )(h, comb, x, wg_all, wu_all, wd_all)


FN_TM = 512


def _final_norm_kernel(x_ref, g_ref, o_ref):
    o_ref[...] = _rms(x_ref[...], g_ref[...], RMS_EPS)


def _final_norm(x, g):
    t = x.shape[0]
    return pl.pallas_call(
        _final_norm_kernel,
        grid=(t // FN_TM,),
        in_specs=[pl.BlockSpec((FN_TM, D_MODEL), lambda i: (i, 0)),
                  pl.BlockSpec((1, D_MODEL), lambda i: (0, 0))],
        out_specs=pl.BlockSpec((FN_TM, D_MODEL), lambda i: (i, 0)),
        out_shape=jax.ShapeDtypeStruct((t, D_MODEL), F32),
        compiler_params=_params(("arbitrary",)),
        name="final_norm",
    )(x, g)


def kernel(x, mem, positions, norm_mix, w_in, da_lam_q1, da_lam_k1, da_lam_q2, da_lam_k2, da_head_norm, w_da_out, cv_dw_w, cv_dw_b, cv_ln_g, cv_ln_b, w_cv_out, ssm_lam_re, ssm_lam_im, ssm_log_dt, ssm_b_re, ssm_b_im, ssm_c_re, ssm_c_im, ssm_d, w_ssm_glu, b_ssm_glu, w_ssm_out, w_mix_out, norm_xa, norm_mem, w_xa_q, w_xa_kv, w_xa_out, norm_ffn, w_router_group, b_router_group, w_router_expert, b_router_expert, w_exp_gate, w_exp_up, w_exp_down, norm_final):
    bsz, seq, _ = x.shape
    assert bsz == 1 and seq == SEQ
    nl = w_in.shape[0]
    xs = x.reshape(seq, D_MODEL).astype(F32)
    mem2 = mem.reshape(MEM_LEN, D_MODEL).astype(F32)

    inv_freq = ROPE_THETA ** (-jnp.arange(0, DA_HEAD_DIM, 2, dtype=F32) / DA_HEAD_DIM)
    ang = positions.reshape(seq).astype(F32)[:, None] * inv_freq
    cos = jnp.cos(ang)
    sin = jnp.sin(ang)
    cos_t = jnp.concatenate([cos, cos, cos, cos], axis=-1)
    sin_t = jnp.concatenate([-sin, sin, -sin, sin], axis=-1)

    w_in_b = jnp.concatenate([w_in[..., IN_MAIN:], w_in[..., :IN_MAIN]], axis=-1).astype(BF16)
    w_da_b = w_da_out.astype(BF16)
    w_cv_b = w_cv_out.astype(BF16)
    w_glu_b = w_ssm_glu.astype(BF16)
    w_so_b = w_ssm_out.astype(BF16)
    w_mix_b = w_mix_out.astype(BF16)
    w_xq_b = w_xa_q.astype(BF16)
    w_xkv_b = w_xa_kv.astype(BF16)
    w_xo_b = w_xa_out.astype(BF16)
    w_eg_b = w_exp_gate.astype(BF16)
    w_eu_b = w_exp_up.astype(BF16)
    w_ed_b = w_exp_down.astype(BF16)
    pad = LANES - MOE_GROUPS - MOE_EXPERTS
    w_r = jnp.concatenate([w_router_group, w_router_expert,
                           jnp.zeros((nl, D_MODEL, pad), F32)], axis=-1).astype(BF16)
    b_r = jnp.concatenate([b_router_group, b_router_expert, jnp.zeros((nl, pad), F32)], axis=-1).astype(F32)

    def vec3(a):
        return a.astype(F32).reshape(nl, 1, a.shape[-1])

    norm_mix, da_head_norm, cv_dw_b, cv_ln_g, cv_ln_b, b_ssm_glu, norm_xa, norm_mem, norm_ffn, b_r = map(
        vec3, (norm_mix, da_head_norm, cv_dw_b, cv_ln_g, cv_ln_b, b_ssm_glu, norm_xa, norm_mem, norm_ffn, b_r))
    cv_dw_w = cv_dw_w.astype(F32)

    lam_inits = jnp.asarray([0.8 - 0.6 * math.exp(-0.3 * l) for l in range(nl)], F32)
    lam_pack = jnp.stack([da_lam_q1, da_lam_k1, da_lam_q2, da_lam_k2], axis=1).astype(F32)
    lam_pack = jnp.concatenate(
        [lam_pack, jnp.broadcast_to(lam_inits[:, None, None], (nl, 4, DA_HEAD_DIM))], axis=1)

    ssm_mats = _ssm_matrices(ssm_lam_re, ssm_lam_im, ssm_log_dt, ssm_b_re, ssm_b_im,
                             ssm_c_re, ssm_c_im, ssm_d)

    for l in range(nl):
        proj = _inproj(xs, norm_mix, w_in_b, l, cos_t, sin_t)
        o_a = _attention(proj, lam_pack, da_head_norm, l)
        z_b = _conv(proj, cv_dw_w, cv_dw_b, cv_ln_g, cv_ln_b, l)
        u_pairs = _to_pairs(proj[:, COL_SSM:])
        y_c = _from_pairs(_ssm(u_pairs, [m[l] for m in ssm_mats]))
        merged = _merge(o_a, z_b, y_c, proj, w_da_b, w_cv_b, w_glu_b, b_ssm_glu, w_so_b, l)
        xs = _matmul_residual(merged, w_mix_b, xs, l)
        kv = _mem_kv(mem2, norm_mem, w_xkv_b, l)
        xs = _xattn(xs, norm_xa, w_xq_b, kv, w_xo_b, l)
        h, comb = _router(xs, norm_ffn, w_r, b_r, l)
        xs = _experts(h, comb, xs, w_eg_b, w_eu_b, w_ed_b, l)
    out = _final_norm(xs, norm_final.reshape(1, D_MODEL))
    return out.reshape(bsz, seq, D_MODEL)
```

```python
import functools
import math

import jax
import jax.numpy as jnp
from jax import lax
from jax.experimental import pallas as pl
from jax.experimental.pallas import tpu as pltpu

F32 = jnp.float32
BF16 = jnp.bfloat16

D_MODEL = 2048
SEQ = 8192
DEPTH = 4
MEM_LEN = 256
DA_HEADS = 8
DA_HEAD_DIM = 64
DA_V_DIM = 128
DA_WIDTH = 1024
ROPE_THETA = 10000.0
CV_WIDTH = 512
CONV_TAPS = 31
SSM_WIDTH = 512
SSM_GROUP = 16
SSM_GROUPS = 32
SSM_STATE = 64
XA_HEADS = 4
XA_HEAD_DIM = 128
XA_WIDTH = 512
MOE_GROUPS = 4
MOE_PER_GROUP = 4
MOE_EXPERTS = 16
MOE_FF = 512
RMS_EPS = 1e-6
HEAD_NORM_EPS = 1e-5
LN_EPS = 1e-5

COL_Q = 0
COL_K = COL_Q + 1024
COL_V = COL_K + 1024
COL_CVA = COL_V + 1024
COL_CVB = COL_CVA + CV_WIDTH
COL_SSM = COL_CVB + CV_WIDTH
COL_GA = COL_SSM + SSM_WIDTH
COL_GB = COL_GA + D_MODEL
COL_GC = COL_GB + D_MODEL
IN_TOTAL = COL_GC + D_MODEL

LANES = 128
VMEM_LIMIT = 56 * 1024 * 1024

SSM_CHUNK = 16
SSM_NCHUNK = SEQ // SSM_CHUNK
SSM_PAIRS = SSM_GROUPS // 2
SSM_PAIR_W = 2 * SSM_CHUNK * SSM_GROUP


def _params(sem, vmem=VMEM_LIMIT):
    return pltpu.CompilerParams(dimension_semantics=sem, vmem_limit_bytes=vmem)


def _vec_spec(width, layer, ngrid):
    if ngrid == 1:
        return pl.BlockSpec((None, 1, width), lambda i: (layer, 0, 0))
    return pl.BlockSpec((None, 1, width), lambda i, j: (layer, 0, 0))


def _rms(xf, g, eps):
    ms = jnp.mean(xf * xf, axis=-1, keepdims=True)
    return xf * lax.rsqrt(ms + eps) * g


INP_TM = 1024
INP_TN = 512
_Q_TILE0 = COL_Q // INP_TN
_K_TILE0 = COL_K // INP_TN
_V_TILE0 = COL_V // INP_TN


def _inproj_kernel(x_ref, g_ref, w_ref, cos_ref, sin_ref, o_ref, h_ref):
    j = pl.program_id(1)

    @pl.when(j == 0)
    def _():
        h_ref[...] = _rms(x_ref[...], g_ref[...], RMS_EPS).astype(BF16)

    acc = jnp.dot(h_ref[...], w_ref[...], preferred_element_type=F32)
    is_rope = jnp.logical_and(j >= _Q_TILE0, j < _V_TILE0)

    @pl.when(is_rope)
    def _():
        scale = jnp.where(j < _K_TILE0, math.log2(math.e) * DA_HEAD_DIM ** -0.5, 1.0).astype(F32)
        cos = cos_ref[...] * scale
        sin = sin_ref[...] * scale
        lane = lax.broadcasted_iota(jnp.int32, (INP_TM, LANES), 1)
        first_half = (lane % DA_HEAD_DIM) < (DA_HEAD_DIM // 2)
        for c in range(INP_TN // LANES):
            t = acc[:, c * LANES:(c + 1) * LANES]
            swapped = jnp.where(first_half, pltpu.roll(t, LANES - 32, 1), pltpu.roll(t, 32, 1))
            o_ref[:, c * LANES:(c + 1) * LANES] = (t * cos + swapped * sin).astype(BF16)

    @pl.when(jnp.logical_not(is_rope))
    def _():
        o_ref[...] = acc.astype(BF16)


def _inproj(x, g, w_all, layer, cos_t, sin_t):
    t = x.shape[0]
    return pl.pallas_call(
        _inproj_kernel,
        grid=(t // INP_TM, IN_TOTAL // INP_TN),
        in_specs=[
            pl.BlockSpec((INP_TM, D_MODEL), lambda i, j: (i, 0)),
            _vec_spec(D_MODEL, layer, 2),
            pl.BlockSpec((None, D_MODEL, INP_TN), lambda i, j: (layer, 0, j)),
            pl.BlockSpec((INP_TM, LANES), lambda i, j: (i, 0)),
            pl.BlockSpec((INP_TM, LANES), lambda i, j: (i, 0)),
        ],
        out_specs=pl.BlockSpec((INP_TM, INP_TN), lambda i, j: (i, j)),
        out_shape=jax.ShapeDtypeStruct((t, IN_TOTAL), BF16),
        scratch_shapes=[pltpu.VMEM((INP_TM, D_MODEL), BF16)],
        compiler_params=_params(("arbitrary", "arbitrary")),
        name="inproj",
    )(x, g, w_all, cos_t, sin_t)


ATT_TQ = 1024
ATT_TK = 1024


ATT_RG = 1024


def _attn_kernel(lam_ref, g_ref, q_ref, k_ref, v_ref, o_ref, qs_ref, m_ref, l_ref, acc_ref):
    i = pl.program_id(1)
    tq = ATT_TQ
    q = q_ref[...]
    lane = lax.broadcasted_iota(jnp.int32, q.shape, 1)
    zero = jnp.zeros_like(q)
    qs_ref[0:tq, :] = jnp.where(lane < DA_HEAD_DIM, q, zero)
    qs_ref[tq:2 * tq, :] = jnp.where(lane >= DA_HEAD_DIM, q, zero)
    m_ref[...] = jnp.full(m_ref.shape, -jnp.inf, F32)
    l_ref[...] = jnp.zeros(l_ref.shape, F32)
    acc_ref[...] = jnp.zeros(acc_ref.shape, F32)

    def update_rows(r0, k, v, mask):
        n = k.shape[0]
        rows = slice(r0, r0 + ATT_RG)
        s = lax.dot_general(qs_ref[rows, :], k, (((1,), (1,)), ((), ())), preferred_element_type=F32)
        if mask is not None:
            s = jnp.where(mask, s, -jnp.inf)
        tiles = [s[:, t * LANES:(t + 1) * LANES] for t in range(n // LANES)]
        mc = functools.reduce(jnp.maximum, tiles)
        m_old = m_ref[rows, :]
        m_new = jnp.maximum(m_old, jnp.max(mc, axis=1, keepdims=True))
        alpha = jnp.exp2(m_old - m_new)
        p_tiles = [jnp.exp2(t - m_new) for t in tiles]
        l_ref[rows, :] = alpha * l_ref[rows, :] + functools.reduce(jnp.add, p_tiles)
        p = jnp.concatenate(p_tiles, axis=1).astype(BF16)
        acc_ref[rows, :] = alpha * acc_ref[rows, :] + jnp.dot(p, v, preferred_element_type=F32)
        m_ref[rows, :] = m_new

    def body(c, carry):
        off = pl.multiple_of(c * ATT_TK, ATT_TK)
        k = k_ref[pl.ds(off, ATT_TK), :]
        v = v_ref[pl.ds(off, ATT_TK), :]
        for r0 in range(0, 2 * tq, ATT_RG):
            update_rows(r0, k, v, None)
        return carry

    lax.fori_loop(0, i * (tq // ATT_TK), body, 0)

    off = pl.multiple_of(i * tq, tq)
    for r0 in range(0, 2 * tq, ATT_RG):
        qo = r0 % tq
        n = qo + ATT_RG
        row = lax.broadcasted_iota(jnp.int32, (ATT_RG, n), 0)
        col = lax.broadcasted_iota(jnp.int32, (ATT_RG, n), 1)
        update_rows(r0, k_ref[pl.ds(off, n), :], v_ref[pl.ds(off, n), :], col <= row + qo)

    lam_init = lam_ref[4:5, 0:1]
    lam = (jnp.exp(jnp.sum(lam_ref[0:1, :] * lam_ref[1:2, :], axis=1, keepdims=True))
           - jnp.exp(jnp.sum(lam_ref[2:3, :] * lam_ref[3:4, :], axis=1, keepdims=True))
           + lam_init)
    acc = acc_ref[...]
    inv_l = 1.0 / jnp.sum(l_ref[...], axis=1, keepdims=True)
    o = acc[0:tq] * inv_l[0:tq] - lam * (acc[tq:2 * tq] * inv_l[tq:2 * tq])
    o = _rms(o, g_ref[...], HEAD_NORM_EPS) * (1.0 - lam_init)
    o_ref[...] = o.astype(BF16)


def _attention(proj, lam_pack, head_g, layer):
    t = proj.shape[0]
    qb, kb, vb = COL_Q // LANES, COL_K // LANES, COL_V // LANES
    return pl.pallas_call(
        _attn_kernel,
        grid=(DA_HEADS, t // ATT_TQ),
        in_specs=[
            pl.BlockSpec((None, 8, DA_HEAD_DIM), lambda h, i: (layer, 0, 0)),
            _vec_spec(DA_V_DIM, layer, 2),
            pl.BlockSpec((ATT_TQ, LANES), lambda h, i: (i, qb + h)),
            pl.BlockSpec((t, LANES), lambda h, i: (0, kb + h)),
            pl.BlockSpec((t, LANES), lambda h, i: (0, vb + h)),
        ],
        out_specs=pl.BlockSpec((ATT_TQ, LANES), lambda h, i: (i, h)),
        out_shape=jax.ShapeDtypeStruct((t, DA_WIDTH), BF16),
        scratch_shapes=[
            pltpu.VMEM((2 * ATT_TQ, LANES), BF16),
            pltpu.VMEM((2 * ATT_TQ, LANES), F32),
            pltpu.VMEM((2 * ATT_TQ, LANES), F32),
            pltpu.VMEM((2 * ATT_TQ, LANES), F32),
        ],
        compiler_params=_params(("arbitrary", "arbitrary")),
        name="diff_attn",
    )(lam_pack, head_g, proj, proj, proj)


CV_TM = 512
CV_HALO = 32
CV_ROWS = 64


def _conv_kernel(a_ref, b_ref, w_ref, bias_ref, g_ref, beta_ref, o_ref, z_ref):
    i = pl.program_id(0)

    @pl.when(i == 0)
    def _():
        z_ref[0:CV_HALO, :] = jnp.zeros((CV_HALO, CV_WIDTH), F32)

    @pl.when(i > 0)
    def _():
        z_ref[0:CV_HALO, :] = z_ref[CV_TM:CV_TM + CV_HALO, :]

    a = a_ref[...].astype(F32)
    b = b_ref[...].astype(F32)
    z_ref[CV_HALO:CV_HALO + CV_TM, :] = a * jax.nn.sigmoid(b)

    base = CV_HALO - (CONV_TAPS - 1)
    for r in range(0, CV_TM, CV_ROWS):
        acc = jnp.zeros((CV_ROWS, CV_WIDTH), F32) + bias_ref[...]
        for j in range(CONV_TAPS):
            acc = acc + w_ref[j:j + 1, :] * z_ref[base + r + j:base + r + j + CV_ROWS, :]
        mu = jnp.mean(acc, axis=-1, keepdims=True)
        xc = acc - mu
        var = jnp.mean(xc * xc, axis=-1, keepdims=True)
        y = xc * lax.rsqrt(var + LN_EPS) * g_ref[...] + beta_ref[...]
        o_ref[r:r + CV_ROWS, :] = (y * jax.nn.sigmoid(y)).astype(BF16)


def _conv(proj, dw_w, dw_b, ln_g, ln_b, layer):
    t = proj.shape[0]
    ab, bb = COL_CVA // CV_WIDTH, COL_CVB // CV_WIDTH
    vec = _vec_spec(CV_WIDTH, layer, 1)
    return pl.pallas_call(
        _conv_kernel,
        grid=(t // CV_TM,),
        in_specs=[
            pl.BlockSpec((CV_TM, CV_WIDTH), lambda i: (i, ab)),
            pl.BlockSpec((CV_TM, CV_WIDTH), lambda i: (i, bb)),
            pl.BlockSpec((None, CONV_TAPS, CV_WIDTH), lambda i: (layer, 0, 0)),
            vec, vec, vec,
        ],
        out_specs=pl.BlockSpec((CV_TM, CV_WIDTH), lambda i: (i, 0)),
        out_shape=jax.ShapeDtypeStruct((t, CV_WIDTH), BF16),
        scratch_shapes=[pltpu.VMEM((CV_HALO + CV_TM, CV_WIDTH), F32)],
        compiler_params=_params(("arbitrary",)),
        name="conformer_conv",
    )(proj, proj, dw_w, dw_b, ln_g, ln_b)


def _ssm_kernel(u_ref, t_ref, wre_ref, wim_ref, vre_ref, vim_ref, ar_ref, ai_ref, d_ref,
                y_ref, sre_ref, sim_ref):
    npair, nchunk = SSM_PAIRS, SSM_NCHUNK
    for p in range(npair):
        u = u_ref[p]
        sre_ref[pl.ds(p, nchunk, stride=npair), :] = jnp.dot(u, wre_ref[p], preferred_element_type=F32)
        sim_ref[pl.ds(p, nchunk, stride=npair), :] = jnp.dot(u, wim_ref[p], preferred_element_type=F32)

    ar = ar_ref[...]
    ai = ai_ref[...]

    def body(c, carry):
        xr, xi = carry
        off = pl.multiple_of(c * npair, npair)
        sr = sre_ref[pl.ds(off, npair), :]
        si = sim_ref[pl.ds(off, npair), :]
        sre_ref[pl.ds(off, npair), :] = xr
        sim_ref[pl.ds(off, npair), :] = xi
        return ar * xr - ai * xi + sr, ar * xi + ai * xr + si

    zero = jnp.zeros((npair, LANES), F32)
    lax.fori_loop(0, nchunk, body, (zero, zero), unroll=8)

    for p in range(npair):
        u = u_ref[p]
        xr = sre_ref[pl.ds(p, nchunk, stride=npair), :].astype(BF16)
        xi = sim_ref[pl.ds(p, nchunk, stride=npair), :].astype(BF16)
        y = jnp.dot(u, t_ref[p], preferred_element_type=F32)
        y = y + jnp.dot(xr, vre_ref[p], preferred_element_type=F32)
        y = y + jnp.dot(xi, vim_ref[p], preferred_element_type=F32)
        y = y + u.astype(F32) * d_ref[p:p + 1, :]
        y_ref[p] = y.astype(BF16)


def _ssm(u_pairs, mats):
    vm = pl.BlockSpec(memory_space=pltpu.VMEM)
    return pl.pallas_call(
        _ssm_kernel,
        in_specs=[vm] * 9,
        out_specs=vm,
        out_shape=jax.ShapeDtypeStruct((SSM_PAIRS, SSM_NCHUNK, SSM_PAIR_W), BF16),
        scratch_shapes=[pltpu.VMEM((SSM_NCHUNK * SSM_PAIRS, LANES), F32),
                        pltpu.VMEM((SSM_NCHUNK * SSM_PAIRS, LANES), F32)],
        compiler_params=pltpu.CompilerParams(vmem_limit_bytes=VMEM_LIMIT),
        name="s5_scan",
    )(u_pairs, *mats)


def _blockdiag2(m):
    l, _, r, c = m.shape
    m = m.reshape(l, SSM_PAIRS, 2, r, c)
    z = jnp.zeros_like(m[:, :, 0])
    top = jnp.concatenate([m[:, :, 0], z], axis=-1)
    bot = jnp.concatenate([z, m[:, :, 1]], axis=-1)
    return jnp.concatenate([top, bot], axis=-2)


def _ssm_matrices(lam_re, lam_im, log_dt, b_re, b_im, c_re, c_im, d_skip):
    hp = lax.Precision.HIGHEST
    lr = lam_re.astype(F32)
    li = lam_im.astype(F32)
    dt = jnp.exp(log_dt.astype(F32))[..., None]
    mag = jnp.exp(lr * dt)
    ab_re = mag * jnp.cos(li * dt)
    ab_im = mag * jnp.sin(li * dt)
    den = lr * lr + li * li
    f_re = ((ab_re - 1.0) * lr + ab_im * li) / den
    f_im = (ab_im * lr - (ab_re - 1.0) * li) / den
    br = b_re.astype(F32)
    bi = b_im.astype(F32)
    bb_re = f_re[..., None] * br - f_im[..., None] * bi
    bb_im = f_re[..., None] * bi + f_im[..., None] * br
    tau = jnp.arange(SSM_CHUNK + 1, dtype=F32)[:, None, None, None]
    pmag = jnp.exp(tau * (lr * dt)[None])
    pw_re = pmag * jnp.cos(tau * (li * dt)[None])
    pw_im = pmag * jnp.sin(tau * (li * dt)[None])
    cr = c_re.astype(F32)
    ci = c_im.astype(F32)
    ca_re = cr[None] * pw_re[:, :, :, None, :] - ci[None] * pw_im[:, :, :, None, :]
    ca_im = cr[None] * pw_im[:, :, :, None, :] + ci[None] * pw_re[:, :, :, None, :]
    kk = (jnp.einsum('tlghp,lgpk->lgthk', ca_re[:SSM_CHUNK], bb_re, precision=hp)
          - jnp.einsum('tlghp,lgpk->lgthk', ca_im[:SSM_CHUNK], bb_im, precision=hp))
    s_idx = jnp.arange(SSM_CHUNK)[:, None]
    t_idx = jnp.arange(SSM_CHUNK)[None, :]
    lag = t_idx - s_idx
    tm = kk[:, :, jnp.clip(lag, 0, SSM_CHUNK - 1)]
    tm = jnp.where((lag >= 0)[None, None, :, :, None, None], tm, 0.0)
    nl = lr.shape[0]
    tm = tm.transpose(0, 1, 2, 5, 3, 4).reshape(nl, SSM_GROUPS, 256, 256)
    rev_re = pw_re[:SSM_CHUNK][::-1]
    rev_im = pw_im[:SSM_CHUNK][::-1]
    w_re = rev_re[..., None] * bb_re[None] - rev_im[..., None] * bb_im[None]
    w_im = rev_re[..., None] * bb_im[None] + rev_im[..., None] * bb_re[None]
    w_re = w_re.transpose(1, 2, 0, 4, 3).reshape(nl, SSM_GROUPS, 256, SSM_STATE)
    w_im = w_im.transpose(1, 2, 0, 4, 3).reshape(nl, SSM_GROUPS, 256, SSM_STATE)
    v_re = ca_re[1:].transpose(1, 2, 4, 0, 3).reshape(nl, SSM_GROUPS, SSM_STATE, 256)
    v_im = (-ca_im[1:]).transpose(1, 2, 4, 0, 3).reshape(nl, SSM_GROUPS, SSM_STATE, 256)
    a_re = pw_re[SSM_CHUNK].reshape(nl, SSM_PAIRS, LANES)
    a_im = pw_im[SSM_CHUNK].reshape(nl, SSM_PAIRS, LANES)
    d_t = jnp.broadcast_to(d_skip.astype(F32)[:, :, None, :], (nl, SSM_GROUPS, SSM_CHUNK, SSM_GROUP))
    d_t = d_t.reshape(nl, SSM_PAIRS, SSM_PAIR_W)
    return (_blockdiag2(tm).astype(BF16), _blockdiag2(w_re).astype(BF16), _blockdiag2(w_im).astype(BF16),
            _blockdiag2(v_re).astype(BF16), _blockdiag2(v_im).astype(BF16), a_re, a_im, d_t)


def _to_pairs(u):
    t = u.shape[0]
    u = u.reshape(t // SSM_CHUNK, SSM_CHUNK, SSM_PAIRS, 2, SSM_GROUP)
    return u.transpose(2, 0, 3, 1, 4).reshape(SSM_PAIRS, t // SSM_CHUNK, SSM_PAIR_W)


def _from_pairs(y):
    nchunk = y.shape[1]
    y = y.reshape(SSM_PAIRS, nchunk, 2, SSM_CHUNK, SSM_GROUP)
    return y.transpose(1, 3, 0, 2, 4).reshape(nchunk * SSM_CHUNK, SSM_WIDTH)


MG_TM = 512
MG_TN = 512


def _merge_mix_kernel(oa_ref, zb_ref, yc_ref, ga_ref, gb_ref, gc_ref, wda_ref, wcv_ref, wglu_ref,
                      bglu_ref, wso_ref, wmix_ref, x_ref, o_ref, sc_ref):
    n = pl.program_id(1)

    @pl.when(n == 0)
    def _():
        glu = jnp.dot(yc_ref[...], wglu_ref[...], preferred_element_type=F32) + bglu_ref[...]
        sc_ref[...] = (glu[:, :SSM_WIDTH] * jax.nn.sigmoid(glu[:, SSM_WIDTH:])).astype(BF16)
        o_ref[...] = x_ref[...]

    y_a = jnp.dot(oa_ref[...], wda_ref[...], preferred_element_type=F32)
    y_b = jnp.dot(zb_ref[...], wcv_ref[...], preferred_element_type=F32)
    y_c = jnp.dot(sc_ref[...], wso_ref[...], preferred_element_type=F32)
    merged = (jax.nn.sigmoid(ga_ref[...].astype(F32)) * y_a
              + jax.nn.sigmoid(gb_ref[...].astype(F32)) * y_b
              + jax.nn.sigmoid(gc_ref[...].astype(F32)) * y_c)
    o_ref[...] += jnp.dot(merged.astype(BF16), wmix_ref[...], preferred_element_type=F32)


def _merge_mix(o_a, z_b, y_c, proj, w_da, w_cv, w_glu, b_glu, w_so, w_mix, x, layer):
    t = o_a.shape[0]
    ga, gb, gc = COL_GA // MG_TN, COL_GB // MG_TN, COL_GC // MG_TN

    def wcol(k):
        return pl.BlockSpec((None, k, MG_TN), lambda i, n: (layer, 0, n))

    return pl.pallas_call(
        _merge_mix_kernel,
        grid=(t // MG_TM, D_MODEL // MG_TN),
        in_specs=[
            pl.BlockSpec((MG_TM, DA_WIDTH), lambda i, n: (i, 0)),
            pl.BlockSpec((MG_TM, CV_WIDTH), lambda i, n: (i, 0)),
            pl.BlockSpec((MG_TM, SSM_WIDTH), lambda i, n: (i, 0)),
            pl.BlockSpec((MG_TM, MG_TN), lambda i, n: (i, ga + n)),
            pl.BlockSpec((MG_TM, MG_TN), lambda i, n: (i, gb + n)),
            pl.BlockSpec((MG_TM, MG_TN), lambda i, n: (i, gc + n)),
            wcol(DA_WIDTH), wcol(CV_WIDTH),
            pl.BlockSpec((None, SSM_WIDTH, 2 * SSM_WIDTH), lambda i, n: (layer, 0, 0)),
            _vec_spec(2 * SSM_WIDTH, layer, 2),
            wcol(SSM_WIDTH),
            pl.BlockSpec((None, MG_TN, D_MODEL), lambda i, n: (layer, n, 0)),
            pl.BlockSpec((MG_TM, D_MODEL), lambda i, n: (i, 0)),
        ],
        out_specs=pl.BlockSpec((MG_TM, D_MODEL), lambda i, n: (i, 0)),
        out_shape=jax.ShapeDtypeStruct((t, D_MODEL), F32),
        scratch_shapes=[pltpu.VMEM((MG_TM, SSM_WIDTH), BF16)],
        compiler_params=_params(("arbitrary", "arbitrary")),
        name="merge_mix",
    )(o_a, z_b, y_c, proj, proj, proj, w_da, w_cv, w_glu, b_glu, w_so, w_mix, x)


def _norm_matmul_kernel(x_ref, g_ref, w_ref, o_ref):
    h = _rms(x_ref[...], g_ref[...], RMS_EPS).astype(BF16)
    o_ref[...] = jnp.dot(h, w_ref[...], preferred_element_type=F32).astype(o_ref.dtype)


def _mem_kv(mem, g_all, w_all, layer):
    m = mem.shape[0]
    n = 2 * XA_WIDTH
    return pl.pallas_call(
        _norm_matmul_kernel,
        grid=(1,),
        in_specs=[
            pl.BlockSpec((m, D_MODEL), lambda i: (0, 0)),
            _vec_spec(D_MODEL, layer, 1),
            pl.BlockSpec((None, D_MODEL, n), lambda i: (layer, 0, 0)),
        ],
        out_specs=pl.BlockSpec((m, n), lambda i: (0, 0)),
        out_shape=jax.ShapeDtypeStruct((m, n), BF16),
        compiler_params=_params(("arbitrary",)),
        name="mem_kv",
    )(mem, g_all, w_all)


XA_TM = 512


def _xattn_kernel(x_ref, g_ref, wq_ref, kv_ref, wo_ref, o_ref):
    x = x_ref[...]
    h = _rms(x, g_ref[...], RMS_EPS).astype(BF16)
    q = jnp.dot(h, wq_ref[...], preferred_element_type=F32).astype(BF16)
    heads = []
    for hd in range(XA_HEADS):
        lo = hd * XA_HEAD_DIM
        k = kv_ref[:, lo:lo + XA_HEAD_DIM]
        v = kv_ref[:, XA_WIDTH + lo:XA_WIDTH + lo + XA_HEAD_DIM]
        s = lax.dot_general(q[:, lo:lo + XA_HEAD_DIM], k, (((1,), (1,)), ((), ())),
                            preferred_element_type=F32) * (XA_HEAD_DIM ** -0.5)
        m = jnp.max(s, axis=-1, keepdims=True)
        e = jnp.exp(s - m)
        p = e / jnp.sum(e, axis=-1, keepdims=True)
        heads.append(jnp.dot(p.astype(BF16), v, preferred_element_type=F32).astype(BF16))
    o = jnp.concatenate(heads, axis=-1)
    o_ref[...] = x + jnp.dot(o, wo_ref[...], preferred_element_type=F32)


def _xattn(x, g_all, wq_all, kv, wo_all, layer):
    t = x.shape[0]
    return pl.pallas_call(
        _xattn_kernel,
        grid=(t // XA_TM,),
        in_specs=[
            pl.BlockSpec((XA_TM, D_MODEL), lambda i: (i, 0)),
            _vec_spec(D_MODEL, layer, 1),
            pl.BlockSpec((None, D_MODEL, XA_WIDTH), lambda i: (layer, 0, 0)),
            pl.BlockSpec((MEM_LEN, 2 * XA_WIDTH), lambda i: (0, 0)),
            pl.BlockSpec((None, XA_WIDTH, D_MODEL), lambda i: (layer, 0, 0)),
        ],
        out_specs=pl.BlockSpec((XA_TM, D_MODEL), lambda i: (i, 0)),
        out_shape=jax.ShapeDtypeStruct((t, D_MODEL), F32),
        compiler_params=_params(("arbitrary",)),
        name="mem_xattn",
    )(x, g_all, wq_all, kv, wo_all)


RT_TM = 512
_E_LANE0 = MOE_GROUPS


def _router_kernel(x_ref, g_ref, w_ref, b_ref, h_ref, comb_ref):
    h = _rms(x_ref[...], g_ref[...], RMS_EPS).astype(BF16)
    h_ref[...] = h
    logits = jnp.dot(h, w_ref[...], preferred_element_type=F32) + b_ref[...]
    lane = lax.broadcasted_iota(jnp.int32, logits.shape, 1).astype(F32)
    neg = jnp.float32(-jnp.inf)
    big = jnp.float32(LANES)

    def first_argmax(vals):
        top = jnp.max(vals, axis=-1, keepdims=True)
        idx = jnp.min(jnp.where(vals == top, lane, big), axis=-1, keepdims=True)
        return top, idx

    gl = jnp.where(lane < MOE_GROUPS, logits, neg)
    g_top, g_idx = first_argmax(gl)
    g_w = 1.0 / jnp.sum(jnp.exp(gl - g_top), axis=-1, keepdims=True)
    e_lane = lane - _E_LANE0
    in_group = jnp.logical_and(e_lane >= g_idx * MOE_PER_GROUP, e_lane < (g_idx + 1) * MOE_PER_GROUP)
    el = jnp.where(in_group, logits, neg)
    v1, i1 = first_argmax(el)
    el2 = jnp.where(lane == i1, neg, el)
    v2, i2 = first_argmax(el2)
    e2 = jnp.exp(v2 - v1)
    w1 = 1.0 / (1.0 + e2)
    w2 = e2 / (1.0 + e2)
    comb_ref[...] = (jnp.where(lane == i1, w1 * g_w, 0.0) + jnp.where(lane == i2, w2 * g_w, 0.0))


def _router(x, g_all, w_r, b_r, layer):
    t = x.shape[0]
    return pl.pallas_call(
        _router_kernel,
        grid=(t // RT_TM,),
        in_specs=[
            pl.BlockSpec((RT_TM, D_MODEL), lambda i: (i, 0)),
            _vec_spec(D_MODEL, layer, 1),
            pl.BlockSpec((None, D_MODEL, LANES), lambda i: (layer, 0, 0)),
            _vec_spec(LANES, layer, 1),
        ],
        out_specs=[
            pl.BlockSpec((RT_TM, D_MODEL), lambda i: (i, 0)),
            pl.BlockSpec((RT_TM, LANES), lambda i: (i, 0)),
        ],
        out_shape=[jax.ShapeDtypeStruct((t, D_MODEL), BF16), jax.ShapeDtypeStruct((t, LANES), F32)],
        compiler_params=_params(("arbitrary",)),
        name="moe_router",
    )(x, g_all, w_r, b_r)


EX_TM = 512


def _experts_kernel(h_ref, comb_ref, x_ref, wg_ref, wu_ref, wd_ref, o_ref):
    e = pl.program_id(1)

    @pl.when(e == 0)
    def _():
        o_ref[...] = x_ref[...]

    h = h_ref[...]
    gate = jnp.dot(h, wg_ref[...], preferred_element_type=F32)
    up = jnp.dot(h, wu_ref[...], preferred_element_type=F32)
    act = (gate * jax.nn.sigmoid(gate) * up).astype(BF16)
    lane = lax.broadcasted_iota(jnp.int32, comb_ref.shape, 1)
    w = jnp.sum(jnp.where(lane == e + _E_LANE0, comb_ref[...], 0.0), axis=-1, keepdims=True)
    o_ref[...] += w * jnp.dot(act, wd_ref[...], preferred_element_type=F32)


def _experts(h, comb, x, wg_all, wu_all, wd_all, layer):
    t = h.shape[0]
    return pl.pallas_call(
        _experts_kernel,
        grid=(t // EX_TM, MOE_EXPERTS),
        in_specs=[
            pl.BlockSpec((EX_TM, D_MODEL), lambda i, e: (i, 0)),
            pl.BlockSpec((EX_TM, LANES), lambda i, e: (i, 0)),
            pl.BlockSpec((EX_TM, D_MODEL), lambda i, e: (i, 0)),
            pl.BlockSpec((None, None, D_MODEL, MOE_FF), lambda i, e: (layer, e, 0, 0)),
            pl.BlockSpec((None, None, D_MODEL, MOE_FF), lambda i, e: (layer, e, 0, 0)),
            pl.BlockSpec((None, None, MOE_FF, D_MODEL), lambda i, e: (layer, e, 0, 0)),
        ],
        out_specs=pl.BlockSpec((EX_TM, D_MODEL), lambda i, e: (i, 0)),
        out_shape=jax.ShapeDtypeStruct((t, D_MODEL), F32),
        compiler_params=_params(("arbitrary", "arbitrary")),
        name="moe_experts",
    )(h, comb, x, wg_all, wu_all, wd_all)


FN_TM = 512


def _final_norm_kernel(x_ref, g_ref, o_ref):
    o_ref[...] = _rms(x_ref[...], g_ref[...], RMS_EPS)


def _final_norm(x, g):
    t = x.shape[0]
    return pl.pallas_call(
        _final_norm_kernel,
        grid=(t // FN_TM,),
        in_specs=[pl.BlockSpec((FN_TM, D_MODEL), lambda i: (i, 0)),
                  pl.BlockSpec((1, D_MODEL), lambda i: (0, 0))],
        out_specs=pl.BlockSpec((FN_TM, D_MODEL), lambda i: (i, 0)),
        out_shape=jax.ShapeDtypeStruct((t, D_MODEL), F32),
        compiler_params=_params(("arbitrary",)),
        name="final_norm",
    )(x, g)


def kernel(x, mem, positions, norm_mix, w_in, da_lam_q1, da_lam_k1, da_lam_q2, da_lam_k2, da_head_norm, w_da_out, cv_dw_w, cv_dw_b, cv_ln_g, cv_ln_b, w_cv_out, ssm_lam_re, ssm_lam_im, ssm_log_dt, ssm_b_re, ssm_b_im, ssm_c_re, ssm_c_im, ssm_d, w_ssm_glu, b_ssm_glu, w_ssm_out, w_mix_out, norm_xa, norm_mem, w_xa_q, w_xa_kv, w_xa_out, norm_ffn, w_router_group, b_router_group, w_router_expert, b_router_expert, w_exp_gate, w_exp_up, w_exp_down, norm_final):
    bsz, seq, _ = x.shape
    assert bsz == 1 and seq == SEQ
    nl = w_in.shape[0]
    xs = x.reshape(seq, D_MODEL).astype(F32)
    mem2 = mem.reshape(MEM_LEN, D_MODEL).astype(F32)

    inv_freq = ROPE_THETA ** (-jnp.arange(0, DA_HEAD_DIM, 2, dtype=F32) / DA_HEAD_DIM)
    ang = positions.reshape(seq).astype(F32)[:, None] * inv_freq
    cos = jnp.cos(ang)
    sin = jnp.sin(ang)
    cos_t = jnp.concatenate([cos, cos, cos, cos], axis=-1)
    sin_t = jnp.concatenate([-sin, sin, -sin, sin], axis=-1)

    w_in_b = w_in.astype(BF16)
    w_da_b = w_da_out.astype(BF16)
    w_cv_b = w_cv_out.astype(BF16)
    w_glu_b = w_ssm_glu.astype(BF16)
    w_so_b = w_ssm_out.astype(BF16)
    w_mix_b = w_mix_out.astype(BF16)
    w_xq_b = w_xa_q.astype(BF16)
    w_xkv_b = w_xa_kv.astype(BF16)
    w_xo_b = w_xa_out.astype(BF16)
    w_eg_b = w_exp_gate.astype(BF16)
    w_eu_b = w_exp_up.astype(BF16)
    w_ed_b = w_exp_down.astype(BF16)
    pad = LANES - MOE_GROUPS - MOE_EXPERTS
    w_r = jnp.concatenate([w_router_group, w_router_expert,
                           jnp.zeros((nl, D_MODEL, pad), F32)], axis=-1).astype(BF16)
    b_r = jnp.concatenate([b_router_group, b_router_expert, jnp.zeros((nl, pad), F32)], axis=-1).astype(F32)

    def vec3(a):
        return a.astype(F32).reshape(nl, 1, a.shape[-1])

    norm_mix, da_head_norm, cv_dw_b, cv_ln_g, cv_ln_b, b_ssm_glu, norm_xa, norm_mem, norm_ffn, b_r = map(
        vec3, (norm_mix, da_head_norm, cv_dw_b, cv_ln_g, cv_ln_b, b_ssm_glu, norm_xa, norm_mem, norm_ffn, b_r))
    cv_dw_w = cv_dw_w.astype(F32)

    lam_inits = jnp.asarray([0.8 - 0.6 * math.exp(-0.3 * l) for l in range(nl)], F32)
    lam_pack = jnp.stack([da_lam_q1, da_lam_k1, da_lam_q2, da_lam_k2], axis=1).astype(F32)
    lam_pack = jnp.concatenate(
        [lam_pack, jnp.broadcast_to(lam_inits[:, None, None], (nl, 4, DA_HEAD_DIM))], axis=1)

    ssm_mats = _ssm_matrices(ssm_lam_re, ssm_lam_im, ssm_log_dt, ssm_b_re, ssm_b_im,
                             ssm_c_re, ssm_c_im, ssm_d)

    for l in range(nl):
        proj = _inproj(xs, norm_mix, w_in_b, l, cos_t, sin_t)
        o_a = _attention(proj, lam_pack, da_head_norm, l)
        z_b = _conv(proj, cv_dw_w, cv_dw_b, cv_ln_g, cv_ln_b, l)
        u_pairs = _to_pairs(proj[:, COL_SSM:COL_SSM + SSM_WIDTH])
        y_c = _from_pairs(_ssm(u_pairs, [m[l] for m in ssm_mats]))
        xs = _merge_mix(o_a, z_b, y_c, proj, w_da_b, w_cv_b, w_glu_b, b_ssm_glu, w_so_b, w_mix_b, xs, l)
        kv = _mem_kv(mem2, norm_mem, w_xkv_b, l)
        xs = _xattn(xs, norm_xa, w_xq_b, kv, w_xo_b, l)
        h, comb = _router(xs, norm_ffn, w_r, b_r, l)
        xs = _experts(h, comb, xs, w_eg_b, w_eu_b, w_ed_b, l)
    out = _final_norm(xs, norm_final.reshape(1, D_MODEL))
    return out.reshape(bsz, seq, D_MODEL)
```

```python
import functools
import math

import jax
import jax.numpy as jnp
from jax import lax
from jax.experimental import pallas as pl
from jax.experimental.pallas import tpu as pltpu

F32 = jnp.float32
BF16 = jnp.bfloat16

D_MODEL = 2048
SEQ = 8192
DEPTH = 4
MEM_LEN = 256
DA_HEADS = 8
DA_HEAD_DIM = 64
DA_V_DIM = 128
DA_WIDTH = 1024
ROPE_THETA = 10000.0
CV_WIDTH = 512
CONV_TAPS = 31
SSM_WIDTH = 512
SSM_GROUP = 16
SSM_GROUPS = 32
SSM_STATE = 64
XA_HEADS = 4
XA_HEAD_DIM = 128
XA_WIDTH = 512
MOE_GROUPS = 4
MOE_PER_GROUP = 4
MOE_EXPERTS = 16
MOE_FF = 512
RMS_EPS = 1e-6
HEAD_NORM_EPS = 1e-5
LN_EPS = 1e-5

COL_Q = 0
COL_K = COL_Q + 1024
COL_V = COL_K + 1024
COL_CVA = COL_V + 1024
COL_CVB = COL_CVA + CV_WIDTH
COL_SSM = COL_CVB + CV_WIDTH
COL_GA = COL_SSM + SSM_WIDTH
COL_GB = COL_GA + D_MODEL
COL_GC = COL_GB + D_MODEL
IN_TOTAL = COL_GC + D_MODEL

LANES = 128
VMEM_LIMIT = 56 * 1024 * 1024

SSM_CHUNK = 16
SSM_NCHUNK = SEQ // SSM_CHUNK
SSM_PAIRS = SSM_GROUPS // 2
SSM_PAIR_W = 2 * SSM_CHUNK * SSM_GROUP


def _params(sem, vmem=VMEM_LIMIT):
    return pltpu.CompilerParams(dimension_semantics=sem, vmem_limit_bytes=vmem)


def _vec_spec(width, layer, ngrid):
    if ngrid == 1:
        return pl.BlockSpec((None, 1, width), lambda i: (layer, 0, 0))
    return pl.BlockSpec((None, 1, width), lambda i, j: (layer, 0, 0))


def _rms(xf, g, eps):
    ms = jnp.mean(xf * xf, axis=-1, keepdims=True)
    return xf * lax.rsqrt(ms + eps) * g


INP_TM = 1024
INP_TN = 512
_Q_TILE0 = COL_Q // INP_TN
_K_TILE0 = COL_K // INP_TN
_V_TILE0 = COL_V // INP_TN


def _inproj_kernel(x_ref, g_ref, w_ref, cos_ref, sin_ref, o_ref, h_ref):
    j = pl.program_id(1)

    @pl.when(j == 0)
    def _():
        h_ref[...] = _rms(x_ref[...], g_ref[...], RMS_EPS).astype(BF16)

    acc = jnp.dot(h_ref[...], w_ref[...], preferred_element_type=F32)
    is_rope = jnp.logical_and(j >= _Q_TILE0, j < _V_TILE0)

    @pl.when(is_rope)
    def _():
        scale = jnp.where(j < _K_TILE0, math.log2(math.e) * DA_HEAD_DIM ** -0.5, 1.0).astype(F32)
        cos = cos_ref[...] * scale
        sin = sin_ref[...] * scale
        lane = lax.broadcasted_iota(jnp.int32, (INP_TM, LANES), 1)
        first_half = (lane % DA_HEAD_DIM) < (DA_HEAD_DIM // 2)
        for c in range(INP_TN // LANES):
            t = acc[:, c * LANES:(c + 1) * LANES]
            swapped = jnp.where(first_half, pltpu.roll(t, LANES - 32, 1), pltpu.roll(t, 32, 1))
            o_ref[:, c * LANES:(c + 1) * LANES] = (t * cos + swapped * sin).astype(BF16)

    @pl.when(jnp.logical_not(is_rope))
    def _():
        o_ref[...] = acc.astype(BF16)


def _inproj(x, g, w_all, layer, cos_t, sin_t):
    t = x.shape[0]
    return pl.pallas_call(
        _inproj_kernel,
        grid=(t // INP_TM, IN_TOTAL // INP_TN),
        in_specs=[
            pl.BlockSpec((INP_TM, D_MODEL), lambda i, j: (i, 0)),
            _vec_spec(D_MODEL, layer, 2),
            pl.BlockSpec((None, D_MODEL, INP_TN), lambda i, j: (layer, 0, j)),
            pl.BlockSpec((INP_TM, LANES), lambda i, j: (i, 0)),
            pl.BlockSpec((INP_TM, LANES), lambda i, j: (i, 0)),
        ],
        out_specs=pl.BlockSpec((INP_TM, INP_TN), lambda i, j: (i, j)),
        out_shape=jax.ShapeDtypeStruct((t, IN_TOTAL), BF16),
        scratch_shapes=[pltpu.VMEM((INP_TM, D_MODEL), BF16)],
        compiler_params=_params(("arbitrary", "arbitrary")),
        name="inproj",
    )(x, g, w_all, cos_t, sin_t)


ATT_TQ = 1024
ATT_TK = 1024


ATT_RG = 1024


def _attn_kernel(lam_ref, g_ref, q_ref, k_ref, v_ref, o_ref, qs_ref, m_ref, l_ref, acc_ref):
    i = pl.program_id(1)
    tq = ATT_TQ
    q = q_ref[...]
    lane = lax.broadcasted_iota(jnp.int32, q.shape, 1)
    zero = jnp.zeros_like(q)
    qs_ref[0:tq, :] = jnp.where(lane < DA_HEAD_DIM, q, zero)
    qs_ref[tq:2 * tq, :] = jnp.where(lane >= DA_HEAD_DIM, q, zero)
    m_ref[...] = jnp.full(m_ref.shape, -jnp.inf, F32)
    l_ref[...] = jnp.zeros(l_ref.shape, F32)
    acc_ref[...] = jnp.zeros(acc_ref.shape, F32)

    def update_rows(r0, k, v, mask):
        n = k.shape[0]
        rows = slice(r0, r0 + ATT_RG)
        s = lax.dot_general(qs_ref[rows, :], k, (((1,), (1,)), ((), ())), preferred_element_type=F32)
        if mask is not None:
            s = jnp.where(mask, s, -jnp.inf)
        tiles = [s[:, t * LANES:(t + 1) * LANES] for t in range(n // LANES)]
        mc = functools.reduce(jnp.maximum, tiles)
        m_old = m_ref[rows, :]
        m_new = jnp.maximum(m_old, jnp.max(mc, axis=1, keepdims=True))
        alpha = jnp.exp2(m_old - m_new)
        p_tiles = [jnp.exp2(t - m_new) for t in tiles]
        l_ref[rows, :] = alpha * l_ref[rows, :] + functools.reduce(jnp.add, p_tiles)
        p = jnp.concatenate(p_tiles, axis=1).astype(BF16)
        acc_ref[rows, :] = alpha * acc_ref[rows, :] + jnp.dot(p, v, preferred_element_type=F32)
        m_ref[rows, :] = m_new

    def body(c, carry):
        off = pl.multiple_of(c * ATT_TK, ATT_TK)
        k = k_ref[pl.ds(off, ATT_TK), :]
        v = v_ref[pl.ds(off, ATT_TK), :]
        for r0 in range(0, 2 * tq, ATT_RG):
            update_rows(r0, k, v, None)
        return carry

    lax.fori_loop(0, i * (tq // ATT_TK), body, 0)

    off = pl.multiple_of(i * tq, tq)
    for r0 in range(0, 2 * tq, ATT_RG):
        qo = r0 % tq
        n = qo + ATT_RG
        row = lax.broadcasted_iota(jnp.int32, (ATT_RG, n), 0)
        col = lax.broadcasted_iota(jnp.int32, (ATT_RG, n), 1)
        update_rows(r0, k_ref[pl.ds(off, n), :], v_ref[pl.ds(off, n), :], col <= row + qo)

    lam_init = lam_ref[4:5, 0:1]
    lam = (jnp.exp(jnp.sum(lam_ref[0:1, :] * lam_ref[1:2, :], axis=1, keepdims=True))
           - jnp.exp(jnp.sum(lam_ref[2:3, :] * lam_ref[3:4, :], axis=1, keepdims=True))
           + lam_init)
    acc = acc_ref[...]
    inv_l = 1.0 / jnp.sum(l_ref[...], axis=1, keepdims=True)
    o = acc[0:tq] * inv_l[0:tq] - lam * (acc[tq:2 * tq] * inv_l[tq:2 * tq])
    o = _rms(o, g_ref[...], HEAD_NORM_EPS) * (1.0 - lam_init)
    o_ref[...] = o.astype(BF16)


def _attention(proj, lam_pack, head_g, layer):
    t = proj.shape[0]
    qb, kb, vb = COL_Q // LANES, COL_K // LANES, COL_V // LANES
    return pl.pallas_call(
        _attn_kernel,
        grid=(DA_HEADS, t // ATT_TQ),
        in_specs=[
            pl.BlockSpec((None, 8, DA_HEAD_DIM), lambda h, i: (layer, 0, 0)),
            _vec_spec(DA_V_DIM, layer, 2),
            pl.BlockSpec((ATT_TQ, LANES), lambda h, i: (i, qb + h)),
            pl.BlockSpec((t, LANES), lambda h, i: (0, kb + h)),
            pl.BlockSpec((t, LANES), lambda h, i: (0, vb + h)),
        ],
        out_specs=pl.BlockSpec((ATT_TQ, LANES), lambda h, i: (i, h)),
        out_shape=jax.ShapeDtypeStruct((t, DA_WIDTH), BF16),
        scratch_shapes=[
            pltpu.VMEM((2 * ATT_TQ, LANES), BF16),
            pltpu.VMEM((2 * ATT_TQ, LANES), F32),
            pltpu.VMEM((2 * ATT_TQ, LANES), F32),
            pltpu.VMEM((2 * ATT_TQ, LANES), F32),
        ],
        compiler_params=_params(("arbitrary", "arbitrary")),
        name="diff_attn",
    )(lam_pack, head_g, proj, proj, proj)


CV_TM = 512
CV_HALO = 32
CV_ROWS = 64


def _conv_kernel(a_ref, b_ref, w_ref, bias_ref, g_ref, beta_ref, o_ref, z_ref):
    i = pl.program_id(0)

    @pl.when(i == 0)
    def _():
        z_ref[0:CV_HALO, :] = jnp.zeros((CV_HALO, CV_WIDTH), F32)

    @pl.when(i > 0)
    def _():
        z_ref[0:CV_HALO, :] = z_ref[CV_TM:CV_TM + CV_HALO, :]

    a = a_ref[...].astype(F32)
    b = b_ref[...].astype(F32)
    z_ref[CV_HALO:CV_HALO + CV_TM, :] = a * jax.nn.sigmoid(b)

    base = CV_HALO - (CONV_TAPS - 1)
    for r in range(0, CV_TM, CV_ROWS):
        acc = jnp.zeros((CV_ROWS, CV_WIDTH), F32) + bias_ref[...]
        for j in range(CONV_TAPS):
            acc = acc + w_ref[j:j + 1, :] * z_ref[base + r + j:base + r + j + CV_ROWS, :]
        mu = jnp.mean(acc, axis=-1, keepdims=True)
        xc = acc - mu
        var = jnp.mean(xc * xc, axis=-1, keepdims=True)
        y = xc * lax.rsqrt(var + LN_EPS) * g_ref[...] + beta_ref[...]
        o_ref[r:r + CV_ROWS, :] = (y * jax.nn.sigmoid(y)).astype(BF16)


def _conv(proj, dw_w, dw_b, ln_g, ln_b, layer):
    t = proj.shape[0]
    ab, bb = COL_CVA // CV_WIDTH, COL_CVB // CV_WIDTH
    vec = _vec_spec(CV_WIDTH, layer, 1)
    return pl.pallas_call(
        _conv_kernel,
        grid=(t // CV_TM,),
        in_specs=[
            pl.BlockSpec((CV_TM, CV_WIDTH), lambda i: (i, ab)),
            pl.BlockSpec((CV_TM, CV_WIDTH), lambda i: (i, bb)),
            pl.BlockSpec((None, CONV_TAPS, CV_WIDTH), lambda i: (layer, 0, 0)),
            vec, vec, vec,
        ],
        out_specs=pl.BlockSpec((CV_TM, CV_WIDTH), lambda i: (i, 0)),
        out_shape=jax.ShapeDtypeStruct((t, CV_WIDTH), BF16),
        scratch_shapes=[pltpu.VMEM((CV_HALO + CV_TM, CV_WIDTH), F32)],
        compiler_params=_params(("arbitrary",)),
        name="conformer_conv",
    )(proj, proj, dw_w, dw_b, ln_g, ln_b)


def _ssm_kernel(u_ref, t_ref, wre_ref, wim_ref, vre_ref, vim_ref, ar_ref, ai_ref, d_ref,
                y_ref, sre_ref, sim_ref):
    npair, nchunk = SSM_PAIRS, SSM_NCHUNK
    for p in range(npair):
        u = u_ref[p]
        sre_ref[pl.ds(p, nchunk, stride=npair), :] = jnp.dot(u, wre_ref[p], preferred_element_type=F32)
        sim_ref[pl.ds(p, nchunk, stride=npair), :] = jnp.dot(u, wim_ref[p], preferred_element_type=F32)

    ar = ar_ref[...]
    ai = ai_ref[...]

    def body(c, carry):
        xr, xi = carry
        off = pl.multiple_of(c * npair, npair)
        sr = sre_ref[pl.ds(off, npair), :]
        si = sim_ref[pl.ds(off, npair), :]
        sre_ref[pl.ds(off, npair), :] = xr
        sim_ref[pl.ds(off, npair), :] = xi
        return ar * xr - ai * xi + sr, ar * xi + ai * xr + si

    zero = jnp.zeros((npair, LANES), F32)
    lax.fori_loop(0, nchunk, body, (zero, zero), unroll=8)

    for p in range(npair):
        u = u_ref[p]
        xr = sre_ref[pl.ds(p, nchunk, stride=npair), :].astype(BF16)
        xi = sim_ref[pl.ds(p, nchunk, stride=npair), :].astype(BF16)
        y = jnp.dot(u, t_ref[p], preferred_element_type=F32)
        y = y + jnp.dot(xr, vre_ref[p], preferred_element_type=F32)
        y = y + jnp.dot(xi, vim_ref[p], preferred_element_type=F32)
        y = y + u.astype(F32) * d_ref[p:p + 1, :]
        y_ref[p] = y.astype(BF16)


def _ssm(u_pairs, mats):
    vm = pl.BlockSpec(memory_space=pltpu.VMEM)
    return pl.pallas_call(
        _ssm_kernel,
        in_specs=[vm] * 9,
        out_specs=vm,
        out_shape=jax.ShapeDtypeStruct((SSM_PAIRS, SSM_NCHUNK, SSM_PAIR_W), BF16),
        scratch_shapes=[pltpu.VMEM((SSM_NCHUNK * SSM_PAIRS, LANES), F32),
                        pltpu.VMEM((SSM_NCHUNK * SSM_PAIRS, LANES), F32)],
        compiler_params=pltpu.CompilerParams(vmem_limit_bytes=VMEM_LIMIT),
        name="s5_scan",
    )(u_pairs, *mats)


def _blockdiag2(m):
    l, _, r, c = m.shape
    m = m.reshape(l, SSM_PAIRS, 2, r, c)
    z = jnp.zeros_like(m[:, :, 0])
    top = jnp.concatenate([m[:, :, 0], z], axis=-1)
    bot = jnp.concatenate([z, m[:, :, 1]], axis=-1)
    return jnp.concatenate([top, bot], axis=-2)


def _ssm_matrices(lam_re, lam_im, log_dt, b_re, b_im, c_re, c_im, d_skip):
    hp = lax.Precision.HIGHEST
    lr = lam_re.astype(F32)
    li = lam_im.astype(F32)
    dt = jnp.exp(log_dt.astype(F32))[..., None]
    mag = jnp.exp(lr * dt)
    ab_re = mag * jnp.cos(li * dt)
    ab_im = mag * jnp.sin(li * dt)
    den = lr * lr + li * li
    f_re = ((ab_re - 1.0) * lr + ab_im * li) / den
    f_im = (ab_im * lr - (ab_re - 1.0) * li) / den
    br = b_re.astype(F32)
    bi = b_im.astype(F32)
    bb_re = f_re[..., None] * br - f_im[..., None] * bi
    bb_im = f_re[..., None] * bi + f_im[..., None] * br
    tau = jnp.arange(SSM_CHUNK + 1, dtype=F32)[:, None, None, None]
    pmag = jnp.exp(tau * (lr * dt)[None])
    pw_re = pmag * jnp.cos(tau * (li * dt)[None])
    pw_im = pmag * jnp.sin(tau * (li * dt)[None])
    cr = c_re.astype(F32)
    ci = c_im.astype(F32)
    ca_re = cr[None] * pw_re[:, :, :, None, :] - ci[None] * pw_im[:, :, :, None, :]
    ca_im = cr[None] * pw_im[:, :, :, None, :] + ci[None] * pw_re[:, :, :, None, :]
    kk = (jnp.einsum('tlghp,lgpk->lgthk', ca_re[:SSM_CHUNK], bb_re, precision=hp)
          - jnp.einsum('tlghp,lgpk->lgthk', ca_im[:SSM_CHUNK], bb_im, precision=hp))
    s_idx = jnp.arange(SSM_CHUNK)[:, None]
    t_idx = jnp.arange(SSM_CHUNK)[None, :]
    lag = t_idx - s_idx
    tm = kk[:, :, jnp.clip(lag, 0, SSM_CHUNK - 1)]
    tm = jnp.where((lag >= 0)[None, None, :, :, None, None], tm, 0.0)
    nl = lr.shape[0]
    tm = tm.transpose(0, 1, 2, 5, 3, 4).reshape(nl, SSM_GROUPS, 256, 256)
    rev_re = pw_re[:SSM_CHUNK][::-1]
    rev_im = pw_im[:SSM_CHUNK][::-1]
    w_re = rev_re[..., None] * bb_re[None] - rev_im[..., None] * bb_im[None]
    w_im = rev_re[..., None] * bb_im[None] + rev_im[..., None] * bb_re[None]
    w_re = w_re.transpose(1, 2, 0, 4, 3).reshape(nl, SSM_GROUPS, 256, SSM_STATE)
    w_im = w_im.transpose(1, 2, 0, 4, 3).reshape(nl, SSM_GROUPS, 256, SSM_STATE)
    v_re = ca_re[1:].transpose(1, 2, 4, 0, 3).reshape(nl, SSM_GROUPS, SSM_STATE, 256)
    v_im = (-ca_im[1:]).transpose(1, 2, 4, 0, 3).reshape(nl, SSM_GROUPS, SSM_STATE, 256)
    a_re = pw_re[SSM_CHUNK].reshape(nl, SSM_PAIRS, LANES)
    a_im = pw_im[SSM_CHUNK].reshape(nl, SSM_PAIRS, LANES)
    d_t = jnp.broadcast_to(d_skip.astype(F32)[:, :, None, :], (nl, SSM_GROUPS, SSM_CHUNK, SSM_GROUP))
    d_t = d_t.reshape(nl, SSM_PAIRS, SSM_PAIR_W)
    return (_blockdiag2(tm).astype(BF16), _blockdiag2(w_re).astype(BF16), _blockdiag2(w_im).astype(BF16),
            _blockdiag2(v_re).astype(BF16), _blockdiag2(v_im).astype(BF16), a_re, a_im, d_t)


def _to_pairs(u):
    t = u.shape[0]
    u = u.reshape(t // SSM_CHUNK, SSM_CHUNK, SSM_PAIRS, 2, SSM_GROUP)
    return u.transpose(2, 0, 3, 1, 4).reshape(SSM_PAIRS, t // SSM_CHUNK, SSM_PAIR_W)


def _from_pairs(y):
    nchunk = y.shape[1]
    y = y.reshape(SSM_PAIRS, nchunk, 2, SSM_CHUNK, SSM_GROUP)
    return y.transpose(1, 3, 0, 2, 4).reshape(nchunk * SSM_CHUNK, SSM_WIDTH)


MG_TM = 512
MG_TN = 512


def _merge_mix_kernel(oa_ref, zb_ref, yc_ref, ga_ref, gb_ref, gc_ref, wda_ref, wcv_ref, wglu_ref,
                      bglu_ref, wso_ref, wmix_ref, x_ref, o_ref, sc_ref):
    n = pl.program_id(1)

    @pl.when(n == 0)
    def _():
        glu = jnp.dot(yc_ref[...], wglu_ref[...], preferred_element_type=F32) + bglu_ref[...]
        sc_ref[...] = (glu[:, :SSM_WIDTH] * jax.nn.sigmoid(glu[:, SSM_WIDTH:])).astype(BF16)
        o_ref[...] = x_ref[...]

    y_a = jnp.dot(oa_ref[...], wda_ref[...], preferred_element_type=F32)
    y_b = jnp.dot(zb_ref[...], wcv_ref[...], preferred_element_type=F32)
    y_c = jnp.dot(sc_ref[...], wso_ref[...], preferred_element_type=F32)
    merged = (jax.nn.sigmoid(ga_ref[...].astype(F32)) * y_a
              + jax.nn.sigmoid(gb_ref[...].astype(F32)) * y_b
              + jax.nn.sigmoid(gc_ref[...].astype(F32)) * y_c)
    o_ref[...] += jnp.dot(merged.astype(BF16), wmix_ref[...], preferred_element_type=F32)


def _merge_mix(o_a, z_b, y_c, proj, w_da, w_cv, w_glu, b_glu, w_so, w_mix, x, layer):
    t = o_a.shape[0]
    ga, gb, gc = COL_GA // MG_TN, COL_GB // MG_TN, COL_GC // MG_TN

    def wcol(k):
        return pl.BlockSpec((None, k, MG_TN), lambda i, n: (layer, 0, n))

    return pl.pallas_call(
        _merge_mix_kernel,
        grid=(t // MG_TM, D_MODEL // MG_TN),
        in_specs=[
            pl.BlockSpec((MG_TM, DA_WIDTH), lambda i, n: (i, 0)),
            pl.BlockSpec((MG_TM, CV_WIDTH), lambda i, n: (i, 0)),
            pl.BlockSpec((MG_TM, SSM_WIDTH), lambda i, n: (i, 0)),
            pl.BlockSpec((MG_TM, MG_TN), lambda i, n: (i, ga + n)),
            pl.BlockSpec((MG_TM, MG_TN), lambda i, n: (i, gb + n)),
            pl.BlockSpec((MG_TM, MG_TN), lambda i, n: (i, gc + n)),
            wcol(DA_WIDTH), wcol(CV_WIDTH),
            pl.BlockSpec((None, SSM_WIDTH, 2 * SSM_WIDTH), lambda i, n: (layer, 0, 0)),
            _vec_spec(2 * SSM_WIDTH, layer, 2),
            wcol(SSM_WIDTH),
            pl.BlockSpec((None, MG_TN, D_MODEL), lambda i, n: (layer, n, 0)),
            pl.BlockSpec((MG_TM, D_MODEL), lambda i, n: (i, 0)),
        ],
        out_specs=pl.BlockSpec((MG_TM, D_MODEL), lambda i, n: (i, 0)),
        out_shape=jax.ShapeDtypeStruct((t, D_MODEL), F32),
        scratch_shapes=[pltpu.VMEM((MG_TM, SSM_WIDTH), BF16)],
        compiler_params=_params(("arbitrary", "arbitrary")),
        name="merge_mix",
    )(o_a, z_b, y_c, proj, proj, proj, w_da, w_cv, w_glu, b_glu, w_so, w_mix, x)


def _norm_matmul_kernel(x_ref, g_ref, w_ref, o_ref):
    h = _rms(x_ref[...], g_ref[...], RMS_EPS).astype(BF16)
    o_ref[...] = jnp.dot(h, w_ref[...], preferred_element_type=F32).astype(o_ref.dtype)


def _mem_kv(mem, g_all, w_all, layer):
    m = mem.shape[0]
    n = 2 * XA_WIDTH
    return pl.pallas_call(
        _norm_matmul_kernel,
        grid=(1,),
        in_specs=[
            pl.BlockSpec((m, D_MODEL), lambda i: (0, 0)),
            _vec_spec(D_MODEL, layer, 1),
            pl.BlockSpec((None, D_MODEL, n), lambda i: (layer, 0, 0)),
        ],
        out_specs=pl.BlockSpec((m, n), lambda i: (0, 0)),
        out_shape=jax.ShapeDtypeStruct((m, n), BF16),
        compiler_params=_params(("arbitrary",)),
        name="mem_kv",
    )(mem, g_all, w_all)


XA_TM = 512


def _xattn_kernel(x_ref, g_ref, wq_ref, kv_ref, wo_ref, o_ref):
    x = x_ref[...]
    h = _rms(x, g_ref[...], RMS_EPS).astype(BF16)
    q = jnp.dot(h, wq_ref[...], preferred_element_type=F32).astype(BF16)
    heads = []
    for hd in range(XA_HEADS):
        lo = hd * XA_HEAD_DIM
        k = kv_ref[:, lo:lo + XA_HEAD_DIM]
        v = kv_ref[:, XA_WIDTH + lo:XA_WIDTH + lo + XA_HEAD_DIM]
        s = lax.dot_general(q[:, lo:lo + XA_HEAD_DIM], k, (((1,), (1,)), ((), ())),
                            preferred_element_type=F32) * (XA_HEAD_DIM ** -0.5)
        m = jnp.max(s, axis=-1, keepdims=True)
        e = jnp.exp(s - m)
        p = e / jnp.sum(e, axis=-1, keepdims=True)
        heads.append(jnp.dot(p.astype(BF16), v, preferred_element_type=F32).astype(BF16))
    o = jnp.concatenate(heads, axis=-1)
    o_ref[...] = x + jnp.dot(o, wo_ref[...], preferred_element_type=F32)


def _xattn(x, g_all, wq_all, kv, wo_all, layer):
    t = x.shape[0]
    return pl.pallas_call(
        _xattn_kernel,
        grid=(t // XA_TM,),
        in_specs=[
            pl.BlockSpec((XA_TM, D_MODEL), lambda i: (i, 0)),
            _vec_spec(D_MODEL, layer, 1),
            pl.BlockSpec((None, D_MODEL, XA_WIDTH), lambda i: (layer, 0, 0)),
            pl.BlockSpec((MEM_LEN, 2 * XA_WIDTH), lambda i: (0, 0)),
            pl.BlockSpec((None, XA_WIDTH, D_MODEL), lambda i: (layer, 0, 0)),
        ],
        out_specs=pl.BlockSpec((XA_TM, D_MODEL), lambda i: (i, 0)),
        out_shape=jax.ShapeDtypeStruct((t, D_MODEL), F32),
        compiler_params=_params(("arbitrary",)),
        name="mem_xattn",
    )(x, g_all, wq_all, kv, wo_all)


RT_TM = 512
_E_LANE0 = MOE_GROUPS


def _router_kernel(x_ref, g_ref, w_ref, b_ref, meta_ref, cnt_ref, run_ref):
    i = pl.program_id(0)

    @pl.when(i == 0)
    def _():
        run_ref[...] = jnp.zeros(run_ref.shape, F32)

    h = _rms(x_ref[...], g_ref[...], RMS_EPS).astype(BF16)
    logits = jnp.dot(h, w_ref[...], preferred_element_type=F32) + b_ref[...]
    lane = lax.broadcasted_iota(jnp.int32, logits.shape, 1).astype(F32)
    neg = jnp.float32(-jnp.inf)
    big = jnp.float32(LANES)

    def first_argmax(vals):
        top = jnp.max(vals, axis=-1, keepdims=True)
        idx = jnp.min(jnp.where(vals == top, lane, big), axis=-1, keepdims=True)
        return top, idx

    gl = jnp.where(lane < MOE_GROUPS, logits, neg)
    g_top, g_idx = first_argmax(gl)
    g_w = 1.0 / jnp.sum(jnp.exp(gl - g_top), axis=-1, keepdims=True)
    e_lane = lane - _E_LANE0
    in_group = jnp.logical_and(e_lane >= g_idx * MOE_PER_GROUP, e_lane < (g_idx + 1) * MOE_PER_GROUP)
    el = jnp.where(in_group, logits, neg)
    v1, i1 = first_argmax(el)
    el2 = jnp.where(lane == i1, neg, el)
    v2, i2 = first_argmax(el2)
    e2 = jnp.exp(v2 - v1)
    w1 = 1.0 / (1.0 + e2)
    w2 = e2 / (1.0 + e2)
    oh1 = (lane == i1).astype(F32)
    oh2 = (lane == i2).astype(F32)
    both = oh1 + oh2
    row = lax.broadcasted_iota(jnp.int32, (RT_TM, RT_TM), 0)
    col = lax.broadcasted_iota(jnp.int32, (RT_TM, RT_TM), 1)
    earlier = jnp.where(col < row, 1.0, 0.0).astype(BF16)
    before = jnp.dot(earlier, both.astype(BF16), preferred_element_type=F32) + run_ref[...]
    rank1 = jnp.sum(oh1 * before, axis=-1, keepdims=True)
    rank2 = jnp.sum(oh2 * before, axis=-1, keepdims=True)
    run_ref[...] += jnp.sum(both, axis=0, keepdims=True)
    cnt_ref[...] = run_ref[...]
    meta = jnp.where(lane == 0.0, i1 - _E_LANE0, 0.0)
    for k, val in enumerate((i2 - _E_LANE0, rank1, rank2, w1 * g_w, w2 * g_w), start=1):
        meta = jnp.where(lane == float(k), val, meta)
    meta_ref[...] = meta


def _router(x, g_all, w_r, b_r, layer):
    t = x.shape[0]
    return pl.pallas_call(
        _router_kernel,
        grid=(t // RT_TM,),
        in_specs=[
            pl.BlockSpec((RT_TM, D_MODEL), lambda i: (i, 0)),
            _vec_spec(D_MODEL, layer, 1),
            pl.BlockSpec((None, D_MODEL, LANES), lambda i: (layer, 0, 0)),
            _vec_spec(LANES, layer, 1),
        ],
        out_specs=[
            pl.BlockSpec((RT_TM, LANES), lambda i: (i, 0)),
            pl.BlockSpec((1, LANES), lambda i: (0, 0)),
        ],
        out_shape=[jax.ShapeDtypeStruct((t, LANES), F32), jax.ShapeDtypeStruct((1, LANES), F32)],
        scratch_shapes=[pltpu.VMEM((1, LANES), F32)],
        compiler_params=_params(("arbitrary",)),
        name="moe_router",
    )(x, g_all, w_r, b_r)


EX_TM = 256
EX_ROWS = 2 * SEQ + MOE_EXPERTS * EX_TM
EX_TILES = EX_ROWS // EX_TM


def _row_copy(src_hbm, src_row, dst_vmem, dst_row, sem):
    return pltpu.make_async_copy(src_hbm.at[pl.ds(src_row, 1), :], dst_vmem.at[pl.ds(dst_row, 1), :], sem)


def _experts_kernel(te_ref, nv_ref, idx_ref, x_hbm, g_ref, wg_ref, wu_ref, wd_ref, o_ref, xbuf, sem):
    r = pl.program_id(0)

    @pl.when(r < nv_ref[0])
    def _():
        base = r * EX_TM

        def issue(j, c):
            _row_copy(x_hbm, idx_ref[base + j], xbuf, j, sem).start()
            return c

        lax.fori_loop(0, EX_TM, issue, 0, unroll=8)

        def wait(j, c):
            _row_copy(x_hbm, 0, xbuf, j, sem).wait()
            return c

        lax.fori_loop(0, EX_TM, wait, 0, unroll=8)
        h = _rms(xbuf[...], g_ref[...], RMS_EPS).astype(BF16)
        gate = jnp.dot(h, wg_ref[...], preferred_element_type=F32)
        up = jnp.dot(h, wu_ref[...], preferred_element_type=F32)
        act = (gate * jax.nn.sigmoid(gate) * up).astype(BF16)
        o_ref[...] = jnp.dot(act, wd_ref[...], preferred_element_type=F32)

    @pl.when(r >= nv_ref[0])
    def _():
        o_ref[...] = jnp.zeros(o_ref.shape, F32)


def _experts(tile_expert, n_valid, gather_idx, x, g_all, wg_all, wu_all, wd_all, layer):
    def wspec(k, n):
        return pl.BlockSpec((None, None, k, n), lambda r, te, nv, idx: (layer, te[r], 0, 0))

    grid_spec = pltpu.PrefetchScalarGridSpec(
        num_scalar_prefetch=3,
        grid=(EX_TILES,),
        in_specs=[
            pl.BlockSpec(memory_space=pl.ANY),
            pl.BlockSpec((None, 1, D_MODEL), lambda r, te, nv, idx: (layer, 0, 0)),
            wspec(D_MODEL, MOE_FF), wspec(D_MODEL, MOE_FF), wspec(MOE_FF, D_MODEL),
        ],
        out_specs=pl.BlockSpec((EX_TM, D_MODEL), lambda r, te, nv, idx: (r, 0)),
        scratch_shapes=[pltpu.VMEM((EX_TM, D_MODEL), F32), pltpu.SemaphoreType.DMA(())],
    )
    return pl.pallas_call(
        _experts_kernel,
        grid_spec=grid_spec,
        out_shape=jax.ShapeDtypeStruct((EX_ROWS, D_MODEL), F32),
        compiler_params=_params(("arbitrary",)),
        name="moe_experts",
    )(tile_expert, n_valid, gather_idx, x, g_all, wg_all, wu_all, wd_all)


CB_TM = 256


def _combine_kernel(p1_ref, p2_ref, y_hbm, x_ref, meta_ref, o_ref, y1buf, y2buf, sem):
    i = pl.program_id(0)
    base = i * CB_TM

    def issue(j, c):
        _row_copy(y_hbm, p1_ref[base + j], y1buf, j, sem).start()
        _row_copy(y_hbm, p2_ref[base + j], y2buf, j, sem).start()
        return c

    lax.fori_loop(0, CB_TM, issue, 0, unroll=8)

    def wait(j, c):
        _row_copy(y_hbm, 0, y1buf, j, sem).wait()
        _row_copy(y_hbm, 0, y2buf, j, sem).wait()
        return c

    lax.fori_loop(0, CB_TM, wait, 0, unroll=8)
    meta = meta_ref[...]
    o_ref[...] = x_ref[...] + meta[:, 4:5] * y1buf[...] + meta[:, 5:6] * y2buf[...]


def _combine(pos1, pos2, ys, x, meta):
    t = x.shape[0]
    grid_spec = pltpu.PrefetchScalarGridSpec(
        num_scalar_prefetch=2,
        grid=(t // CB_TM,),
        in_specs=[
            pl.BlockSpec(memory_space=pl.ANY),
            pl.BlockSpec((CB_TM, D_MODEL), lambda i, p1, p2: (i, 0)),
            pl.BlockSpec((CB_TM, LANES), lambda i, p1, p2: (i, 0)),
        ],
        out_specs=pl.BlockSpec((CB_TM, D_MODEL), lambda i, p1, p2: (i, 0)),
        scratch_shapes=[pltpu.VMEM((CB_TM, D_MODEL), F32), pltpu.VMEM((CB_TM, D_MODEL), F32),
                        pltpu.SemaphoreType.DMA(())],
    )
    return pl.pallas_call(
        _combine_kernel,
        grid_spec=grid_spec,
        out_shape=jax.ShapeDtypeStruct((t, D_MODEL), F32),
        compiler_params=_params(("arbitrary",)),
        name="moe_combine",
    )(pos1, pos2, ys, x, meta)


def _dispatch_plan(meta, cnt):
    t = meta.shape[0]
    e1 = meta[:, 0].astype(jnp.int32)
    e2 = meta[:, 1].astype(jnp.int32)
    r1 = meta[:, 2].astype(jnp.int32)
    r2 = meta[:, 3].astype(jnp.int32)
    counts = cnt[0, _E_LANE0:_E_LANE0 + MOE_EXPERTS].astype(jnp.int32)
    padded = ((counts + EX_TM - 1) // EX_TM) * EX_TM
    ends = jnp.cumsum(padded)
    starts = ends - padded
    pos1 = starts[e1] + r1
    pos2 = starts[e2] + r2
    tile_start = jnp.arange(EX_TILES, dtype=jnp.int32) * EX_TM
    tile_expert = jnp.minimum(jnp.searchsorted(ends, tile_start, side='right'), MOE_EXPERTS - 1).astype(jnp.int32)
    n_valid = (ends[-1:] // EX_TM).astype(jnp.int32)
    tok = jnp.arange(t, dtype=jnp.int32)
    gather_idx = jnp.zeros((EX_ROWS,), jnp.int32).at[pos1].set(tok).at[pos2].set(tok)
    return pos1, pos2, tile_expert, n_valid, gather_idx


FN_TM = 512


def _final_norm_kernel(x_ref, g_ref, o_ref):
    o_ref[...] = _rms(x_ref[...], g_ref[...], RMS_EPS)


def _final_norm(x, g):
    t = x.shape[0]
    return pl.pallas_call(
        _final_norm_kernel,
        grid=(t // FN_TM,),
        in_specs=[pl.BlockSpec((FN_TM, D_MODEL), lambda i: (i, 0)),
                  pl.BlockSpec((1, D_MODEL), lambda i: (0, 0))],
        out_specs=pl.BlockSpec((FN_TM, D_MODEL), lambda i: (i, 0)),
        out_shape=jax.ShapeDtypeStruct((t, D_MODEL), F32),
        compiler_params=_params(("arbitrary",)),
        name="final_norm",
    )(x, g)


def kernel(x, mem, positions, norm_mix, w_in, da_lam_q1, da_lam_k1, da_lam_q2, da_lam_k2, da_head_norm, w_da_out, cv_dw_w, cv_dw_b, cv_ln_g, cv_ln_b, w_cv_out, ssm_lam_re, ssm_lam_im, ssm_log_dt, ssm_b_re, ssm_b_im, ssm_c_re, ssm_c_im, ssm_d, w_ssm_glu, b_ssm_glu, w_ssm_out, w_mix_out, norm_xa, norm_mem, w_xa_q, w_xa_kv, w_xa_out, norm_ffn, w_router_group, b_router_group, w_router_expert, b_router_expert, w_exp_gate, w_exp_up, w_exp_down, norm_final):
    bsz, seq, _ = x.shape
    assert bsz == 1 and seq == SEQ
    nl = w_in.shape[0]
    xs = x.reshape(seq, D_MODEL).astype(F32)
    mem2 = mem.reshape(MEM_LEN, D_MODEL).astype(F32)

    inv_freq = ROPE_THETA ** (-jnp.arange(0, DA_HEAD_DIM, 2, dtype=F32) / DA_HEAD_DIM)
    ang = positions.reshape(seq).astype(F32)[:, None] * inv_freq
    cos = jnp.cos(ang)
    sin = jnp.sin(ang)
    cos_t = jnp.concatenate([cos, cos, cos, cos], axis=-1)
    sin_t = jnp.concatenate([-sin, sin, -sin, sin], axis=-1)

    w_in_b = w_in.astype(BF16)
    w_da_b = w_da_out.astype(BF16)
    w_cv_b = w_cv_out.astype(BF16)
    w_glu_b = w_ssm_glu.astype(BF16)
    w_so_b = w_ssm_out.astype(BF16)
    w_mix_b = w_mix_out.astype(BF16)
    w_xq_b = w_xa_q.astype(BF16)
    w_xkv_b = w_xa_kv.astype(BF16)
    w_xo_b = w_xa_out.astype(BF16)
    w_eg_b = w_exp_gate.astype(BF16)
    w_eu_b = w_exp_up.astype(BF16)
    w_ed_b = w_exp_down.astype(BF16)
    pad = LANES - MOE_GROUPS - MOE_EXPERTS
    w_r = jnp.concatenate([w_router_group, w_router_expert,
                           jnp.zeros((nl, D_MODEL, pad), F32)], axis=-1).astype(BF16)
    b_r = jnp.concatenate([b_router_group, b_router_expert, jnp.zeros((nl, pad), F32)], axis=-1).astype(F32)

    def vec3(a):
        return a.astype(F32).reshape(nl, 1, a.shape[-1])

    norm_mix, da_head_norm, cv_dw_b, cv_ln_g, cv_ln_b, b_ssm_glu, norm_xa, norm_mem, norm_ffn, b_r = map(
        vec3, (norm_mix, da_head_norm, cv_dw_b, cv_ln_g, cv_ln_b, b_ssm_glu, norm_xa, norm_mem, norm_ffn, b_r))
    cv_dw_w = cv_dw_w.astype(F32)

    lam_inits = jnp.asarray([0.8 - 0.6 * math.exp(-0.3 * l) for l in range(nl)], F32)
    lam_pack = jnp.stack([da_lam_q1, da_lam_k1, da_lam_q2, da_lam_k2], axis=1).astype(F32)
    lam_pack = jnp.concatenate(
        [lam_pack, jnp.broadcast_to(lam_inits[:, None, None], (nl, 4, DA_HEAD_DIM))], axis=1)

    ssm_mats = _ssm_matrices(ssm_lam_re, ssm_lam_im, ssm_log_dt, ssm_b_re, ssm_b_im,
                             ssm_c_re, ssm_c_im, ssm_d)

    for l in range(nl):
        proj = _inproj(xs, norm_mix, w_in_b, l, cos_t, sin_t)
        o_a = _attention(proj, lam_pack, da_head_norm, l)
        z_b = _conv(proj, cv_dw_w, cv_dw_b, cv_ln_g, cv_ln_b, l)
        u_pairs = _to_pairs(proj[:, COL_SSM:COL_SSM + SSM_WIDTH])
        y_c = _from_pairs(_ssm(u_pairs, [m[l] for m in ssm_mats]))
        xs = _merge_mix(o_a, z_b, y_c, proj, w_da_b, w_cv_b, w_glu_b, b_ssm_glu, w_so_b, w_mix_b, xs, l)
        kv = _mem_kv(mem2, norm_mem, w_xkv_b, l)
        xs = _xattn(xs, norm_xa, w_xq_b, kv, w_xo_b, l)
        meta, cnt = _router(xs, norm_ffn, w_r, b_r, l)
        pos1, pos2, tile_expert, n_valid, gather_idx = _dispatch_plan(meta, cnt)
        ys = _experts(tile_expert, n_valid, gather_idx, xs, norm_ffn, w_eg_b, w_eu_b, w_ed_b, l)
        xs = _combine(pos1, pos2, ys, xs, meta)
    out = _final_norm(xs, norm_final.reshape(1, D_MODEL))
    return out.reshape(bsz, seq, D_MODEL)
```

```python
import functools
import math

import jax
import jax.numpy as jnp
from jax import lax
from jax.experimental import pallas as pl
from jax.experimental.pallas import tpu as pltpu

F32 = jnp.float32
BF16 = jnp.bfloat16

D_MODEL = 2048
SEQ = 8192
DEPTH = 4
MEM_LEN = 256
DA_HEADS = 8
DA_HEAD_DIM = 64
DA_V_DIM = 128
DA_WIDTH = 1024
ROPE_THETA = 10000.0
CV_WIDTH = 512
CONV_TAPS = 31
SSM_WIDTH = 512
SSM_GROUP = 16
SSM_GROUPS = 32
SSM_STATE = 64
XA_HEADS = 4
XA_HEAD_DIM = 128
XA_WIDTH = 512
MOE_GROUPS = 4
MOE_PER_GROUP = 4
MOE_EXPERTS = 16
MOE_FF = 512
RMS_EPS = 1e-6
HEAD_NORM_EPS = 1e-5
LN_EPS = 1e-5

COL_Q = 0
COL_K = COL_Q + 1024
COL_V = COL_K + 1024
COL_CVA = COL_V + 1024
COL_CVB = COL_CVA + CV_WIDTH
COL_SSM = COL_CVB + CV_WIDTH
COL_GA = COL_SSM + SSM_WIDTH
COL_GB = COL_GA + D_MODEL
COL_GC = COL_GB + D_MODEL
IN_TOTAL = COL_GC + D_MODEL

LANES = 128
VMEM_LIMIT = 56 * 1024 * 1024

SSM_CHUNK = 16
SSM_NCHUNK = SEQ // SSM_CHUNK
SSM_PAIRS = SSM_GROUPS // 2
SSM_PAIR_W = 2 * SSM_CHUNK * SSM_GROUP


def _params(sem, vmem=VMEM_LIMIT):
    return pltpu.CompilerParams(dimension_semantics=sem, vmem_limit_bytes=vmem)


def _vec_spec(width, layer, ngrid):
    if ngrid == 1:
        return pl.BlockSpec((None, 1, width), lambda i: (layer, 0, 0))
    return pl.BlockSpec((None, 1, width), lambda i, j: (layer, 0, 0))


def _rms(xf, g, eps):
    ms = jnp.mean(xf * xf, axis=-1, keepdims=True)
    return xf * lax.rsqrt(ms + eps) * g


INP_TM = 1024
INP_TN = 512
_Q_TILE0 = COL_Q // INP_TN
_K_TILE0 = COL_K // INP_TN
_V_TILE0 = COL_V // INP_TN


def _inproj_kernel(x_ref, g_ref, w_ref, cos_ref, sin_ref, o_ref, h_ref):
    j = pl.program_id(1)

    @pl.when(j == 0)
    def _():
        h_ref[...] = _rms(x_ref[...], g_ref[...], RMS_EPS).astype(BF16)

    acc = jnp.dot(h_ref[...], w_ref[...], preferred_element_type=F32)
    is_rope = jnp.logical_and(j >= _Q_TILE0, j < _V_TILE0)

    @pl.when(is_rope)
    def _():
        scale = jnp.where(j < _K_TILE0, math.log2(math.e) * DA_HEAD_DIM ** -0.5, 1.0).astype(F32)
        cos = cos_ref[...] * scale
        sin = sin_ref[...] * scale
        lane = lax.broadcasted_iota(jnp.int32, (INP_TM, LANES), 1)
        first_half = (lane % DA_HEAD_DIM) < (DA_HEAD_DIM // 2)
        for c in range(INP_TN // LANES):
            t = acc[:, c * LANES:(c + 1) * LANES]
            swapped = jnp.where(first_half, pltpu.roll(t, LANES - 32, 1), pltpu.roll(t, 32, 1))
            o_ref[:, c * LANES:(c + 1) * LANES] = (t * cos + swapped * sin).astype(BF16)

    @pl.when(jnp.logical_not(is_rope))
    def _():
        o_ref[...] = acc.astype(BF16)


def _inproj(x, g, w_all, layer, cos_t, sin_t):
    t = x.shape[0]
    return pl.pallas_call(
        _inproj_kernel,
        grid=(t // INP_TM, IN_TOTAL // INP_TN),
        in_specs=[
            pl.BlockSpec((INP_TM, D_MODEL), lambda i, j: (i, 0)),
            _vec_spec(D_MODEL, layer, 2),
            pl.BlockSpec((None, D_MODEL, INP_TN), lambda i, j: (layer, 0, j)),
            pl.BlockSpec((INP_TM, LANES), lambda i, j: (i, 0)),
            pl.BlockSpec((INP_TM, LANES), lambda i, j: (i, 0)),
        ],
        out_specs=pl.BlockSpec((INP_TM, INP_TN), lambda i, j: (i, j)),
        out_shape=jax.ShapeDtypeStruct((t, IN_TOTAL), BF16),
        scratch_shapes=[pltpu.VMEM((INP_TM, D_MODEL), BF16)],
        compiler_params=_params(("arbitrary", "arbitrary")),
        name="inproj",
    )(x, g, w_all, cos_t, sin_t)


ATT_TQ = 1024
ATT_TK = 1024


ATT_RG = 1024


def _attn_kernel(lam_ref, g_ref, q_ref, k_ref, v_ref, o_ref, qs_ref, m_ref, l_ref, acc_ref):
    i = pl.program_id(1)
    tq = ATT_TQ
    q = q_ref[...]
    lane = lax.broadcasted_iota(jnp.int32, q.shape, 1)
    zero = jnp.zeros_like(q)
    qs_ref[0:tq, :] = jnp.where(lane < DA_HEAD_DIM, q, zero)
    qs_ref[tq:2 * tq, :] = jnp.where(lane >= DA_HEAD_DIM, q, zero)
    m_ref[...] = jnp.full(m_ref.shape, -jnp.inf, F32)
    l_ref[...] = jnp.zeros(l_ref.shape, F32)
    acc_ref[...] = jnp.zeros(acc_ref.shape, F32)

    def update_rows(r0, k, v, mask):
        n = k.shape[0]
        rows = slice(r0, r0 + ATT_RG)
        s = lax.dot_general(qs_ref[rows, :], k, (((1,), (1,)), ((), ())), preferred_element_type=F32)
        if mask is not None:
            s = jnp.where(mask, s, -jnp.inf)
        tiles = [s[:, t * LANES:(t + 1) * LANES] for t in range(n // LANES)]
        mc = functools.reduce(jnp.maximum, tiles)
        m_old = m_ref[rows, :]
        m_new = jnp.maximum(m_old, jnp.max(mc, axis=1, keepdims=True))
        alpha = jnp.exp2(m_old - m_new)
        p_tiles = [jnp.exp2(t - m_new) for t in tiles]
        l_ref[rows, :] = alpha * l_ref[rows, :] + functools.reduce(jnp.add, p_tiles)
        p = jnp.concatenate(p_tiles, axis=1).astype(BF16)
        acc_ref[rows, :] = alpha * acc_ref[rows, :] + jnp.dot(p, v, preferred_element_type=F32)
        m_ref[rows, :] = m_new

    def body(c, carry):
        off = pl.multiple_of(c * ATT_TK, ATT_TK)
        k = k_ref[pl.ds(off, ATT_TK), :]
        v = v_ref[pl.ds(off, ATT_TK), :]
        for r0 in range(0, 2 * tq, ATT_RG):
            update_rows(r0, k, v, None)
        return carry

    lax.fori_loop(0, i * (tq // ATT_TK), body, 0)

    off = pl.multiple_of(i * tq, tq)
    for r0 in range(0, 2 * tq, ATT_RG):
        qo = r0 % tq
        n = qo + ATT_RG
        row = lax.broadcasted_iota(jnp.int32, (ATT_RG, n), 0)
        col = lax.broadcasted_iota(jnp.int32, (ATT_RG, n), 1)
        update_rows(r0, k_ref[pl.ds(off, n), :], v_ref[pl.ds(off, n), :], col <= row + qo)

    lam_init = lam_ref[4:5, 0:1]
    lam = (jnp.exp(jnp.sum(lam_ref[0:1, :] * lam_ref[1:2, :], axis=1, keepdims=True))
           - jnp.exp(jnp.sum(lam_ref[2:3, :] * lam_ref[3:4, :], axis=1, keepdims=True))
           + lam_init)
    acc = acc_ref[...]
    inv_l = 1.0 / jnp.sum(l_ref[...], axis=1, keepdims=True)
    o = acc[0:tq] * inv_l[0:tq] - lam * (acc[tq:2 * tq] * inv_l[tq:2 * tq])
    o = _rms(o, g_ref[...], HEAD_NORM_EPS) * (1.0 - lam_init)
    o_ref[...] = o.astype(BF16)


def _attention(proj, lam_pack, head_g, layer):
    t = proj.shape[0]
    qb, kb, vb = COL_Q // LANES, COL_K // LANES, COL_V // LANES
    return pl.pallas_call(
        _attn_kernel,
        grid=(DA_HEADS, t // ATT_TQ),
        in_specs=[
            pl.BlockSpec((None, 8, DA_HEAD_DIM), lambda h, i: (layer, 0, 0)),
            _vec_spec(DA_V_DIM, layer, 2),
            pl.BlockSpec((ATT_TQ, LANES), lambda h, i: (i, qb + h)),
            pl.BlockSpec((t, LANES), lambda h, i: (0, kb + h)),
            pl.BlockSpec((t, LANES), lambda h, i: (0, vb + h)),
        ],
        out_specs=pl.BlockSpec((ATT_TQ, LANES), lambda h, i: (i, h)),
        out_shape=jax.ShapeDtypeStruct((t, DA_WIDTH), BF16),
        scratch_shapes=[
            pltpu.VMEM((2 * ATT_TQ, LANES), BF16),
            pltpu.VMEM((2 * ATT_TQ, LANES), F32),
            pltpu.VMEM((2 * ATT_TQ, LANES), F32),
            pltpu.VMEM((2 * ATT_TQ, LANES), F32),
        ],
        compiler_params=_params(("arbitrary", "arbitrary")),
        name="diff_attn",
    )(lam_pack, head_g, proj, proj, proj)


CV_TM = 512
CV_HALO = 32
CV_ROWS = 64


def _conv_kernel(a_ref, b_ref, w_ref, bias_ref, g_ref, beta_ref, o_ref, z_ref):
    i = pl.program_id(0)

    @pl.when(i == 0)
    def _():
        z_ref[0:CV_HALO, :] = jnp.zeros((CV_HALO, CV_WIDTH), F32)

    @pl.when(i > 0)
    def _():
        z_ref[0:CV_HALO, :] = z_ref[CV_TM:CV_TM + CV_HALO, :]

    a = a_ref[...].astype(F32)
    b = b_ref[...].astype(F32)
    z_ref[CV_HALO:CV_HALO + CV_TM, :] = a * jax.nn.sigmoid(b)

    base = CV_HALO - (CONV_TAPS - 1)
    for r in range(0, CV_TM, CV_ROWS):
        acc = jnp.zeros((CV_ROWS, CV_WIDTH), F32) + bias_ref[...]
        for j in range(CONV_TAPS):
            acc = acc + w_ref[j:j + 1, :] * z_ref[base + r + j:base + r + j + CV_ROWS, :]
        mu = jnp.mean(acc, axis=-1, keepdims=True)
        xc = acc - mu
        var = jnp.mean(xc * xc, axis=-1, keepdims=True)
        y = xc * lax.rsqrt(var + LN_EPS) * g_ref[...] + beta_ref[...]
        o_ref[r:r + CV_ROWS, :] = (y * jax.nn.sigmoid(y)).astype(BF16)


def _conv(proj, dw_w, dw_b, ln_g, ln_b, layer):
    t = proj.shape[0]
    ab, bb = COL_CVA // CV_WIDTH, COL_CVB // CV_WIDTH
    vec = _vec_spec(CV_WIDTH, layer, 1)
    return pl.pallas_call(
        _conv_kernel,
        grid=(t // CV_TM,),
        in_specs=[
            pl.BlockSpec((CV_TM, CV_WIDTH), lambda i: (i, ab)),
            pl.BlockSpec((CV_TM, CV_WIDTH), lambda i: (i, bb)),
            pl.BlockSpec((None, CONV_TAPS, CV_WIDTH), lambda i: (layer, 0, 0)),
            vec, vec, vec,
        ],
        out_specs=pl.BlockSpec((CV_TM, CV_WIDTH), lambda i: (i, 0)),
        out_shape=jax.ShapeDtypeStruct((t, CV_WIDTH), BF16),
        scratch_shapes=[pltpu.VMEM((CV_HALO + CV_TM, CV_WIDTH), F32)],
        compiler_params=_params(("arbitrary",)),
        name="conformer_conv",
    )(proj, proj, dw_w, dw_b, ln_g, ln_b)


def _ssm_kernel(u_ref, t_ref, wre_ref, wim_ref, vre_ref, vim_ref, ar_ref, ai_ref, d_ref,
                y_ref, sre_ref, sim_ref):
    npair, nchunk = SSM_PAIRS, SSM_NCHUNK
    for p in range(npair):
        u = u_ref[p]
        sre_ref[pl.ds(p, nchunk, stride=npair), :] = jnp.dot(u, wre_ref[p], preferred_element_type=F32)
        sim_ref[pl.ds(p, nchunk, stride=npair), :] = jnp.dot(u, wim_ref[p], preferred_element_type=F32)

    ar = ar_ref[...]
    ai = ai_ref[...]

    def body(c, carry):
        xr, xi = carry
        off = pl.multiple_of(c * npair, npair)
        sr = sre_ref[pl.ds(off, npair), :]
        si = sim_ref[pl.ds(off, npair), :]
        sre_ref[pl.ds(off, npair), :] = xr
        sim_ref[pl.ds(off, npair), :] = xi
        return ar * xr - ai * xi + sr, ar * xi + ai * xr + si

    zero = jnp.zeros((npair, LANES), F32)
    lax.fori_loop(0, nchunk, body, (zero, zero), unroll=8)

    for p in range(npair):
        u = u_ref[p]
        xr = sre_ref[pl.ds(p, nchunk, stride=npair), :].astype(BF16)
        xi = sim_ref[pl.ds(p, nchunk, stride=npair), :].astype(BF16)
        y = jnp.dot(u, t_ref[p], preferred_element_type=F32)
        y = y + jnp.dot(xr, vre_ref[p], preferred_element_type=F32)
        y = y + jnp.dot(xi, vim_ref[p], preferred_element_type=F32)
        y = y + u.astype(F32) * d_ref[p:p + 1, :]
        y_ref[p] = y.astype(BF16)


def _ssm(u_pairs, mats):
    vm = pl.BlockSpec(memory_space=pltpu.VMEM)
    return pl.pallas_call(
        _ssm_kernel,
        in_specs=[vm] * 9,
        out_specs=vm,
        out_shape=jax.ShapeDtypeStruct((SSM_PAIRS, SSM_NCHUNK, SSM_PAIR_W), BF16),
        scratch_shapes=[pltpu.VMEM((SSM_NCHUNK * SSM_PAIRS, LANES), F32),
                        pltpu.VMEM((SSM_NCHUNK * SSM_PAIRS, LANES), F32)],
        compiler_params=pltpu.CompilerParams(vmem_limit_bytes=VMEM_LIMIT),
        name="s5_scan",
    )(u_pairs, *mats)


def _blockdiag2(m):
    l, _, r, c = m.shape
    m = m.reshape(l, SSM_PAIRS, 2, r, c)
    z = jnp.zeros_like(m[:, :, 0])
    top = jnp.concatenate([m[:, :, 0], z], axis=-1)
    bot = jnp.concatenate([z, m[:, :, 1]], axis=-1)
    return jnp.concatenate([top, bot], axis=-2)


def _ssm_matrices(lam_re, lam_im, log_dt, b_re, b_im, c_re, c_im, d_skip):
    hp = lax.Precision.HIGHEST
    lr = lam_re.astype(F32)
    li = lam_im.astype(F32)
    dt = jnp.exp(log_dt.astype(F32))[..., None]
    mag = jnp.exp(lr * dt)
    ab_re = mag * jnp.cos(li * dt)
    ab_im = mag * jnp.sin(li * dt)
    den = lr * lr + li * li
    f_re = ((ab_re - 1.0) * lr + ab_im * li) / den
    f_im = (ab_im * lr - (ab_re - 1.0) * li) / den
    br = b_re.astype(F32)
    bi = b_im.astype(F32)
    bb_re = f_re[..., None] * br - f_im[..., None] * bi
    bb_im = f_re[..., None] * bi + f_im[..., None] * br
    tau = jnp.arange(SSM_CHUNK + 1, dtype=F32)[:, None, None, None]
    pmag = jnp.exp(tau * (lr * dt)[None])
    pw_re = pmag * jnp.cos(tau * (li * dt)[None])
    pw_im = pmag * jnp.sin(tau * (li * dt)[None])
    cr = c_re.astype(F32)
    ci = c_im.astype(F32)
    ca_re = cr[None] * pw_re[:, :, :, None, :] - ci[None] * pw_im[:, :, :, None, :]
    ca_im = cr[None] * pw_im[:, :, :, None, :] + ci[None] * pw_re[:, :, :, None, :]
    kk = (jnp.einsum('tlghp,lgpk->lgthk', ca_re[:SSM_CHUNK], bb_re, precision=hp)
          - jnp.einsum('tlghp,lgpk->lgthk', ca_im[:SSM_CHUNK], bb_im, precision=hp))
    s_idx = jnp.arange(SSM_CHUNK)[:, None]
    t_idx = jnp.arange(SSM_CHUNK)[None, :]
    lag = t_idx - s_idx
    tm = kk[:, :, jnp.clip(lag, 0, SSM_CHUNK - 1)]
    tm = jnp.where((lag >= 0)[None, None, :, :, None, None], tm, 0.0)
    nl = lr.shape[0]
    tm = tm.transpose(0, 1, 2, 5, 3, 4).reshape(nl, SSM_GROUPS, 256, 256)
    rev_re = pw_re[:SSM_CHUNK][::-1]
    rev_im = pw_im[:SSM_CHUNK][::-1]
    w_re = rev_re[..., None] * bb_re[None] - rev_im[..., None] * bb_im[None]
    w_im = rev_re[..., None] * bb_im[None] + rev_im[..., None] * bb_re[None]
    w_re = w_re.transpose(1, 2, 0, 4, 3).reshape(nl, SSM_GROUPS, 256, SSM_STATE)
    w_im = w_im.transpose(1, 2, 0, 4, 3).reshape(nl, SSM_GROUPS, 256, SSM_STATE)
    v_re = ca_re[1:].transpose(1, 2, 4, 0, 3).reshape(nl, SSM_GROUPS, SSM_STATE, 256)
    v_im = (-ca_im[1:]).transpose(1, 2, 4, 0, 3).reshape(nl, SSM_GROUPS, SSM_STATE, 256)
    a_re = pw_re[SSM_CHUNK].reshape(nl, SSM_PAIRS, LANES)
    a_im = pw_im[SSM_CHUNK].reshape(nl, SSM_PAIRS, LANES)
    d_t = jnp.broadcast_to(d_skip.astype(F32)[:, :, None, :], (nl, SSM_GROUPS, SSM_CHUNK, SSM_GROUP))
    d_t = d_t.reshape(nl, SSM_PAIRS, SSM_PAIR_W)
    return (_blockdiag2(tm).astype(BF16), _blockdiag2(w_re).astype(BF16), _blockdiag2(w_im).astype(BF16),
            _blockdiag2(v_re).astype(BF16), _blockdiag2(v_im).astype(BF16), a_re, a_im, d_t)


def _to_pairs(u):
    t = u.shape[0]
    u = u.reshape(t // SSM_CHUNK, SSM_CHUNK, SSM_PAIRS, 2, SSM_GROUP)
    return u.transpose(2, 0, 3, 1, 4).reshape(SSM_PAIRS, t // SSM_CHUNK, SSM_PAIR_W)


def _from_pairs(y):
    nchunk = y.shape[1]
    y = y.reshape(SSM_PAIRS, nchunk, 2, SSM_CHUNK, SSM_GROUP)
    return y.transpose(1, 3, 0, 2, 4).reshape(nchunk * SSM_CHUNK, SSM_WIDTH)


MG_TM = 512
MG_TN = 512


def _merge_mix_kernel(oa_ref, zb_ref, yc_ref, ga_ref, gb_ref, gc_ref, wda_ref, wcv_ref, wglu_ref,
                      bglu_ref, wso_ref, wmix_ref, x_ref, o_ref, sc_ref):
    n = pl.program_id(1)

    @pl.when(n == 0)
    def _():
        glu = jnp.dot(yc_ref[...], wglu_ref[...], preferred_element_type=F32) + bglu_ref[...]
        sc_ref[...] = (glu[:, :SSM_WIDTH] * jax.nn.sigmoid(glu[:, SSM_WIDTH:])).astype(BF16)
        o_ref[...] = x_ref[...]

    y_a = jnp.dot(oa_ref[...], wda_ref[...], preferred_element_type=F32)
    y_b = jnp.dot(zb_ref[...], wcv_ref[...], preferred_element_type=F32)
    y_c = jnp.dot(sc_ref[...], wso_ref[...], preferred_element_type=F32)
    merged = (jax.nn.sigmoid(ga_ref[...].astype(F32)) * y_a
              + jax.nn.sigmoid(gb_ref[...].astype(F32)) * y_b
              + jax.nn.sigmoid(gc_ref[...].astype(F32)) * y_c)
    o_ref[...] += jnp.dot(merged.astype(BF16), wmix_ref[...], preferred_element_type=F32)


def _merge_mix(o_a, z_b, y_c, proj, w_da, w_cv, w_glu, b_glu, w_so, w_mix, x, layer):
    t = o_a.shape[0]
    ga, gb, gc = COL_GA // MG_TN, COL_GB // MG_TN, COL_GC // MG_TN

    def wcol(k):
        return pl.BlockSpec((None, k, MG_TN), lambda i, n: (layer, 0, n))

    return pl.pallas_call(
        _merge_mix_kernel,
        grid=(t // MG_TM, D_MODEL // MG_TN),
        in_specs=[
            pl.BlockSpec((MG_TM, DA_WIDTH), lambda i, n: (i, 0)),
            pl.BlockSpec((MG_TM, CV_WIDTH), lambda i, n: (i, 0)),
            pl.BlockSpec((MG_TM, SSM_WIDTH), lambda i, n: (i, 0)),
            pl.BlockSpec((MG_TM, MG_TN), lambda i, n: (i, ga + n)),
            pl.BlockSpec((MG_TM, MG_TN), lambda i, n: (i, gb + n)),
            pl.BlockSpec((MG_TM, MG_TN), lambda i, n: (i, gc + n)),
            wcol(DA_WIDTH), wcol(CV_WIDTH),
            pl.BlockSpec((None, SSM_WIDTH, 2 * SSM_WIDTH), lambda i, n: (layer, 0, 0)),
            _vec_spec(2 * SSM_WIDTH, layer, 2),
            wcol(SSM_WIDTH),
            pl.BlockSpec((None, MG_TN, D_MODEL), lambda i, n: (layer, n, 0)),
            pl.BlockSpec((MG_TM, D_MODEL), lambda i, n: (i, 0)),
        ],
        out_specs=pl.BlockSpec((MG_TM, D_MODEL), lambda i, n: (i, 0)),
        out_shape=jax.ShapeDtypeStruct((t, D_MODEL), F32),
        scratch_shapes=[pltpu.VMEM((MG_TM, SSM_WIDTH), BF16)],
        compiler_params=_params(("arbitrary", "arbitrary")),
        name="merge_mix",
    )(o_a, z_b, y_c, proj, proj, proj, w_da, w_cv, w_glu, b_glu, w_so, w_mix, x)


def _norm_matmul_kernel(x_ref, g_ref, w_ref, o_ref):
    h = _rms(x_ref[...], g_ref[...], RMS_EPS).astype(BF16)
    o_ref[...] = jnp.dot(h, w_ref[...], preferred_element_type=F32).astype(o_ref.dtype)


def _mem_kv(mem, g_all, w_all, layer):
    m = mem.shape[0]
    n = 2 * XA_WIDTH
    return pl.pallas_call(
        _norm_matmul_kernel,
        grid=(1,),
        in_specs=[
            pl.BlockSpec((m, D_MODEL), lambda i: (0, 0)),
            _vec_spec(D_MODEL, layer, 1),
            pl.BlockSpec((None, D_MODEL, n), lambda i: (layer, 0, 0)),
        ],
        out_specs=pl.BlockSpec((m, n), lambda i: (0, 0)),
        out_shape=jax.ShapeDtypeStruct((m, n), BF16),
        compiler_params=_params(("arbitrary",)),
        name="mem_kv",
    )(mem, g_all, w_all)


XA_TM = 512


def _xattn_kernel(x_ref, g_ref, wq_ref, kv_ref, wo_ref, o_ref):
    x = x_ref[...]
    h = _rms(x, g_ref[...], RMS_EPS).astype(BF16)
    q = jnp.dot(h, wq_ref[...], preferred_element_type=F32).astype(BF16)
    heads = []
    for hd in range(XA_HEADS):
        lo = hd * XA_HEAD_DIM
        k = kv_ref[:, lo:lo + XA_HEAD_DIM]
        v = kv_ref[:, XA_WIDTH + lo:XA_WIDTH + lo + XA_HEAD_DIM]
        s = lax.dot_general(q[:, lo:lo + XA_HEAD_DIM], k, (((1,), (1,)), ((), ())),
                            preferred_element_type=F32) * (XA_HEAD_DIM ** -0.5)
        m = jnp.max(s, axis=-1, keepdims=True)
        e = jnp.exp(s - m)
        p = e / jnp.sum(e, axis=-1, keepdims=True)
        heads.append(jnp.dot(p.astype(BF16), v, preferred_element_type=F32).astype(BF16))
    o = jnp.concatenate(heads, axis=-1)
    o_ref[...] = x + jnp.dot(o, wo_ref[...], preferred_element_type=F32)


def _xattn(x, g_all, wq_all, kv, wo_all, layer):
    t = x.shape[0]
    return pl.pallas_call(
        _xattn_kernel,
        grid=(t // XA_TM,),
        in_specs=[
            pl.BlockSpec((XA_TM, D_MODEL), lambda i: (i, 0)),
            _vec_spec(D_MODEL, layer, 1),
            pl.BlockSpec((None, D_MODEL, XA_WIDTH), lambda i: (layer, 0, 0)),
            pl.BlockSpec((MEM_LEN, 2 * XA_WIDTH), lambda i: (0, 0)),
            pl.BlockSpec((None, XA_WIDTH, D_MODEL), lambda i: (layer, 0, 0)),
        ],
        out_specs=pl.BlockSpec((XA_TM, D_MODEL), lambda i: (i, 0)),
        out_shape=jax.ShapeDtypeStruct((t, D_MODEL), F32),
        compiler_params=_params(("arbitrary",)),
        name="mem_xattn",
    )(x, g_all, wq_all, kv, wo_all)


RT_TM = 512
_E_LANE0 = MOE_GROUPS


def _router_kernel(x_ref, g_ref, w_ref, b_ref, meta_ref, cnt_ref, run_ref):
    i = pl.program_id(0)

    @pl.when(i == 0)
    def _():
        run_ref[...] = jnp.zeros(run_ref.shape, F32)

    h = _rms(x_ref[...], g_ref[...], RMS_EPS).astype(BF16)
    logits = jnp.dot(h, w_ref[...], preferred_element_type=F32) + b_ref[...]
    lane = lax.broadcasted_iota(jnp.int32, logits.shape, 1).astype(F32)
    neg = jnp.float32(-jnp.inf)
    big = jnp.float32(LANES)

    def first_argmax(vals):
        top = jnp.max(vals, axis=-1, keepdims=True)
        idx = jnp.min(jnp.where(vals == top, lane, big), axis=-1, keepdims=True)
        return top, idx

    gl = jnp.where(lane < MOE_GROUPS, logits, neg)
    g_top, g_idx = first_argmax(gl)
    g_w = 1.0 / jnp.sum(jnp.exp(gl - g_top), axis=-1, keepdims=True)
    e_lane = lane - _E_LANE0
    in_group = jnp.logical_and(e_lane >= g_idx * MOE_PER_GROUP, e_lane < (g_idx + 1) * MOE_PER_GROUP)
    el = jnp.where(in_group, logits, neg)
    v1, i1 = first_argmax(el)
    el2 = jnp.where(lane == i1, neg, el)
    v2, i2 = first_argmax(el2)
    e2 = jnp.exp(v2 - v1)
    w1 = 1.0 / (1.0 + e2)
    w2 = e2 / (1.0 + e2)
    oh1 = (lane == i1).astype(F32)
    oh2 = (lane == i2).astype(F32)
    both = oh1 + oh2
    row = lax.broadcasted_iota(jnp.int32, (RT_TM, RT_TM), 0)
    col = lax.broadcasted_iota(jnp.int32, (RT_TM, RT_TM), 1)
    earlier = jnp.where(col < row, 1.0, 0.0).astype(BF16)
    before = jnp.dot(earlier, both.astype(BF16), preferred_element_type=F32) + run_ref[...]
    rank1 = jnp.sum(oh1 * before, axis=-1, keepdims=True)
    rank2 = jnp.sum(oh2 * before, axis=-1, keepdims=True)
    run_ref[...] += jnp.sum(both, axis=0, keepdims=True)
    cnt_ref[...] = run_ref[...]
    meta = jnp.where(lane == 0.0, i1 - _E_LANE0, 0.0)
    for k, val in enumerate((i2 - _E_LANE0, rank1, rank2, w1 * g_w, w2 * g_w), start=1):
        meta = jnp.where(lane == float(k), val, meta)
    meta_ref[...] = meta


def _router(x, g_all, w_r, b_r, layer):
    t = x.shape[0]
    return pl.pallas_call(
        _router_kernel,
        grid=(t // RT_TM,),
        in_specs=[
            pl.BlockSpec((RT_TM, D_MODEL), lambda i: (i, 0)),
            _vec_spec(D_MODEL, layer, 1),
            pl.BlockSpec((None, D_MODEL, LANES), lambda i: (layer, 0, 0)),
            _vec_spec(LANES, layer, 1),
        ],
        out_specs=[
            pl.BlockSpec((RT_TM, LANES), lambda i: (i, 0)),
            pl.BlockSpec((1, LANES), lambda i: (0, 0)),
        ],
        out_shape=[jax.ShapeDtypeStruct((t, LANES), F32), jax.ShapeDtypeStruct((1, LANES), F32)],
        scratch_shapes=[pltpu.VMEM((1, LANES), F32)],
        compiler_params=_params(("arbitrary",)),
        name="moe_router",
    )(x, g_all, w_r, b_r)


EX_TM = 256
EX_ROWS = 2 * SEQ
EX_TILES = EX_ROWS // EX_TM
EX_STEPS = EX_TILES + MOE_EXPERTS - 1


def _row_copy(src, src_row, dst, dst_row, sem):
    return pltpu.make_async_copy(src.at[pl.ds(src_row, 1), :], dst.at[pl.ds(dst_row, 1), :], sem)


DP_TM = 256


def _dispatch_kernel(p1_ref, p2_ref, x_ref, xs_hbm, sem):
    base = pl.program_id(0) * DP_TM

    def issue(j, c):
        _row_copy(x_ref, j, xs_hbm, p1_ref[base + j], sem).start()
        _row_copy(x_ref, j, xs_hbm, p2_ref[base + j], sem).start()
        return c

    lax.fori_loop(0, DP_TM, issue, 0, unroll=8)

    def wait(j, c):
        _row_copy(x_ref, j, xs_hbm, 0, sem).wait()
        _row_copy(x_ref, j, xs_hbm, 0, sem).wait()
        return c

    lax.fori_loop(0, DP_TM, wait, 0, unroll=8)


def _dispatch(pos1, pos2, x):
    t = x.shape[0]
    grid_spec = pltpu.PrefetchScalarGridSpec(
        num_scalar_prefetch=2,
        grid=(t // DP_TM,),
        in_specs=[pl.BlockSpec((DP_TM, D_MODEL), lambda i, p1, p2: (i, 0))],
        out_specs=pl.BlockSpec(memory_space=pl.ANY),
        scratch_shapes=[pltpu.SemaphoreType.DMA(())],
    )
    return pl.pallas_call(
        _dispatch_kernel,
        grid_spec=grid_spec,
        out_shape=jax.ShapeDtypeStruct((EX_ROWS, D_MODEL), F32),
        compiler_params=_params(("arbitrary",)),
        name="moe_dispatch",
    )(pos1, pos2, x)


def _experts_kernel(se_ref, st_ref, lo_ref, hi_ref, first_ref, wnew_ref, x_ref, g_ref, wg_ref, wu_ref,
                    wd_ref, o_ref, wgb_ref, wub_ref, wdb_ref):
    s = pl.program_id(0)
    lo = lo_ref[s]
    hi = hi_ref[s]

    @pl.when(wnew_ref[s] == 1)
    def _():
        wgb_ref[...] = wg_ref[...].astype(BF16)
        wub_ref[...] = wu_ref[...].astype(BF16)
        wdb_ref[...] = wd_ref[...].astype(BF16)

    @pl.when(hi > lo)
    def _():
        h = _rms(x_ref[...], g_ref[...], RMS_EPS).astype(BF16)
        gate = jnp.dot(h, wgb_ref[...], preferred_element_type=F32)
        up = jnp.dot(h, wub_ref[...], preferred_element_type=F32)
        act = (gate * jax.nn.sigmoid(gate) * up).astype(BF16)
        res = jnp.dot(act, wdb_ref[...], preferred_element_type=F32)
        row = lax.broadcasted_iota(jnp.int32, (EX_TM, 1), 0)
        mine = jnp.logical_and(row >= lo, row < hi)

        @pl.when(first_ref[s] == 1)
        def _():
            o_ref[...] = jnp.where(mine, res, 0.0)

        @pl.when(first_ref[s] == 0)
        def _():
            o_ref[...] = jnp.where(mine, res, o_ref[...])


def _experts(plan, xs_sorted, g_all, wg_all, wu_all, wd_all, layer):
    def wspec(k, n):
        return pl.BlockSpec((None, None, k, n), lambda s, se, st, lo, hi, fi, wn: (layer, se[s], 0, 0))

    grid_spec = pltpu.PrefetchScalarGridSpec(
        num_scalar_prefetch=6,
        grid=(EX_STEPS,),
        in_specs=[
            pl.BlockSpec((EX_TM, D_MODEL), lambda s, se, st, lo, hi, fi, wn: (st[s], 0)),
            pl.BlockSpec((None, 1, D_MODEL), lambda s, se, st, lo, hi, fi, wn: (layer, 0, 0)),
            wspec(D_MODEL, MOE_FF), wspec(D_MODEL, MOE_FF), wspec(MOE_FF, D_MODEL),
        ],
        out_specs=pl.BlockSpec((EX_TM, D_MODEL), lambda s, se, st, lo, hi, fi, wn: (st[s], 0)),
        scratch_shapes=[pltpu.VMEM((D_MODEL, MOE_FF), BF16), pltpu.VMEM((D_MODEL, MOE_FF), BF16),
                        pltpu.VMEM((MOE_FF, D_MODEL), BF16)],
    )
    return pl.pallas_call(
        _experts_kernel,
        grid_spec=grid_spec,
        out_shape=jax.ShapeDtypeStruct((EX_ROWS, D_MODEL), F32),
        compiler_params=_params(("arbitrary",)),
        name="moe_experts",
    )(*plan, xs_sorted, g_all, wg_all, wu_all, wd_all)


CB_TM = 256


def _combine_kernel(p1_ref, p2_ref, y_hbm, x_ref, meta_ref, o_ref, y1buf, y2buf, sem):
    i = pl.program_id(0)
    base = i * CB_TM

    def issue(j, c):
        _row_copy(y_hbm, p1_ref[base + j], y1buf, j, sem).start()
        _row_copy(y_hbm, p2_ref[base + j], y2buf, j, sem).start()
        return c

    lax.fori_loop(0, CB_TM, issue, 0, unroll=8)

    def wait(j, c):
        _row_copy(y_hbm, 0, y1buf, j, sem).wait()
        _row_copy(y_hbm, 0, y2buf, j, sem).wait()
        return c

    lax.fori_loop(0, CB_TM, wait, 0, unroll=8)
    meta = meta_ref[...]
    o_ref[...] = x_ref[...] + meta[:, 4:5] * y1buf[...] + meta[:, 5:6] * y2buf[...]


def _combine(pos1, pos2, ys, x, meta):
    t = x.shape[0]
    grid_spec = pltpu.PrefetchScalarGridSpec(
        num_scalar_prefetch=2,
        grid=(t // CB_TM,),
        in_specs=[
            pl.BlockSpec(memory_space=pl.ANY),
            pl.BlockSpec((CB_TM, D_MODEL), lambda i, p1, p2: (i, 0)),
            pl.BlockSpec((CB_TM, LANES), lambda i, p1, p2: (i, 0)),
        ],
        out_specs=pl.BlockSpec((CB_TM, D_MODEL), lambda i, p1, p2: (i, 0)),
        scratch_shapes=[pltpu.VMEM((CB_TM, D_MODEL), F32), pltpu.VMEM((CB_TM, D_MODEL), F32),
                        pltpu.SemaphoreType.DMA(())],
    )
    return pl.pallas_call(
        _combine_kernel,
        grid_spec=grid_spec,
        out_shape=jax.ShapeDtypeStruct((t, D_MODEL), F32),
        compiler_params=_params(("arbitrary",)),
        name="moe_combine",
    )(pos1, pos2, ys, x, meta)


def _dispatch_plan(meta, cnt):
    t = meta.shape[0]
    e1 = meta[:, 0].astype(jnp.int32)
    e2 = meta[:, 1].astype(jnp.int32)
    r1 = meta[:, 2].astype(jnp.int32)
    r2 = meta[:, 3].astype(jnp.int32)
    counts = cnt[0, _E_LANE0:_E_LANE0 + MOE_EXPERTS].astype(jnp.int32)
    ends = jnp.cumsum(counts)
    starts = ends - counts
    pos1 = starts[e1] + r1
    pos2 = starts[e2] + r2
    t_lo = starts // EX_TM
    t_hi = (ends + EX_TM - 1) // EX_TM
    nsteps = jnp.where(counts > 0, t_hi - t_lo, 0)
    step_end = jnp.cumsum(nsteps)
    step_start = step_end - nsteps
    s = jnp.arange(EX_STEPS, dtype=jnp.int32)
    se = jnp.minimum(jnp.sum((s[:, None] >= step_end[None, :]).astype(jnp.int32), axis=1), MOE_EXPERTS - 1)
    valid = s < step_end[-1]
    st = jnp.where(valid, t_lo[se] + s - step_start[se], EX_TILES - 1)
    lo = jnp.where(valid, jnp.clip(starts[se] - st * EX_TM, 0, EX_TM), 0)
    hi = jnp.where(valid, jnp.clip(ends[se] - st * EX_TM, 0, EX_TM), 0)
    first = jnp.concatenate([jnp.ones((1,), jnp.int32), (st[1:] != st[:-1]).astype(jnp.int32)])
    wnew = jnp.concatenate([jnp.ones((1,), jnp.int32), (se[1:] != se[:-1]).astype(jnp.int32)])
    plan = tuple(a.astype(jnp.int32) for a in (se, st, lo, hi, first, wnew))
    return pos1.astype(jnp.int32), pos2.astype(jnp.int32), plan


FN_TM = 512


def _final_norm_kernel(x_ref, g_ref, o_ref):
    o_ref[...] = _rms(x_ref[...], g_ref[...], RMS_EPS)


def _final_norm(x, g):
    t = x.shape[0]
    return pl.pallas_call(
        _final_norm_kernel,
        grid=(t // FN_TM,),
        in_specs=[pl.BlockSpec((FN_TM, D_MODEL), lambda i: (i, 0)),
                  pl.BlockSpec((1, D_MODEL), lambda i: (0, 0))],
        out_specs=pl.BlockSpec((FN_TM, D_MODEL), lambda i: (i, 0)),
        out_shape=jax.ShapeDtypeStruct((t, D_MODEL), F32),
        compiler_params=_params(("arbitrary",)),
        name="final_norm",
    )(x, g)


def kernel(x, mem, positions, norm_mix, w_in, da_lam_q1, da_lam_k1, da_lam_q2, da_lam_k2, da_head_norm, w_da_out, cv_dw_w, cv_dw_b, cv_ln_g, cv_ln_b, w_cv_out, ssm_lam_re, ssm_lam_im, ssm_log_dt, ssm_b_re, ssm_b_im, ssm_c_re, ssm_c_im, ssm_d, w_ssm_glu, b_ssm_glu, w_ssm_out, w_mix_out, norm_xa, norm_mem, w_xa_q, w_xa_kv, w_xa_out, norm_ffn, w_router_group, b_router_group, w_router_expert, b_router_expert, w_exp_gate, w_exp_up, w_exp_down, norm_final):
    bsz, seq, _ = x.shape
    assert bsz == 1 and seq == SEQ
    nl = w_in.shape[0]
    xs = x.reshape(seq, D_MODEL).astype(F32)
    mem2 = mem.reshape(MEM_LEN, D_MODEL).astype(F32)

    inv_freq = ROPE_THETA ** (-jnp.arange(0, DA_HEAD_DIM, 2, dtype=F32) / DA_HEAD_DIM)
    ang = positions.reshape(seq).astype(F32)[:, None] * inv_freq
    cos = jnp.cos(ang)
    sin = jnp.sin(ang)
    cos_t = jnp.concatenate([cos, cos, cos, cos], axis=-1)
    sin_t = jnp.concatenate([-sin, sin, -sin, sin], axis=-1)

    w_in_b = w_in.astype(BF16)
    w_da_b = w_da_out.astype(BF16)
    w_cv_b = w_cv_out.astype(BF16)
    w_glu_b = w_ssm_glu.astype(BF16)
    w_so_b = w_ssm_out.astype(BF16)
    w_mix_b = w_mix_out.astype(BF16)
    w_xq_b = w_xa_q.astype(BF16)
    w_xkv_b = w_xa_kv.astype(BF16)
    w_xo_b = w_xa_out.astype(BF16)
    pad = LANES - MOE_GROUPS - MOE_EXPERTS
    w_r = jnp.concatenate([w_router_group, w_router_expert,
                           jnp.zeros((nl, D_MODEL, pad), F32)], axis=-1).astype(BF16)
    b_r = jnp.concatenate([b_router_group, b_router_expert, jnp.zeros((nl, pad), F32)], axis=-1).astype(F32)

    def vec3(a):
        return a.astype(F32).reshape(nl, 1, a.shape[-1])

    norm_mix, da_head_norm, cv_dw_b, cv_ln_g, cv_ln_b, b_ssm_glu, norm_xa, norm_mem, norm_ffn, b_r = map(
        vec3, (norm_mix, da_head_norm, cv_dw_b, cv_ln_g, cv_ln_b, b_ssm_glu, norm_xa, norm_mem, norm_ffn, b_r))
    cv_dw_w = cv_dw_w.astype(F32)

    lam_inits = jnp.asarray([0.8 - 0.6 * math.exp(-0.3 * l) for l in range(nl)], F32)
    lam_pack = jnp.stack([da_lam_q1, da_lam_k1, da_lam_q2, da_lam_k2], axis=1).astype(F32)
    lam_pack = jnp.concatenate(
        [lam_pack, jnp.broadcast_to(lam_inits[:, None, None], (nl, 4, DA_HEAD_DIM))], axis=1)

    ssm_mats = _ssm_matrices(ssm_lam_re, ssm_lam_im, ssm_log_dt, ssm_b_re, ssm_b_im,
                             ssm_c_re, ssm_c_im, ssm_d)

    for l in range(nl):
        proj = _inproj(xs, norm_mix, w_in_b, l, cos_t, sin_t)
        o_a = _attention(proj, lam_pack, da_head_norm, l)
        z_b = _conv(proj, cv_dw_w, cv_dw_b, cv_ln_g, cv_ln_b, l)
        u_pairs = _to_pairs(proj[:, COL_SSM:COL_SSM + SSM_WIDTH])
        y_c = _from_pairs(_ssm(u_pairs, [m[l] for m in ssm_mats]))
        xs = _merge_mix(o_a, z_b, y_c, proj, w_da_b, w_cv_b, w_glu_b, b_ssm_glu, w_so_b, w_mix_b, xs, l)
        kv = _mem_kv(mem2, norm_mem, w_xkv_b, l)
        xs = _xattn(xs, norm_xa, w_xq_b, kv, w_xo_b, l)
        meta, cnt = _router(xs, norm_ffn, w_r, b_r, l)
        pos1, pos2, plan = _dispatch_plan(meta, cnt)
        xs_sorted = _dispatch(pos1, pos2, xs)
        ys = _experts(plan, xs_sorted, norm_ffn, w_exp_gate, w_exp_up, w_exp_down, l)
        xs = _combine(pos1, pos2, ys, xs, meta)
    out = _final_norm(xs, norm_final.reshape(1, D_MODEL))
    return out.reshape(bsz, seq, D_MODEL)
```

```python
import functools
import math

import jax
import jax.numpy as jnp
from jax import lax
from jax.experimental import pallas as pl
from jax.experimental.pallas import tpu as pltpu

F32 = jnp.float32
BF16 = jnp.bfloat16

D_MODEL = 2048
SEQ = 8192
DEPTH = 4
MEM_LEN = 256
DA_HEADS = 8
DA_HEAD_DIM = 64
DA_V_DIM = 128
DA_WIDTH = 1024
ROPE_THETA = 10000.0
CV_WIDTH = 512
CONV_TAPS = 31
SSM_WIDTH = 512
SSM_GROUP = 16
SSM_GROUPS = 32
SSM_STATE = 64
XA_HEADS = 4
XA_HEAD_DIM = 128
XA_WIDTH = 512
MOE_GROUPS = 4
MOE_PER_GROUP = 4
MOE_EXPERTS = 16
MOE_FF = 512
RMS_EPS = 1e-6
HEAD_NORM_EPS = 1e-5
LN_EPS = 1e-5

COL_Q = 0
COL_K = COL_Q + 1024
COL_V = COL_K + 1024
COL_CVA = COL_V + 1024
COL_CVB = COL_CVA + CV_WIDTH
COL_SSM = COL_CVB + CV_WIDTH
COL_GA = COL_SSM + SSM_WIDTH
COL_GB = COL_GA + D_MODEL
COL_GC = COL_GB + D_MODEL
IN_TOTAL = COL_GC + D_MODEL

LANES = 128
VMEM_LIMIT = 56 * 1024 * 1024

SSM_CHUNK = 16
SSM_NCHUNK = SEQ // SSM_CHUNK
SSM_PAIRS = SSM_GROUPS // 2
SSM_PAIR_W = 2 * SSM_CHUNK * SSM_GROUP


def _params(sem, vmem=VMEM_LIMIT):
    return pltpu.CompilerParams(dimension_semantics=sem, vmem_limit_bytes=vmem)


def _vec_spec(width, layer, ngrid):
    if ngrid == 1:
        return pl.BlockSpec((None, 1, width), lambda i: (layer, 0, 0))
    return pl.BlockSpec((None, 1, width), lambda i, j: (layer, 0, 0))


def _rms(xf, g, eps):
    ms = jnp.mean(xf * xf, axis=-1, keepdims=True)
    return xf * lax.rsqrt(ms + eps) * g


INP_TM = 1024
INP_TN = 512
_Q_TILE0 = COL_Q // INP_TN
_K_TILE0 = COL_K // INP_TN
_V_TILE0 = COL_V // INP_TN


def _inproj_kernel(x_ref, g_ref, w_ref, cos_ref, sin_ref, o_ref, h_ref):
    j = pl.program_id(1)

    @pl.when(j == 0)
    def _():
        h_ref[...] = _rms(x_ref[...], g_ref[...], RMS_EPS).astype(BF16)

    acc = jnp.dot(h_ref[...], w_ref[...], preferred_element_type=F32)
    is_rope = jnp.logical_and(j >= _Q_TILE0, j < _V_TILE0)

    @pl.when(is_rope)
    def _():
        scale = jnp.where(j < _K_TILE0, math.log2(math.e) * DA_HEAD_DIM ** -0.5, 1.0).astype(F32)
        cos = cos_ref[...] * scale
        sin = sin_ref[...] * scale
        lane = lax.broadcasted_iota(jnp.int32, (INP_TM, LANES), 1)
        first_half = (lane % DA_HEAD_DIM) < (DA_HEAD_DIM // 2)
        for c in range(INP_TN // LANES):
            t = acc[:, c * LANES:(c + 1) * LANES]
            swapped = jnp.where(first_half, pltpu.roll(t, LANES - 32, 1), pltpu.roll(t, 32, 1))
            o_ref[:, c * LANES:(c + 1) * LANES] = (t * cos + swapped * sin).astype(BF16)

    @pl.when(jnp.logical_not(is_rope))
    def _():
        o_ref[...] = acc.astype(BF16)


def _inproj(x, g, w_all, layer, cos_t, sin_t):
    t = x.shape[0]
    return pl.pallas_call(
        _inproj_kernel,
        grid=(t // INP_TM, IN_TOTAL // INP_TN),
        in_specs=[
            pl.BlockSpec((INP_TM, D_MODEL), lambda i, j: (i, 0)),
            _vec_spec(D_MODEL, layer, 2),
            pl.BlockSpec((None, D_MODEL, INP_TN), lambda i, j: (layer, 0, j)),
            pl.BlockSpec((INP_TM, LANES), lambda i, j: (i, 0)),
            pl.BlockSpec((INP_TM, LANES), lambda i, j: (i, 0)),
        ],
        out_specs=pl.BlockSpec((INP_TM, INP_TN), lambda i, j: (i, j)),
        out_shape=jax.ShapeDtypeStruct((t, IN_TOTAL), BF16),
        scratch_shapes=[pltpu.VMEM((INP_TM, D_MODEL), BF16)],
        compiler_params=_params(("arbitrary", "arbitrary")),
        name="inproj",
    )(x, g, w_all, cos_t, sin_t)


ATT_TQ = 1024
ATT_TK = 1024


ATT_RG = 1024


def _attn_kernel(lam_ref, g_ref, q_ref, k_ref, v_ref, o_ref, qs_ref, m_ref, l_ref, acc_ref):
    i = pl.program_id(1)
    tq = ATT_TQ
    q = q_ref[...]
    lane = lax.broadcasted_iota(jnp.int32, q.shape, 1)
    zero = jnp.zeros_like(q)
    qs_ref[0:tq, :] = jnp.where(lane < DA_HEAD_DIM, q, zero)
    qs_ref[tq:2 * tq, :] = jnp.where(lane >= DA_HEAD_DIM, q, zero)
    m_ref[...] = jnp.full(m_ref.shape, -jnp.inf, F32)
    l_ref[...] = jnp.zeros(l_ref.shape, F32)
    acc_ref[...] = jnp.zeros(acc_ref.shape, F32)

    def update_rows(r0, k, v, mask):
        n = k.shape[0]
        rows = slice(r0, r0 + ATT_RG)
        s = lax.dot_general(qs_ref[rows, :], k, (((1,), (1,)), ((), ())), preferred_element_type=F32)
        if mask is not None:
            s = jnp.where(mask, s, -jnp.inf)
        tiles = [s[:, t * LANES:(t + 1) * LANES] for t in range(n // LANES)]
        mc = functools.reduce(jnp.maximum, tiles)
        m_old = m_ref[rows, :]
        m_new = jnp.maximum(m_old, jnp.max(mc, axis=1, keepdims=True))
        alpha = jnp.exp2(m_old - m_new)
        p_tiles = [jnp.exp2(t - m_new) for t in tiles]
        l_ref[rows, :] = alpha * l_ref[rows, :] + functools.reduce(jnp.add, p_tiles)
        p = jnp.concatenate(p_tiles, axis=1).astype(BF16)
        acc_ref[rows, :] = alpha * acc_ref[rows, :] + jnp.dot(p, v, preferred_element_type=F32)
        m_ref[rows, :] = m_new

    def body(c, carry):
        off = pl.multiple_of(c * ATT_TK, ATT_TK)
        k = k_ref[pl.ds(off, ATT_TK), :]
        v = v_ref[pl.ds(off, ATT_TK), :]
        for r0 in range(0, 2 * tq, ATT_RG):
            update_rows(r0, k, v, None)
        return carry

    lax.fori_loop(0, i * (tq // ATT_TK), body, 0)

    off = pl.multiple_of(i * tq, tq)
    for r0 in range(0, 2 * tq, ATT_RG):
        qo = r0 % tq
        n = qo + ATT_RG
        row = lax.broadcasted_iota(jnp.int32, (ATT_RG, n), 0)
        col = lax.broadcasted_iota(jnp.int32, (ATT_RG, n), 1)
        update_rows(r0, k_ref[pl.ds(off, n), :], v_ref[pl.ds(off, n), :], col <= row + qo)

    lam_init = lam_ref[4:5, 0:1]
    lam = (jnp.exp(jnp.sum(lam_ref[0:1, :] * lam_ref[1:2, :], axis=1, keepdims=True))
           - jnp.exp(jnp.sum(lam_ref[2:3, :] * lam_ref[3:4, :], axis=1, keepdims=True))
           + lam_init)
    acc = acc_ref[...]
    inv_l = 1.0 / jnp.sum(l_ref[...], axis=1, keepdims=True)
    o = acc[0:tq] * inv_l[0:tq] - lam * (acc[tq:2 * tq] * inv_l[tq:2 * tq])
    o = _rms(o, g_ref[...], HEAD_NORM_EPS) * (1.0 - lam_init)
    o_ref[...] = o.astype(BF16)


def _attention(proj, lam_pack, head_g, layer):
    t = proj.shape[0]
    qb, kb, vb = COL_Q // LANES, COL_K // LANES, COL_V // LANES
    return pl.pallas_call(
        _attn_kernel,
        grid=(DA_HEADS, t // ATT_TQ),
        in_specs=[
            pl.BlockSpec((None, 8, DA_HEAD_DIM), lambda h, i: (layer, 0, 0)),
            _vec_spec(DA_V_DIM, layer, 2),
            pl.BlockSpec((ATT_TQ, LANES), lambda h, i: (i, qb + h)),
            pl.BlockSpec((t, LANES), lambda h, i: (0, kb + h)),
            pl.BlockSpec((t, LANES), lambda h, i: (0, vb + h)),
        ],
        out_specs=pl.BlockSpec((ATT_TQ, LANES), lambda h, i: (i, h)),
        out_shape=jax.ShapeDtypeStruct((t, DA_WIDTH), BF16),
        scratch_shapes=[
            pltpu.VMEM((2 * ATT_TQ, LANES), BF16),
            pltpu.VMEM((2 * ATT_TQ, LANES), F32),
            pltpu.VMEM((2 * ATT_TQ, LANES), F32),
            pltpu.VMEM((2 * ATT_TQ, LANES), F32),
        ],
        compiler_params=_params(("arbitrary", "arbitrary")),
        name="diff_attn",
    )(lam_pack, head_g, proj, proj, proj)


CV_TM = 512
CV_HALO = 32
CV_ROWS = 64


def _conv_kernel(a_ref, b_ref, w_ref, bias_ref, g_ref, beta_ref, o_ref, z_ref):
    i = pl.program_id(0)

    @pl.when(i == 0)
    def _():
        z_ref[0:CV_HALO, :] = jnp.zeros((CV_HALO, CV_WIDTH), F32)

    @pl.when(i > 0)
    def _():
        z_ref[0:CV_HALO, :] = z_ref[CV_TM:CV_TM + CV_HALO, :]

    a = a_ref[...].astype(F32)
    b = b_ref[...].astype(F32)
    z_ref[CV_HALO:CV_HALO + CV_TM, :] = a * jax.nn.sigmoid(b)

    base = CV_HALO - (CONV_TAPS - 1)
    for r in range(0, CV_TM, CV_ROWS):
        acc = jnp.zeros((CV_ROWS, CV_WIDTH), F32) + bias_ref[...]
        for j in range(CONV_TAPS):
            acc = acc + w_ref[j:j + 1, :] * z_ref[base + r + j:base + r + j + CV_ROWS, :]
        mu = jnp.mean(acc, axis=-1, keepdims=True)
        xc = acc - mu
        var = jnp.mean(xc * xc, axis=-1, keepdims=True)
        y = xc * lax.rsqrt(var + LN_EPS) * g_ref[...] + beta_ref[...]
        o_ref[r:r + CV_ROWS, :] = (y * jax.nn.sigmoid(y)).astype(BF16)


def _conv(proj, dw_w, dw_b, ln_g, ln_b, layer):
    t = proj.shape[0]
    ab, bb = COL_CVA // CV_WIDTH, COL_CVB // CV_WIDTH
    vec = _vec_spec(CV_WIDTH, layer, 1)
    return pl.pallas_call(
        _conv_kernel,
        grid=(t // CV_TM,),
        in_specs=[
            pl.BlockSpec((CV_TM, CV_WIDTH), lambda i: (i, ab)),
            pl.BlockSpec((CV_TM, CV_WIDTH), lambda i: (i, bb)),
            pl.BlockSpec((None, CONV_TAPS, CV_WIDTH), lambda i: (layer, 0, 0)),
            vec, vec, vec,
        ],
        out_specs=pl.BlockSpec((CV_TM, CV_WIDTH), lambda i: (i, 0)),
        out_shape=jax.ShapeDtypeStruct((t, CV_WIDTH), BF16),
        scratch_shapes=[pltpu.VMEM((CV_HALO + CV_TM, CV_WIDTH), F32)],
        compiler_params=_params(("arbitrary",)),
        name="conformer_conv",
    )(proj, proj, dw_w, dw_b, ln_g, ln_b)


SSM_CB = 128
SSM_RB = SSM_CB * SSM_CHUNK
SSM_SLAB = 2 * SSM_GROUP
SSM_SLABS = LANES // SSM_SLAB


def _pick_slabs(pieces, src_slab, lane_slab):
    out = None
    for k, piece in enumerate(pieces):
        shift = ((k - src_slab) * SSM_SLAB) % LANES
        moved = piece if shift == 0 else pltpu.roll(piece, shift, 1)
        out = moved if out is None else jnp.where(lane_slab == k, moved, out)
    return out


def _ssm_kernel(u_ref, t_ref, wre_ref, wim_ref, vre_ref, vim_ref, ar_ref, ai_ref, d_ref, y_ref,
                uf_ref, up_ref, sre_ref, sim_ref, yp_ref, yf_ref, xr_ref, xi_ref):
    npair, cb = SSM_PAIRS, SSM_CB

    @pl.when(pl.program_id(0) == 0)
    def _():
        xr_ref[...] = jnp.zeros(xr_ref.shape, F32)
        xi_ref[...] = jnp.zeros(xi_ref.shape, F32)

    ntile = SSM_WIDTH // LANES
    for j in range(ntile):
        uf_ref[j] = u_ref[:, j * LANES:(j + 1) * LANES].astype(F32)
    lane_slab = lax.shift_right_logical(lax.broadcasted_iota(jnp.int32, (cb, LANES), 1), 5)

    for q in range(SSM_CHUNK // SSM_SLABS):
        for tile in range(ntile):
            src = uf_ref.at[tile]
            pieces = [src[pl.ds(SSM_SLABS * q + k, cb, stride=SSM_CHUNK), :] for k in range(SSM_SLABS)]
            for slab in range(SSM_SLABS):
                up_ref[SSM_SLABS * tile + slab, :, q * LANES:(q + 1) * LANES] = _pick_slabs(
                    pieces, slab, lane_slab).astype(BF16)

    for p in range(npair):
        u = up_ref[p]
        sre_ref[pl.ds(p, cb, stride=npair), :] = jnp.dot(u, wre_ref[p], preferred_element_type=F32)
        sim_ref[pl.ds(p, cb, stride=npair), :] = jnp.dot(u, wim_ref[p], preferred_element_type=F32)

    ar = ar_ref[...]
    ai = ai_ref[...]

    def body(c, carry):
        xr, xi = carry
        off = pl.multiple_of(c * npair, npair)
        sr = sre_ref[pl.ds(off, npair), :]
        si = sim_ref[pl.ds(off, npair), :]
        sre_ref[pl.ds(off, npair), :] = xr
        sim_ref[pl.ds(off, npair), :] = xi
        return ar * xr - ai * xi + sr, ar * xi + ai * xr + si

    xr, xi = lax.fori_loop(0, cb, body, (xr_ref[...], xi_ref[...]), unroll=8)
    xr_ref[...] = xr
    xi_ref[...] = xi

    for p in range(npair):
        u = up_ref[p]
        sr = sre_ref[pl.ds(p, cb, stride=npair), :].astype(BF16)
        si = sim_ref[pl.ds(p, cb, stride=npair), :].astype(BF16)
        y = jnp.dot(u, t_ref[p], preferred_element_type=F32)
        y = y + jnp.dot(sr, vre_ref[p], preferred_element_type=F32)
        yp_ref[p] = y + jnp.dot(si, vim_ref[p], preferred_element_type=F32)

    for tile in range(ntile):
        for q in range(SSM_CHUNK // SSM_SLABS):
            pieces = [yp_ref[SSM_SLABS * tile + k, :, q * LANES:(q + 1) * LANES] for k in range(SSM_SLABS)]
            dst = yf_ref.at[tile]
            for slab in range(SSM_SLABS):
                dst[pl.ds(SSM_SLABS * q + slab, cb, stride=SSM_CHUNK), :] = _pick_slabs(pieces, slab, lane_slab)

    for j in range(ntile):
        cols = slice(j * LANES, (j + 1) * LANES)
        y_ref[:, cols] = (yf_ref[j] + uf_ref[j] * d_ref[:, cols]).astype(BF16)


def _ssm(proj, mats):
    t = proj.shape[0]
    vm = pl.BlockSpec(memory_space=pltpu.VMEM)
    return pl.pallas_call(
        _ssm_kernel,
        grid=(t // SSM_RB,),
        in_specs=[pl.BlockSpec((SSM_RB, SSM_WIDTH), lambda i: (i, COL_SSM // SSM_WIDTH))] + [vm] * 8,
        out_specs=pl.BlockSpec((SSM_RB, SSM_WIDTH), lambda i: (i, 0)),
        out_shape=jax.ShapeDtypeStruct((t, SSM_WIDTH), BF16),
        scratch_shapes=[
            pltpu.VMEM((SSM_WIDTH // LANES, SSM_RB, LANES), F32),
            pltpu.VMEM((SSM_PAIRS, SSM_CB, SSM_PAIR_W), BF16),
            pltpu.VMEM((SSM_CB * SSM_PAIRS, LANES), F32),
            pltpu.VMEM((SSM_CB * SSM_PAIRS, LANES), F32),
            pltpu.VMEM((SSM_PAIRS, SSM_CB, SSM_PAIR_W), F32),
            pltpu.VMEM((SSM_WIDTH // LANES, SSM_RB, LANES), F32),
            pltpu.VMEM((SSM_PAIRS, LANES), F32),
            pltpu.VMEM((SSM_PAIRS, LANES), F32),
        ],
        compiler_params=_params(("arbitrary",)),
        name="s5_scan",
    )(proj, *mats)


def _ssm_matrices(lam_re, lam_im, log_dt, b_re, b_im, c_re, c_im, d_skip):
    hp = lax.Precision.HIGHEST
    lr = lam_re.astype(F32)
    li = lam_im.astype(F32)
    dt = jnp.exp(log_dt.astype(F32))[..., None]
    mag = jnp.exp(lr * dt)
    ab_re = mag * jnp.cos(li * dt)
    ab_im = mag * jnp.sin(li * dt)
    den = lr * lr + li * li
    f_re = ((ab_re - 1.0) * lr + ab_im * li) / den
    f_im = (ab_im * lr - (ab_re - 1.0) * li) / den
    br = b_re.astype(F32)
    bi = b_im.astype(F32)
    bb_re = f_re[..., None] * br - f_im[..., None] * bi
    bb_im = f_re[..., None] * bi + f_im[..., None] * br
    tau = jnp.arange(SSM_CHUNK + 1, dtype=F32)[:, None, None, None]
    pmag = jnp.exp(tau * (lr * dt)[None])
    pw_re = pmag * jnp.cos(tau * (li * dt)[None])
    pw_im = pmag * jnp.sin(tau * (li * dt)[None])
    cr = c_re.astype(F32)
    ci = c_im.astype(F32)
    ca_re = cr[None] * pw_re[:, :, :, None, :] - ci[None] * pw_im[:, :, :, None, :]
    ca_im = cr[None] * pw_im[:, :, :, None, :] + ci[None] * pw_re[:, :, :, None, :]
    kk = (jnp.einsum('tlghp,lgpk->lgthk', ca_re[:SSM_CHUNK], bb_re, precision=hp)
          - jnp.einsum('tlghp,lgpk->lgthk', ca_im[:SSM_CHUNK], bb_im, precision=hp))
    s_idx = jnp.arange(SSM_CHUNK)[:, None]
    t_idx = jnp.arange(SSM_CHUNK)[None, :]
    lag = t_idx - s_idx
    tm = kk[:, :, jnp.clip(lag, 0, SSM_CHUNK - 1)]
    tm = jnp.where((lag >= 0)[None, None, :, :, None, None], tm, 0.0)
    nl = lr.shape[0]
    eye2 = jnp.eye(2, dtype=F32)
    pair = (nl, SSM_PAIRS, 2)
    tm = tm.transpose(0, 1, 2, 5, 3, 4)
    tm = jnp.einsum('lpeshtk,ef->lpsehtfk', tm.reshape(pair + tm.shape[2:]), eye2)
    tm = tm.reshape(nl, SSM_PAIRS, SSM_PAIR_W, SSM_PAIR_W)
    rev_re = pw_re[:SSM_CHUNK][::-1]
    rev_im = pw_im[:SSM_CHUNK][::-1]
    w_re = rev_re[..., None] * bb_re[None] - rev_im[..., None] * bb_im[None]
    w_im = rev_re[..., None] * bb_im[None] + rev_im[..., None] * bb_re[None]

    def pair_w(w):
        w = w.transpose(1, 2, 0, 4, 3)
        w = jnp.einsum('lpeshn,ef->lpsehfn', w.reshape(pair + w.shape[2:]), eye2)
        return w.reshape(nl, SSM_PAIRS, SSM_PAIR_W, LANES)

    def pair_v(v):
        v = v.transpose(1, 2, 4, 0, 3)
        v = jnp.einsum('lpenth,ef->lpfnteh', v.reshape(pair + v.shape[2:]), eye2)
        return v.reshape(nl, SSM_PAIRS, LANES, SSM_PAIR_W)

    a_re = pw_re[SSM_CHUNK].reshape(nl, SSM_PAIRS, LANES)
    a_im = pw_im[SSM_CHUNK].reshape(nl, SSM_PAIRS, LANES)
    d_t = d_skip.astype(F32).reshape(nl, 1, SSM_WIDTH)
    return (tm.astype(BF16), pair_w(w_re).astype(BF16), pair_w(w_im).astype(BF16),
            pair_v(ca_re[1:]).astype(BF16), pair_v(-ca_im[1:]).astype(BF16), a_re, a_im, d_t)


MG_TM = 512
MG_TN = 512


def _merge_mix_kernel(oa_ref, zb_ref, yc_ref, ga_ref, gb_ref, gc_ref, wda_ref, wcv_ref, wglu_ref,
                      bglu_ref, wso_ref, wmix_ref, x_ref, o_ref, sc_ref):
    n = pl.program_id(1)

    @pl.when(n == 0)
    def _():
        glu = jnp.dot(yc_ref[...], wglu_ref[...], preferred_element_type=F32) + bglu_ref[...]
        sc_ref[...] = (glu[:, :SSM_WIDTH] * jax.nn.sigmoid(glu[:, SSM_WIDTH:])).astype(BF16)
        o_ref[...] = x_ref[...]

    y_a = jnp.dot(oa_ref[...], wda_ref[...], preferred_element_type=F32)
    y_b = jnp.dot(zb_ref[...], wcv_ref[...], preferred_element_type=F32)
    y_c = jnp.dot(sc_ref[...], wso_ref[...], preferred_element_type=F32)
    merged = (jax.nn.sigmoid(ga_ref[...].astype(F32)) * y_a
              + jax.nn.sigmoid(gb_ref[...].astype(F32)) * y_b
              + jax.nn.sigmoid(gc_ref[...].astype(F32)) * y_c)
    o_ref[...] += jnp.dot(merged.astype(BF16), wmix_ref[...], preferred_element_type=F32)


def _merge_mix(o_a, z_b, y_c, proj, w_da, w_cv, w_glu, b_glu, w_so, w_mix, x, layer):
    t = o_a.shape[0]
    ga, gb, gc = COL_GA // MG_TN, COL_GB // MG_TN, COL_GC // MG_TN

    def wcol(k):
        return pl.BlockSpec((None, k, MG_TN), lambda i, n: (layer, 0, n))

    return pl.pallas_call(
        _merge_mix_kernel,
        grid=(t // MG_TM, D_MODEL // MG_TN),
        in_specs=[
            pl.BlockSpec((MG_TM, DA_WIDTH), lambda i, n: (i, 0)),
            pl.BlockSpec((MG_TM, CV_WIDTH), lambda i, n: (i, 0)),
            pl.BlockSpec((MG_TM, SSM_WIDTH), lambda i, n: (i, 0)),
            pl.BlockSpec((MG_TM, MG_TN), lambda i, n: (i, ga + n)),
            pl.BlockSpec((MG_TM, MG_TN), lambda i, n: (i, gb + n)),
            pl.BlockSpec((MG_TM, MG_TN), lambda i, n: (i, gc + n)),
            wcol(DA_WIDTH), wcol(CV_WIDTH),
            pl.BlockSpec((None, SSM_WIDTH, 2 * SSM_WIDTH), lambda i, n: (layer, 0, 0)),
            _vec_spec(2 * SSM_WIDTH, layer, 2),
            wcol(SSM_WIDTH),
            pl.BlockSpec((None, MG_TN, D_MODEL), lambda i, n: (layer, n, 0)),
            pl.BlockSpec((MG_TM, D_MODEL), lambda i, n: (i, 0)),
        ],
        out_specs=pl.BlockSpec((MG_TM, D_MODEL), lambda i, n: (i, 0)),
        out_shape=jax.ShapeDtypeStruct((t, D_MODEL), F32),
        scratch_shapes=[pltpu.VMEM((MG_TM, SSM_WIDTH), BF16)],
        compiler_params=_params(("arbitrary", "arbitrary")),
        name="merge_mix",
    )(o_a, z_b, y_c, proj, proj, proj, w_da, w_cv, w_glu, b_glu, w_so, w_mix, x)


def _norm_matmul_kernel(x_ref, g_ref, w_ref, o_ref):
    h = _rms(x_ref[...], g_ref[...], RMS_EPS).astype(BF16)
    o_ref[...] = jnp.dot(h, w_ref[...], preferred_element_type=F32).astype(o_ref.dtype)


def _mem_kv(mem, g_all, w_all, layer):
    m = mem.shape[0]
    n = 2 * XA_WIDTH
    return pl.pallas_call(
        _norm_matmul_kernel,
        grid=(1,),
        in_specs=[
            pl.BlockSpec((m, D_MODEL), lambda i: (0, 0)),
            _vec_spec(D_MODEL, layer, 1),
            pl.BlockSpec((None, D_MODEL, n), lambda i: (layer, 0, 0)),
        ],
        out_specs=pl.BlockSpec((m, n), lambda i: (0, 0)),
        out_shape=jax.ShapeDtypeStruct((m, n), BF16),
        compiler_params=_params(("arbitrary",)),
        name="mem_kv",
    )(mem, g_all, w_all)


XA_TM = 512


def _xattn_kernel(x_ref, g_ref, wq_ref, kv_ref, wo_ref, o_ref):
    x = x_ref[...]
    h = _rms(x, g_ref[...], RMS_EPS).astype(BF16)
    q = jnp.dot(h, wq_ref[...], preferred_element_type=F32).astype(BF16)
    heads = []
    for hd in range(XA_HEADS):
        lo = hd * XA_HEAD_DIM
        k = kv_ref[:, lo:lo + XA_HEAD_DIM]
        v = kv_ref[:, XA_WIDTH + lo:XA_WIDTH + lo + XA_HEAD_DIM]
        s = lax.dot_general(q[:, lo:lo + XA_HEAD_DIM], k, (((1,), (1,)), ((), ())),
                            preferred_element_type=F32) * (XA_HEAD_DIM ** -0.5)
        m = jnp.max(s, axis=-1, keepdims=True)
        e = jnp.exp(s - m)
        p = e / jnp.sum(e, axis=-1, keepdims=True)
        heads.append(jnp.dot(p.astype(BF16), v, preferred_element_type=F32).astype(BF16))
    o = jnp.concatenate(heads, axis=-1)
    o_ref[...] = x + jnp.dot(o, wo_ref[...], preferred_element_type=F32)


def _xattn(x, g_all, wq_all, kv, wo_all, layer):
    t = x.shape[0]
    return pl.pallas_call(
        _xattn_kernel,
        grid=(t // XA_TM,),
        in_specs=[
            pl.BlockSpec((XA_TM, D_MODEL), lambda i: (i, 0)),
            _vec_spec(D_MODEL, layer, 1),
            pl.BlockSpec((None, D_MODEL, XA_WIDTH), lambda i: (layer, 0, 0)),
            pl.BlockSpec((MEM_LEN, 2 * XA_WIDTH), lambda i: (0, 0)),
            pl.BlockSpec((None, XA_WIDTH, D_MODEL), lambda i: (layer, 0, 0)),
        ],
        out_specs=pl.BlockSpec((XA_TM, D_MODEL), lambda i: (i, 0)),
        out_shape=jax.ShapeDtypeStruct((t, D_MODEL), F32),
        compiler_params=_params(("arbitrary",)),
        name="mem_xattn",
    )(x, g_all, wq_all, kv, wo_all)


RT_TM = 512
_E_LANE0 = MOE_GROUPS


def _router_kernel(x_ref, g_ref, w_ref, b_ref, meta_ref, cnt_ref, run_ref):
    i = pl.program_id(0)

    @pl.when(i == 0)
    def _():
        run_ref[...] = jnp.zeros(run_ref.shape, F32)

    h = _rms(x_ref[...], g_ref[...], RMS_EPS).astype(BF16)
    logits = jnp.dot(h, w_ref[...], preferred_element_type=F32) + b_ref[...]
    lane = lax.broadcasted_iota(jnp.int32, logits.shape, 1).astype(F32)
    neg = jnp.float32(-jnp.inf)
    big = jnp.float32(LANES)

    def first_argmax(vals):
        top = jnp.max(vals, axis=-1, keepdims=True)
        idx = jnp.min(jnp.where(vals == top, lane, big), axis=-1, keepdims=True)
        return top, idx

    gl = jnp.where(lane < MOE_GROUPS, logits, neg)
    g_top, g_idx = first_argmax(gl)
    g_w = 1.0 / jnp.sum(jnp.exp(gl - g_top), axis=-1, keepdims=True)
    e_lane = lane - _E_LANE0
    in_group = jnp.logical_and(e_lane >= g_idx * MOE_PER_GROUP, e_lane < (g_idx + 1) * MOE_PER_GROUP)
    el = jnp.where(in_group, logits, neg)
    v1, i1 = first_argmax(el)
    el2 = jnp.where(lane == i1, neg, el)
    v2, i2 = first_argmax(el2)
    e2 = jnp.exp(v2 - v1)
    w1 = 1.0 / (1.0 + e2)
    w2 = e2 / (1.0 + e2)
    oh1 = (lane == i1).astype(F32)
    oh2 = (lane == i2).astype(F32)
    both = oh1 + oh2
    row = lax.broadcasted_iota(jnp.int32, (RT_TM, RT_TM), 0)
    col = lax.broadcasted_iota(jnp.int32, (RT_TM, RT_TM), 1)
    earlier = jnp.where(col < row, 1.0, 0.0).astype(BF16)
    before = jnp.dot(earlier, both.astype(BF16), preferred_element_type=F32) + run_ref[...]
    rank1 = jnp.sum(oh1 * before, axis=-1, keepdims=True)
    rank2 = jnp.sum(oh2 * before, axis=-1, keepdims=True)
    run_ref[...] += jnp.sum(both, axis=0, keepdims=True)
    cnt_ref[...] = run_ref[...]
    meta = jnp.where(lane == 0.0, i1 - _E_LANE0, 0.0)
    for k, val in enumerate((i2 - _E_LANE0, rank1, rank2, w1 * g_w, w2 * g_w), start=1):
        meta = jnp.where(lane == float(k), val, meta)
    meta_ref[...] = meta


def _router(x, g_all, w_r, b_r, layer):
    t = x.shape[0]
    return pl.pallas_call(
        _router_kernel,
        grid=(t // RT_TM,),
        in_specs=[
            pl.BlockSpec((RT_TM, D_MODEL), lambda i: (i, 0)),
            _vec_spec(D_MODEL, layer, 1),
            pl.BlockSpec((None, D_MODEL, LANES), lambda i: (layer, 0, 0)),
            _vec_spec(LANES, layer, 1),
        ],
        out_specs=[
            pl.BlockSpec((RT_TM, LANES), lambda i: (i, 0)),
            pl.BlockSpec((1, LANES), lambda i: (0, 0)),
        ],
        out_shape=[jax.ShapeDtypeStruct((t, LANES), F32), jax.ShapeDtypeStruct((1, LANES), F32)],
        scratch_shapes=[pltpu.VMEM((1, LANES), F32)],
        compiler_params=_params(("arbitrary",)),
        name="moe_router",
    )(x, g_all, w_r, b_r)


EX_TM = 256
EX_ROWS = 2 * SEQ
EX_TILES = EX_ROWS // EX_TM
EX_STEPS = EX_TILES + MOE_EXPERTS - 1


def _row_copy(src, src_row, dst, dst_row, sem):
    return pltpu.make_async_copy(src.at[pl.ds(src_row, 1), :], dst.at[pl.ds(dst_row, 1), :], sem)


DP_TM = 256


def _dispatch_kernel(p1_ref, p2_ref, x_ref, xs_hbm, sem):
    base = pl.program_id(0) * DP_TM

    def issue(j, c):
        _row_copy(x_ref, j, xs_hbm, p1_ref[base + j], sem).start()
        _row_copy(x_ref, j, xs_hbm, p2_ref[base + j], sem).start()
        return c

    lax.fori_loop(0, DP_TM, issue, 0, unroll=8)

    def wait(j, c):
        _row_copy(x_ref, j, xs_hbm, 0, sem).wait()
        _row_copy(x_ref, j, xs_hbm, 0, sem).wait()
        return c

    lax.fori_loop(0, DP_TM, wait, 0, unroll=8)


def _dispatch(pos1, pos2, x):
    t = x.shape[0]
    grid_spec = pltpu.PrefetchScalarGridSpec(
        num_scalar_prefetch=2,
        grid=(t // DP_TM,),
        in_specs=[pl.BlockSpec((DP_TM, D_MODEL), lambda i, p1, p2: (i, 0))],
        out_specs=pl.BlockSpec(memory_space=pl.ANY),
        scratch_shapes=[pltpu.SemaphoreType.DMA(())],
    )
    return pl.pallas_call(
        _dispatch_kernel,
        grid_spec=grid_spec,
        out_shape=jax.ShapeDtypeStruct((EX_ROWS, D_MODEL), F32),
        compiler_params=_params(("arbitrary",)),
        name="moe_dispatch",
    )(pos1, pos2, x)


def _experts_kernel(se_ref, st_ref, lo_ref, hi_ref, first_ref, wnew_ref, x_ref, g_ref, wg_ref, wu_ref,
                    wd_ref, o_ref, wgb_ref, wub_ref, wdb_ref):
    s = pl.program_id(0)
    lo = lo_ref[s]
    hi = hi_ref[s]

    @pl.when(wnew_ref[s] == 1)
    def _():
        wgb_ref[...] = wg_ref[...].astype(BF16)
        wub_ref[...] = wu_ref[...].astype(BF16)
        wdb_ref[...] = wd_ref[...].astype(BF16)

    @pl.when(hi > lo)
    def _():
        h = _rms(x_ref[...], g_ref[...], RMS_EPS).astype(BF16)
        gate = jnp.dot(h, wgb_ref[...], preferred_element_type=F32)
        up = jnp.dot(h, wub_ref[...], preferred_element_type=F32)
        act = (gate * jax.nn.sigmoid(gate) * up).astype(BF16)
        res = jnp.dot(act, wdb_ref[...], preferred_element_type=F32)
        row = lax.broadcasted_iota(jnp.int32, (EX_TM, 1), 0)
        mine = jnp.logical_and(row >= lo, row < hi)

        @pl.when(first_ref[s] == 1)
        def _():
            o_ref[...] = jnp.where(mine, res, 0.0)

        @pl.when(first_ref[s] == 0)
        def _():
            o_ref[...] = jnp.where(mine, res, o_ref[...])


def _experts(plan, xs_sorted, g_all, wg_all, wu_all, wd_all, layer):
    def wspec(k, n):
        return pl.BlockSpec((None, None, k, n), lambda s, se, st, lo, hi, fi, wn: (layer, se[s], 0, 0))

    grid_spec = pltpu.PrefetchScalarGridSpec(
        num_scalar_prefetch=6,
        grid=(EX_STEPS,),
        in_specs=[
            pl.BlockSpec((EX_TM, D_MODEL), lambda s, se, st, lo, hi, fi, wn: (st[s], 0)),
            pl.BlockSpec((None, 1, D_MODEL), lambda s, se, st, lo, hi, fi, wn: (layer, 0, 0)),
            wspec(D_MODEL, MOE_FF), wspec(D_MODEL, MOE_FF), wspec(MOE_FF, D_MODEL),
        ],
        out_specs=pl.BlockSpec((EX_TM, D_MODEL), lambda s, se, st, lo, hi, fi, wn: (st[s], 0)),
        scratch_shapes=[pltpu.VMEM((D_MODEL, MOE_FF), BF16), pltpu.VMEM((D_MODEL, MOE_FF), BF16),
                        pltpu.VMEM((MOE_FF, D_MODEL), BF16)],
    )
    return pl.pallas_call(
        _experts_kernel,
        grid_spec=grid_spec,
        out_shape=jax.ShapeDtypeStruct((EX_ROWS, D_MODEL), F32),
        compiler_params=_params(("arbitrary",)),
        name="moe_experts",
    )(*plan, xs_sorted, g_all, wg_all, wu_all, wd_all)


CB_TM = 256


def _combine_kernel(p1_ref, p2_ref, y_hbm, x_ref, meta_ref, o_ref, y1buf, y2buf, sem):
    i = pl.program_id(0)
    slot = lax.rem(i, 2)

    def fetch(tile, slot_):
        base = tile * CB_TM

        def issue(j, c):
            _row_copy(y_hbm, p1_ref[base + j], y1buf.at[slot_], j, sem.at[slot_]).start()
            _row_copy(y_hbm, p2_ref[base + j], y2buf.at[slot_], j, sem.at[slot_]).start()
            return c

        lax.fori_loop(0, CB_TM, issue, 0, unroll=8)

    @pl.when(i == 0)
    def _():
        fetch(0, 0)

    @pl.when(i + 1 < pl.num_programs(0))
    def _():
        fetch(i + 1, 1 - slot)

    def wait(j, c):
        _row_copy(y_hbm, 0, y1buf.at[slot], j, sem.at[slot]).wait()
        _row_copy(y_hbm, 0, y2buf.at[slot], j, sem.at[slot]).wait()
        return c

    lax.fori_loop(0, CB_TM, wait, 0, unroll=8)
    meta = meta_ref[...]
    o_ref[...] = x_ref[...] + meta[:, 4:5] * y1buf[slot] + meta[:, 5:6] * y2buf[slot]


def _combine(pos1, pos2, ys, x, meta):
    t = x.shape[0]
    grid_spec = pltpu.PrefetchScalarGridSpec(
        num_scalar_prefetch=2,
        grid=(t // CB_TM,),
        in_specs=[
            pl.BlockSpec(memory_space=pl.ANY),
            pl.BlockSpec((CB_TM, D_MODEL), lambda i, p1, p2: (i, 0)),
            pl.BlockSpec((CB_TM, LANES), lambda i, p1, p2: (i, 0)),
        ],
        out_specs=pl.BlockSpec((CB_TM, D_MODEL), lambda i, p1, p2: (i, 0)),
        scratch_shapes=[pltpu.VMEM((2, CB_TM, D_MODEL), F32), pltpu.VMEM((2, CB_TM, D_MODEL), F32),
                        pltpu.SemaphoreType.DMA((2,))],
    )
    return pl.pallas_call(
        _combine_kernel,
        grid_spec=grid_spec,
        out_shape=jax.ShapeDtypeStruct((t, D_MODEL), F32),
        compiler_params=_params(("arbitrary",)),
        name="moe_combine",
    )(pos1, pos2, ys, x, meta)


def _dispatch_plan(meta, cnt):
    t = meta.shape[0]
    e1 = meta[:, 0].astype(jnp.int32)
    e2 = meta[:, 1].astype(jnp.int32)
    r1 = meta[:, 2].astype(jnp.int32)
    r2 = meta[:, 3].astype(jnp.int32)
    counts = cnt[0, _E_LANE0:_E_LANE0 + MOE_EXPERTS].astype(jnp.int32)
    ends = jnp.cumsum(counts)
    starts = ends - counts
    pos1 = starts[e1] + r1
    pos2 = starts[e2] + r2
    t_lo = starts // EX_TM
    t_hi = (ends + EX_TM - 1) // EX_TM
    nsteps = jnp.where(counts > 0, t_hi - t_lo, 0)
    step_end = jnp.cumsum(nsteps)
    step_start = step_end - nsteps
    s = jnp.arange(EX_STEPS, dtype=jnp.int32)
    se = jnp.minimum(jnp.sum((s[:, None] >= step_end[None, :]).astype(jnp.int32), axis=1), MOE_EXPERTS - 1)
    valid = s < step_end[-1]
    st = jnp.where(valid, t_lo[se] + s - step_start[se], EX_TILES - 1)
    lo = jnp.where(valid, jnp.clip(starts[se] - st * EX_TM, 0, EX_TM), 0)
    hi = jnp.where(valid, jnp.clip(ends[se] - st * EX_TM, 0, EX_TM), 0)
    first = jnp.concatenate([jnp.ones((1,), jnp.int32), (st[1:] != st[:-1]).astype(jnp.int32)])
    wnew = jnp.concatenate([jnp.ones((1,), jnp.int32), (se[1:] != se[:-1]).astype(jnp.int32)])
    plan = tuple(a.astype(jnp.int32) for a in (se, st, lo, hi, first, wnew))
    return pos1.astype(jnp.int32), pos2.astype(jnp.int32), plan


FN_TM = 512


def _final_norm_kernel(x_ref, g_ref, o_ref):
    o_ref[...] = _rms(x_ref[...], g_ref[...], RMS_EPS)


def _final_norm(x, g):
    t = x.shape[0]
    return pl.pallas_call(
        _final_norm_kernel,
        grid=(t // FN_TM,),
        in_specs=[pl.BlockSpec((FN_TM, D_MODEL), lambda i: (i, 0)),
                  pl.BlockSpec((1, D_MODEL), lambda i: (0, 0))],
        out_specs=pl.BlockSpec((FN_TM, D_MODEL), lambda i: (i, 0)),
        out_shape=jax.ShapeDtypeStruct((t, D_MODEL), F32),
        compiler_params=_params(("arbitrary",)),
        name="final_norm",
    )(x, g)


def kernel(x, mem, positions, norm_mix, w_in, da_lam_q1, da_lam_k1, da_lam_q2, da_lam_k2, da_head_norm, w_da_out, cv_dw_w, cv_dw_b, cv_ln_g, cv_ln_b, w_cv_out, ssm_lam_re, ssm_lam_im, ssm_log_dt, ssm_b_re, ssm_b_im, ssm_c_re, ssm_c_im, ssm_d, w_ssm_glu, b_ssm_glu, w_ssm_out, w_mix_out, norm_xa, norm_mem, w_xa_q, w_xa_kv, w_xa_out, norm_ffn, w_router_group, b_router_group, w_router_expert, b_router_expert, w_exp_gate, w_exp_up, w_exp_down, norm_final):
    bsz, seq, _ = x.shape
    assert bsz == 1 and seq == SEQ
    nl = w_in.shape[0]
    xs = x.reshape(seq, D_MODEL).astype(F32)
    mem2 = mem.reshape(MEM_LEN, D_MODEL).astype(F32)

    inv_freq = ROPE_THETA ** (-jnp.arange(0, DA_HEAD_DIM, 2, dtype=F32) / DA_HEAD_DIM)
    ang = positions.reshape(seq).astype(F32)[:, None] * inv_freq
    cos = jnp.cos(ang)
    sin = jnp.sin(ang)
    cos_t = jnp.concatenate([cos, cos, cos, cos], axis=-1)
    sin_t = jnp.concatenate([-sin, sin, -sin, sin], axis=-1)

    w_in_b = w_in.astype(BF16)
    w_da_b = w_da_out.astype(BF16)
    w_cv_b = w_cv_out.astype(BF16)
    w_glu_b = w_ssm_glu.astype(BF16)
    w_so_b = w_ssm_out.astype(BF16)
    w_mix_b = w_mix_out.astype(BF16)
    w_xq_b = w_xa_q.astype(BF16)
    w_xkv_b = w_xa_kv.astype(BF16)
    w_xo_b = w_xa_out.astype(BF16)
    pad = LANES - MOE_GROUPS - MOE_EXPERTS
    w_r = jnp.concatenate([w_router_group, w_router_expert,
                           jnp.zeros((nl, D_MODEL, pad), F32)], axis=-1).astype(BF16)
    b_r = jnp.concatenate([b_router_group, b_router_expert, jnp.zeros((nl, pad), F32)], axis=-1).astype(F32)

    def vec3(a):
        return a.astype(F32).reshape(nl, 1, a.shape[-1])

    norm_mix, da_head_norm, cv_dw_b, cv_ln_g, cv_ln_b, b_ssm_glu, norm_xa, norm_mem, norm_ffn, b_r = map(
        vec3, (norm_mix, da_head_norm, cv_dw_b, cv_ln_g, cv_ln_b, b_ssm_glu, norm_xa, norm_mem, norm_ffn, b_r))
    cv_dw_w = cv_dw_w.astype(F32)

    lam_inits = jnp.asarray([0.8 - 0.6 * math.exp(-0.3 * l) for l in range(nl)], F32)
    lam_pack = jnp.stack([da_lam_q1, da_lam_k1, da_lam_q2, da_lam_k2], axis=1).astype(F32)
    lam_pack = jnp.concatenate(
        [lam_pack, jnp.broadcast_to(lam_inits[:, None, None], (nl, 4, DA_HEAD_DIM))], axis=1)

    ssm_mats = _ssm_matrices(ssm_lam_re, ssm_lam_im, ssm_log_dt, ssm_b_re, ssm_b_im,
                             ssm_c_re, ssm_c_im, ssm_d)

    for l in range(nl):
        proj = _inproj(xs, norm_mix, w_in_b, l, cos_t, sin_t)
        o_a = _attention(proj, lam_pack, da_head_norm, l)
        z_b = _conv(proj, cv_dw_w, cv_dw_b, cv_ln_g, cv_ln_b, l)
        y_c = _ssm(proj, [m[l] for m in ssm_mats])
        xs = _merge_mix(o_a, z_b, y_c, proj, w_da_b, w_cv_b, w_glu_b, b_ssm_glu, w_so_b, w_mix_b, xs, l)
        kv = _mem_kv(mem2, norm_mem, w_xkv_b, l)
        xs = _xattn(xs, norm_xa, w_xq_b, kv, w_xo_b, l)
        meta, cnt = _router(xs, norm_ffn, w_r, b_r, l)
        pos1, pos2, plan = _dispatch_plan(meta, cnt)
        xs_sorted = _dispatch(pos1, pos2, xs)
        ys = _experts(plan, xs_sorted, norm_ffn, w_exp_gate, w_exp_up, w_exp_down, l)
        xs = _combine(pos1, pos2, ys, xs, meta)
    out = _final_norm(xs, norm_final.reshape(1, D_MODEL))
    return out.reshape(bsz, seq, D_MODEL)
```

```python
import functools
import math

import jax
import jax.numpy as jnp
from jax import lax
from jax.experimental import pallas as pl
from jax.experimental.pallas import tpu as pltpu

F32 = jnp.float32
BF16 = jnp.bfloat16

D_MODEL = 2048
SEQ = 8192
DEPTH = 4
MEM_LEN = 256
DA_HEADS = 8
DA_HEAD_DIM = 64
DA_V_DIM = 128
DA_WIDTH = 1024
ROPE_THETA = 10000.0
CV_WIDTH = 512
CONV_TAPS = 31
SSM_WIDTH = 512
SSM_GROUP = 16
SSM_GROUPS = 32
SSM_STATE = 64
XA_HEADS = 4
XA_HEAD_DIM = 128
XA_WIDTH = 512
MOE_GROUPS = 4
MOE_PER_GROUP = 4
MOE_EXPERTS = 16
MOE_FF = 512
RMS_EPS = 1e-6
HEAD_NORM_EPS = 1e-5
LN_EPS = 1e-5

COL_Q = 0
COL_K = COL_Q + 1024
COL_V = COL_K + 1024
COL_CVA = COL_V + 1024
COL_CVB = COL_CVA + CV_WIDTH
COL_SSM = COL_CVB + CV_WIDTH
COL_GA = COL_SSM + SSM_WIDTH
COL_GB = COL_GA + D_MODEL
COL_GC = COL_GB + D_MODEL
IN_TOTAL = COL_GC + D_MODEL

LANES = 128
VMEM_LIMIT = 56 * 1024 * 1024

SSM_CHUNK = 16
SSM_NCHUNK = SEQ // SSM_CHUNK
SSM_PAIRS = SSM_GROUPS // 2
SSM_PAIR_W = 2 * SSM_CHUNK * SSM_GROUP


def _params(sem, vmem=VMEM_LIMIT):
    return pltpu.CompilerParams(dimension_semantics=sem, vmem_limit_bytes=vmem)


def _vec_spec(width, layer, ngrid):
    if ngrid == 1:
        return pl.BlockSpec((None, 1, width), lambda i: (layer, 0, 0))
    return pl.BlockSpec((None, 1, width), lambda i, j: (layer, 0, 0))


def _rms(xf, g, eps):
    ms = jnp.mean(xf * xf, axis=-1, keepdims=True)
    return xf * lax.rsqrt(ms + eps) * g


INP_TM = 1024
INP_TN = 512
_Q_TILE0 = COL_Q // INP_TN
_K_TILE0 = COL_K // INP_TN
_V_TILE0 = COL_V // INP_TN


def _inproj_kernel(x_ref, g_ref, w_ref, cos_ref, sin_ref, o_ref, h_ref):
    j = pl.program_id(1)

    @pl.when(j == 0)
    def _():
        h_ref[...] = _rms(x_ref[...], g_ref[...], RMS_EPS).astype(BF16)

    acc = jnp.dot(h_ref[...], w_ref[...], preferred_element_type=F32)
    is_rope = jnp.logical_and(j >= _Q_TILE0, j < _V_TILE0)

    @pl.when(is_rope)
    def _():
        scale = jnp.where(j < _K_TILE0, math.log2(math.e) * DA_HEAD_DIM ** -0.5, 1.0).astype(F32)
        cos = cos_ref[...] * scale
        sin = sin_ref[...] * scale
        lane = lax.broadcasted_iota(jnp.int32, (INP_TM, LANES), 1)
        first_half = (lane % DA_HEAD_DIM) < (DA_HEAD_DIM // 2)
        for c in range(INP_TN // LANES):
            t = acc[:, c * LANES:(c + 1) * LANES]
            swapped = jnp.where(first_half, pltpu.roll(t, LANES - 32, 1), pltpu.roll(t, 32, 1))
            o_ref[:, c * LANES:(c + 1) * LANES] = (t * cos + swapped * sin).astype(BF16)

    @pl.when(jnp.logical_not(is_rope))
    def _():
        o_ref[...] = acc.astype(BF16)


def _inproj(x, g, w_all, layer, cos_t, sin_t):
    t = x.shape[0]
    return pl.pallas_call(
        _inproj_kernel,
        grid=(t // INP_TM, IN_TOTAL // INP_TN),
        in_specs=[
            pl.BlockSpec((INP_TM, D_MODEL), lambda i, j: (i, 0)),
            _vec_spec(D_MODEL, layer, 2),
            pl.BlockSpec((None, D_MODEL, INP_TN), lambda i, j: (layer, 0, j)),
            pl.BlockSpec((INP_TM, LANES), lambda i, j: (i, 0)),
            pl.BlockSpec((INP_TM, LANES), lambda i, j: (i, 0)),
        ],
        out_specs=pl.BlockSpec((INP_TM, INP_TN), lambda i, j: (i, j)),
        out_shape=jax.ShapeDtypeStruct((t, IN_TOTAL), BF16),
        scratch_shapes=[pltpu.VMEM((INP_TM, D_MODEL), BF16)],
        compiler_params=_params(("arbitrary", "arbitrary")),
        name="inproj",
    )(x, g, w_all, cos_t, sin_t)


ATT_TQ = 1024
ATT_TK = 1024


ATT_RG = 1024


def _attn_kernel(lam_ref, g_ref, q_ref, k_ref, v_ref, o_ref, qs_ref, m_ref, l_ref, acc_ref):
    i = pl.program_id(1)
    tq = ATT_TQ
    q = q_ref[...]
    lane = lax.broadcasted_iota(jnp.int32, q.shape, 1)
    zero = jnp.zeros_like(q)
    qs_ref[0:tq, :] = jnp.where(lane < DA_HEAD_DIM, q, zero)
    qs_ref[tq:2 * tq, :] = jnp.where(lane >= DA_HEAD_DIM, q, zero)
    m_ref[...] = jnp.full(m_ref.shape, -jnp.inf, F32)
    l_ref[...] = jnp.zeros(l_ref.shape, F32)
    acc_ref[...] = jnp.zeros(acc_ref.shape, F32)

    def update_rows(r0, k, v, mask):
        n = k.shape[0]
        rows = slice(r0, r0 + ATT_RG)
        s = lax.dot_general(qs_ref[rows, :], k, (((1,), (1,)), ((), ())), preferred_element_type=F32)
        if mask is not None:
            s = jnp.where(mask, s, -jnp.inf)
        tiles = [s[:, t * LANES:(t + 1) * LANES] for t in range(n // LANES)]
        mc = functools.reduce(jnp.maximum, tiles)
        m_old = m_ref[rows, :]
        m_new = jnp.maximum(m_old, jnp.max(mc, axis=1, keepdims=True))
        alpha = jnp.exp2(m_old - m_new)
        p_tiles = [jnp.exp2(t - m_new) for t in tiles]
        l_ref[rows, :] = alpha * l_ref[rows, :] + functools.reduce(jnp.add, p_tiles)
        p = jnp.concatenate(p_tiles, axis=1).astype(BF16)
        acc_ref[rows, :] = alpha * acc_ref[rows, :] + jnp.dot(p, v, preferred_element_type=F32)
        m_ref[rows, :] = m_new

    def body(c, carry):
        off = pl.multiple_of(c * ATT_TK, ATT_TK)
        k = k_ref[pl.ds(off, ATT_TK), :]
        v = v_ref[pl.ds(off, ATT_TK), :]
        for r0 in range(0, 2 * tq, ATT_RG):
            update_rows(r0, k, v, None)
        return carry

    lax.fori_loop(0, i * (tq // ATT_TK), body, 0)

    off = pl.multiple_of(i * tq, tq)
    for r0 in range(0, 2 * tq, ATT_RG):
        qo = r0 % tq
        n = qo + ATT_RG
        row = lax.broadcasted_iota(jnp.int32, (ATT_RG, n), 0)
        col = lax.broadcasted_iota(jnp.int32, (ATT_RG, n), 1)
        update_rows(r0, k_ref[pl.ds(off, n), :], v_ref[pl.ds(off, n), :], col <= row + qo)

    lam_init = lam_ref[4:5, 0:1]
    lam = (jnp.exp(jnp.sum(lam_ref[0:1, :] * lam_ref[1:2, :], axis=1, keepdims=True))
           - jnp.exp(jnp.sum(lam_ref[2:3, :] * lam_ref[3:4, :], axis=1, keepdims=True))
           + lam_init)
    acc = acc_ref[...]
    inv_l = 1.0 / jnp.sum(l_ref[...], axis=1, keepdims=True)
    o = acc[0:tq] * inv_l[0:tq] - lam * (acc[tq:2 * tq] * inv_l[tq:2 * tq])
    o = _rms(o, g_ref[...], HEAD_NORM_EPS) * (1.0 - lam_init)
    o_ref[...] = o.astype(BF16)


def _attention(proj, lam_pack, head_g, layer):
    t = proj.shape[0]
    qb, kb, vb = COL_Q // LANES, COL_K // LANES, COL_V // LANES
    return pl.pallas_call(
        _attn_kernel,
        grid=(DA_HEADS, t // ATT_TQ),
        in_specs=[
            pl.BlockSpec((None, 8, DA_HEAD_DIM), lambda h, i: (layer, 0, 0)),
            _vec_spec(DA_V_DIM, layer, 2),
            pl.BlockSpec((ATT_TQ, LANES), lambda h, i: (i, qb + h)),
            pl.BlockSpec((t, LANES), lambda h, i: (0, kb + h)),
            pl.BlockSpec((t, LANES), lambda h, i: (0, vb + h)),
        ],
        out_specs=pl.BlockSpec((ATT_TQ, LANES), lambda h, i: (i, h)),
        out_shape=jax.ShapeDtypeStruct((t, DA_WIDTH), BF16),
        scratch_shapes=[
            pltpu.VMEM((2 * ATT_TQ, LANES), BF16),
            pltpu.VMEM((2 * ATT_TQ, LANES), F32),
            pltpu.VMEM((2 * ATT_TQ, LANES), F32),
            pltpu.VMEM((2 * ATT_TQ, LANES), F32),
        ],
        compiler_params=_params(("arbitrary", "arbitrary")),
        name="diff_attn",
    )(lam_pack, head_g, proj, proj, proj)


CV_TM = 512
CV_HALO = 32
CV_ROWS = 64


def _conv_kernel(a_ref, b_ref, w_ref, bias_ref, g_ref, beta_ref, o_ref, z_ref):
    i = pl.program_id(0)

    @pl.when(i == 0)
    def _():
        z_ref[0:CV_HALO, :] = jnp.zeros((CV_HALO, CV_WIDTH), F32)

    @pl.when(i > 0)
    def _():
        z_ref[0:CV_HALO, :] = z_ref[CV_TM:CV_TM + CV_HALO, :]

    a = a_ref[...].astype(F32)
    b = b_ref[...].astype(F32)
    z_ref[CV_HALO:CV_HALO + CV_TM, :] = a * jax.nn.sigmoid(b)

    base = CV_HALO - (CONV_TAPS - 1)
    for r in range(0, CV_TM, CV_ROWS):
        acc = jnp.zeros((CV_ROWS, CV_WIDTH), F32) + bias_ref[...]
        for j in range(CONV_TAPS):
            acc = acc + w_ref[j:j + 1, :] * z_ref[base + r + j:base + r + j + CV_ROWS, :]
        mu = jnp.mean(acc, axis=-1, keepdims=True)
        xc = acc - mu
        var = jnp.mean(xc * xc, axis=-1, keepdims=True)
        y = xc * lax.rsqrt(var + LN_EPS) * g_ref[...] + beta_ref[...]
        o_ref[r:r + CV_ROWS, :] = (y * jax.nn.sigmoid(y)).astype(BF16)


def _conv(proj, dw_w, dw_b, ln_g, ln_b, layer):
    t = proj.shape[0]
    ab, bb = COL_CVA // CV_WIDTH, COL_CVB // CV_WIDTH
    vec = _vec_spec(CV_WIDTH, layer, 1)
    return pl.pallas_call(
        _conv_kernel,
        grid=(t // CV_TM,),
        in_specs=[
            pl.BlockSpec((CV_TM, CV_WIDTH), lambda i: (i, ab)),
            pl.BlockSpec((CV_TM, CV_WIDTH), lambda i: (i, bb)),
            pl.BlockSpec((None, CONV_TAPS, CV_WIDTH), lambda i: (layer, 0, 0)),
            vec, vec, vec,
        ],
        out_specs=pl.BlockSpec((CV_TM, CV_WIDTH), lambda i: (i, 0)),
        out_shape=jax.ShapeDtypeStruct((t, CV_WIDTH), BF16),
        scratch_shapes=[pltpu.VMEM((CV_HALO + CV_TM, CV_WIDTH), F32)],
        compiler_params=_params(("arbitrary",)),
        name="conformer_conv",
    )(proj, proj, dw_w, dw_b, ln_g, ln_b)


SSM_CB = 128
SSM_RB = SSM_CB * SSM_CHUNK
SSM_SLAB = 2 * SSM_GROUP
SSM_SLABS = LANES // SSM_SLAB


def _pick_slabs(pieces, src_slab, lane_slab):
    out = None
    for k, piece in enumerate(pieces):
        shift = ((k - src_slab) * SSM_SLAB) % LANES
        moved = piece if shift == 0 else pltpu.roll(piece, shift, 1)
        out = moved if out is None else jnp.where(lane_slab == k, moved, out)
    return out


def _ssm_kernel(u_ref, t_ref, wre_ref, wim_ref, vre_ref, vim_ref, ar_ref, ai_ref, d_ref, y_ref,
                uf_ref, up_ref, sre_ref, sim_ref, yp_ref, yf_ref, xr_ref, xi_ref):
    npair, cb = SSM_PAIRS, SSM_CB

    @pl.when(pl.program_id(0) == 0)
    def _():
        xr_ref[...] = jnp.zeros(xr_ref.shape, F32)
        xi_ref[...] = jnp.zeros(xi_ref.shape, F32)

    ntile = SSM_WIDTH // LANES
    for j in range(ntile):
        uf_ref[j] = u_ref[:, j * LANES:(j + 1) * LANES].astype(F32)
    lane_slab = lax.shift_right_logical(lax.broadcasted_iota(jnp.int32, (cb, LANES), 1), 5)

    for q in range(SSM_CHUNK // SSM_SLABS):
        for tile in range(ntile):
            src = uf_ref.at[tile]
            pieces = [src[pl.ds(SSM_SLABS * q + k, cb, stride=SSM_CHUNK), :] for k in range(SSM_SLABS)]
            for slab in range(SSM_SLABS):
                up_ref[SSM_SLABS * tile + slab, :, q * LANES:(q + 1) * LANES] = _pick_slabs(
                    pieces, slab, lane_slab).astype(BF16)

    for p in range(npair):
        u = up_ref[p]
        sre_ref[pl.ds(p, cb, stride=npair), :] = jnp.dot(u, wre_ref[p], preferred_element_type=F32)
        sim_ref[pl.ds(p, cb, stride=npair), :] = jnp.dot(u, wim_ref[p], preferred_element_type=F32)

    ar = ar_ref[...]
    ai = ai_ref[...]

    def body(c, carry):
        xr, xi = carry
        off = pl.multiple_of(c * npair, npair)
        sr = sre_ref[pl.ds(off, npair), :]
        si = sim_ref[pl.ds(off, npair), :]
        sre_ref[pl.ds(off, npair), :] = xr
        sim_ref[pl.ds(off, npair), :] = xi
        return ar * xr - ai * xi + sr, ar * xi + ai * xr + si

    xr, xi = lax.fori_loop(0, cb, body, (xr_ref[...], xi_ref[...]), unroll=8)
    xr_ref[...] = xr
    xi_ref[...] = xi

    for p in range(npair):
        u = up_ref[p]
        sr = sre_ref[pl.ds(p, cb, stride=npair), :].astype(BF16)
        si = sim_ref[pl.ds(p, cb, stride=npair), :].astype(BF16)
        nt = (((1,), (1,)), ((), ()))
        y = jnp.dot(u, t_ref[p], preferred_element_type=F32)
        y = y + lax.dot_general(sr, vre_ref[p], nt, preferred_element_type=F32)
        yp_ref[p] = y + lax.dot_general(si, vim_ref[p], nt, preferred_element_type=F32)

    for tile in range(ntile):
        for q in range(SSM_CHUNK // SSM_SLABS):
            pieces = [yp_ref[SSM_SLABS * tile + k, :, q * LANES:(q + 1) * LANES] for k in range(SSM_SLABS)]
            dst = yf_ref.at[tile]
            for slab in range(SSM_SLABS):
                dst[pl.ds(SSM_SLABS * q + slab, cb, stride=SSM_CHUNK), :] = _pick_slabs(pieces, slab, lane_slab)

    for j in range(ntile):
        cols = slice(j * LANES, (j + 1) * LANES)
        y_ref[:, cols] = (yf_ref[j] + uf_ref[j] * d_ref[:, cols]).astype(BF16)


def _ssm(proj, mats):
    t = proj.shape[0]
    vm = pl.BlockSpec(memory_space=pltpu.VMEM)
    return pl.pallas_call(
        _ssm_kernel,
        grid=(t // SSM_RB,),
        in_specs=[pl.BlockSpec((SSM_RB, SSM_WIDTH), lambda i: (i, COL_SSM // SSM_WIDTH))] + [vm] * 8,
        out_specs=pl.BlockSpec((SSM_RB, SSM_WIDTH), lambda i: (i, 0)),
        out_shape=jax.ShapeDtypeStruct((t, SSM_WIDTH), BF16),
        scratch_shapes=[
            pltpu.VMEM((SSM_WIDTH // LANES, SSM_RB, LANES), F32),
            pltpu.VMEM((SSM_PAIRS, SSM_CB, SSM_PAIR_W), BF16),
            pltpu.VMEM((SSM_CB * SSM_PAIRS, LANES), F32),
            pltpu.VMEM((SSM_CB * SSM_PAIRS, LANES), F32),
            pltpu.VMEM((SSM_PAIRS, SSM_CB, SSM_PAIR_W), F32),
            pltpu.VMEM((SSM_WIDTH // LANES, SSM_RB, LANES), F32),
            pltpu.VMEM((SSM_PAIRS, LANES), F32),
            pltpu.VMEM((SSM_PAIRS, LANES), F32),
        ],
        compiler_params=_params(("arbitrary",)),
        name="s5_scan",
    )(proj, *mats)


def _ssm_gen_kernel(k_ref, bre_ref, bim_ref, cre_ref, cim_ref, pr_ref, pi_ref,
                    t_ref, wre_ref, wim_ref, vre_ref, vim_ref):
    k = k_ref[...]
    lane = lax.broadcasted_iota(jnp.int32, k.shape, 1)
    bre, bim, cre, cim = bre_ref[...], bim_ref[...], cre_ref[...], cim_ref[...]
    for s in range(SSM_CHUNK):
        rows = slice(s * SSM_SLAB, (s + 1) * SSM_SLAB)
        moved = k if s == 0 else pltpu.roll(k, s * SSM_SLAB, 1)
        t_ref[rows, :] = jnp.where(lane >= s * SSM_SLAB, moved, 0.0).astype(BF16)
        pr = pr_ref[SSM_CHUNK - 1 - s:SSM_CHUNK - s, :]
        pi = pi_ref[SSM_CHUNK - 1 - s:SSM_CHUNK - s, :]
        wre_ref[rows, :] = (bre * pr - bim * pi).astype(BF16)
        wim_ref[rows, :] = (bre * pi + bim * pr).astype(BF16)
        qr = pr_ref[s + 1:s + 2, :]
        qi = pi_ref[s + 1:s + 2, :]
        vre_ref[rows, :] = (cre * qr - cim * qi).astype(BF16)
        vim_ref[rows, :] = (-(cre * qi + cim * qr)).astype(BF16)


def _ssm_matrices(lam_re, lam_im, log_dt, b_re, b_im, c_re, c_im, d_skip):
    hp = lax.Precision.HIGHEST
    nl = lam_re.shape[0]
    lr = lam_re.astype(F32)
    li = lam_im.astype(F32)
    dt = jnp.exp(log_dt.astype(F32))[..., None]
    mag = jnp.exp(lr * dt)
    ab_re = mag * jnp.cos(li * dt)
    ab_im = mag * jnp.sin(li * dt)
    den = lr * lr + li * li
    f_re = ((ab_re - 1.0) * lr + ab_im * li) / den
    f_im = (ab_im * lr - (ab_re - 1.0) * li) / den
    br = b_re.astype(F32)
    bi = b_im.astype(F32)
    bb_re = f_re[..., None] * br - f_im[..., None] * bi
    bb_im = f_re[..., None] * bi + f_im[..., None] * br
    tau = jnp.arange(SSM_CHUNK + 1, dtype=F32)[None, None, :, None]
    pmag = jnp.exp(tau * (lr * dt)[:, :, None, :])
    pw_re = pmag * jnp.cos(tau * (li * dt)[:, :, None, :])
    pw_im = pmag * jnp.sin(tau * (li * dt)[:, :, None, :])
    cr = c_re.astype(F32)
    ci = c_im.astype(F32)
    ca_re = cr[:, :, None] * pw_re[:, :, :SSM_CHUNK, None, :] - ci[:, :, None] * pw_im[:, :, :SSM_CHUNK, None, :]
    ca_im = cr[:, :, None] * pw_im[:, :, :SSM_CHUNK, None, :] + ci[:, :, None] * pw_re[:, :, :SSM_CHUNK, None, :]
    kk = (jnp.einsum('lgthp,lgpk->lgthk', ca_re, bb_re, precision=hp)
          - jnp.einsum('lgthp,lgpk->lgthk', ca_im, bb_im, precision=hp))
    eye2 = jnp.eye(2, dtype=F32)

    def paired(a):
        return a.reshape((nl, SSM_PAIRS, 2) + a.shape[2:])

    kcat = jnp.einsum('lpexhk,ef->lpekxfh', paired(kk), eye2).reshape(nl, SSM_PAIRS, SSM_SLAB, SSM_PAIR_W)
    bbd_re = jnp.einsum('lpenk,ef->lpekfn', paired(bb_re), eye2).reshape(nl, SSM_PAIRS, SSM_SLAB, LANES)
    bbd_im = jnp.einsum('lpenk,ef->lpekfn', paired(bb_im), eye2).reshape(nl, SSM_PAIRS, SSM_SLAB, LANES)
    cbd_re = jnp.einsum('lpehn,ef->lpehfn', paired(cr), eye2).reshape(nl, SSM_PAIRS, SSM_SLAB, LANES)
    cbd_im = jnp.einsum('lpehn,ef->lpehfn', paired(ci), eye2).reshape(nl, SSM_PAIRS, SSM_SLAB, LANES)
    pwp_re = paired(pw_re).transpose(0, 1, 3, 2, 4).reshape(nl, SSM_PAIRS, SSM_CHUNK + 1, LANES)
    pwp_im = paired(pw_im).transpose(0, 1, 3, 2, 4).reshape(nl, SSM_PAIRS, SSM_CHUNK + 1, LANES)

    def spec(r, c):
        return pl.BlockSpec((None, None, r, c), lambda l, p: (l, p, 0, 0))

    wide = jax.ShapeDtypeStruct((nl, SSM_PAIRS, SSM_PAIR_W, SSM_PAIR_W), BF16)
    tall = jax.ShapeDtypeStruct((nl, SSM_PAIRS, SSM_PAIR_W, LANES), BF16)
    tm, w_re, w_im, vt_re, vt_im = pl.pallas_call(
        _ssm_gen_kernel,
        grid=(nl, SSM_PAIRS),
        in_specs=[spec(SSM_SLAB, SSM_PAIR_W)] + [spec(SSM_SLAB, LANES)] * 4 + [spec(SSM_CHUNK + 1, LANES)] * 2,
        out_specs=[spec(SSM_PAIR_W, SSM_PAIR_W)] + [spec(SSM_PAIR_W, LANES)] * 4,
        out_shape=[wide, tall, tall, tall, tall],
        compiler_params=_params(("arbitrary", "arbitrary")),
        name="s5_matrices",
    )(kcat, bbd_re, bbd_im, cbd_re, cbd_im, pwp_re, pwp_im)
    a_re = pwp_re[:, :, SSM_CHUNK]
    a_im = pwp_im[:, :, SSM_CHUNK]
    d_t = d_skip.astype(F32).reshape(nl, 1, SSM_WIDTH)
    return tm, w_re, w_im, vt_re, vt_im, a_re, a_im, d_t


MG_TM = 512
MG_TN = 512


def _merge_mix_kernel(oa_ref, zb_ref, yc_ref, ga_ref, gb_ref, gc_ref, wda_ref, wcv_ref, wglu_ref,
                      bglu_ref, wso_ref, wmix_ref, x_ref, o_ref, sc_ref):
    n = pl.program_id(1)

    @pl.when(n == 0)
    def _():
        glu = jnp.dot(yc_ref[...], wglu_ref[...], preferred_element_type=F32) + bglu_ref[...]
        sc_ref[...] = (glu[:, :SSM_WIDTH] * jax.nn.sigmoid(glu[:, SSM_WIDTH:])).astype(BF16)
        o_ref[...] = x_ref[...]

    y_a = jnp.dot(oa_ref[...], wda_ref[...], preferred_element_type=F32)
    y_b = jnp.dot(zb_ref[...], wcv_ref[...], preferred_element_type=F32)
    y_c = jnp.dot(sc_ref[...], wso_ref[...], preferred_element_type=F32)
    merged = (jax.nn.sigmoid(ga_ref[...].astype(F32)) * y_a
              + jax.nn.sigmoid(gb_ref[...].astype(F32)) * y_b
              + jax.nn.sigmoid(gc_ref[...].astype(F32)) * y_c)
    o_ref[...] += jnp.dot(merged.astype(BF16), wmix_ref[...], preferred_element_type=F32)


def _merge_mix(o_a, z_b, y_c, proj, w_da, w_cv, w_glu, b_glu, w_so, w_mix, x, layer):
    t = o_a.shape[0]
    ga, gb, gc = COL_GA // MG_TN, COL_GB // MG_TN, COL_GC // MG_TN

    def wcol(k):
        return pl.BlockSpec((None, k, MG_TN), lambda i, n: (layer, 0, n))

    return pl.pallas_call(
        _merge_mix_kernel,
        grid=(t // MG_TM, D_MODEL // MG_TN),
        in_specs=[
            pl.BlockSpec((MG_TM, DA_WIDTH), lambda i, n: (i, 0)),
            pl.BlockSpec((MG_TM, CV_WIDTH), lambda i, n: (i, 0)),
            pl.BlockSpec((MG_TM, SSM_WIDTH), lambda i, n: (i, 0)),
            pl.BlockSpec((MG_TM, MG_TN), lambda i, n: (i, ga + n)),
            pl.BlockSpec((MG_TM, MG_TN), lambda i, n: (i, gb + n)),
            pl.BlockSpec((MG_TM, MG_TN), lambda i, n: (i, gc + n)),
            wcol(DA_WIDTH), wcol(CV_WIDTH),
            pl.BlockSpec((None, SSM_WIDTH, 2 * SSM_WIDTH), lambda i, n: (layer, 0, 0)),
            _vec_spec(2 * SSM_WIDTH, layer, 2),
            wcol(SSM_WIDTH),
            pl.BlockSpec((None, MG_TN, D_MODEL), lambda i, n: (layer, n, 0)),
            pl.BlockSpec((MG_TM, D_MODEL), lambda i, n: (i, 0)),
        ],
        out_specs=pl.BlockSpec((MG_TM, D_MODEL), lambda i, n: (i, 0)),
        out_shape=jax.ShapeDtypeStruct((t, D_MODEL), F32),
        scratch_shapes=[pltpu.VMEM((MG_TM, SSM_WIDTH), BF16)],
        compiler_params=_params(("arbitrary", "arbitrary")),
        name="merge_mix",
    )(o_a, z_b, y_c, proj, proj, proj, w_da, w_cv, w_glu, b_glu, w_so, w_mix, x)


def _norm_matmul_kernel(x_ref, g_ref, w_ref, o_ref):
    h = _rms(x_ref[...], g_ref[...], RMS_EPS).astype(BF16)
    o_ref[...] = jnp.dot(h, w_ref[...], preferred_element_type=F32).astype(o_ref.dtype)


def _mem_kv(mem, g_all, w_all, layer):
    m = mem.shape[0]
    n = 2 * XA_WIDTH
    return pl.pallas_call(
        _norm_matmul_kernel,
        grid=(1,),
        in_specs=[
            pl.BlockSpec((m, D_MODEL), lambda i: (0, 0)),
            _vec_spec(D_MODEL, layer, 1),
            pl.BlockSpec((None, D_MODEL, n), lambda i: (layer, 0, 0)),
        ],
        out_specs=pl.BlockSpec((m, n), lambda i: (0, 0)),
        out_shape=jax.ShapeDtypeStruct((m, n), BF16),
        compiler_params=_params(("arbitrary",)),
        name="mem_kv",
    )(mem, g_all, w_all)


XA_TM = 512


def _xattn_kernel(x_ref, g_ref, wq_ref, kv_ref, wo_ref, o_ref):
    x = x_ref[...]
    h = _rms(x, g_ref[...], RMS_EPS).astype(BF16)
    q = jnp.dot(h, wq_ref[...], preferred_element_type=F32).astype(BF16)
    heads = []
    for hd in range(XA_HEADS):
        lo = hd * XA_HEAD_DIM
        k = kv_ref[:, lo:lo + XA_HEAD_DIM]
        v = kv_ref[:, XA_WIDTH + lo:XA_WIDTH + lo + XA_HEAD_DIM]
        s = lax.dot_general(q[:, lo:lo + XA_HEAD_DIM], k, (((1,), (1,)), ((), ())),
                            preferred_element_type=F32) * (XA_HEAD_DIM ** -0.5)
        m = jnp.max(s, axis=-1, keepdims=True)
        e = jnp.exp(s - m)
        p = e / jnp.sum(e, axis=-1, keepdims=True)
        heads.append(jnp.dot(p.astype(BF16), v, preferred_element_type=F32).astype(BF16))
    o = jnp.concatenate(heads, axis=-1)
    o_ref[...] = x + jnp.dot(o, wo_ref[...], preferred_element_type=F32)


def _xattn(x, g_all, wq_all, kv, wo_all, layer):
    t = x.shape[0]
    return pl.pallas_call(
        _xattn_kernel,
        grid=(t // XA_TM,),
        in_specs=[
            pl.BlockSpec((XA_TM, D_MODEL), lambda i: (i, 0)),
            _vec_spec(D_MODEL, layer, 1),
            pl.BlockSpec((None, D_MODEL, XA_WIDTH), lambda i: (layer, 0, 0)),
            pl.BlockSpec((MEM_LEN, 2 * XA_WIDTH), lambda i: (0, 0)),
            pl.BlockSpec((None, XA_WIDTH, D_MODEL), lambda i: (layer, 0, 0)),
        ],
        out_specs=pl.BlockSpec((XA_TM, D_MODEL), lambda i: (i, 0)),
        out_shape=jax.ShapeDtypeStruct((t, D_MODEL), F32),
        compiler_params=_params(("arbitrary",)),
        name="mem_xattn",
    )(x, g_all, wq_all, kv, wo_all)


RT_TM = 512
_E_LANE0 = MOE_GROUPS


def _router_kernel(x_ref, g_ref, w_ref, b_ref, meta_ref, cnt_ref, run_ref):
    i = pl.program_id(0)

    @pl.when(i == 0)
    def _():
        run_ref[...] = jnp.zeros(run_ref.shape, F32)

    h = _rms(x_ref[...], g_ref[...], RMS_EPS).astype(BF16)
    logits = jnp.dot(h, w_ref[...], preferred_element_type=F32) + b_ref[...]
    lane = lax.broadcasted_iota(jnp.int32, logits.shape, 1).astype(F32)
    neg = jnp.float32(-jnp.inf)
    big = jnp.float32(LANES)

    def first_argmax(vals):
        top = jnp.max(vals, axis=-1, keepdims=True)
        idx = jnp.min(jnp.where(vals == top, lane, big), axis=-1, keepdims=True)
        return top, idx

    gl = jnp.where(lane < MOE_GROUPS, logits, neg)
    g_top, g_idx = first_argmax(gl)
    g_w = 1.0 / jnp.sum(jnp.exp(gl - g_top), axis=-1, keepdims=True)
    e_lane = lane - _E_LANE0
    in_group = jnp.logical_and(e_lane >= g_idx * MOE_PER_GROUP, e_lane < (g_idx + 1) * MOE_PER_GROUP)
    el = jnp.where(in_group, logits, neg)
    v1, i1 = first_argmax(el)
    el2 = jnp.where(lane == i1, neg, el)
    v2, i2 = first_argmax(el2)
    e2 = jnp.exp(v2 - v1)
    w1 = 1.0 / (1.0 + e2)
    w2 = e2 / (1.0 + e2)
    oh1 = (lane == i1).astype(F32)
    oh2 = (lane == i2).astype(F32)
    both = oh1 + oh2
    row = lax.broadcasted_iota(jnp.int32, (RT_TM, RT_TM), 0)
    col = lax.broadcasted_iota(jnp.int32, (RT_TM, RT_TM), 1)
    earlier = jnp.where(col < row, 1.0, 0.0).astype(BF16)
    before = jnp.dot(earlier, both.astype(BF16), preferred_element_type=F32) + run_ref[...]
    rank1 = jnp.sum(oh1 * before, axis=-1, keepdims=True)
    rank2 = jnp.sum(oh2 * before, axis=-1, keepdims=True)
    run_ref[...] += jnp.sum(both, axis=0, keepdims=True)
    cnt_ref[...] = run_ref[...]
    meta = jnp.where(lane == 0.0, i1 - _E_LANE0, 0.0)
    for k, val in enumerate((i2 - _E_LANE0, rank1, rank2, w1 * g_w, w2 * g_w), start=1):
        meta = jnp.where(lane == float(k), val, meta)
    meta_ref[...] = meta


def _router(x, g_all, w_r, b_r, layer):
    t = x.shape[0]
    return pl.pallas_call(
        _router_kernel,
        grid=(t // RT_TM,),
        in_specs=[
            pl.BlockSpec((RT_TM, D_MODEL), lambda i: (i, 0)),
            _vec_spec(D_MODEL, layer, 1),
            pl.BlockSpec((None, D_MODEL, LANES), lambda i: (layer, 0, 0)),
            _vec_spec(LANES, layer, 1),
        ],
        out_specs=[
            pl.BlockSpec((RT_TM, LANES), lambda i: (i, 0)),
            pl.BlockSpec((1, LANES), lambda i: (0, 0)),
        ],
        out_shape=[jax.ShapeDtypeStruct((t, LANES), F32), jax.ShapeDtypeStruct((1, LANES), F32)],
        scratch_shapes=[pltpu.VMEM((1, LANES), F32)],
        compiler_params=_params(("arbitrary",)),
        name="moe_router",
    )(x, g_all, w_r, b_r)


EX_TM = 256
EX_ROWS = 2 * SEQ
EX_TILES = EX_ROWS // EX_TM
EX_STEPS = EX_TILES + MOE_EXPERTS - 1


def _row_copy(src, src_row, dst, dst_row, sem):
    return pltpu.make_async_copy(src.at[pl.ds(src_row, 1), :], dst.at[pl.ds(dst_row, 1), :], sem)


DP_TM = 256


def _dispatch_kernel(p1_ref, p2_ref, x_ref, xs_hbm, sem):
    base = pl.program_id(0) * DP_TM

    def issue(j, c):
        _row_copy(x_ref, j, xs_hbm, p1_ref[base + j], sem).start()
        _row_copy(x_ref, j, xs_hbm, p2_ref[base + j], sem).start()
        return c

    lax.fori_loop(0, DP_TM, issue, 0, unroll=8)

    def wait(j, c):
        _row_copy(x_ref, j, xs_hbm, 0, sem).wait()
        _row_copy(x_ref, j, xs_hbm, 0, sem).wait()
        return c

    lax.fori_loop(0, DP_TM, wait, 0, unroll=8)


def _dispatch(pos1, pos2, x):
    t = x.shape[0]
    grid_spec = pltpu.PrefetchScalarGridSpec(
        num_scalar_prefetch=2,
        grid=(t // DP_TM,),
        in_specs=[pl.BlockSpec((DP_TM, D_MODEL), lambda i, p1, p2: (i, 0))],
        out_specs=pl.BlockSpec(memory_space=pl.ANY),
        scratch_shapes=[pltpu.SemaphoreType.DMA(())],
    )
    return pl.pallas_call(
        _dispatch_kernel,
        grid_spec=grid_spec,
        out_shape=jax.ShapeDtypeStruct((EX_ROWS, D_MODEL), F32),
        compiler_params=_params(("arbitrary",)),
        name="moe_dispatch",
    )(pos1, pos2, x)


def _experts_kernel(se_ref, st_ref, lo_ref, hi_ref, first_ref, wnew_ref, x_ref, g_ref, wg_ref, wu_ref,
                    wd_ref, o_ref, wgb_ref, wub_ref, wdb_ref):
    s = pl.program_id(0)
    lo = lo_ref[s]
    hi = hi_ref[s]

    @pl.when(wnew_ref[s] == 1)
    def _():
        wgb_ref[...] = wg_ref[...].astype(BF16)
        wub_ref[...] = wu_ref[...].astype(BF16)
        wdb_ref[...] = wd_ref[...].astype(BF16)

    @pl.when(hi > lo)
    def _():
        h = _rms(x_ref[...], g_ref[...], RMS_EPS).astype(BF16)
        gate = jnp.dot(h, wgb_ref[...], preferred_element_type=F32)
        up = jnp.dot(h, wub_ref[...], preferred_element_type=F32)
        act = (gate * jax.nn.sigmoid(gate) * up).astype(BF16)
        res = jnp.dot(act, wdb_ref[...], preferred_element_type=F32)
        row = lax.broadcasted_iota(jnp.int32, (EX_TM, 1), 0)
        mine = jnp.logical_and(row >= lo, row < hi)

        @pl.when(first_ref[s] == 1)
        def _():
            o_ref[...] = jnp.where(mine, res, 0.0)

        @pl.when(first_ref[s] == 0)
        def _():
            o_ref[...] = jnp.where(mine, res, o_ref[...])


def _experts(plan, xs_sorted, g_all, wg_all, wu_all, wd_all, layer):
    def wspec(k, n):
        return pl.BlockSpec((None, None, k, n), lambda s, se, st, lo, hi, fi, wn: (layer, se[s], 0, 0))

    grid_spec = pltpu.PrefetchScalarGridSpec(
        num_scalar_prefetch=6,
        grid=(EX_STEPS,),
        in_specs=[
            pl.BlockSpec((EX_TM, D_MODEL), lambda s, se, st, lo, hi, fi, wn: (st[s], 0)),
            pl.BlockSpec((None, 1, D_MODEL), lambda s, se, st, lo, hi, fi, wn: (layer, 0, 0)),
            wspec(D_MODEL, MOE_FF), wspec(D_MODEL, MOE_FF), wspec(MOE_FF, D_MODEL),
        ],
        out_specs=pl.BlockSpec((EX_TM, D_MODEL), lambda s, se, st, lo, hi, fi, wn: (st[s], 0)),
        scratch_shapes=[pltpu.VMEM((D_MODEL, MOE_FF), BF16), pltpu.VMEM((D_MODEL, MOE_FF), BF16),
                        pltpu.VMEM((MOE_FF, D_MODEL), BF16)],
    )
    return pl.pallas_call(
        _experts_kernel,
        grid_spec=grid_spec,
        out_shape=jax.ShapeDtypeStruct((EX_ROWS, D_MODEL), F32),
        compiler_params=_params(("arbitrary",)),
        name="moe_experts",
    )(*plan, xs_sorted, g_all, wg_all, wu_all, wd_all)


CB_TM = 256


def _combine_kernel(p1_ref, p2_ref, y_hbm, x_ref, meta_ref, o_ref, y1buf, y2buf, sem):
    i = pl.program_id(0)
    slot = lax.rem(i, 2)

    def fetch(tile, slot_):
        base = tile * CB_TM

        def issue(j, c):
            _row_copy(y_hbm, p1_ref[base + j], y1buf.at[slot_], j, sem.at[slot_]).start()
            _row_copy(y_hbm, p2_ref[base + j], y2buf.at[slot_], j, sem.at[slot_]).start()
            return c

        lax.fori_loop(0, CB_TM, issue, 0, unroll=8)

    @pl.when(i == 0)
    def _():
        fetch(0, 0)

    @pl.when(i + 1 < pl.num_programs(0))
    def _():
        fetch(i + 1, 1 - slot)

    def wait(j, c):
        _row_copy(y_hbm, 0, y1buf.at[slot], j, sem.at[slot]).wait()
        _row_copy(y_hbm, 0, y2buf.at[slot], j, sem.at[slot]).wait()
        return c

    lax.fori_loop(0, CB_TM, wait, 0, unroll=8)
    meta = meta_ref[...]
    o_ref[...] = x_ref[...] + meta[:, 4:5] * y1buf[slot] + meta[:, 5:6] * y2buf[slot]


def _combine(pos1, pos2, ys, x, meta):
    t = x.shape[0]
    grid_spec = pltpu.PrefetchScalarGridSpec(
        num_scalar_prefetch=2,
        grid=(t // CB_TM,),
        in_specs=[
            pl.BlockSpec(memory_space=pl.ANY),
            pl.BlockSpec((CB_TM, D_MODEL), lambda i, p1, p2: (i, 0)),
            pl.BlockSpec((CB_TM, LANES), lambda i, p1, p2: (i, 0)),
        ],
        out_specs=pl.BlockSpec((CB_TM, D_MODEL), lambda i, p1, p2: (i, 0)),
        scratch_shapes=[pltpu.VMEM((2, CB_TM, D_MODEL), F32), pltpu.VMEM((2, CB_TM, D_MODEL), F32),
                        pltpu.SemaphoreType.DMA((2,))],
    )
    return pl.pallas_call(
        _combine_kernel,
        grid_spec=grid_spec,
        out_shape=jax.ShapeDtypeStruct((t, D_MODEL), F32),
        compiler_params=_params(("arbitrary",)),
        name="moe_combine",
    )(pos1, pos2, ys, x, meta)


def _dispatch_plan(meta, cnt):
    t = meta.shape[0]
    e1 = meta[:, 0].astype(jnp.int32)
    e2 = meta[:, 1].astype(jnp.int32)
    r1 = meta[:, 2].astype(jnp.int32)
    r2 = meta[:, 3].astype(jnp.int32)
    counts = cnt[0, _E_LANE0:_E_LANE0 + MOE_EXPERTS].astype(jnp.int32)
    ends = jnp.cumsum(counts)
    starts = ends - counts
    pos1 = starts[e1] + r1
    pos2 = starts[e2] + r2
    t_lo = starts // EX_TM
    t_hi = (ends + EX_TM - 1) // EX_TM
    nsteps = jnp.where(counts > 0, t_hi - t_lo, 0)
    step_end = jnp.cumsum(nsteps)
    step_start = step_end - nsteps
    s = jnp.arange(EX_STEPS, dtype=jnp.int32)
    se = jnp.minimum(jnp.sum((s[:, None] >= step_end[None, :]).astype(jnp.int32), axis=1), MOE_EXPERTS - 1)
    valid = s < step_end[-1]
    st = jnp.where(valid, t_lo[se] + s - step_start[se], EX_TILES - 1)
    lo = jnp.where(valid, jnp.clip(starts[se] - st * EX_TM, 0, EX_TM), 0)
    hi = jnp.where(valid, jnp.clip(ends[se] - st * EX_TM, 0, EX_TM), 0)
    first = jnp.concatenate([jnp.ones((1,), jnp.int32), (st[1:] != st[:-1]).astype(jnp.int32)])
    wnew = jnp.concatenate([jnp.ones((1,), jnp.int32), (se[1:] != se[:-1]).astype(jnp.int32)])
    plan = tuple(a.astype(jnp.int32) for a in (se, st, lo, hi, first, wnew))
    return pos1.astype(jnp.int32), pos2.astype(jnp.int32), plan


FN_TM = 512


def _final_norm_kernel(x_ref, g_ref, o_ref):
    o_ref[...] = _rms(x_ref[...], g_ref[...], RMS_EPS)


def _final_norm(x, g):
    t = x.shape[0]
    return pl.pallas_call(
        _final_norm_kernel,
        grid=(t // FN_TM,),
        in_specs=[pl.BlockSpec((FN_TM, D_MODEL), lambda i: (i, 0)),
                  pl.BlockSpec((1, D_MODEL), lambda i: (0, 0))],
        out_specs=pl.BlockSpec((FN_TM, D_MODEL), lambda i: (i, 0)),
        out_shape=jax.ShapeDtypeStruct((t, D_MODEL), F32),
        compiler_params=_params(("arbitrary",)),
        name="final_norm",
    )(x, g)


def kernel(x, mem, positions, norm_mix, w_in, da_lam_q1, da_lam_k1, da_lam_q2, da_lam_k2, da_head_norm, w_da_out, cv_dw_w, cv_dw_b, cv_ln_g, cv_ln_b, w_cv_out, ssm_lam_re, ssm_lam_im, ssm_log_dt, ssm_b_re, ssm_b_im, ssm_c_re, ssm_c_im, ssm_d, w_ssm_glu, b_ssm_glu, w_ssm_out, w_mix_out, norm_xa, norm_mem, w_xa_q, w_xa_kv, w_xa_out, norm_ffn, w_router_group, b_router_group, w_router_expert, b_router_expert, w_exp_gate, w_exp_up, w_exp_down, norm_final):
    bsz, seq, _ = x.shape
    assert bsz == 1 and seq == SEQ
    nl = w_in.shape[0]
    xs = x.reshape(seq, D_MODEL).astype(F32)
    mem2 = mem.reshape(MEM_LEN, D_MODEL).astype(F32)

    inv_freq = ROPE_THETA ** (-jnp.arange(0, DA_HEAD_DIM, 2, dtype=F32) / DA_HEAD_DIM)
    ang = positions.reshape(seq).astype(F32)[:, None] * inv_freq
    cos = jnp.cos(ang)
    sin = jnp.sin(ang)
    cos_t = jnp.concatenate([cos, cos, cos, cos], axis=-1)
    sin_t = jnp.concatenate([-sin, sin, -sin, sin], axis=-1)

    w_in_b = w_in.astype(BF16)
    w_da_b = w_da_out.astype(BF16)
    w_cv_b = w_cv_out.astype(BF16)
    w_glu_b = w_ssm_glu.astype(BF16)
    w_so_b = w_ssm_out.astype(BF16)
    w_mix_b = w_mix_out.astype(BF16)
    w_xq_b = w_xa_q.astype(BF16)
    w_xkv_b = w_xa_kv.astype(BF16)
    w_xo_b = w_xa_out.astype(BF16)
    pad = LANES - MOE_GROUPS - MOE_EXPERTS
    w_r = jnp.concatenate([w_router_group, w_router_expert,
                           jnp.zeros((nl, D_MODEL, pad), F32)], axis=-1).astype(BF16)
    b_r = jnp.concatenate([b_router_group, b_router_expert, jnp.zeros((nl, pad), F32)], axis=-1).astype(F32)

    def vec3(a):
        return a.astype(F32).reshape(nl, 1, a.shape[-1])

    norm_mix, da_head_norm, cv_dw_b, cv_ln_g, cv_ln_b, b_ssm_glu, norm_xa, norm_mem, norm_ffn, b_r = map(
        vec3, (norm_mix, da_head_norm, cv_dw_b, cv_ln_g, cv_ln_b, b_ssm_glu, norm_xa, norm_mem, norm_ffn, b_r))
    cv_dw_w = cv_dw_w.astype(F32)

    lam_inits = jnp.asarray([0.8 - 0.6 * math.exp(-0.3 * l) for l in range(nl)], F32)
    lam_pack = jnp.stack([da_lam_q1, da_lam_k1, da_lam_q2, da_lam_k2], axis=1).astype(F32)
    lam_pack = jnp.concatenate(
        [lam_pack, jnp.broadcast_to(lam_inits[:, None, None], (nl, 4, DA_HEAD_DIM))], axis=1)

    ssm_mats = _ssm_matrices(ssm_lam_re, ssm_lam_im, ssm_log_dt, ssm_b_re, ssm_b_im,
                             ssm_c_re, ssm_c_im, ssm_d)

    for l in range(nl):
        proj = _inproj(xs, norm_mix, w_in_b, l, cos_t, sin_t)
        o_a = _attention(proj, lam_pack, da_head_norm, l)
        z_b = _conv(proj, cv_dw_w, cv_dw_b, cv_ln_g, cv_ln_b, l)
        y_c = _ssm(proj, [m[l] for m in ssm_mats])
        xs = _merge_mix(o_a, z_b, y_c, proj, w_da_b, w_cv_b, w_glu_b, b_ssm_glu, w_so_b, w_mix_b, xs, l)
        kv = _mem_kv(mem2, norm_mem, w_xkv_b, l)
        xs = _xattn(xs, norm_xa, w_xq_b, kv, w_xo_b, l)
        meta, cnt = _router(xs, norm_ffn, w_r, b_r, l)
        pos1, pos2, plan = _dispatch_plan(meta, cnt)
        xs_sorted = _dispatch(pos1, pos2, xs)
        ys = _experts(plan, xs_sorted, norm_ffn, w_exp_gate, w_exp_up, w_exp_down, l)
        xs = _combine(pos1, pos2, ys, xs, meta)
    out = _final_norm(xs, norm_final.reshape(1, D_MODEL))
    return out.reshape(bsz, seq, D_MODEL)
```

```python
import functools
import math

import jax
import jax.numpy as jnp
from jax import lax
from jax.experimental import pallas as pl
from jax.experimental.pallas import tpu as pltpu

F32 = jnp.float32
BF16 = jnp.bfloat16

D_MODEL = 2048
SEQ = 8192
DEPTH = 4
MEM_LEN = 256
DA_HEADS = 8
DA_HEAD_DIM = 64
DA_V_DIM = 128
DA_WIDTH = 1024
ROPE_THETA = 10000.0
CV_WIDTH = 512
CONV_TAPS = 31
SSM_WIDTH = 512
SSM_GROUP = 16
SSM_GROUPS = 32
SSM_STATE = 64
XA_HEADS = 4
XA_HEAD_DIM = 128
XA_WIDTH = 512
MOE_GROUPS = 4
MOE_PER_GROUP = 4
MOE_EXPERTS = 16
MOE_FF = 512
RMS_EPS = 1e-6
HEAD_NORM_EPS = 1e-5
LN_EPS = 1e-5

COL_Q = 0
COL_K = COL_Q + 1024
COL_V = COL_K + 1024
COL_CVA = COL_V + 1024
COL_CVB = COL_CVA + CV_WIDTH
COL_SSM = COL_CVB + CV_WIDTH
COL_GA = COL_SSM + SSM_WIDTH
COL_GB = COL_GA + D_MODEL
COL_GC = COL_GB + D_MODEL
IN_TOTAL = COL_GC + D_MODEL

LANES = 128
VMEM_LIMIT = 56 * 1024 * 1024

SSM_CHUNK = 16
SSM_NCHUNK = SEQ // SSM_CHUNK
SSM_PAIRS = SSM_GROUPS // 2
SSM_PAIR_W = 2 * SSM_CHUNK * SSM_GROUP


def _params(sem, vmem=VMEM_LIMIT):
    return pltpu.CompilerParams(dimension_semantics=sem, vmem_limit_bytes=vmem)


def _vec_spec(width, layer, ngrid):
    if ngrid == 1:
        return pl.BlockSpec((None, 1, width), lambda i: (layer, 0, 0))
    return pl.BlockSpec((None, 1, width), lambda i, j: (layer, 0, 0))


def _rms(xf, g, eps):
    ms = jnp.mean(xf * xf, axis=-1, keepdims=True)
    return xf * lax.rsqrt(ms + eps) * g


INP_TM = 1024
INP_TN = 512
_Q_TILE0 = COL_Q // INP_TN
_K_TILE0 = COL_K // INP_TN
_V_TILE0 = COL_V // INP_TN


def _inproj_kernel(x_ref, g_ref, w_ref, cos_ref, sin_ref, o_ref, h_ref):
    j = pl.program_id(1)

    @pl.when(j == 0)
    def _():
        h_ref[...] = _rms(x_ref[...], g_ref[...], RMS_EPS).astype(BF16)

    acc = jnp.dot(h_ref[...], w_ref[...].astype(BF16), preferred_element_type=F32)
    is_rope = jnp.logical_and(j >= _Q_TILE0, j < _V_TILE0)

    @pl.when(is_rope)
    def _():
        scale = jnp.where(j < _K_TILE0, math.log2(math.e) * DA_HEAD_DIM ** -0.5, 1.0).astype(F32)
        cos = cos_ref[...] * scale
        sin = sin_ref[...] * scale
        lane = lax.broadcasted_iota(jnp.int32, (INP_TM, LANES), 1)
        first_half = (lane % DA_HEAD_DIM) < (DA_HEAD_DIM // 2)
        for c in range(INP_TN // LANES):
            t = acc[:, c * LANES:(c + 1) * LANES]
            swapped = jnp.where(first_half, pltpu.roll(t, LANES - 32, 1), pltpu.roll(t, 32, 1))
            o_ref[:, c * LANES:(c + 1) * LANES] = (t * cos + swapped * sin).astype(BF16)

    @pl.when(jnp.logical_not(is_rope))
    def _():
        o_ref[...] = acc.astype(BF16)


def _inproj(x, g, w_all, layer, cos_t, sin_t):
    t = x.shape[0]
    return pl.pallas_call(
        _inproj_kernel,
        grid=(t // INP_TM, IN_TOTAL // INP_TN),
        in_specs=[
            pl.BlockSpec((INP_TM, D_MODEL), lambda i, j: (i, 0)),
            _vec_spec(D_MODEL, layer, 2),
            pl.BlockSpec((None, D_MODEL, INP_TN), lambda i, j: (layer, 0, j)),
            pl.BlockSpec((INP_TM, LANES), lambda i, j: (i, 0)),
            pl.BlockSpec((INP_TM, LANES), lambda i, j: (i, 0)),
        ],
        out_specs=pl.BlockSpec((INP_TM, INP_TN), lambda i, j: (i, j)),
        out_shape=jax.ShapeDtypeStruct((t, IN_TOTAL), BF16),
        scratch_shapes=[pltpu.VMEM((INP_TM, D_MODEL), BF16)],
        compiler_params=_params(("arbitrary", "arbitrary")),
        name="inproj",
    )(x, g, w_all, cos_t, sin_t)


ATT_TQ = 1024
ATT_TK = 1024


ATT_RG = 1024
ATT_DRG = 512


def _attn_kernel(lam_ref, g_ref, q_ref, k_ref, v_ref, o_ref, qs_ref, m_ref, l_ref, acc_ref):
    i = pl.program_id(1)
    tq = ATT_TQ
    q = q_ref[...]
    lane = lax.broadcasted_iota(jnp.int32, q.shape, 1)
    zero = jnp.zeros_like(q)
    qs_ref[0:tq, :] = jnp.where(lane < DA_HEAD_DIM, q, zero)
    qs_ref[tq:2 * tq, :] = jnp.where(lane >= DA_HEAD_DIM, q, zero)
    m_ref[...] = jnp.full(m_ref.shape, -jnp.inf, F32)
    l_ref[...] = jnp.zeros(l_ref.shape, F32)
    acc_ref[...] = jnp.zeros(acc_ref.shape, F32)

    def update_rows(r0, nrows, k, v, mask):
        n = k.shape[0]
        rows = slice(r0, r0 + nrows)
        s = lax.dot_general(qs_ref[rows, :], k, (((1,), (1,)), ((), ())), preferred_element_type=F32)
        if mask is not None:
            s = jnp.where(mask, s, -jnp.inf)
        tiles = [s[:, t * LANES:(t + 1) * LANES] for t in range(n // LANES)]
        mc = functools.reduce(jnp.maximum, tiles)
        m_old = m_ref[rows, :]
        m_new = jnp.maximum(m_old, jnp.max(mc, axis=1, keepdims=True))
        alpha = jnp.exp2(m_old - m_new)
        p_tiles = [jnp.exp2(t - m_new) for t in tiles]
        l_ref[rows, :] = alpha * l_ref[rows, :] + functools.reduce(jnp.add, p_tiles)
        p = jnp.concatenate(p_tiles, axis=1).astype(BF16)
        acc_ref[rows, :] = alpha * acc_ref[rows, :] + jnp.dot(p, v, preferred_element_type=F32)
        m_ref[rows, :] = m_new

    def body(c, carry):
        off = pl.multiple_of(c * ATT_TK, ATT_TK)
        k = k_ref[pl.ds(off, ATT_TK), :]
        v = v_ref[pl.ds(off, ATT_TK), :]
        for r0 in range(0, 2 * tq, ATT_RG):
            update_rows(r0, ATT_RG, k, v, None)
        return carry

    lax.fori_loop(0, i * (tq // ATT_TK), body, 0)

    off = pl.multiple_of(i * tq, tq)
    for r0 in range(0, 2 * tq, ATT_DRG):
        qo = r0 % tq
        n = qo + ATT_DRG
        row = lax.broadcasted_iota(jnp.int32, (ATT_DRG, n), 0)
        col = lax.broadcasted_iota(jnp.int32, (ATT_DRG, n), 1)
        update_rows(r0, ATT_DRG, k_ref[pl.ds(off, n), :], v_ref[pl.ds(off, n), :], col <= row + qo)

    lam_init = lam_ref[4:5, 0:1]
    lam = (jnp.exp(jnp.sum(lam_ref[0:1, :] * lam_ref[1:2, :], axis=1, keepdims=True))
           - jnp.exp(jnp.sum(lam_ref[2:3, :] * lam_ref[3:4, :], axis=1, keepdims=True))
           + lam_init)
    acc = acc_ref[...]
    inv_l = 1.0 / jnp.sum(l_ref[...], axis=1, keepdims=True)
    o = acc[0:tq] * inv_l[0:tq] - lam * (acc[tq:2 * tq] * inv_l[tq:2 * tq])
    o = _rms(o, g_ref[...], HEAD_NORM_EPS) * (1.0 - lam_init)
    o_ref[...] = o.astype(BF16)


def _attention(proj, lam_pack, head_g, layer):
    t = proj.shape[0]
    qb, kb, vb = COL_Q // LANES, COL_K // LANES, COL_V // LANES
    return pl.pallas_call(
        _attn_kernel,
        grid=(DA_HEADS, t // ATT_TQ),
        in_specs=[
            pl.BlockSpec((None, 8, DA_HEAD_DIM), lambda h, i: (layer, 0, 0)),
            _vec_spec(DA_V_DIM, layer, 2),
            pl.BlockSpec((ATT_TQ, LANES), lambda h, i: (i, qb + h)),
            pl.BlockSpec((t, LANES), lambda h, i: (0, kb + h)),
            pl.BlockSpec((t, LANES), lambda h, i: (0, vb + h)),
        ],
        out_specs=pl.BlockSpec((ATT_TQ, LANES), lambda h, i: (i, h)),
        out_shape=jax.ShapeDtypeStruct((t, DA_WIDTH), BF16),
        scratch_shapes=[
            pltpu.VMEM((2 * ATT_TQ, LANES), BF16),
            pltpu.VMEM((2 * ATT_TQ, LANES), F32),
            pltpu.VMEM((2 * ATT_TQ, LANES), F32),
            pltpu.VMEM((2 * ATT_TQ, LANES), F32),
        ],
        compiler_params=_params(("arbitrary", "arbitrary")),
        name="diff_attn",
    )(lam_pack, head_g, proj, proj, proj)


CV_TM = 512
CV_HALO = 32
CV_ROWS = 64
CV_SUB = 8
CV_SHIFT_ROWS = CV_TM + (CONV_TAPS - 1) // CV_SUB * CV_SUB


def _conv_kernel(a_ref, b_ref, w_ref, bias_ref, g_ref, beta_ref, o_ref, z_ref, zs_ref):
    i = pl.program_id(0)

    @pl.when(i == 0)
    def _():
        z_ref[0:CV_HALO, :] = jnp.zeros((CV_HALO, CV_WIDTH), F32)

    @pl.when(i > 0)
    def _():
        z_ref[0:CV_HALO, :] = z_ref[CV_TM:CV_TM + CV_HALO, :]

    a = a_ref[...].astype(F32)
    b = b_ref[...].astype(F32)
    z_ref[CV_HALO:CV_HALO + CV_TM, :] = a * jax.nn.sigmoid(b)

    base = CV_HALO - (CONV_TAPS - 1)
    for res in range(CV_SUB):
        n = CV_TM + (CONV_TAPS - 1 - res) // CV_SUB * CV_SUB
        zs_ref[res, 0:n, :] = z_ref[base + res:base + res + n, :]
    for r in range(0, CV_TM, CV_ROWS):
        acc = jnp.zeros((CV_ROWS, CV_WIDTH), F32) + bias_ref[...]
        for j in range(CONV_TAPS):
            blk, res = divmod(j, CV_SUB)
            lo = r + blk * CV_SUB
            acc = acc + w_ref[j:j + 1, :] * zs_ref[res, lo:lo + CV_ROWS, :]
        mu = jnp.mean(acc, axis=-1, keepdims=True)
        xc = acc - mu
        var = jnp.mean(xc * xc, axis=-1, keepdims=True)
        y = xc * lax.rsqrt(var + LN_EPS) * g_ref[...] + beta_ref[...]
        o_ref[r:r + CV_ROWS, :] = (y * jax.nn.sigmoid(y)).astype(BF16)


def _conv(proj, dw_w, dw_b, ln_g, ln_b, layer):
    t = proj.shape[0]
    ab, bb = COL_CVA // CV_WIDTH, COL_CVB // CV_WIDTH
    vec = _vec_spec(CV_WIDTH, layer, 1)
    return pl.pallas_call(
        _conv_kernel,
        grid=(t // CV_TM,),
        in_specs=[
            pl.BlockSpec((CV_TM, CV_WIDTH), lambda i: (i, ab)),
            pl.BlockSpec((CV_TM, CV_WIDTH), lambda i: (i, bb)),
            pl.BlockSpec((None, CONV_TAPS, CV_WIDTH), lambda i: (layer, 0, 0)),
            vec, vec, vec,
        ],
        out_specs=pl.BlockSpec((CV_TM, CV_WIDTH), lambda i: (i, 0)),
        out_shape=jax.ShapeDtypeStruct((t, CV_WIDTH), BF16),
        scratch_shapes=[pltpu.VMEM((CV_HALO + CV_TM, CV_WIDTH), F32),
                        pltpu.VMEM((CV_SUB, CV_SHIFT_ROWS, CV_WIDTH), F32)],
        compiler_params=_params(("arbitrary",)),
        name="conformer_conv",
    )(proj, proj, dw_w, dw_b, ln_g, ln_b)


SSM_CB = 128
SSM_RB = SSM_CB * SSM_CHUNK
SSM_SLAB = 2 * SSM_GROUP
SSM_SLABS = LANES // SSM_SLAB


def _pick_slabs(pieces, src_slab, lane_slab):
    out = None
    for k, piece in enumerate(pieces):
        shift = ((k - src_slab) * SSM_SLAB) % LANES
        moved = piece if shift == 0 else pltpu.roll(piece, shift, 1)
        out = moved if out is None else jnp.where(lane_slab == k, moved, out)
    return out


def _ssm_kernel(u_ref, t_ref, wre_ref, wim_ref, vre_ref, vim_ref, ar_ref, ai_ref, d_ref, y_ref,
                uf_ref, up_ref, sre_ref, sim_ref, yp_ref, yf_ref, xr_ref, xi_ref):
    npair, cb = SSM_PAIRS, SSM_CB

    @pl.when(pl.program_id(0) == 0)
    def _():
        xr_ref[...] = jnp.zeros(xr_ref.shape, F32)
        xi_ref[...] = jnp.zeros(xi_ref.shape, F32)

    ntile = SSM_WIDTH // LANES
    for j in range(ntile):
        uf_ref[j] = u_ref[:, j * LANES:(j + 1) * LANES].astype(F32)
    lane_slab = lax.shift_right_logical(lax.broadcasted_iota(jnp.int32, (cb, LANES), 1), 5)

    for q in range(SSM_CHUNK // SSM_SLABS):
        for tile in range(ntile):
            src = uf_ref.at[tile]
            pieces = [src[pl.ds(SSM_SLABS * q + k, cb, stride=SSM_CHUNK), :] for k in range(SSM_SLABS)]
            for slab in range(SSM_SLABS):
                up_ref[SSM_SLABS * tile + slab, :, q * LANES:(q + 1) * LANES] = _pick_slabs(
                    pieces, slab, lane_slab).astype(BF16)

    for p in range(npair):
        u = up_ref[p]
        sre_ref[pl.ds(p, cb, stride=npair), :] = jnp.dot(u, wre_ref[p], preferred_element_type=F32)
        sim_ref[pl.ds(p, cb, stride=npair), :] = jnp.dot(u, wim_ref[p], preferred_element_type=F32)

    ar = ar_ref[...]
    ai = ai_ref[...]

    def body(c, carry):
        xr, xi = carry
        off = pl.multiple_of(c * npair, npair)
        sr = sre_ref[pl.ds(off, npair), :]
        si = sim_ref[pl.ds(off, npair), :]
        sre_ref[pl.ds(off, npair), :] = xr
        sim_ref[pl.ds(off, npair), :] = xi
        return ar * xr - ai * xi + sr, ar * xi + ai * xr + si

    xr, xi = lax.fori_loop(0, cb, body, (xr_ref[...], xi_ref[...]), unroll=8)
    xr_ref[...] = xr
    xi_ref[...] = xi

    for p in range(npair):
        u = up_ref[p]
        sr = sre_ref[pl.ds(p, cb, stride=npair), :].astype(BF16)
        si = sim_ref[pl.ds(p, cb, stride=npair), :].astype(BF16)
        nt = (((1,), (1,)), ((), ()))
        y = jnp.dot(u, t_ref[p], preferred_element_type=F32)
        y = y + lax.dot_general(sr, vre_ref[p], nt, preferred_element_type=F32)
        yp_ref[p] = y + lax.dot_general(si, vim_ref[p], nt, preferred_element_type=F32)

    for tile in range(ntile):
        for q in range(SSM_CHUNK // SSM_SLABS):
            pieces = [yp_ref[SSM_SLABS * tile + k, :, q * LANES:(q + 1) * LANES] for k in range(SSM_SLABS)]
            dst = yf_ref.at[tile]
            for slab in range(SSM_SLABS):
                dst[pl.ds(SSM_SLABS * q + slab, cb, stride=SSM_CHUNK), :] = _pick_slabs(pieces, slab, lane_slab)

    for j in range(ntile):
        cols = slice(j * LANES, (j + 1) * LANES)
        y_ref[:, cols] = (yf_ref[j] + uf_ref[j] * d_ref[:, cols]).astype(BF16)


def _ssm(proj, mats):
    t = proj.shape[0]
    vm = pl.BlockSpec(memory_space=pltpu.VMEM)
    return pl.pallas_call(
        _ssm_kernel,
        grid=(t // SSM_RB,),
        in_specs=[pl.BlockSpec((SSM_RB, SSM_WIDTH), lambda i: (i, COL_SSM // SSM_WIDTH))] + [vm] * 8,
        out_specs=pl.BlockSpec((SSM_RB, SSM_WIDTH), lambda i: (i, 0)),
        out_shape=jax.ShapeDtypeStruct((t, SSM_WIDTH), BF16),
        scratch_shapes=[
            pltpu.VMEM((SSM_WIDTH // LANES, SSM_RB, LANES), F32),
            pltpu.VMEM((SSM_PAIRS, SSM_CB, SSM_PAIR_W), BF16),
            pltpu.VMEM((SSM_CB * SSM_PAIRS, LANES), F32),
            pltpu.VMEM((SSM_CB * SSM_PAIRS, LANES), F32),
            pltpu.VMEM((SSM_PAIRS, SSM_CB, SSM_PAIR_W), F32),
            pltpu.VMEM((SSM_WIDTH // LANES, SSM_RB, LANES), F32),
            pltpu.VMEM((SSM_PAIRS, LANES), F32),
            pltpu.VMEM((SSM_PAIRS, LANES), F32),
        ],
        compiler_params=_params(("arbitrary",)),
        name="s5_scan",
    )(proj, *mats)


def _ssm_gen_kernel(k_ref, bre_ref, bim_ref, cre_ref, cim_ref, pr_ref, pi_ref,
                    t_ref, wre_ref, wim_ref, vre_ref, vim_ref):
    k = k_ref[...]
    lane = lax.broadcasted_iota(jnp.int32, k.shape, 1)
    bre, bim, cre, cim = bre_ref[...], bim_ref[...], cre_ref[...], cim_ref[...]
    for s in range(SSM_CHUNK):
        rows = slice(s * SSM_SLAB, (s + 1) * SSM_SLAB)
        moved = k if s == 0 else pltpu.roll(k, s * SSM_SLAB, 1)
        t_ref[rows, :] = jnp.where(lane >= s * SSM_SLAB, moved, 0.0).astype(BF16)
        pr = pr_ref[SSM_CHUNK - 1 - s:SSM_CHUNK - s, :]
        pi = pi_ref[SSM_CHUNK - 1 - s:SSM_CHUNK - s, :]
        wre_ref[rows, :] = (bre * pr - bim * pi).astype(BF16)
        wim_ref[rows, :] = (bre * pi + bim * pr).astype(BF16)
        qr = pr_ref[s + 1:s + 2, :]
        qi = pi_ref[s + 1:s + 2, :]
        vre_ref[rows, :] = (cre * qr - cim * qi).astype(BF16)
        vim_ref[rows, :] = (-(cre * qi + cim * qr)).astype(BF16)


def _ssm_matrices(lam_re, lam_im, log_dt, b_re, b_im, c_re, c_im, d_skip):
    hp = lax.Precision.HIGHEST
    nl = lam_re.shape[0]
    lr = lam_re.astype(F32)
    li = lam_im.astype(F32)
    dt = jnp.exp(log_dt.astype(F32))[..., None]
    mag = jnp.exp(lr * dt)
    ab_re = mag * jnp.cos(li * dt)
    ab_im = mag * jnp.sin(li * dt)
    den = lr * lr + li * li
    f_re = ((ab_re - 1.0) * lr + ab_im * li) / den
    f_im = (ab_im * lr - (ab_re - 1.0) * li) / den
    br = b_re.astype(F32)
    bi = b_im.astype(F32)
    bb_re = f_re[..., None] * br - f_im[..., None] * bi
    bb_im = f_re[..., None] * bi + f_im[..., None] * br
    tau = jnp.arange(SSM_CHUNK + 1, dtype=F32)[None, None, :, None]
    pmag = jnp.exp(tau * (lr * dt)[:, :, None, :])
    pw_re = pmag * jnp.cos(tau * (li * dt)[:, :, None, :])
    pw_im = pmag * jnp.sin(tau * (li * dt)[:, :, None, :])
    cr = c_re.astype(F32)
    ci = c_im.astype(F32)
    ca_re = cr[:, :, None] * pw_re[:, :, :SSM_CHUNK, None, :] - ci[:, :, None] * pw_im[:, :, :SSM_CHUNK, None, :]
    ca_im = cr[:, :, None] * pw_im[:, :, :SSM_CHUNK, None, :] + ci[:, :, None] * pw_re[:, :, :SSM_CHUNK, None, :]
    kk = (jnp.einsum('lgthp,lgpk->lgthk', ca_re, bb_re, precision=hp)
          - jnp.einsum('lgthp,lgpk->lgthk', ca_im, bb_im, precision=hp))
    eye2 = jnp.eye(2, dtype=F32)

    def paired(a):
        return a.reshape((nl, SSM_PAIRS, 2) + a.shape[2:])

    kcat = jnp.einsum('lpexhk,ef->lpekxfh', paired(kk), eye2).reshape(nl, SSM_PAIRS, SSM_SLAB, SSM_PAIR_W)
    bbd_re = jnp.einsum('lpenk,ef->lpekfn', paired(bb_re), eye2).reshape(nl, SSM_PAIRS, SSM_SLAB, LANES)
    bbd_im = jnp.einsum('lpenk,ef->lpekfn', paired(bb_im), eye2).reshape(nl, SSM_PAIRS, SSM_SLAB, LANES)
    cbd_re = jnp.einsum('lpehn,ef->lpehfn', paired(cr), eye2).reshape(nl, SSM_PAIRS, SSM_SLAB, LANES)
    cbd_im = jnp.einsum('lpehn,ef->lpehfn', paired(ci), eye2).reshape(nl, SSM_PAIRS, SSM_SLAB, LANES)
    pwp_re = paired(pw_re).transpose(0, 1, 3, 2, 4).reshape(nl, SSM_PAIRS, SSM_CHUNK + 1, LANES)
    pwp_im = paired(pw_im).transpose(0, 1, 3, 2, 4).reshape(nl, SSM_PAIRS, SSM_CHUNK + 1, LANES)

    def spec(r, c):
        return pl.BlockSpec((None, None, r, c), lambda l, p: (l, p, 0, 0))

    wide = jax.ShapeDtypeStruct((nl, SSM_PAIRS, SSM_PAIR_W, SSM_PAIR_W), BF16)
    tall = jax.ShapeDtypeStruct((nl, SSM_PAIRS, SSM_PAIR_W, LANES), BF16)
    tm, w_re, w_im, vt_re, vt_im = pl.pallas_call(
        _ssm_gen_kernel,
        grid=(nl, SSM_PAIRS),
        in_specs=[spec(SSM_SLAB, SSM_PAIR_W)] + [spec(SSM_SLAB, LANES)] * 4 + [spec(SSM_CHUNK + 1, LANES)] * 2,
        out_specs=[spec(SSM_PAIR_W, SSM_PAIR_W)] + [spec(SSM_PAIR_W, LANES)] * 4,
        out_shape=[wide, tall, tall, tall, tall],
        compiler_params=_params(("arbitrary", "arbitrary")),
        name="s5_matrices",
    )(kcat, bbd_re, bbd_im, cbd_re, cbd_im, pwp_re, pwp_im)
    a_re = pwp_re[:, :, SSM_CHUNK]
    a_im = pwp_im[:, :, SSM_CHUNK]
    d_t = d_skip.astype(F32).reshape(nl, 1, SSM_WIDTH)
    return tm, w_re, w_im, vt_re, vt_im, a_re, a_im, d_t


MG_TM = 512
MG_TN = 512


def _merge_mix_kernel(oa_ref, zb_ref, yc_ref, ga_ref, gb_ref, gc_ref, wda_ref, wcv_ref, wglu_ref,
                      bglu_ref, wso_ref, wmix_ref, x_ref, o_ref, sc_ref):
    n = pl.program_id(1)

    @pl.when(n == 0)
    def _():
        glu = jnp.dot(yc_ref[...], wglu_ref[...], preferred_element_type=F32) + bglu_ref[...]
        sc_ref[...] = (glu[:, :SSM_WIDTH] * jax.nn.sigmoid(glu[:, SSM_WIDTH:])).astype(BF16)
        o_ref[...] = x_ref[...]

    y_a = jnp.dot(oa_ref[...], wda_ref[...], preferred_element_type=F32)
    y_b = jnp.dot(zb_ref[...], wcv_ref[...], preferred_element_type=F32)
    y_c = jnp.dot(sc_ref[...], wso_ref[...], preferred_element_type=F32)
    merged = (jax.nn.sigmoid(ga_ref[...].astype(F32)) * y_a
              + jax.nn.sigmoid(gb_ref[...].astype(F32)) * y_b
              + jax.nn.sigmoid(gc_ref[...].astype(F32)) * y_c)
    o_ref[...] += jnp.dot(merged.astype(BF16), wmix_ref[...], preferred_element_type=F32)


def _merge_mix(o_a, z_b, y_c, proj, w_da, w_cv, w_glu, b_glu, w_so, w_mix, x, layer):
    t = o_a.shape[0]
    ga, gb, gc = COL_GA // MG_TN, COL_GB // MG_TN, COL_GC // MG_TN

    def wcol(k):
        return pl.BlockSpec((None, k, MG_TN), lambda i, n: (layer, 0, n))

    return pl.pallas_call(
        _merge_mix_kernel,
        grid=(t // MG_TM, D_MODEL // MG_TN),
        in_specs=[
            pl.BlockSpec((MG_TM, DA_WIDTH), lambda i, n: (i, 0)),
            pl.BlockSpec((MG_TM, CV_WIDTH), lambda i, n: (i, 0)),
            pl.BlockSpec((MG_TM, SSM_WIDTH), lambda i, n: (i, 0)),
            pl.BlockSpec((MG_TM, MG_TN), lambda i, n: (i, ga + n)),
            pl.BlockSpec((MG_TM, MG_TN), lambda i, n: (i, gb + n)),
            pl.BlockSpec((MG_TM, MG_TN), lambda i, n: (i, gc + n)),
            wcol(DA_WIDTH), wcol(CV_WIDTH),
            pl.BlockSpec((None, SSM_WIDTH, 2 * SSM_WIDTH), lambda i, n: (layer, 0, 0)),
            _vec_spec(2 * SSM_WIDTH, layer, 2),
            wcol(SSM_WIDTH),
            pl.BlockSpec((None, MG_TN, D_MODEL), lambda i, n: (layer, n, 0)),
            pl.BlockSpec((MG_TM, D_MODEL), lambda i, n: (i, 0)),
        ],
        out_specs=pl.BlockSpec((MG_TM, D_MODEL), lambda i, n: (i, 0)),
        out_shape=jax.ShapeDtypeStruct((t, D_MODEL), F32),
        scratch_shapes=[pltpu.VMEM((MG_TM, SSM_WIDTH), BF16)],
        compiler_params=_params(("arbitrary", "arbitrary")),
        name="merge_mix",
    )(o_a, z_b, y_c, proj, proj, proj, w_da, w_cv, w_glu, b_glu, w_so, w_mix, x)


def _norm_matmul_kernel(x_ref, g_ref, w_ref, o_ref):
    h = _rms(x_ref[...], g_ref[...], RMS_EPS).astype(BF16)
    o_ref[...] = jnp.dot(h, w_ref[...], preferred_element_type=F32).astype(o_ref.dtype)


def _mem_kv(mem, g_all, w_all, layer):
    m = mem.shape[0]
    n = 2 * XA_WIDTH
    return pl.pallas_call(
        _norm_matmul_kernel,
        grid=(1,),
        in_specs=[
            pl.BlockSpec((m, D_MODEL), lambda i: (0, 0)),
            _vec_spec(D_MODEL, layer, 1),
            pl.BlockSpec((None, D_MODEL, n), lambda i: (layer, 0, 0)),
        ],
        out_specs=pl.BlockSpec((m, n), lambda i: (0, 0)),
        out_shape=jax.ShapeDtypeStruct((m, n), BF16),
        compiler_params=_params(("arbitrary",)),
        name="mem_kv",
    )(mem, g_all, w_all)


XA_TM = 512


def _xattn_kernel(x_ref, g_ref, wq_ref, kv_ref, wo_ref, o_ref):
    x = x_ref[...]
    h = _rms(x, g_ref[...], RMS_EPS).astype(BF16)
    q = jnp.dot(h, wq_ref[...], preferred_element_type=F32).astype(BF16)
    heads = []
    for hd in range(XA_HEADS):
        lo = hd * XA_HEAD_DIM
        k = kv_ref[:, lo:lo + XA_HEAD_DIM]
        v = kv_ref[:, XA_WIDTH + lo:XA_WIDTH + lo + XA_HEAD_DIM]
        s = lax.dot_general(q[:, lo:lo + XA_HEAD_DIM], k, (((1,), (1,)), ((), ())),
                            preferred_element_type=F32) * (XA_HEAD_DIM ** -0.5)
        m = jnp.max(s, axis=-1, keepdims=True)
        e = jnp.exp(s - m)
        p = e / jnp.sum(e, axis=-1, keepdims=True)
        heads.append(jnp.dot(p.astype(BF16), v, preferred_element_type=F32).astype(BF16))
    o = jnp.concatenate(heads, axis=-1)
    o_ref[...] = x + jnp.dot(o, wo_ref[...], preferred_element_type=F32)


def _xattn(x, g_all, wq_all, kv, wo_all, layer):
    t = x.shape[0]
    return pl.pallas_call(
        _xattn_kernel,
        grid=(t // XA_TM,),
        in_specs=[
            pl.BlockSpec((XA_TM, D_MODEL), lambda i: (i, 0)),
            _vec_spec(D_MODEL, layer, 1),
            pl.BlockSpec((None, D_MODEL, XA_WIDTH), lambda i: (layer, 0, 0)),
            pl.BlockSpec((MEM_LEN, 2 * XA_WIDTH), lambda i: (0, 0)),
            pl.BlockSpec((None, XA_WIDTH, D_MODEL), lambda i: (layer, 0, 0)),
        ],
        out_specs=pl.BlockSpec((XA_TM, D_MODEL), lambda i: (i, 0)),
        out_shape=jax.ShapeDtypeStruct((t, D_MODEL), F32),
        compiler_params=_params(("arbitrary",)),
        name="mem_xattn",
    )(x, g_all, wq_all, kv, wo_all)


RT_TM = 512
_E_LANE0 = MOE_GROUPS


def _router_kernel(x_ref, g_ref, w_ref, b_ref, meta_ref, cnt_ref, run_ref):
    i = pl.program_id(0)

    @pl.when(i == 0)
    def _():
        run_ref[...] = jnp.zeros(run_ref.shape, F32)

    h = _rms(x_ref[...], g_ref[...], RMS_EPS).astype(BF16)
    logits = jnp.dot(h, w_ref[...], preferred_element_type=F32) + b_ref[...]
    lane = lax.broadcasted_iota(jnp.int32, logits.shape, 1).astype(F32)
    neg = jnp.float32(-jnp.inf)
    big = jnp.float32(LANES)

    def first_argmax(vals):
        top = jnp.max(vals, axis=-1, keepdims=True)
        idx = jnp.min(jnp.where(vals == top, lane, big), axis=-1, keepdims=True)
        return top, idx

    gl = jnp.where(lane < MOE_GROUPS, logits, neg)
    g_top, g_idx = first_argmax(gl)
    g_w = 1.0 / jnp.sum(jnp.exp(gl - g_top), axis=-1, keepdims=True)
    e_lane = lane - _E_LANE0
    in_group = jnp.logical_and(e_lane >= g_idx * MOE_PER_GROUP, e_lane < (g_idx + 1) * MOE_PER_GROUP)
    el = jnp.where(in_group, logits, neg)
    v1, i1 = first_argmax(el)
    el2 = jnp.where(lane == i1, neg, el)
    v2, i2 = first_argmax(el2)
    e2 = jnp.exp(v2 - v1)
    w1 = 1.0 / (1.0 + e2)
    w2 = e2 / (1.0 + e2)
    oh1 = (lane == i1).astype(F32)
    oh2 = (lane == i2).astype(F32)
    both = oh1 + oh2
    row = lax.broadcasted_iota(jnp.int32, (RT_TM, RT_TM), 0)
    col = lax.broadcasted_iota(jnp.int32, (RT_TM, RT_TM), 1)
    earlier = jnp.where(col < row, 1.0, 0.0).astype(BF16)
    before = jnp.dot(earlier, both.astype(BF16), preferred_element_type=F32) + run_ref[...]
    rank1 = jnp.sum(oh1 * before, axis=-1, keepdims=True)
    rank2 = jnp.sum(oh2 * before, axis=-1, keepdims=True)
    run_ref[...] += jnp.sum(both, axis=0, keepdims=True)
    cnt_ref[...] = run_ref[...]
    meta = jnp.where(lane == 0.0, i1 - _E_LANE0, 0.0)
    for k, val in enumerate((i2 - _E_LANE0, rank1, rank2, w1 * g_w, w2 * g_w), start=1):
        meta = jnp.where(lane == float(k), val, meta)
    meta_ref[...] = meta


def _router(x, g_all, w_r, b_r, layer):
    t = x.shape[0]
    return pl.pallas_call(
        _router_kernel,
        grid=(t // RT_TM,),
        in_specs=[
            pl.BlockSpec((RT_TM, D_MODEL), lambda i: (i, 0)),
            _vec_spec(D_MODEL, layer, 1),
            pl.BlockSpec((None, D_MODEL, LANES), lambda i: (layer, 0, 0)),
            _vec_spec(LANES, layer, 1),
        ],
        out_specs=[
            pl.BlockSpec((RT_TM, LANES), lambda i: (i, 0)),
            pl.BlockSpec((1, LANES), lambda i: (0, 0)),
        ],
        out_shape=[jax.ShapeDtypeStruct((t, LANES), F32), jax.ShapeDtypeStruct((1, LANES), F32)],
        scratch_shapes=[pltpu.VMEM((1, LANES), F32)],
        compiler_params=_params(("arbitrary",)),
        name="moe_router",
    )(x, g_all, w_r, b_r)


EX_TM = 256
EX_ROWS = 2 * SEQ
EX_TILES = EX_ROWS // EX_TM
EX_STEPS = EX_TILES + MOE_EXPERTS - 1


def _row_copy(src, src_row, dst, dst_row, sem):
    return pltpu.make_async_copy(src.at[pl.ds(src_row, 1), :], dst.at[pl.ds(dst_row, 1), :], sem)


DP_TM = 256


def _dispatch_kernel(p1_ref, p2_ref, x_ref, xs_hbm, sem):
    base = pl.program_id(0) * DP_TM

    def issue(j, c):
        _row_copy(x_ref, j, xs_hbm, p1_ref[base + j], sem).start()
        _row_copy(x_ref, j, xs_hbm, p2_ref[base + j], sem).start()
        return c

    lax.fori_loop(0, DP_TM, issue, 0, unroll=8)

    def wait(j, c):
        _row_copy(x_ref, j, xs_hbm, 0, sem).wait()
        _row_copy(x_ref, j, xs_hbm, 0, sem).wait()
        return c

    lax.fori_loop(0, DP_TM, wait, 0, unroll=8)


def _dispatch(pos1, pos2, x):
    t = x.shape[0]
    grid_spec = pltpu.PrefetchScalarGridSpec(
        num_scalar_prefetch=2,
        grid=(t // DP_TM,),
        in_specs=[pl.BlockSpec((DP_TM, D_MODEL), lambda i, p1, p2: (i, 0))],
        out_specs=pl.BlockSpec(memory_space=pl.ANY),
        scratch_shapes=[pltpu.SemaphoreType.DMA(())],
    )
    return pl.pallas_call(
        _dispatch_kernel,
        grid_spec=grid_spec,
        out_shape=jax.ShapeDtypeStruct((EX_ROWS, D_MODEL), F32),
        compiler_params=_params(("arbitrary",)),
        name="moe_dispatch",
    )(pos1, pos2, x)


def _experts_kernel(se_ref, st_ref, lo_ref, hi_ref, first_ref, wnew_ref, x_ref, g_ref, wg_ref, wu_ref,
                    wd_ref, o_ref, wgb_ref, wub_ref, wdb_ref):
    s = pl.program_id(0)
    lo = lo_ref[s]
    hi = hi_ref[s]

    @pl.when(wnew_ref[s] == 1)
    def _():
        wgb_ref[...] = wg_ref[...].astype(BF16)
        wub_ref[...] = wu_ref[...].astype(BF16)
        wdb_ref[...] = wd_ref[...].astype(BF16)

    @pl.when(hi > lo)
    def _():
        h = _rms(x_ref[...], g_ref[...], RMS_EPS).astype(BF16)
        gate = jnp.dot(h, wgb_ref[...], preferred_element_type=F32)
        up = jnp.dot(h, wub_ref[...], preferred_element_type=F32)
        act = (gate * jax.nn.sigmoid(gate) * up).astype(BF16)
        res = jnp.dot(act, wdb_ref[...], preferred_element_type=F32)
        row = lax.broadcasted_iota(jnp.int32, (EX_TM, 1), 0)
        mine = jnp.logical_and(row >= lo, row < hi)

        @pl.when(first_ref[s] == 1)
        def _():
            o_ref[...] = jnp.where(mine, res, 0.0)

        @pl.when(first_ref[s] == 0)
        def _():
            o_ref[...] = jnp.where(mine, res, o_ref[...])


def _experts(plan, xs_sorted, g_all, wg_all, wu_all, wd_all, layer):
    def wspec(k, n):
        return pl.BlockSpec((None, None, k, n), lambda s, se, st, lo, hi, fi, wn: (layer, se[s], 0, 0))

    grid_spec = pltpu.PrefetchScalarGridSpec(
        num_scalar_prefetch=6,
        grid=(EX_STEPS,),
        in_specs=[
            pl.BlockSpec((EX_TM, D_MODEL), lambda s, se, st, lo, hi, fi, wn: (st[s], 0)),
            pl.BlockSpec((None, 1, D_MODEL), lambda s, se, st, lo, hi, fi, wn: (layer, 0, 0)),
            wspec(D_MODEL, MOE_FF), wspec(D_MODEL, MOE_FF), wspec(MOE_FF, D_MODEL),
        ],
        out_specs=pl.BlockSpec((EX_TM, D_MODEL), lambda s, se, st, lo, hi, fi, wn: (st[s], 0)),
        scratch_shapes=[pltpu.VMEM((D_MODEL, MOE_FF), BF16), pltpu.VMEM((D_MODEL, MOE_FF), BF16),
                        pltpu.VMEM((MOE_FF, D_MODEL), BF16)],
    )
    return pl.pallas_call(
        _experts_kernel,
        grid_spec=grid_spec,
        out_shape=jax.ShapeDtypeStruct((EX_ROWS, D_MODEL), F32),
        compiler_params=_params(("arbitrary",)),
        name="moe_experts",
    )(*plan, xs_sorted, g_all, wg_all, wu_all, wd_all)


CB_TM = 256


def _combine_kernel(p1_ref, p2_ref, y_hbm, x_ref, meta_ref, o_ref, y1buf, y2buf, sem):
    i = pl.program_id(0)
    slot = lax.rem(i, 2)

    def fetch(tile, slot_):
        base = tile * CB_TM

        def issue(j, c):
            _row_copy(y_hbm, p1_ref[base + j], y1buf.at[slot_], j, sem.at[slot_]).start()
            _row_copy(y_hbm, p2_ref[base + j], y2buf.at[slot_], j, sem.at[slot_]).start()
            return c

        lax.fori_loop(0, CB_TM, issue, 0, unroll=8)

    @pl.when(i == 0)
    def _():
        fetch(0, 0)

    @pl.when(i + 1 < pl.num_programs(0))
    def _():
        fetch(i + 1, 1 - slot)

    def wait(j, c):
        _row_copy(y_hbm, 0, y1buf.at[slot], j, sem.at[slot]).wait()
        _row_copy(y_hbm, 0, y2buf.at[slot], j, sem.at[slot]).wait()
        return c

    lax.fori_loop(0, CB_TM, wait, 0, unroll=8)
    meta = meta_ref[...]
    o_ref[...] = x_ref[...] + meta[:, 4:5] * y1buf[slot] + meta[:, 5:6] * y2buf[slot]


def _combine(pos1, pos2, ys, x, meta):
    t = x.shape[0]
    grid_spec = pltpu.PrefetchScalarGridSpec(
        num_scalar_prefetch=2,
        grid=(t // CB_TM,),
        in_specs=[
            pl.BlockSpec(memory_space=pl.ANY),
            pl.BlockSpec((CB_TM, D_MODEL), lambda i, p1, p2: (i, 0)),
            pl.BlockSpec((CB_TM, LANES), lambda i, p1, p2: (i, 0)),
        ],
        out_specs=pl.BlockSpec((CB_TM, D_MODEL), lambda i, p1, p2: (i, 0)),
        scratch_shapes=[pltpu.VMEM((2, CB_TM, D_MODEL), F32), pltpu.VMEM((2, CB_TM, D_MODEL), F32),
                        pltpu.SemaphoreType.DMA((2,))],
    )
    return pl.pallas_call(
        _combine_kernel,
        grid_spec=grid_spec,
        out_shape=jax.ShapeDtypeStruct((t, D_MODEL), F32),
        compiler_params=_params(("arbitrary",)),
        name="moe_combine",
    )(pos1, pos2, ys, x, meta)


def _dispatch_plan(meta, cnt):
    t = meta.shape[0]
    e1 = meta[:, 0].astype(jnp.int32)
    e2 = meta[:, 1].astype(jnp.int32)
    r1 = meta[:, 2].astype(jnp.int32)
    r2 = meta[:, 3].astype(jnp.int32)
    counts = cnt[0, _E_LANE0:_E_LANE0 + MOE_EXPERTS].astype(jnp.int32)
    ends = jnp.cumsum(counts)
    starts = ends - counts
    pos1 = starts[e1] + r1
    pos2 = starts[e2] + r2
    t_lo = starts // EX_TM
    t_hi = (ends + EX_TM - 1) // EX_TM
    nsteps = jnp.where(counts > 0, t_hi - t_lo, 0)
    step_end = jnp.cumsum(nsteps)
    step_start = step_end - nsteps
    s = jnp.arange(EX_STEPS, dtype=jnp.int32)
    se = jnp.minimum(jnp.sum((s[:, None] >= step_end[None, :]).astype(jnp.int32), axis=1), MOE_EXPERTS - 1)
    valid = s < step_end[-1]
    st = jnp.where(valid, t_lo[se] + s - step_start[se], EX_TILES - 1)
    lo = jnp.where(valid, jnp.clip(starts[se] - st * EX_TM, 0, EX_TM), 0)
    hi = jnp.where(valid, jnp.clip(ends[se] - st * EX_TM, 0, EX_TM), 0)
    first = jnp.concatenate([jnp.ones((1,), jnp.int32), (st[1:] != st[:-1]).astype(jnp.int32)])
    wnew = jnp.concatenate([jnp.ones((1,), jnp.int32), (se[1:] != se[:-1]).astype(jnp.int32)])
    plan = tuple(a.astype(jnp.int32) for a in (se, st, lo, hi, first, wnew))
    return pos1.astype(jnp.int32), pos2.astype(jnp.int32), plan


FN_TM = 512


def _final_norm_kernel(x_ref, g_ref, o_ref):
    o_ref[...] = _rms(x_ref[...], g_ref[...], RMS_EPS)


def _final_norm(x, g):
    t = x.shape[0]
    return pl.pallas_call(
        _final_norm_kernel,
        grid=(t // FN_TM,),
        in_specs=[pl.BlockSpec((FN_TM, D_MODEL), lambda i: (i, 0)),
                  pl.BlockSpec((1, D_MODEL), lambda i: (0, 0))],
        out_specs=pl.BlockSpec((FN_TM, D_MODEL), lambda i: (i, 0)),
        out_shape=jax.ShapeDtypeStruct((t, D_MODEL), F32),
        compiler_params=_params(("arbitrary",)),
        name="final_norm",
    )(x, g)


def kernel(x, mem, positions, norm_mix, w_in, da_lam_q1, da_lam_k1, da_lam_q2, da_lam_k2, da_head_norm, w_da_out, cv_dw_w, cv_dw_b, cv_ln_g, cv_ln_b, w_cv_out, ssm_lam_re, ssm_lam_im, ssm_log_dt, ssm_b_re, ssm_b_im, ssm_c_re, ssm_c_im, ssm_d, w_ssm_glu, b_ssm_glu, w_ssm_out, w_mix_out, norm_xa, norm_mem, w_xa_q, w_xa_kv, w_xa_out, norm_ffn, w_router_group, b_router_group, w_router_expert, b_router_expert, w_exp_gate, w_exp_up, w_exp_down, norm_final):
    bsz, seq, _ = x.shape
    assert bsz == 1 and seq == SEQ
    nl = w_in.shape[0]
    xs = x.reshape(seq, D_MODEL).astype(F32)
    mem2 = mem.reshape(MEM_LEN, D_MODEL).astype(F32)

    inv_freq = ROPE_THETA ** (-jnp.arange(0, DA_HEAD_DIM, 2, dtype=F32) / DA_HEAD_DIM)
    ang = positions.reshape(seq).astype(F32)[:, None] * inv_freq
    cos = jnp.cos(ang)
    sin = jnp.sin(ang)
    cos_t = jnp.concatenate([cos, cos, cos, cos], axis=-1)
    sin_t = jnp.concatenate([-sin, sin, -sin, sin], axis=-1)

    w_da_b = w_da_out.astype(BF16)
    w_cv_b = w_cv_out.astype(BF16)
    w_glu_b = w_ssm_glu.astype(BF16)
    w_so_b = w_ssm_out.astype(BF16)
    w_mix_b = w_mix_out.astype(BF16)
    w_xq_b = w_xa_q.astype(BF16)
    w_xkv_b = w_xa_kv.astype(BF16)
    w_xo_b = w_xa_out.astype(BF16)
    pad = LANES - MOE_GROUPS - MOE_EXPERTS
    w_r = jnp.concatenate([w_router_group, w_router_expert,
                           jnp.zeros((nl, D_MODEL, pad), F32)], axis=-1).astype(BF16)
    b_r = jnp.concatenate([b_router_group, b_router_expert, jnp.zeros((nl, pad), F32)], axis=-1).astype(F32)

    def vec3(a):
        return a.astype(F32).reshape(nl, 1, a.shape[-1])

    norm_mix, da_head_norm, cv_dw_b, cv_ln_g, cv_ln_b, b_ssm_glu, norm_xa, norm_mem, norm_ffn, b_r = map(
        vec3, (norm_mix, da_head_norm, cv_dw_b, cv_ln_g, cv_ln_b, b_ssm_glu, norm_xa, norm_mem, norm_ffn, b_r))
    cv_dw_w = cv_dw_w.astype(F32)

    lam_inits = jnp.asarray([0.8 - 0.6 * math.exp(-0.3 * l) for l in range(nl)], F32)
    lam_pack = jnp.stack([da_lam_q1, da_lam_k1, da_lam_q2, da_lam_k2], axis=1).astype(F32)
    lam_pack = jnp.concatenate(
        [lam_pack, jnp.broadcast_to(lam_inits[:, None, None], (nl, 4, DA_HEAD_DIM))], axis=1)

    ssm_mats = _ssm_matrices(ssm_lam_re, ssm_lam_im, ssm_log_dt, ssm_b_re, ssm_b_im,
                             ssm_c_re, ssm_c_im, ssm_d)

    for l in range(nl):
        proj = _inproj(xs, norm_mix, w_in, l, cos_t, sin_t)
        o_a = _attention(proj, lam_pack, da_head_norm, l)
        z_b = _conv(proj, cv_dw_w, cv_dw_b, cv_ln_g, cv_ln_b, l)
        y_c = _ssm(proj, [m[l] for m in ssm_mats])
        xs = _merge_mix(o_a, z_b, y_c, proj, w_da_b, w_cv_b, w_glu_b, b_ssm_glu, w_so_b, w_mix_b, xs, l)
        kv = _mem_kv(mem2, norm_mem, w_xkv_b, l)
        xs = _xattn(xs, norm_xa, w_xq_b, kv, w_xo_b, l)
        meta, cnt = _router(xs, norm_ffn, w_r, b_r, l)
        pos1, pos2, plan = _dispatch_plan(meta, cnt)
        xs_sorted = _dispatch(pos1, pos2, xs)
        ys = _experts(plan, xs_sorted, norm_ffn, w_exp_gate, w_exp_up, w_exp_down, l)
        xs = _combine(pos1, pos2, ys, xs, meta)
    out = _final_norm(xs, norm_final.reshape(1, D_MODEL))
    return out.reshape(bsz, seq, D_MODEL)
```

```python
import functools
import math

import jax
import jax.numpy as jnp
from jax import lax
from jax.experimental import pallas as pl
from jax.experimental.pallas import tpu as pltpu

F32 = jnp.float32
BF16 = jnp.bfloat16

D_MODEL = 2048
SEQ = 8192
DEPTH = 4
MEM_LEN = 256
DA_HEADS = 8
DA_HEAD_DIM = 64
DA_V_DIM = 128
DA_WIDTH = 1024
ROPE_THETA = 10000.0
CV_WIDTH = 512
CONV_TAPS = 31
SSM_WIDTH = 512
SSM_GROUP = 16
SSM_GROUPS = 32
SSM_STATE = 64
XA_HEADS = 4
XA_HEAD_DIM = 128
XA_WIDTH = 512
MOE_GROUPS = 4
MOE_PER_GROUP = 4
MOE_EXPERTS = 16
MOE_FF = 512
RMS_EPS = 1e-6
HEAD_NORM_EPS = 1e-5
LN_EPS = 1e-5

COL_Q = 0
COL_K = COL_Q + 1024
COL_V = COL_K + 1024
COL_CVA = COL_V + 1024
COL_CVB = COL_CVA + CV_WIDTH
COL_SSM = COL_CVB + CV_WIDTH
COL_GA = COL_SSM + SSM_WIDTH
COL_GB = COL_GA + D_MODEL
COL_GC = COL_GB + D_MODEL
IN_TOTAL = COL_GC + D_MODEL

LANES = 128
VMEM_LIMIT = 56 * 1024 * 1024

SSM_CHUNK = 16
SSM_NCHUNK = SEQ // SSM_CHUNK
SSM_PAIRS = SSM_GROUPS // 2
SSM_PAIR_W = 2 * SSM_CHUNK * SSM_GROUP


def _params(sem, vmem=VMEM_LIMIT):
    return pltpu.CompilerParams(dimension_semantics=sem, vmem_limit_bytes=vmem)


def _vec_spec(width, layer, ngrid):
    if ngrid == 1:
        return pl.BlockSpec((None, 1, width), lambda i: (layer, 0, 0))
    return pl.BlockSpec((None, 1, width), lambda i, j: (layer, 0, 0))


def _rms(xf, g, eps):
    ms = jnp.mean(xf * xf, axis=-1, keepdims=True)
    return xf * lax.rsqrt(ms + eps) * g


INP_TM = 2048
INP_TN = 512
_Q_TILE0 = COL_Q // INP_TN
_K_TILE0 = COL_K // INP_TN
_V_TILE0 = COL_V // INP_TN


def _inproj_kernel(h_ref, w_ref, cos_ref, sin_ref, o_ref):
    j = pl.program_id(1)
    acc = jnp.dot(h_ref[...], w_ref[...].astype(BF16), preferred_element_type=F32)
    is_rope = jnp.logical_and(j >= _Q_TILE0, j < _V_TILE0)

    @pl.when(is_rope)
    def _():
        scale = jnp.where(j < _K_TILE0, math.log2(math.e) * DA_HEAD_DIM ** -0.5, 1.0).astype(F32)
        cos = cos_ref[...] * scale
        sin = sin_ref[...] * scale
        lane = lax.broadcasted_iota(jnp.int32, (INP_TM, LANES), 1)
        first_half = (lane % DA_HEAD_DIM) < (DA_HEAD_DIM // 2)
        for c in range(INP_TN // LANES):
            t = acc[:, c * LANES:(c + 1) * LANES]
            swapped = jnp.where(first_half, pltpu.roll(t, LANES - 32, 1), pltpu.roll(t, 32, 1))
            o_ref[:, c * LANES:(c + 1) * LANES] = (t * cos + swapped * sin).astype(BF16)

    @pl.when(jnp.logical_not(is_rope))
    def _():
        o_ref[...] = acc.astype(BF16)


def _inproj(h, w_all, layer, cos_t, sin_t):
    t = h.shape[0]
    return pl.pallas_call(
        _inproj_kernel,
        grid=(t // INP_TM, IN_TOTAL // INP_TN),
        in_specs=[
            pl.BlockSpec((INP_TM, D_MODEL), lambda i, j: (i, 0)),
            pl.BlockSpec((None, D_MODEL, INP_TN), lambda i, j: (layer, 0, j)),
            pl.BlockSpec((INP_TM, LANES), lambda i, j: (i, 0)),
            pl.BlockSpec((INP_TM, LANES), lambda i, j: (i, 0)),
        ],
        out_specs=pl.BlockSpec((INP_TM, INP_TN), lambda i, j: (i, j)),
        out_shape=jax.ShapeDtypeStruct((t, IN_TOTAL), BF16),
        compiler_params=_params(("arbitrary", "arbitrary")),
        name="inproj",
    )(h, w_all, cos_t, sin_t)


ATT_TQ = 1024
ATT_TK = 1024


ATT_RG = 1024
ATT_DRG = 512


def _attn_kernel(lam_ref, g_ref, q_ref, k_ref, v_ref, o_ref, qs_ref, m_ref, l_ref, acc_ref):
    i = pl.program_id(1)
    tq = ATT_TQ
    q = q_ref[...]
    lane = lax.broadcasted_iota(jnp.int32, q.shape, 1)
    zero = jnp.zeros_like(q)
    qs_ref[0:tq, :] = jnp.where(lane < DA_HEAD_DIM, q, zero)
    qs_ref[tq:2 * tq, :] = jnp.where(lane >= DA_HEAD_DIM, q, zero)
    m_ref[...] = jnp.full(m_ref.shape, -jnp.inf, F32)
    l_ref[...] = jnp.zeros(l_ref.shape, F32)
    acc_ref[...] = jnp.zeros(acc_ref.shape, F32)

    def update_rows(r0, nrows, k, v, mask):
        n = k.shape[0]
        rows = slice(r0, r0 + nrows)
        s = lax.dot_general(qs_ref[rows, :], k, (((1,), (1,)), ((), ())), preferred_element_type=F32)
        if mask is not None:
            s = jnp.where(mask, s, -jnp.inf)
        tiles = [s[:, t * LANES:(t + 1) * LANES] for t in range(n // LANES)]
        mc = functools.reduce(jnp.maximum, tiles)
        m_old = m_ref[rows, :]
        m_new = jnp.maximum(m_old, jnp.max(mc, axis=1, keepdims=True))
        alpha = jnp.exp2(m_old - m_new)
        p_tiles = [jnp.exp2(t - m_new) for t in tiles]
        l_ref[rows, :] = alpha * l_ref[rows, :] + functools.reduce(jnp.add, p_tiles)
        p = jnp.concatenate(p_tiles, axis=1).astype(BF16)
        acc_ref[rows, :] = alpha * acc_ref[rows, :] + jnp.dot(p, v, preferred_element_type=F32)
        m_ref[rows, :] = m_new

    def body(c, carry):
        off = pl.multiple_of(c * ATT_TK, ATT_TK)
        k = k_ref[pl.ds(off, ATT_TK), :]
        v = v_ref[pl.ds(off, ATT_TK), :]
        for r0 in range(0, 2 * tq, ATT_RG):
            update_rows(r0, ATT_RG, k, v, None)
        return carry

    lax.fori_loop(0, i * (tq // ATT_TK), body, 0)

    off = pl.multiple_of(i * tq, tq)
    for r0 in range(0, 2 * tq, ATT_DRG):
        qo = r0 % tq
        n = qo + ATT_DRG
        row = lax.broadcasted_iota(jnp.int32, (ATT_DRG, n), 0)
        col = lax.broadcasted_iota(jnp.int32, (ATT_DRG, n), 1)
        update_rows(r0, ATT_DRG, k_ref[pl.ds(off, n), :], v_ref[pl.ds(off, n), :], col <= row + qo)

    lam_init = lam_ref[4:5, 0:1]
    lam = (jnp.exp(jnp.sum(lam_ref[0:1, :] * lam_ref[1:2, :], axis=1, keepdims=True))
           - jnp.exp(jnp.sum(lam_ref[2:3, :] * lam_ref[3:4, :], axis=1, keepdims=True))
           + lam_init)
    acc = acc_ref[...]
    inv_l = 1.0 / jnp.sum(l_ref[...], axis=1, keepdims=True)
    o = acc[0:tq] * inv_l[0:tq] - lam * (acc[tq:2 * tq] * inv_l[tq:2 * tq])
    o = _rms(o, g_ref[...], HEAD_NORM_EPS) * (1.0 - lam_init)
    o_ref[...] = o.astype(BF16)


def _attention(proj, lam_pack, head_g, layer):
    t = proj.shape[0]
    qb, kb, vb = COL_Q // LANES, COL_K // LANES, COL_V // LANES
    return pl.pallas_call(
        _attn_kernel,
        grid=(DA_HEADS, t // ATT_TQ),
        in_specs=[
            pl.BlockSpec((None, 8, DA_HEAD_DIM), lambda h, i: (layer, 0, 0)),
            _vec_spec(DA_V_DIM, layer, 2),
            pl.BlockSpec((ATT_TQ, LANES), lambda h, i: (i, qb + h)),
            pl.BlockSpec((t, LANES), lambda h, i: (0, kb + h)),
            pl.BlockSpec((t, LANES), lambda h, i: (0, vb + h)),
        ],
        out_specs=pl.BlockSpec((ATT_TQ, LANES), lambda h, i: (i, h)),
        out_shape=jax.ShapeDtypeStruct((t, DA_WIDTH), BF16),
        scratch_shapes=[
            pltpu.VMEM((2 * ATT_TQ, LANES), BF16),
            pltpu.VMEM((2 * ATT_TQ, LANES), F32),
            pltpu.VMEM((2 * ATT_TQ, LANES), F32),
            pltpu.VMEM((2 * ATT_TQ, LANES), F32),
        ],
        compiler_params=_params(("arbitrary", "arbitrary")),
        name="diff_attn",
    )(lam_pack, head_g, proj, proj, proj)


CV_TM = 512
CV_HALO = 32
CV_ROWS = 64
CV_SUB = 8
CV_SHIFT_ROWS = CV_TM + (CONV_TAPS - 1) // CV_SUB * CV_SUB


def _conv_kernel(a_ref, b_ref, w_ref, bias_ref, g_ref, beta_ref, o_ref, z_ref, zs_ref):
    i = pl.program_id(0)

    @pl.when(i == 0)
    def _():
        z_ref[0:CV_HALO, :] = jnp.zeros((CV_HALO, CV_WIDTH), F32)

    @pl.when(i > 0)
    def _():
        z_ref[0:CV_HALO, :] = z_ref[CV_TM:CV_TM + CV_HALO, :]

    a = a_ref[...].astype(F32)
    b = b_ref[...].astype(F32)
    z_ref[CV_HALO:CV_HALO + CV_TM, :] = a * jax.nn.sigmoid(b)

    base = CV_HALO - (CONV_TAPS - 1)
    for res in range(CV_SUB):
        n = CV_TM + (CONV_TAPS - 1 - res) // CV_SUB * CV_SUB
        zs_ref[res, 0:n, :] = z_ref[base + res:base + res + n, :]
    for r in range(0, CV_TM, CV_ROWS):
        acc = jnp.zeros((CV_ROWS, CV_WIDTH), F32) + bias_ref[...]
        for j in range(CONV_TAPS):
            blk, res = divmod(j, CV_SUB)
            lo = r + blk * CV_SUB
            acc = acc + w_ref[j:j + 1, :] * zs_ref[res, lo:lo + CV_ROWS, :]
        mu = jnp.mean(acc, axis=-1, keepdims=True)
        xc = acc - mu
        var = jnp.mean(xc * xc, axis=-1, keepdims=True)
        y = xc * lax.rsqrt(var + LN_EPS) * g_ref[...] + beta_ref[...]
        o_ref[r:r + CV_ROWS, :] = (y * jax.nn.sigmoid(y)).astype(BF16)


def _conv(proj, dw_w, dw_b, ln_g, ln_b, layer):
    t = proj.shape[0]
    ab, bb = COL_CVA // CV_WIDTH, COL_CVB // CV_WIDTH
    vec = _vec_spec(CV_WIDTH, layer, 1)
    return pl.pallas_call(
        _conv_kernel,
        grid=(t // CV_TM,),
        in_specs=[
            pl.BlockSpec((CV_TM, CV_WIDTH), lambda i: (i, ab)),
            pl.BlockSpec((CV_TM, CV_WIDTH), lambda i: (i, bb)),
            pl.BlockSpec((None, CONV_TAPS, CV_WIDTH), lambda i: (layer, 0, 0)),
            vec, vec, vec,
        ],
        out_specs=pl.BlockSpec((CV_TM, CV_WIDTH), lambda i: (i, 0)),
        out_shape=jax.ShapeDtypeStruct((t, CV_WIDTH), BF16),
        scratch_shapes=[pltpu.VMEM((CV_HALO + CV_TM, CV_WIDTH), F32),
                        pltpu.VMEM((CV_SUB, CV_SHIFT_ROWS, CV_WIDTH), F32)],
        compiler_params=_params(("arbitrary",)),
        name="conformer_conv",
    )(proj, proj, dw_w, dw_b, ln_g, ln_b)


SSM_CB = 128
SSM_RB = SSM_CB * SSM_CHUNK
SSM_SLAB = 2 * SSM_GROUP
SSM_SLABS = LANES // SSM_SLAB


def _pick_slabs(pieces, src_slab, lane_slab):
    out = None
    for k, piece in enumerate(pieces):
        shift = ((k - src_slab) * SSM_SLAB) % LANES
        moved = piece if shift == 0 else pltpu.roll(piece, shift, 1)
        out = moved if out is None else jnp.where(lane_slab == k, moved, out)
    return out


def _ssm_kernel(u_ref, t_ref, wre_ref, wim_ref, vre_ref, vim_ref, ar_ref, ai_ref, d_ref, y_ref,
                uf_ref, up_ref, sre_ref, sim_ref, yp_ref, yf_ref, xr_ref, xi_ref):
    npair, cb = SSM_PAIRS, SSM_CB

    @pl.when(pl.program_id(0) == 0)
    def _():
        xr_ref[...] = jnp.zeros(xr_ref.shape, F32)
        xi_ref[...] = jnp.zeros(xi_ref.shape, F32)

    ntile = SSM_WIDTH // LANES
    for j in range(ntile):
        uf_ref[j] = u_ref[:, j * LANES:(j + 1) * LANES].astype(F32)
    lane_slab = lax.shift_right_logical(lax.broadcasted_iota(jnp.int32, (cb, LANES), 1), 5)

    for q in range(SSM_CHUNK // SSM_SLABS):
        for tile in range(ntile):
            src = uf_ref.at[tile]
            pieces = [src[pl.ds(SSM_SLABS * q + k, cb, stride=SSM_CHUNK), :] for k in range(SSM_SLABS)]
            for slab in range(SSM_SLABS):
                up_ref[SSM_SLABS * tile + slab, :, q * LANES:(q + 1) * LANES] = _pick_slabs(
                    pieces, slab, lane_slab).astype(BF16)

    for p in range(npair):
        u = up_ref[p]
        sre_ref[pl.ds(p, cb, stride=npair), :] = jnp.dot(u, wre_ref[p], preferred_element_type=F32)
        sim_ref[pl.ds(p, cb, stride=npair), :] = jnp.dot(u, wim_ref[p], preferred_element_type=F32)

    ar = ar_ref[...]
    ai = ai_ref[...]

    def body(c, carry):
        xr, xi = carry
        off = pl.multiple_of(c * npair, npair)
        sr = sre_ref[pl.ds(off, npair), :]
        si = sim_ref[pl.ds(off, npair), :]
        sre_ref[pl.ds(off, npair), :] = xr
        sim_ref[pl.ds(off, npair), :] = xi
        return ar * xr - ai * xi + sr, ar * xi + ai * xr + si

    xr, xi = lax.fori_loop(0, cb, body, (xr_ref[...], xi_ref[...]), unroll=8)
    xr_ref[...] = xr
    xi_ref[...] = xi

    for p in range(npair):
        u = up_ref[p]
        sr = sre_ref[pl.ds(p, cb, stride=npair), :].astype(BF16)
        si = sim_ref[pl.ds(p, cb, stride=npair), :].astype(BF16)
        nt = (((1,), (1,)), ((), ()))
        y = jnp.dot(u, t_ref[p], preferred_element_type=F32)
        y = y + lax.dot_general(sr, vre_ref[p], nt, preferred_element_type=F32)
        yp_ref[p] = y + lax.dot_general(si, vim_ref[p], nt, preferred_element_type=F32)

    for tile in range(ntile):
        for q in range(SSM_CHUNK // SSM_SLABS):
            pieces = [yp_ref[SSM_SLABS * tile + k, :, q * LANES:(q + 1) * LANES] for k in range(SSM_SLABS)]
            dst = yf_ref.at[tile]
            for slab in range(SSM_SLABS):
                dst[pl.ds(SSM_SLABS * q + slab, cb, stride=SSM_CHUNK), :] = _pick_slabs(pieces, slab, lane_slab)

    for j in range(ntile):
        cols = slice(j * LANES, (j + 1) * LANES)
        y_ref[:, cols] = (yf_ref[j] + uf_ref[j] * d_ref[:, cols]).astype(BF16)


def _ssm(proj, mats):
    t = proj.shape[0]
    vm = pl.BlockSpec(memory_space=pltpu.VMEM)
    return pl.pallas_call(
        _ssm_kernel,
        grid=(t // SSM_RB,),
        in_specs=[pl.BlockSpec((SSM_RB, SSM_WIDTH), lambda i: (i, COL_SSM // SSM_WIDTH))] + [vm] * 8,
        out_specs=pl.BlockSpec((SSM_RB, SSM_WIDTH), lambda i: (i, 0)),
        out_shape=jax.ShapeDtypeStruct((t, SSM_WIDTH), BF16),
        scratch_shapes=[
            pltpu.VMEM((SSM_WIDTH // LANES, SSM_RB, LANES), F32),
            pltpu.VMEM((SSM_PAIRS, SSM_CB, SSM_PAIR_W), BF16),
            pltpu.VMEM((SSM_CB * SSM_PAIRS, LANES), F32),
            pltpu.VMEM((SSM_CB * SSM_PAIRS, LANES), F32),
            pltpu.VMEM((SSM_PAIRS, SSM_CB, SSM_PAIR_W), F32),
            pltpu.VMEM((SSM_WIDTH // LANES, SSM_RB, LANES), F32),
            pltpu.VMEM((SSM_PAIRS, LANES), F32),
            pltpu.VMEM((SSM_PAIRS, LANES), F32),
        ],
        compiler_params=_params(("arbitrary",)),
        name="s5_scan",
    )(proj, *mats)


def _ssm_gen_kernel(k_ref, bre_ref, bim_ref, cre_ref, cim_ref, pr_ref, pi_ref,
                    t_ref, wre_ref, wim_ref, vre_ref, vim_ref):
    k = k_ref[...]
    lane = lax.broadcasted_iota(jnp.int32, k.shape, 1)
    bre, bim, cre, cim = bre_ref[...], bim_ref[...], cre_ref[...], cim_ref[...]
    for s in range(SSM_CHUNK):
        rows = slice(s * SSM_SLAB, (s + 1) * SSM_SLAB)
        moved = k if s == 0 else pltpu.roll(k, s * SSM_SLAB, 1)
        t_ref[rows, :] = jnp.where(lane >= s * SSM_SLAB, moved, 0.0).astype(BF16)
        pr = pr_ref[SSM_CHUNK - 1 - s:SSM_CHUNK - s, :]
        pi = pi_ref[SSM_CHUNK - 1 - s:SSM_CHUNK - s, :]
        wre_ref[rows, :] = (bre * pr - bim * pi).astype(BF16)
        wim_ref[rows, :] = (bre * pi + bim * pr).astype(BF16)
        qr = pr_ref[s + 1:s + 2, :]
        qi = pi_ref[s + 1:s + 2, :]
        vre_ref[rows, :] = (cre * qr - cim * qi).astype(BF16)
        vim_ref[rows, :] = (-(cre * qi + cim * qr)).astype(BF16)


def _ssm_matrices(lam_re, lam_im, log_dt, b_re, b_im, c_re, c_im, d_skip):
    hp = lax.Precision.HIGHEST
    nl = lam_re.shape[0]
    lr = lam_re.astype(F32)
    li = lam_im.astype(F32)
    dt = jnp.exp(log_dt.astype(F32))[..., None]
    mag = jnp.exp(lr * dt)
    ab_re = mag * jnp.cos(li * dt)
    ab_im = mag * jnp.sin(li * dt)
    den = lr * lr + li * li
    f_re = ((ab_re - 1.0) * lr + ab_im * li) / den
    f_im = (ab_im * lr - (ab_re - 1.0) * li) / den
    br = b_re.astype(F32)
    bi = b_im.astype(F32)
    bb_re = f_re[..., None] * br - f_im[..., None] * bi
    bb_im = f_re[..., None] * bi + f_im[..., None] * br
    tau = jnp.arange(SSM_CHUNK + 1, dtype=F32)[None, None, :, None]
    pmag = jnp.exp(tau * (lr * dt)[:, :, None, :])
    pw_re = pmag * jnp.cos(tau * (li * dt)[:, :, None, :])
    pw_im = pmag * jnp.sin(tau * (li * dt)[:, :, None, :])
    cr = c_re.astype(F32)
    ci = c_im.astype(F32)
    ca_re = cr[:, :, None] * pw_re[:, :, :SSM_CHUNK, None, :] - ci[:, :, None] * pw_im[:, :, :SSM_CHUNK, None, :]
    ca_im = cr[:, :, None] * pw_im[:, :, :SSM_CHUNK, None, :] + ci[:, :, None] * pw_re[:, :, :SSM_CHUNK, None, :]
    kk = (jnp.einsum('lgthp,lgpk->lgthk', ca_re, bb_re, precision=hp)
          - jnp.einsum('lgthp,lgpk->lgthk', ca_im, bb_im, precision=hp))
    eye2 = jnp.eye(2, dtype=F32)

    def paired(a):
        return a.reshape((nl, SSM_PAIRS, 2) + a.shape[2:])

    kcat = jnp.einsum('lpexhk,ef->lpekxfh', paired(kk), eye2).reshape(nl, SSM_PAIRS, SSM_SLAB, SSM_PAIR_W)
    bbd_re = jnp.einsum('lpenk,ef->lpekfn', paired(bb_re), eye2).reshape(nl, SSM_PAIRS, SSM_SLAB, LANES)
    bbd_im = jnp.einsum('lpenk,ef->lpekfn', paired(bb_im), eye2).reshape(nl, SSM_PAIRS, SSM_SLAB, LANES)
    cbd_re = jnp.einsum('lpehn,ef->lpehfn', paired(cr), eye2).reshape(nl, SSM_PAIRS, SSM_SLAB, LANES)
    cbd_im = jnp.einsum('lpehn,ef->lpehfn', paired(ci), eye2).reshape(nl, SSM_PAIRS, SSM_SLAB, LANES)
    pwp_re = paired(pw_re).transpose(0, 1, 3, 2, 4).reshape(nl, SSM_PAIRS, SSM_CHUNK + 1, LANES)
    pwp_im = paired(pw_im).transpose(0, 1, 3, 2, 4).reshape(nl, SSM_PAIRS, SSM_CHUNK + 1, LANES)

    def spec(r, c):
        return pl.BlockSpec((None, None, r, c), lambda l, p: (l, p, 0, 0))

    wide = jax.ShapeDtypeStruct((nl, SSM_PAIRS, SSM_PAIR_W, SSM_PAIR_W), BF16)
    tall = jax.ShapeDtypeStruct((nl, SSM_PAIRS, SSM_PAIR_W, LANES), BF16)
    tm, w_re, w_im, vt_re, vt_im = pl.pallas_call(
        _ssm_gen_kernel,
        grid=(nl, SSM_PAIRS),
        in_specs=[spec(SSM_SLAB, SSM_PAIR_W)] + [spec(SSM_SLAB, LANES)] * 4 + [spec(SSM_CHUNK + 1, LANES)] * 2,
        out_specs=[spec(SSM_PAIR_W, SSM_PAIR_W)] + [spec(SSM_PAIR_W, LANES)] * 4,
        out_shape=[wide, tall, tall, tall, tall],
        compiler_params=_params(("arbitrary", "arbitrary")),
        name="s5_matrices",
    )(kcat, bbd_re, bbd_im, cbd_re, cbd_im, pwp_re, pwp_im)
    a_re = pwp_re[:, :, SSM_CHUNK]
    a_im = pwp_im[:, :, SSM_CHUNK]
    d_t = d_skip.astype(F32).reshape(nl, 1, SSM_WIDTH)
    return tm, w_re, w_im, vt_re, vt_im, a_re, a_im, d_t


MG_TM = 512
MG_TN = 512


def _merge_mix_kernel(oa_ref, zb_ref, yc_ref, ga_ref, gb_ref, gc_ref, wda_ref, wcv_ref, wglu_ref,
                      bglu_ref, wso_ref, wmix_ref, x_ref, o_ref, sc_ref):
    n = pl.program_id(1)

    @pl.when(n == 0)
    def _():
        glu = jnp.dot(yc_ref[...], wglu_ref[...], preferred_element_type=F32) + bglu_ref[...]
        sc_ref[...] = (glu[:, :SSM_WIDTH] * jax.nn.sigmoid(glu[:, SSM_WIDTH:])).astype(BF16)
        o_ref[...] = x_ref[...]

    y_a = jnp.dot(oa_ref[...], wda_ref[...], preferred_element_type=F32)
    y_b = jnp.dot(zb_ref[...], wcv_ref[...], preferred_element_type=F32)
    y_c = jnp.dot(sc_ref[...], wso_ref[...], preferred_element_type=F32)
    merged = (jax.nn.sigmoid(ga_ref[...].astype(F32)) * y_a
              + jax.nn.sigmoid(gb_ref[...].astype(F32)) * y_b
              + jax.nn.sigmoid(gc_ref[...].astype(F32)) * y_c)
    o_ref[...] += jnp.dot(merged.astype(BF16), wmix_ref[...], preferred_element_type=F32)


def _merge_mix(o_a, z_b, y_c, proj, w_da, w_cv, w_glu, b_glu, w_so, w_mix, x, layer):
    t = o_a.shape[0]
    ga, gb, gc = COL_GA // MG_TN, COL_GB // MG_TN, COL_GC // MG_TN

    def wcol(k):
        return pl.BlockSpec((None, k, MG_TN), lambda i, n: (layer, 0, n))

    return pl.pallas_call(
        _merge_mix_kernel,
        grid=(t // MG_TM, D_MODEL // MG_TN),
        in_specs=[
            pl.BlockSpec((MG_TM, DA_WIDTH), lambda i, n: (i, 0)),
            pl.BlockSpec((MG_TM, CV_WIDTH), lambda i, n: (i, 0)),
            pl.BlockSpec((MG_TM, SSM_WIDTH), lambda i, n: (i, 0)),
            pl.BlockSpec((MG_TM, MG_TN), lambda i, n: (i, ga + n)),
            pl.BlockSpec((MG_TM, MG_TN), lambda i, n: (i, gb + n)),
            pl.BlockSpec((MG_TM, MG_TN), lambda i, n: (i, gc + n)),
            wcol(DA_WIDTH), wcol(CV_WIDTH),
            pl.BlockSpec((None, SSM_WIDTH, 2 * SSM_WIDTH), lambda i, n: (layer, 0, 0)),
            _vec_spec(2 * SSM_WIDTH, layer, 2),
            wcol(SSM_WIDTH),
            pl.BlockSpec((None, MG_TN, D_MODEL), lambda i, n: (layer, n, 0)),
            pl.BlockSpec((MG_TM, D_MODEL), lambda i, n: (i, 0)),
        ],
        out_specs=pl.BlockSpec((MG_TM, D_MODEL), lambda i, n: (i, 0)),
        out_shape=jax.ShapeDtypeStruct((t, D_MODEL), F32),
        scratch_shapes=[pltpu.VMEM((MG_TM, SSM_WIDTH), BF16)],
        compiler_params=_params(("arbitrary", "arbitrary")),
        name="merge_mix",
    )(o_a, z_b, y_c, proj, proj, proj, w_da, w_cv, w_glu, b_glu, w_so, w_mix, x)


def _norm_matmul_kernel(x_ref, g_ref, w_ref, o_ref):
    h = _rms(x_ref[...], g_ref[...], RMS_EPS).astype(BF16)
    o_ref[...] = jnp.dot(h, w_ref[...], preferred_element_type=F32).astype(o_ref.dtype)


def _mem_kv(mem, g_all, w_all, layer):
    m = mem.shape[0]
    n = 2 * XA_WIDTH
    return pl.pallas_call(
        _norm_matmul_kernel,
        grid=(1,),
        in_specs=[
            pl.BlockSpec((m, D_MODEL), lambda i: (0, 0)),
            _vec_spec(D_MODEL, layer, 1),
            pl.BlockSpec((None, D_MODEL, n), lambda i: (layer, 0, 0)),
        ],
        out_specs=pl.BlockSpec((m, n), lambda i: (0, 0)),
        out_shape=jax.ShapeDtypeStruct((m, n), BF16),
        compiler_params=_params(("arbitrary",)),
        name="mem_kv",
    )(mem, g_all, w_all)


XA_TM = 512


def _xattn_kernel(x_ref, g_ref, wq_ref, kv_ref, wo_ref, o_ref):
    x = x_ref[...]
    h = _rms(x, g_ref[...], RMS_EPS).astype(BF16)
    q = jnp.dot(h, wq_ref[...], preferred_element_type=F32).astype(BF16)
    heads = []
    for hd in range(XA_HEADS):
        lo = hd * XA_HEAD_DIM
        k = kv_ref[:, lo:lo + XA_HEAD_DIM]
        v = kv_ref[:, XA_WIDTH + lo:XA_WIDTH + lo + XA_HEAD_DIM]
        s = lax.dot_general(q[:, lo:lo + XA_HEAD_DIM], k, (((1,), (1,)), ((), ())),
                            preferred_element_type=F32) * (XA_HEAD_DIM ** -0.5)
        m = jnp.max(s, axis=-1, keepdims=True)
        e = jnp.exp(s - m)
        p = e / jnp.sum(e, axis=-1, keepdims=True)
        heads.append(jnp.dot(p.astype(BF16), v, preferred_element_type=F32).astype(BF16))
    o = jnp.concatenate(heads, axis=-1)
    o_ref[...] = x + jnp.dot(o, wo_ref[...], preferred_element_type=F32)


def _xattn(x, g_all, wq_all, kv, wo_all, layer):
    t = x.shape[0]
    return pl.pallas_call(
        _xattn_kernel,
        grid=(t // XA_TM,),
        in_specs=[
            pl.BlockSpec((XA_TM, D_MODEL), lambda i: (i, 0)),
            _vec_spec(D_MODEL, layer, 1),
            pl.BlockSpec((None, D_MODEL, XA_WIDTH), lambda i: (layer, 0, 0)),
            pl.BlockSpec((MEM_LEN, 2 * XA_WIDTH), lambda i: (0, 0)),
            pl.BlockSpec((None, XA_WIDTH, D_MODEL), lambda i: (layer, 0, 0)),
        ],
        out_specs=pl.BlockSpec((XA_TM, D_MODEL), lambda i: (i, 0)),
        out_shape=jax.ShapeDtypeStruct((t, D_MODEL), F32),
        compiler_params=_params(("arbitrary",)),
        name="mem_xattn",
    )(x, g_all, wq_all, kv, wo_all)


RT_TM = 512
_E_LANE0 = MOE_GROUPS


def _router_kernel(x_ref, g_ref, w_ref, b_ref, meta_ref, cnt_ref, run_ref):
    i = pl.program_id(0)

    @pl.when(i == 0)
    def _():
        run_ref[...] = jnp.zeros(run_ref.shape, F32)

    h = _rms(x_ref[...], g_ref[...], RMS_EPS).astype(BF16)
    logits = jnp.dot(h, w_ref[...], preferred_element_type=F32) + b_ref[...]
    lane = lax.broadcasted_iota(jnp.int32, logits.shape, 1).astype(F32)
    neg = jnp.float32(-jnp.inf)
    big = jnp.float32(LANES)

    def first_argmax(vals):
        top = jnp.max(vals, axis=-1, keepdims=True)
        idx = jnp.min(jnp.where(vals == top, lane, big), axis=-1, keepdims=True)
        return top, idx

    gl = jnp.where(lane < MOE_GROUPS, logits, neg)
    g_top, g_idx = first_argmax(gl)
    g_w = 1.0 / jnp.sum(jnp.exp(gl - g_top), axis=-1, keepdims=True)
    e_lane = lane - _E_LANE0
    in_group = jnp.logical_and(e_lane >= g_idx * MOE_PER_GROUP, e_lane < (g_idx + 1) * MOE_PER_GROUP)
    el = jnp.where(in_group, logits, neg)
    v1, i1 = first_argmax(el)
    el2 = jnp.where(lane == i1, neg, el)
    v2, i2 = first_argmax(el2)
    e2 = jnp.exp(v2 - v1)
    w1 = 1.0 / (1.0 + e2)
    w2 = e2 / (1.0 + e2)
    oh1 = (lane == i1).astype(F32)
    oh2 = (lane == i2).astype(F32)
    both = oh1 + oh2
    row = lax.broadcasted_iota(jnp.int32, (RT_TM, RT_TM), 0)
    col = lax.broadcasted_iota(jnp.int32, (RT_TM, RT_TM), 1)
    earlier = jnp.where(col < row, 1.0, 0.0).astype(BF16)
    before = jnp.dot(earlier, both.astype(BF16), preferred_element_type=F32) + run_ref[...]
    rank1 = jnp.sum(oh1 * before, axis=-1, keepdims=True)
    rank2 = jnp.sum(oh2 * before, axis=-1, keepdims=True)
    run_ref[...] += jnp.sum(both, axis=0, keepdims=True)
    cnt_ref[...] = run_ref[...]
    meta = jnp.where(lane == 0.0, i1 - _E_LANE0, 0.0)
    for k, val in enumerate((i2 - _E_LANE0, rank1, rank2, w1 * g_w, w2 * g_w), start=1):
        meta = jnp.where(lane == float(k), val, meta)
    meta_ref[...] = meta


def _router(x, g_all, w_r, b_r, layer):
    t = x.shape[0]
    return pl.pallas_call(
        _router_kernel,
        grid=(t // RT_TM,),
        in_specs=[
            pl.BlockSpec((RT_TM, D_MODEL), lambda i: (i, 0)),
            _vec_spec(D_MODEL, layer, 1),
            pl.BlockSpec((None, D_MODEL, LANES), lambda i: (layer, 0, 0)),
            _vec_spec(LANES, layer, 1),
        ],
        out_specs=[
            pl.BlockSpec((RT_TM, LANES), lambda i: (i, 0)),
            pl.BlockSpec((1, LANES), lambda i: (0, 0)),
        ],
        out_shape=[jax.ShapeDtypeStruct((t, LANES), F32), jax.ShapeDtypeStruct((1, LANES), F32)],
        scratch_shapes=[pltpu.VMEM((1, LANES), F32)],
        compiler_params=_params(("arbitrary",)),
        name="moe_router",
    )(x, g_all, w_r, b_r)


EX_TM = 256
EX_ROWS = 2 * SEQ
EX_TILES = EX_ROWS // EX_TM
EX_STEPS = EX_TILES + MOE_EXPERTS - 1


def _row_copy(src, src_row, dst, dst_row, sem):
    return pltpu.make_async_copy(src.at[pl.ds(src_row, 1), :], dst.at[pl.ds(dst_row, 1), :], sem)


DP_TM = 256


def _dispatch_kernel(p1_ref, p2_ref, x_ref, xs_hbm, sem):
    base = pl.program_id(0) * DP_TM

    def issue(j, c):
        _row_copy(x_ref, j, xs_hbm, p1_ref[base + j], sem).start()
        _row_copy(x_ref, j, xs_hbm, p2_ref[base + j], sem).start()
        return c

    lax.fori_loop(0, DP_TM, issue, 0, unroll=8)

    def wait(j, c):
        _row_copy(x_ref, j, xs_hbm, 0, sem).wait()
        _row_copy(x_ref, j, xs_hbm, 0, sem).wait()
        return c

    lax.fori_loop(0, DP_TM, wait, 0, unroll=8)


def _dispatch(pos1, pos2, x):
    t = x.shape[0]
    grid_spec = pltpu.PrefetchScalarGridSpec(
        num_scalar_prefetch=2,
        grid=(t // DP_TM,),
        in_specs=[pl.BlockSpec((DP_TM, D_MODEL), lambda i, p1, p2: (i, 0))],
        out_specs=pl.BlockSpec(memory_space=pl.ANY),
        scratch_shapes=[pltpu.SemaphoreType.DMA(())],
    )
    return pl.pallas_call(
        _dispatch_kernel,
        grid_spec=grid_spec,
        out_shape=jax.ShapeDtypeStruct((EX_ROWS, D_MODEL), F32),
        compiler_params=_params(("arbitrary",)),
        name="moe_dispatch",
    )(pos1, pos2, x)


def _experts_kernel(se_ref, st_ref, lo_ref, hi_ref, first_ref, wnew_ref, x_ref, g_ref, wg_ref, wu_ref,
                    wd_ref, o_ref, wgb_ref, wub_ref, wdb_ref):
    s = pl.program_id(0)
    lo = lo_ref[s]
    hi = hi_ref[s]

    @pl.when(wnew_ref[s] == 1)
    def _():
        wgb_ref[...] = wg_ref[...].astype(BF16)
        wub_ref[...] = wu_ref[...].astype(BF16)
        wdb_ref[...] = wd_ref[...].astype(BF16)

    @pl.when(hi > lo)
    def _():
        h = _rms(x_ref[...], g_ref[...], RMS_EPS).astype(BF16)
        gate = jnp.dot(h, wgb_ref[...], preferred_element_type=F32)
        up = jnp.dot(h, wub_ref[...], preferred_element_type=F32)
        act = (gate * jax.nn.sigmoid(gate) * up).astype(BF16)
        res = jnp.dot(act, wdb_ref[...], preferred_element_type=F32)
        row = lax.broadcasted_iota(jnp.int32, (EX_TM, 1), 0)
        mine = jnp.logical_and(row >= lo, row < hi)

        @pl.when(first_ref[s] == 1)
        def _():
            o_ref[...] = jnp.where(mine, res, 0.0)

        @pl.when(first_ref[s] == 0)
        def _():
            o_ref[...] = jnp.where(mine, res, o_ref[...])


def _experts(plan, xs_sorted, g_all, wg_all, wu_all, wd_all, layer):
    def wspec(k, n):
        return pl.BlockSpec((None, None, k, n), lambda s, se, st, lo, hi, fi, wn: (layer, se[s], 0, 0))

    grid_spec = pltpu.PrefetchScalarGridSpec(
        num_scalar_prefetch=6,
        grid=(EX_STEPS,),
        in_specs=[
            pl.BlockSpec((EX_TM, D_MODEL), lambda s, se, st, lo, hi, fi, wn: (st[s], 0)),
            pl.BlockSpec((None, 1, D_MODEL), lambda s, se, st, lo, hi, fi, wn: (layer, 0, 0)),
            wspec(D_MODEL, MOE_FF), wspec(D_MODEL, MOE_FF), wspec(MOE_FF, D_MODEL),
        ],
        out_specs=pl.BlockSpec((EX_TM, D_MODEL), lambda s, se, st, lo, hi, fi, wn: (st[s], 0)),
        scratch_shapes=[pltpu.VMEM((D_MODEL, MOE_FF), BF16), pltpu.VMEM((D_MODEL, MOE_FF), BF16),
                        pltpu.VMEM((MOE_FF, D_MODEL), BF16)],
    )
    return pl.pallas_call(
        _experts_kernel,
        grid_spec=grid_spec,
        out_shape=jax.ShapeDtypeStruct((EX_ROWS, D_MODEL), F32),
        compiler_params=_params(("arbitrary",)),
        name="moe_experts",
    )(*plan, xs_sorted, g_all, wg_all, wu_all, wd_all)


CB_TM = 256


def _combine_kernel(p1_ref, p2_ref, y_hbm, x_ref, meta_ref, g_ref, o_ref, n_ref, y1buf, y2buf, sem):
    i = pl.program_id(0)
    slot = lax.rem(i, 2)

    def fetch(tile, slot_):
        base = tile * CB_TM

        def issue(j, c):
            _row_copy(y_hbm, p1_ref[base + j], y1buf.at[slot_], j, sem.at[slot_]).start()
            _row_copy(y_hbm, p2_ref[base + j], y2buf.at[slot_], j, sem.at[slot_]).start()
            return c

        lax.fori_loop(0, CB_TM, issue, 0, unroll=8)

    @pl.when(i == 0)
    def _():
        fetch(0, 0)

    @pl.when(i + 1 < pl.num_programs(0))
    def _():
        fetch(i + 1, 1 - slot)

    def wait(j, c):
        _row_copy(y_hbm, 0, y1buf.at[slot], j, sem.at[slot]).wait()
        _row_copy(y_hbm, 0, y2buf.at[slot], j, sem.at[slot]).wait()
        return c

    lax.fori_loop(0, CB_TM, wait, 0, unroll=8)
    meta = meta_ref[...]
    x_new = x_ref[...] + meta[:, 4:5] * y1buf[slot] + meta[:, 5:6] * y2buf[slot]
    o_ref[...] = x_new
    n_ref[...] = _rms(x_new, g_ref[...], RMS_EPS).astype(n_ref.dtype)


def _combine(pos1, pos2, ys, x, meta, g_next, next_dtype):
    t = x.shape[0]
    tile = pl.BlockSpec((CB_TM, D_MODEL), lambda i, p1, p2: (i, 0))
    grid_spec = pltpu.PrefetchScalarGridSpec(
        num_scalar_prefetch=2,
        grid=(t // CB_TM,),
        in_specs=[
            pl.BlockSpec(memory_space=pl.ANY),
            tile,
            pl.BlockSpec((CB_TM, LANES), lambda i, p1, p2: (i, 0)),
            pl.BlockSpec((1, D_MODEL), lambda i, p1, p2: (0, 0)),
        ],
        out_specs=[tile, tile],
        scratch_shapes=[pltpu.VMEM((2, CB_TM, D_MODEL), F32), pltpu.VMEM((2, CB_TM, D_MODEL), F32),
                        pltpu.SemaphoreType.DMA((2,))],
    )
    return pl.pallas_call(
        _combine_kernel,
        grid_spec=grid_spec,
        out_shape=[jax.ShapeDtypeStruct((t, D_MODEL), F32), jax.ShapeDtypeStruct((t, D_MODEL), next_dtype)],
        compiler_params=_params(("arbitrary",)),
        name="moe_combine",
    )(pos1, pos2, ys, x, meta, g_next)


def _dispatch_plan(meta, cnt):
    t = meta.shape[0]
    e1 = meta[:, 0].astype(jnp.int32)
    e2 = meta[:, 1].astype(jnp.int32)
    r1 = meta[:, 2].astype(jnp.int32)
    r2 = meta[:, 3].astype(jnp.int32)
    counts = cnt[0, _E_LANE0:_E_LANE0 + MOE_EXPERTS].astype(jnp.int32)
    ends = jnp.cumsum(counts)
    starts = ends - counts
    pos1 = starts[e1] + r1
    pos2 = starts[e2] + r2
    t_lo = starts // EX_TM
    t_hi = (ends + EX_TM - 1) // EX_TM
    nsteps = jnp.where(counts > 0, t_hi - t_lo, 0)
    step_end = jnp.cumsum(nsteps)
    step_start = step_end - nsteps
    s = jnp.arange(EX_STEPS, dtype=jnp.int32)
    se = jnp.minimum(jnp.sum((s[:, None] >= step_end[None, :]).astype(jnp.int32), axis=1), MOE_EXPERTS - 1)
    valid = s < step_end[-1]
    st = jnp.where(valid, t_lo[se] + s - step_start[se], EX_TILES - 1)
    lo = jnp.where(valid, jnp.clip(starts[se] - st * EX_TM, 0, EX_TM), 0)
    hi = jnp.where(valid, jnp.clip(ends[se] - st * EX_TM, 0, EX_TM), 0)
    first = jnp.concatenate([jnp.ones((1,), jnp.int32), (st[1:] != st[:-1]).astype(jnp.int32)])
    wnew = jnp.concatenate([jnp.ones((1,), jnp.int32), (se[1:] != se[:-1]).astype(jnp.int32)])
    plan = tuple(a.astype(jnp.int32) for a in (se, st, lo, hi, first, wnew))
    return pos1.astype(jnp.int32), pos2.astype(jnp.int32), plan


FN_TM = 512


def _norm_kernel(x_ref, g_ref, o_ref):
    o_ref[...] = _rms(x_ref[...], g_ref[...], RMS_EPS).astype(o_ref.dtype)


def _first_norm(x, g_all):
    t = x.shape[0]
    return pl.pallas_call(
        _norm_kernel,
        grid=(t // FN_TM,),
        in_specs=[pl.BlockSpec((FN_TM, D_MODEL), lambda i: (i, 0)), _vec_spec(D_MODEL, 0, 1)],
        out_specs=pl.BlockSpec((FN_TM, D_MODEL), lambda i: (i, 0)),
        out_shape=jax.ShapeDtypeStruct((t, D_MODEL), BF16),
        compiler_params=_params(("arbitrary",)),
        name="first_norm",
    )(x, g_all)


def kernel(x, mem, positions, norm_mix, w_in, da_lam_q1, da_lam_k1, da_lam_q2, da_lam_k2, da_head_norm, w_da_out, cv_dw_w, cv_dw_b, cv_ln_g, cv_ln_b, w_cv_out, ssm_lam_re, ssm_lam_im, ssm_log_dt, ssm_b_re, ssm_b_im, ssm_c_re, ssm_c_im, ssm_d, w_ssm_glu, b_ssm_glu, w_ssm_out, w_mix_out, norm_xa, norm_mem, w_xa_q, w_xa_kv, w_xa_out, norm_ffn, w_router_group, b_router_group, w_router_expert, b_router_expert, w_exp_gate, w_exp_up, w_exp_down, norm_final):
    bsz, seq, _ = x.shape
    assert bsz == 1 and seq == SEQ
    nl = w_in.shape[0]
    xs = x.reshape(seq, D_MODEL).astype(F32)
    mem2 = mem.reshape(MEM_LEN, D_MODEL).astype(F32)

    inv_freq = ROPE_THETA ** (-jnp.arange(0, DA_HEAD_DIM, 2, dtype=F32) / DA_HEAD_DIM)
    ang = positions.reshape(seq).astype(F32)[:, None] * inv_freq
    cos = jnp.cos(ang)
    sin = jnp.sin(ang)
    cos_t = jnp.concatenate([cos, cos, cos, cos], axis=-1)
    sin_t = jnp.concatenate([-sin, sin, -sin, sin], axis=-1)

    w_da_b = w_da_out.astype(BF16)
    w_cv_b = w_cv_out.astype(BF16)
    w_glu_b = w_ssm_glu.astype(BF16)
    w_so_b = w_ssm_out.astype(BF16)
    w_mix_b = w_mix_out.astype(BF16)
    w_xq_b = w_xa_q.astype(BF16)
    w_xkv_b = w_xa_kv.astype(BF16)
    w_xo_b = w_xa_out.astype(BF16)
    pad = LANES - MOE_GROUPS - MOE_EXPERTS
    w_r = jnp.concatenate([w_router_group, w_router_expert,
                           jnp.zeros((nl, D_MODEL, pad), F32)], axis=-1).astype(BF16)
    b_r = jnp.concatenate([b_router_group, b_router_expert, jnp.zeros((nl, pad), F32)], axis=-1).astype(F32)

    def vec3(a):
        return a.astype(F32).reshape(nl, 1, a.shape[-1])

    norm_mix, da_head_norm, cv_dw_b, cv_ln_g, cv_ln_b, b_ssm_glu, norm_xa, norm_mem, norm_ffn, b_r = map(
        vec3, (norm_mix, da_head_norm, cv_dw_b, cv_ln_g, cv_ln_b, b_ssm_glu, norm_xa, norm_mem, norm_ffn, b_r))
    cv_dw_w = cv_dw_w.astype(F32)

    lam_inits = jnp.asarray([0.8 - 0.6 * math.exp(-0.3 * l) for l in range(nl)], F32)
    lam_pack = jnp.stack([da_lam_q1, da_lam_k1, da_lam_q2, da_lam_k2], axis=1).astype(F32)
    lam_pack = jnp.concatenate(
        [lam_pack, jnp.broadcast_to(lam_inits[:, None, None], (nl, 4, DA_HEAD_DIM))], axis=1)

    ssm_mats = _ssm_matrices(ssm_lam_re, ssm_lam_im, ssm_log_dt, ssm_b_re, ssm_b_im,
                             ssm_c_re, ssm_c_im, ssm_d)

    h = _first_norm(xs, norm_mix)
    for l in range(nl):
        proj = _inproj(h, w_in, l, cos_t, sin_t)
        o_a = _attention(proj, lam_pack, da_head_norm, l)
        z_b = _conv(proj, cv_dw_w, cv_dw_b, cv_ln_g, cv_ln_b, l)
        y_c = _ssm(proj, [m[l] for m in ssm_mats])
        xs = _merge_mix(o_a, z_b, y_c, proj, w_da_b, w_cv_b, w_glu_b, b_ssm_glu, w_so_b, w_mix_b, xs, l)
        kv = _mem_kv(mem2, norm_mem, w_xkv_b, l)
        xs = _xattn(xs, norm_xa, w_xq_b, kv, w_xo_b, l)
        meta, cnt = _router(xs, norm_ffn, w_r, b_r, l)
        pos1, pos2, plan = _dispatch_plan(meta, cnt)
        xs_sorted = _dispatch(pos1, pos2, xs)
        ys = _experts(plan, xs_sorted, norm_ffn, w_exp_gate, w_exp_up, w_exp_down, l)
        if l + 1 < nl:
            xs, h = _combine(pos1, pos2, ys, xs, meta, norm_mix[l + 1], BF16)
        else:
            xs, out = _combine(pos1, pos2, ys, xs, meta, norm_final.astype(F32).reshape(1, D_MODEL), F32)
    return out.reshape(bsz, seq, D_MODEL)
```

```python
import functools
import math

import jax
import jax.numpy as jnp
from jax import lax
from jax.experimental import pallas as pl
from jax.experimental.pallas import tpu as pltpu

F32 = jnp.float32
BF16 = jnp.bfloat16

D_MODEL = 2048
SEQ = 8192
DEPTH = 4
MEM_LEN = 256
DA_HEADS = 8
DA_HEAD_DIM = 64
DA_V_DIM = 128
DA_WIDTH = 1024
ROPE_THETA = 10000.0
CV_WIDTH = 512
CONV_TAPS = 31
SSM_WIDTH = 512
SSM_GROUP = 16
SSM_GROUPS = 32
SSM_STATE = 64
XA_HEADS = 4
XA_HEAD_DIM = 128
XA_WIDTH = 512
MOE_GROUPS = 4
MOE_PER_GROUP = 4
MOE_EXPERTS = 16
MOE_FF = 512
RMS_EPS = 1e-6
HEAD_NORM_EPS = 1e-5
LN_EPS = 1e-5

COL_Q = 0
COL_K = COL_Q + 1024
COL_V = COL_K + 1024
COL_CVA = COL_V + 1024
COL_CVB = COL_CVA + CV_WIDTH
COL_SSM = COL_CVB + CV_WIDTH
COL_GA = COL_SSM + SSM_WIDTH
COL_GB = COL_GA + D_MODEL
COL_GC = COL_GB + D_MODEL
IN_TOTAL = COL_GC + D_MODEL

LANES = 128
VMEM_LIMIT = 56 * 1024 * 1024

SSM_CHUNK = 16
SSM_NCHUNK = SEQ // SSM_CHUNK
SSM_PAIRS = SSM_GROUPS // 2
SSM_PAIR_W = 2 * SSM_CHUNK * SSM_GROUP


def _params(sem, vmem=VMEM_LIMIT):
    return pltpu.CompilerParams(dimension_semantics=sem, vmem_limit_bytes=vmem)


def _vec_spec(width, layer, ngrid):
    if ngrid == 1:
        return pl.BlockSpec((None, 1, width), lambda i: (layer, 0, 0))
    return pl.BlockSpec((None, 1, width), lambda i, j: (layer, 0, 0))


def _rms(xf, g, eps):
    ms = jnp.mean(xf * xf, axis=-1, keepdims=True)
    return xf * lax.rsqrt(ms + eps) * g


INP_TM = 2048
INP_TN = 512
_Q_TILE0 = COL_Q // INP_TN
_K_TILE0 = COL_K // INP_TN
_V_TILE0 = COL_V // INP_TN


def _inproj_kernel(h_ref, w_ref, cos_ref, sin_ref, o_ref):
    j = pl.program_id(1)
    acc = jnp.dot(h_ref[...], w_ref[...].astype(BF16), preferred_element_type=F32)
    is_rope = jnp.logical_and(j >= _Q_TILE0, j < _V_TILE0)

    @pl.when(is_rope)
    def _():
        scale = jnp.where(j < _K_TILE0, math.log2(math.e) * DA_HEAD_DIM ** -0.5, 1.0).astype(F32)
        cos = cos_ref[...] * scale
        sin = sin_ref[...] * scale
        lane = lax.broadcasted_iota(jnp.int32, (INP_TM, LANES), 1)
        first_half = (lane % DA_HEAD_DIM) < (DA_HEAD_DIM // 2)
        for c in range(INP_TN // LANES):
            t = acc[:, c * LANES:(c + 1) * LANES]
            swapped = jnp.where(first_half, pltpu.roll(t, LANES - 32, 1), pltpu.roll(t, 32, 1))
            o_ref[:, c * LANES:(c + 1) * LANES] = (t * cos + swapped * sin).astype(BF16)

    @pl.when(jnp.logical_not(is_rope))
    def _():
        o_ref[...] = acc.astype(BF16)


def _inproj(h, w_all, layer, cos_t, sin_t):
    t = h.shape[0]
    return pl.pallas_call(
        _inproj_kernel,
        grid=(t // INP_TM, IN_TOTAL // INP_TN),
        in_specs=[
            pl.BlockSpec((INP_TM, D_MODEL), lambda i, j: (i, 0)),
            pl.BlockSpec((None, D_MODEL, INP_TN), lambda i, j: (layer, 0, j)),
            pl.BlockSpec((INP_TM, LANES), lambda i, j: (i, 0)),
            pl.BlockSpec((INP_TM, LANES), lambda i, j: (i, 0)),
        ],
        out_specs=pl.BlockSpec((INP_TM, INP_TN), lambda i, j: (i, j)),
        out_shape=jax.ShapeDtypeStruct((t, IN_TOTAL), BF16),
        compiler_params=_params(("arbitrary", "arbitrary")),
        name="inproj",
    )(h, w_all, cos_t, sin_t)


ATT_TQ = 1024
ATT_TK = 1024


ATT_RG = 1024
ATT_DRG = 512


def _attn_kernel(lam_ref, g_ref, q_ref, k_ref, v_ref, o_ref, qs_ref, m_ref, l_ref, acc_ref):
    i = pl.program_id(1)
    tq = ATT_TQ
    q = q_ref[...]
    lane = lax.broadcasted_iota(jnp.int32, q.shape, 1)
    zero = jnp.zeros_like(q)
    qs_ref[0:tq, :] = jnp.where(lane < DA_HEAD_DIM, q, zero)
    qs_ref[tq:2 * tq, :] = jnp.where(lane >= DA_HEAD_DIM, q, zero)
    m_ref[...] = jnp.full(m_ref.shape, -jnp.inf, F32)
    l_ref[...] = jnp.zeros(l_ref.shape, F32)
    acc_ref[...] = jnp.zeros(acc_ref.shape, F32)

    def update_rows(r0, nrows, k, v, mask):
        n = k.shape[0]
        rows = slice(r0, r0 + nrows)
        s = lax.dot_general(qs_ref[rows, :], k, (((1,), (1,)), ((), ())), preferred_element_type=F32)
        if mask is not None:
            s = jnp.where(mask, s, -jnp.inf)
        tiles = [s[:, t * LANES:(t + 1) * LANES] for t in range(n // LANES)]
        mc = functools.reduce(jnp.maximum, tiles)
        m_old = m_ref[rows, :]
        m_new = jnp.maximum(m_old, jnp.max(mc, axis=1, keepdims=True))
        alpha = jnp.exp2(m_old - m_new)
        p_tiles = [jnp.exp2(t - m_new) for t in tiles]
        l_ref[rows, :] = alpha * l_ref[rows, :] + functools.reduce(jnp.add, p_tiles)
        p = jnp.concatenate(p_tiles, axis=1).astype(BF16)
        acc_ref[rows, :] = alpha * acc_ref[rows, :] + jnp.dot(p, v, preferred_element_type=F32)
        m_ref[rows, :] = m_new

    def body(c, carry):
        off = pl.multiple_of(c * ATT_TK, ATT_TK)
        k = k_ref[pl.ds(off, ATT_TK), :]
        v = v_ref[pl.ds(off, ATT_TK), :]
        for r0 in range(0, 2 * tq, ATT_RG):
            update_rows(r0, ATT_RG, k, v, None)
        return carry

    lax.fori_loop(0, i * (tq // ATT_TK), body, 0)

    off = pl.multiple_of(i * tq, tq)
    for r0 in range(0, 2 * tq, ATT_DRG):
        qo = r0 % tq
        n = qo + ATT_DRG
        row = lax.broadcasted_iota(jnp.int32, (ATT_DRG, n), 0)
        col = lax.broadcasted_iota(jnp.int32, (ATT_DRG, n), 1)
        update_rows(r0, ATT_DRG, k_ref[pl.ds(off, n), :], v_ref[pl.ds(off, n), :], col <= row + qo)

    lam_init = lam_ref[4:5, 0:1]
    lam = (jnp.exp(jnp.sum(lam_ref[0:1, :] * lam_ref[1:2, :], axis=1, keepdims=True))
           - jnp.exp(jnp.sum(lam_ref[2:3, :] * lam_ref[3:4, :], axis=1, keepdims=True))
           + lam_init)
    acc = acc_ref[...]
    inv_l = 1.0 / jnp.sum(l_ref[...], axis=1, keepdims=True)
    o = acc[0:tq] * inv_l[0:tq] - lam * (acc[tq:2 * tq] * inv_l[tq:2 * tq])
    o = _rms(o, g_ref[...], HEAD_NORM_EPS) * (1.0 - lam_init)
    o_ref[...] = o.astype(BF16)


def _attention(proj, lam_pack, head_g, layer):
    t = proj.shape[0]
    qb, kb, vb = COL_Q // LANES, COL_K // LANES, COL_V // LANES
    return pl.pallas_call(
        _attn_kernel,
        grid=(DA_HEADS, t // ATT_TQ),
        in_specs=[
            pl.BlockSpec((None, 8, DA_HEAD_DIM), lambda h, i: (layer, 0, 0)),
            _vec_spec(DA_V_DIM, layer, 2),
            pl.BlockSpec((ATT_TQ, LANES), lambda h, i: (i, qb + h)),
            pl.BlockSpec((t, LANES), lambda h, i: (0, kb + h)),
            pl.BlockSpec((t, LANES), lambda h, i: (0, vb + h)),
        ],
        out_specs=pl.BlockSpec((ATT_TQ, LANES), lambda h, i: (i, h)),
        out_shape=jax.ShapeDtypeStruct((t, DA_WIDTH), BF16),
        scratch_shapes=[
            pltpu.VMEM((2 * ATT_TQ, LANES), BF16),
            pltpu.VMEM((2 * ATT_TQ, LANES), F32),
            pltpu.VMEM((2 * ATT_TQ, LANES), F32),
            pltpu.VMEM((2 * ATT_TQ, LANES), F32),
        ],
        compiler_params=_params(("arbitrary", "arbitrary")),
        name="diff_attn",
    )(lam_pack, head_g, proj, proj, proj)


CV_TM = 512
CV_HALO = 32
CV_ROWS = 64
CV_SUB = 8
CV_SHIFT_ROWS = CV_TM + (CONV_TAPS - 1) // CV_SUB * CV_SUB


def _conv_kernel(a_ref, b_ref, w_ref, bias_ref, g_ref, beta_ref, o_ref, z_ref, zs_ref):
    i = pl.program_id(0)

    @pl.when(i == 0)
    def _():
        z_ref[0:CV_HALO, :] = jnp.zeros((CV_HALO, CV_WIDTH), F32)

    @pl.when(i > 0)
    def _():
        z_ref[0:CV_HALO, :] = z_ref[CV_TM:CV_TM + CV_HALO, :]

    a = a_ref[...].astype(F32)
    b = b_ref[...].astype(F32)
    z_ref[CV_HALO:CV_HALO + CV_TM, :] = a * jax.nn.sigmoid(b)

    base = CV_HALO - (CONV_TAPS - 1)
    for res in range(CV_SUB):
        n = CV_TM + (CONV_TAPS - 1 - res) // CV_SUB * CV_SUB
        zs_ref[res, 0:n, :] = z_ref[base + res:base + res + n, :]
    for r in range(0, CV_TM, CV_ROWS):
        acc = jnp.zeros((CV_ROWS, CV_WIDTH), F32) + bias_ref[...]
        for j in range(CONV_TAPS):
            blk, res = divmod(j, CV_SUB)
            lo = r + blk * CV_SUB
            acc = acc + w_ref[j:j + 1, :] * zs_ref[res, lo:lo + CV_ROWS, :]
        mu = jnp.mean(acc, axis=-1, keepdims=True)
        xc = acc - mu
        var = jnp.mean(xc * xc, axis=-1, keepdims=True)
        y = xc * lax.rsqrt(var + LN_EPS) * g_ref[...] + beta_ref[...]
        o_ref[r:r + CV_ROWS, :] = (y * jax.nn.sigmoid(y)).astype(BF16)


def _conv(proj, dw_w, dw_b, ln_g, ln_b, layer):
    t = proj.shape[0]
    ab, bb = COL_CVA // CV_WIDTH, COL_CVB // CV_WIDTH
    vec = _vec_spec(CV_WIDTH, layer, 1)
    return pl.pallas_call(
        _conv_kernel,
        grid=(t // CV_TM,),
        in_specs=[
            pl.BlockSpec((CV_TM, CV_WIDTH), lambda i: (i, ab)),
            pl.BlockSpec((CV_TM, CV_WIDTH), lambda i: (i, bb)),
            pl.BlockSpec((None, CONV_TAPS, CV_WIDTH), lambda i: (layer, 0, 0)),
            vec, vec, vec,
        ],
        out_specs=pl.BlockSpec((CV_TM, CV_WIDTH), lambda i: (i, 0)),
        out_shape=jax.ShapeDtypeStruct((t, CV_WIDTH), BF16),
        scratch_shapes=[pltpu.VMEM((CV_HALO + CV_TM, CV_WIDTH), F32),
                        pltpu.VMEM((CV_SUB, CV_SHIFT_ROWS, CV_WIDTH), F32)],
        compiler_params=_params(("arbitrary",)),
        name="conformer_conv",
    )(proj, proj, dw_w, dw_b, ln_g, ln_b)


SSM_CB = 128
SSM_RB = SSM_CB * SSM_CHUNK
SSM_SLAB = 2 * SSM_GROUP
SSM_SLABS = LANES // SSM_SLAB


def _pick_slabs(pieces, src_slab, lane_slab):
    out = None
    for k, piece in enumerate(pieces):
        shift = ((k - src_slab) * SSM_SLAB) % LANES
        moved = piece if shift == 0 else pltpu.roll(piece, shift, 1)
        out = moved if out is None else jnp.where(lane_slab == k, moved, out)
    return out


def _ssm_kernel(u_ref, t_ref, wre_ref, wim_ref, vre_ref, vim_ref, ar_ref, ai_ref, d_ref, y_ref,
                uf_ref, up_ref, sre_ref, sim_ref, yp_ref, yf_ref, xr_ref, xi_ref):
    npair, cb = SSM_PAIRS, SSM_CB

    @pl.when(pl.program_id(0) == 0)
    def _():
        xr_ref[...] = jnp.zeros(xr_ref.shape, F32)
        xi_ref[...] = jnp.zeros(xi_ref.shape, F32)

    ntile = SSM_WIDTH // LANES
    for j in range(ntile):
        uf_ref[j] = u_ref[:, j * LANES:(j + 1) * LANES].astype(F32)
    lane_slab = lax.shift_right_logical(lax.broadcasted_iota(jnp.int32, (cb, LANES), 1), 5)

    for q in range(SSM_CHUNK // SSM_SLABS):
        for tile in range(ntile):
            src = uf_ref.at[tile]
            pieces = [src[pl.ds(SSM_SLABS * q + k, cb, stride=SSM_CHUNK), :] for k in range(SSM_SLABS)]
            for slab in range(SSM_SLABS):
                up_ref[SSM_SLABS * tile + slab, :, q * LANES:(q + 1) * LANES] = _pick_slabs(
                    pieces, slab, lane_slab).astype(BF16)

    for p in range(npair):
        u = up_ref[p]
        sre_ref[pl.ds(p, cb, stride=npair), :] = jnp.dot(u, wre_ref[p], preferred_element_type=F32)
        sim_ref[pl.ds(p, cb, stride=npair), :] = jnp.dot(u, wim_ref[p], preferred_element_type=F32)

    ar = ar_ref[...]
    ai = ai_ref[...]

    def body(c, carry):
        xr, xi = carry
        off = pl.multiple_of(c * npair, npair)
        sr = sre_ref[pl.ds(off, npair), :]
        si = sim_ref[pl.ds(off, npair), :]
        sre_ref[pl.ds(off, npair), :] = xr
        sim_ref[pl.ds(off, npair), :] = xi
        return ar * xr - ai * xi + sr, ar * xi + ai * xr + si

    xr, xi = lax.fori_loop(0, cb, body, (xr_ref[...], xi_ref[...]), unroll=8)
    xr_ref[...] = xr
    xi_ref[...] = xi

    for p in range(npair):
        u = up_ref[p]
        sr = sre_ref[pl.ds(p, cb, stride=npair), :].astype(BF16)
        si = sim_ref[pl.ds(p, cb, stride=npair), :].astype(BF16)
        nt = (((1,), (1,)), ((), ()))
        y = jnp.dot(u, t_ref[p], preferred_element_type=F32)
        y = y + lax.dot_general(sr, vre_ref[p], nt, preferred_element_type=F32)
        yp_ref[p] = y + lax.dot_general(si, vim_ref[p], nt, preferred_element_type=F32)

    for tile in range(ntile):
        for q in range(SSM_CHUNK // SSM_SLABS):
            pieces = [yp_ref[SSM_SLABS * tile + k, :, q * LANES:(q + 1) * LANES] for k in range(SSM_SLABS)]
            dst = yf_ref.at[tile]
            for slab in range(SSM_SLABS):
                dst[pl.ds(SSM_SLABS * q + slab, cb, stride=SSM_CHUNK), :] = _pick_slabs(pieces, slab, lane_slab)

    for j in range(ntile):
        cols = slice(j * LANES, (j + 1) * LANES)
        y_ref[:, cols] = (yf_ref[j] + uf_ref[j] * d_ref[:, cols]).astype(BF16)


def _ssm(proj, mats):
    t = proj.shape[0]
    vm = pl.BlockSpec(memory_space=pltpu.VMEM)
    return pl.pallas_call(
        _ssm_kernel,
        grid=(t // SSM_RB,),
        in_specs=[pl.BlockSpec((SSM_RB, SSM_WIDTH), lambda i: (i, COL_SSM // SSM_WIDTH))] + [vm] * 8,
        out_specs=pl.BlockSpec((SSM_RB, SSM_WIDTH), lambda i: (i, 0)),
        out_shape=jax.ShapeDtypeStruct((t, SSM_WIDTH), BF16),
        scratch_shapes=[
            pltpu.VMEM((SSM_WIDTH // LANES, SSM_RB, LANES), F32),
            pltpu.VMEM((SSM_PAIRS, SSM_CB, SSM_PAIR_W), BF16),
            pltpu.VMEM((SSM_CB * SSM_PAIRS, LANES), F32),
            pltpu.VMEM((SSM_CB * SSM_PAIRS, LANES), F32),
            pltpu.VMEM((SSM_PAIRS, SSM_CB, SSM_PAIR_W), F32),
            pltpu.VMEM((SSM_WIDTH // LANES, SSM_RB, LANES), F32),
            pltpu.VMEM((SSM_PAIRS, LANES), F32),
            pltpu.VMEM((SSM_PAIRS, LANES), F32),
        ],
        compiler_params=_params(("arbitrary",)),
        name="s5_scan",
    )(proj, *mats)


def _ssm_gen_kernel(bre_ref, bim_ref, cre_ref, cim_ref, pr_ref, pi_ref,
                    t_ref, wre_ref, wim_ref, vre_ref, vim_ref, car_ref, cai_ref):
    bre, bim, cre, cim = bre_ref[...], bim_ref[...], cre_ref[...], cim_ref[...]
    for tau in range(SSM_CHUNK):
        rows = slice(tau * SSM_SLAB, (tau + 1) * SSM_SLAB)
        pr = pr_ref[tau:tau + 1, :]
        pi = pi_ref[tau:tau + 1, :]
        car_ref[rows, :] = cre * pr - cim * pi
        cai_ref[rows, :] = cre * pi + cim * pr
    nt = (((1,), (1,)), ((), ()))
    hp = lax.Precision.HIGHEST
    k = (lax.dot_general(bre, car_ref[...], nt, precision=hp, preferred_element_type=F32)
         - lax.dot_general(bim, cai_ref[...], nt, precision=hp, preferred_element_type=F32))
    lane = lax.broadcasted_iota(jnp.int32, k.shape, 1)
    for s in range(SSM_CHUNK):
        rows = slice(s * SSM_SLAB, (s + 1) * SSM_SLAB)
        moved = k if s == 0 else pltpu.roll(k, s * SSM_SLAB, 1)
        t_ref[rows, :] = jnp.where(lane >= s * SSM_SLAB, moved, 0.0).astype(BF16)
        pr = pr_ref[SSM_CHUNK - 1 - s:SSM_CHUNK - s, :]
        pi = pi_ref[SSM_CHUNK - 1 - s:SSM_CHUNK - s, :]
        wre_ref[rows, :] = (bre * pr - bim * pi).astype(BF16)
        wim_ref[rows, :] = (bre * pi + bim * pr).astype(BF16)
        qr = pr_ref[s + 1:s + 2, :]
        qi = pi_ref[s + 1:s + 2, :]
        vre_ref[rows, :] = (cre * qr - cim * qi).astype(BF16)
        vim_ref[rows, :] = (-(cre * qi + cim * qr)).astype(BF16)


def _ssm_matrices(lam_re, lam_im, log_dt, b_re, b_im, c_re, c_im, d_skip):
    nl = lam_re.shape[0]
    lr = lam_re.astype(F32)
    li = lam_im.astype(F32)
    dt = jnp.exp(log_dt.astype(F32))[..., None]
    mag = jnp.exp(lr * dt)
    ab_re = mag * jnp.cos(li * dt)
    ab_im = mag * jnp.sin(li * dt)
    den = lr * lr + li * li
    f_re = ((ab_re - 1.0) * lr + ab_im * li) / den
    f_im = (ab_im * lr - (ab_re - 1.0) * li) / den
    br = b_re.astype(F32)
    bi = b_im.astype(F32)
    bb_re = f_re[..., None] * br - f_im[..., None] * bi
    bb_im = f_re[..., None] * bi + f_im[..., None] * br
    tau = jnp.arange(SSM_CHUNK + 1, dtype=F32)[None, None, :, None]
    pmag = jnp.exp(tau * (lr * dt)[:, :, None, :])
    pw_re = pmag * jnp.cos(tau * (li * dt)[:, :, None, :])
    pw_im = pmag * jnp.sin(tau * (li * dt)[:, :, None, :])
    cr = c_re.astype(F32)
    ci = c_im.astype(F32)
    eye2 = jnp.eye(2, dtype=F32)

    def paired(a):
        return a.reshape((nl, SSM_PAIRS, 2) + a.shape[2:])

    bbd_re = jnp.einsum('lpenk,ef->lpekfn', paired(bb_re), eye2).reshape(nl, SSM_PAIRS, SSM_SLAB, LANES)
    bbd_im = jnp.einsum('lpenk,ef->lpekfn', paired(bb_im), eye2).reshape(nl, SSM_PAIRS, SSM_SLAB, LANES)
    cbd_re = jnp.einsum('lpehn,ef->lpehfn', paired(cr), eye2).reshape(nl, SSM_PAIRS, SSM_SLAB, LANES)
    cbd_im = jnp.einsum('lpehn,ef->lpehfn', paired(ci), eye2).reshape(nl, SSM_PAIRS, SSM_SLAB, LANES)
    pwp_re = paired(pw_re).transpose(0, 1, 3, 2, 4).reshape(nl, SSM_PAIRS, SSM_CHUNK + 1, LANES)
    pwp_im = paired(pw_im).transpose(0, 1, 3, 2, 4).reshape(nl, SSM_PAIRS, SSM_CHUNK + 1, LANES)

    def spec(r, c):
        return pl.BlockSpec((None, None, r, c), lambda l, p: (l, p, 0, 0))

    wide = jax.ShapeDtypeStruct((nl, SSM_PAIRS, SSM_PAIR_W, SSM_PAIR_W), BF16)
    tall = jax.ShapeDtypeStruct((nl, SSM_PAIRS, SSM_PAIR_W, LANES), BF16)
    tm, w_re, w_im, vt_re, vt_im = pl.pallas_call(
        _ssm_gen_kernel,
        grid=(nl, SSM_PAIRS),
        in_specs=[spec(SSM_SLAB, LANES)] * 4 + [spec(SSM_CHUNK + 1, LANES)] * 2,
        out_specs=[spec(SSM_PAIR_W, SSM_PAIR_W)] + [spec(SSM_PAIR_W, LANES)] * 4,
        out_shape=[wide, tall, tall, tall, tall],
        scratch_shapes=[pltpu.VMEM((SSM_PAIR_W, LANES), F32), pltpu.VMEM((SSM_PAIR_W, LANES), F32)],
        compiler_params=_params(("arbitrary", "arbitrary")),
        name="s5_matrices",
    )(bbd_re, bbd_im, cbd_re, cbd_im, pwp_re, pwp_im)
    a_re = pwp_re[:, :, SSM_CHUNK]
    a_im = pwp_im[:, :, SSM_CHUNK]
    d_t = d_skip.astype(F32).reshape(nl, 1, SSM_WIDTH)
    return tm, w_re, w_im, vt_re, vt_im, a_re, a_im, d_t


MG_TM = 512
MG_TN = 512


def _merge_mix_kernel(oa_ref, zb_ref, yc_ref, ga_ref, gb_ref, gc_ref, wda_ref, wcv_ref, wglu_ref,
                      bglu_ref, wso_ref, wmix_ref, x_ref, o_ref, sc_ref):
    n = pl.program_id(1)

    @pl.when(n == 0)
    def _():
        glu = jnp.dot(yc_ref[...], wglu_ref[...], preferred_element_type=F32) + bglu_ref[...]
        sc_ref[...] = (glu[:, :SSM_WIDTH] * jax.nn.sigmoid(glu[:, SSM_WIDTH:])).astype(BF16)
        o_ref[...] = x_ref[...]

    y_a = jnp.dot(oa_ref[...], wda_ref[...], preferred_element_type=F32)
    y_b = jnp.dot(zb_ref[...], wcv_ref[...], preferred_element_type=F32)
    y_c = jnp.dot(sc_ref[...], wso_ref[...], preferred_element_type=F32)
    merged = (jax.nn.sigmoid(ga_ref[...].astype(F32)) * y_a
              + jax.nn.sigmoid(gb_ref[...].astype(F32)) * y_b
              + jax.nn.sigmoid(gc_ref[...].astype(F32)) * y_c)
    o_ref[...] += jnp.dot(merged.astype(BF16), wmix_ref[...], preferred_element_type=F32)


def _merge_mix(o_a, z_b, y_c, proj, w_da, w_cv, w_glu, b_glu, w_so, w_mix, x, layer):
    t = o_a.shape[0]
    ga, gb, gc = COL_GA // MG_TN, COL_GB // MG_TN, COL_GC // MG_TN

    def wcol(k):
        return pl.BlockSpec((None, k, MG_TN), lambda i, n: (layer, 0, n))

    return pl.pallas_call(
        _merge_mix_kernel,
        grid=(t // MG_TM, D_MODEL // MG_TN),
        in_specs=[
            pl.BlockSpec((MG_TM, DA_WIDTH), lambda i, n: (i, 0)),
            pl.BlockSpec((MG_TM, CV_WIDTH), lambda i, n: (i, 0)),
            pl.BlockSpec((MG_TM, SSM_WIDTH), lambda i, n: (i, 0)),
            pl.BlockSpec((MG_TM, MG_TN), lambda i, n: (i, ga + n)),
            pl.BlockSpec((MG_TM, MG_TN), lambda i, n: (i, gb + n)),
            pl.BlockSpec((MG_TM, MG_TN), lambda i, n: (i, gc + n)),
            wcol(DA_WIDTH), wcol(CV_WIDTH),
            pl.BlockSpec((None, SSM_WIDTH, 2 * SSM_WIDTH), lambda i, n: (layer, 0, 0)),
            _vec_spec(2 * SSM_WIDTH, layer, 2),
            wcol(SSM_WIDTH),
            pl.BlockSpec((None, MG_TN, D_MODEL), lambda i, n: (layer, n, 0)),
            pl.BlockSpec((MG_TM, D_MODEL), lambda i, n: (i, 0)),
        ],
        out_specs=pl.BlockSpec((MG_TM, D_MODEL), lambda i, n: (i, 0)),
        out_shape=jax.ShapeDtypeStruct((t, D_MODEL), F32),
        scratch_shapes=[pltpu.VMEM((MG_TM, SSM_WIDTH), BF16)],
        compiler_params=_params(("arbitrary", "arbitrary")),
        name="merge_mix",
    )(o_a, z_b, y_c, proj, proj, proj, w_da, w_cv, w_glu, b_glu, w_so, w_mix, x)


def _norm_matmul_kernel(x_ref, g_ref, w_ref, o_ref):
    h = _rms(x_ref[...], g_ref[...], RMS_EPS).astype(BF16)
    o_ref[...] = jnp.dot(h, w_ref[...], preferred_element_type=F32).astype(o_ref.dtype)


def _mem_kv(mem, g_all, w_all, layer):
    m = mem.shape[0]
    n = 2 * XA_WIDTH
    return pl.pallas_call(
        _norm_matmul_kernel,
        grid=(1,),
        in_specs=[
            pl.BlockSpec((m, D_MODEL), lambda i: (0, 0)),
            _vec_spec(D_MODEL, layer, 1),
            pl.BlockSpec((None, D_MODEL, n), lambda i: (layer, 0, 0)),
        ],
        out_specs=pl.BlockSpec((m, n), lambda i: (0, 0)),
        out_shape=jax.ShapeDtypeStruct((m, n), BF16),
        compiler_params=_params(("arbitrary",)),
        name="mem_kv",
    )(mem, g_all, w_all)


XA_TM = 512


def _xattn_kernel(x_ref, g_ref, wq_ref, kv_ref, wo_ref, o_ref):
    x = x_ref[...]
    h = _rms(x, g_ref[...], RMS_EPS).astype(BF16)
    q = jnp.dot(h, wq_ref[...], preferred_element_type=F32).astype(BF16)
    heads = []
    for hd in range(XA_HEADS):
        lo = hd * XA_HEAD_DIM
        k = kv_ref[:, lo:lo + XA_HEAD_DIM]
        v = kv_ref[:, XA_WIDTH + lo:XA_WIDTH + lo + XA_HEAD_DIM]
        s = lax.dot_general(q[:, lo:lo + XA_HEAD_DIM], k, (((1,), (1,)), ((), ())),
                            preferred_element_type=F32) * (XA_HEAD_DIM ** -0.5)
        m = jnp.max(s, axis=-1, keepdims=True)
        e = jnp.exp(s - m)
        p = e / jnp.sum(e, axis=-1, keepdims=True)
        heads.append(jnp.dot(p.astype(BF16), v, preferred_element_type=F32).astype(BF16))
    o = jnp.concatenate(heads, axis=-1)
    o_ref[...] = x + jnp.dot(o, wo_ref[...], preferred_element_type=F32)


def _xattn(x, g_all, wq_all, kv, wo_all, layer):
    t = x.shape[0]
    return pl.pallas_call(
        _xattn_kernel,
        grid=(t // XA_TM,),
        in_specs=[
            pl.BlockSpec((XA_TM, D_MODEL), lambda i: (i, 0)),
            _vec_spec(D_MODEL, layer, 1),
            pl.BlockSpec((None, D_MODEL, XA_WIDTH), lambda i: (layer, 0, 0)),
            pl.BlockSpec((MEM_LEN, 2 * XA_WIDTH), lambda i: (0, 0)),
            pl.BlockSpec((None, XA_WIDTH, D_MODEL), lambda i: (layer, 0, 0)),
        ],
        out_specs=pl.BlockSpec((XA_TM, D_MODEL), lambda i: (i, 0)),
        out_shape=jax.ShapeDtypeStruct((t, D_MODEL), F32),
        compiler_params=_params(("arbitrary",)),
        name="mem_xattn",
    )(x, g_all, wq_all, kv, wo_all)


RT_TM = 512
_E_LANE0 = MOE_GROUPS


def _router_kernel(x_ref, g_ref, w_ref, b_ref, meta_ref, cnt_ref, run_ref):
    i = pl.program_id(0)

    @pl.when(i == 0)
    def _():
        run_ref[...] = jnp.zeros(run_ref.shape, F32)

    h = _rms(x_ref[...], g_ref[...], RMS_EPS).astype(BF16)
    logits = jnp.dot(h, w_ref[...], preferred_element_type=F32) + b_ref[...]
    lane = lax.broadcasted_iota(jnp.int32, logits.shape, 1).astype(F32)
    neg = jnp.float32(-jnp.inf)
    big = jnp.float32(LANES)

    def first_argmax(vals):
        top = jnp.max(vals, axis=-1, keepdims=True)
        idx = jnp.min(jnp.where(vals == top, lane, big), axis=-1, keepdims=True)
        return top, idx

    gl = jnp.where(lane < MOE_GROUPS, logits, neg)
    g_top, g_idx = first_argmax(gl)
    g_w = 1.0 / jnp.sum(jnp.exp(gl - g_top), axis=-1, keepdims=True)
    e_lane = lane - _E_LANE0
    in_group = jnp.logical_and(e_lane >= g_idx * MOE_PER_GROUP, e_lane < (g_idx + 1) * MOE_PER_GROUP)
    el = jnp.where(in_group, logits, neg)
    v1, i1 = first_argmax(el)
    el2 = jnp.where(lane == i1, neg, el)
    v2, i2 = first_argmax(el2)
    e2 = jnp.exp(v2 - v1)
    w1 = 1.0 / (1.0 + e2)
    w2 = e2 / (1.0 + e2)
    oh1 = (lane == i1).astype(F32)
    oh2 = (lane == i2).astype(F32)
    both = oh1 + oh2
    row = lax.broadcasted_iota(jnp.int32, (RT_TM, RT_TM), 0)
    col = lax.broadcasted_iota(jnp.int32, (RT_TM, RT_TM), 1)
    earlier = jnp.where(col < row, 1.0, 0.0).astype(BF16)
    before = jnp.dot(earlier, both.astype(BF16), preferred_element_type=F32) + run_ref[...]
    rank1 = jnp.sum(oh1 * before, axis=-1, keepdims=True)
    rank2 = jnp.sum(oh2 * before, axis=-1, keepdims=True)
    run_ref[...] += jnp.sum(both, axis=0, keepdims=True)
    cnt_ref[...] = run_ref[...]
    meta = jnp.where(lane == 0.0, i1 - _E_LANE0, 0.0)
    for k, val in enumerate((i2 - _E_LANE0, rank1, rank2, w1 * g_w, w2 * g_w), start=1):
        meta = jnp.where(lane == float(k), val, meta)
    meta_ref[...] = meta


def _router(x, g_all, w_r, b_r, layer):
    t = x.shape[0]
    return pl.pallas_call(
        _router_kernel,
        grid=(t // RT_TM,),
        in_specs=[
            pl.BlockSpec((RT_TM, D_MODEL), lambda i: (i, 0)),
            _vec_spec(D_MODEL, layer, 1),
            pl.BlockSpec((None, D_MODEL, LANES), lambda i: (layer, 0, 0)),
            _vec_spec(LANES, layer, 1),
        ],
        out_specs=[
            pl.BlockSpec((RT_TM, LANES), lambda i: (i, 0)),
            pl.BlockSpec((1, LANES), lambda i: (0, 0)),
        ],
        out_shape=[jax.ShapeDtypeStruct((t, LANES), F32), jax.ShapeDtypeStruct((1, LANES), F32)],
        scratch_shapes=[pltpu.VMEM((1, LANES), F32)],
        compiler_params=_params(("arbitrary",)),
        name="moe_router",
    )(x, g_all, w_r, b_r)


EX_TM = 256
EX_ROWS = 2 * SEQ
EX_TILES = EX_ROWS // EX_TM
EX_STEPS = EX_TILES + MOE_EXPERTS - 1


def _row_copy(src, src_row, dst, dst_row, sem):
    return pltpu.make_async_copy(src.at[pl.ds(src_row, 1), :], dst.at[pl.ds(dst_row, 1), :], sem)


DP_TM = 256


def _dispatch_kernel(p1_ref, p2_ref, x_ref, xs_hbm, sem):
    base = pl.program_id(0) * DP_TM

    def issue(j, c):
        _row_copy(x_ref, j, xs_hbm, p1_ref[base + j], sem).start()
        _row_copy(x_ref, j, xs_hbm, p2_ref[base + j], sem).start()
        return c

    lax.fori_loop(0, DP_TM, issue, 0, unroll=True)

    def wait(j, c):
        _row_copy(x_ref, j, xs_hbm, 0, sem).wait()
        _row_copy(x_ref, j, xs_hbm, 0, sem).wait()
        return c

    lax.fori_loop(0, DP_TM, wait, 0, unroll=8)


def _dispatch(pos1, pos2, x):
    t = x.shape[0]
    grid_spec = pltpu.PrefetchScalarGridSpec(
        num_scalar_prefetch=2,
        grid=(t // DP_TM,),
        in_specs=[pl.BlockSpec((DP_TM, D_MODEL), lambda i, p1, p2: (i, 0))],
        out_specs=pl.BlockSpec(memory_space=pl.ANY),
        scratch_shapes=[pltpu.SemaphoreType.DMA(())],
    )
    return pl.pallas_call(
        _dispatch_kernel,
        grid_spec=grid_spec,
        out_shape=jax.ShapeDtypeStruct((EX_ROWS, D_MODEL), F32),
        compiler_params=_params(("arbitrary",)),
        name="moe_dispatch",
    )(pos1, pos2, x)


def _experts_kernel(se_ref, st_ref, lo_ref, hi_ref, first_ref, wnew_ref, x_ref, g_ref, wg_ref, wu_ref,
                    wd_ref, o_ref, wgb_ref, wub_ref, wdb_ref):
    s = pl.program_id(0)
    lo = lo_ref[s]
    hi = hi_ref[s]

    @pl.when(wnew_ref[s] == 1)
    def _():
        wgb_ref[...] = wg_ref[...].astype(BF16)
        wub_ref[...] = wu_ref[...].astype(BF16)
        wdb_ref[...] = wd_ref[...].astype(BF16)

    @pl.when(hi > lo)
    def _():
        h = _rms(x_ref[...], g_ref[...], RMS_EPS).astype(BF16)
        gate = jnp.dot(h, wgb_ref[...], preferred_element_type=F32)
        up = jnp.dot(h, wub_ref[...], preferred_element_type=F32)
        act = (gate * jax.nn.sigmoid(gate) * up).astype(BF16)
        res = jnp.dot(act, wdb_ref[...], preferred_element_type=F32)
        row = lax.broadcasted_iota(jnp.int32, (EX_TM, 1), 0)
        mine = jnp.logical_and(row >= lo, row < hi)

        @pl.when(first_ref[s] == 1)
        def _():
            o_ref[...] = jnp.where(mine, res, 0.0)

        @pl.when(first_ref[s] == 0)
        def _():
            o_ref[...] = jnp.where(mine, res, o_ref[...])


def _experts(plan, xs_sorted, g_all, wg_all, wu_all, wd_all, layer):
    def wspec(k, n):
        return pl.BlockSpec((None, None, k, n), lambda s, se, st, lo, hi, fi, wn: (layer, se[s], 0, 0))

    grid_spec = pltpu.PrefetchScalarGridSpec(
        num_scalar_prefetch=6,
        grid=(EX_STEPS,),
        in_specs=[
            pl.BlockSpec((EX_TM, D_MODEL), lambda s, se, st, lo, hi, fi, wn: (st[s], 0)),
            pl.BlockSpec((None, 1, D_MODEL), lambda s, se, st, lo, hi, fi, wn: (layer, 0, 0)),
            wspec(D_MODEL, MOE_FF), wspec(D_MODEL, MOE_FF), wspec(MOE_FF, D_MODEL),
        ],
        out_specs=pl.BlockSpec((EX_TM, D_MODEL), lambda s, se, st, lo, hi, fi, wn: (st[s], 0)),
        scratch_shapes=[pltpu.VMEM((D_MODEL, MOE_FF), BF16), pltpu.VMEM((D_MODEL, MOE_FF), BF16),
                        pltpu.VMEM((MOE_FF, D_MODEL), BF16)],
    )
    return pl.pallas_call(
        _experts_kernel,
        grid_spec=grid_spec,
        out_shape=jax.ShapeDtypeStruct((EX_ROWS, D_MODEL), F32),
        compiler_params=_params(("arbitrary",)),
        name="moe_experts",
    )(*plan, xs_sorted, g_all, wg_all, wu_all, wd_all)


CB_TM = 256


def _combine_kernel(p1_ref, p2_ref, y_hbm, x_ref, meta_ref, g_ref, o_ref, n_ref, y1buf, y2buf, sem):
    i = pl.program_id(0)
    slot = lax.rem(i, 2)

    def fetch(tile, slot_):
        base = tile * CB_TM

        def issue(j, c):
            _row_copy(y_hbm, p1_ref[base + j], y1buf.at[slot_], j, sem.at[slot_]).start()
            _row_copy(y_hbm, p2_ref[base + j], y2buf.at[slot_], j, sem.at[slot_]).start()
            return c

        lax.fori_loop(0, CB_TM, issue, 0, unroll=True)

    @pl.when(i == 0)
    def _():
        fetch(0, 0)

    @pl.when(i + 1 < pl.num_programs(0))
    def _():
        fetch(i + 1, 1 - slot)

    def wait(j, c):
        _row_copy(y_hbm, 0, y1buf.at[slot], j, sem.at[slot]).wait()
        _row_copy(y_hbm, 0, y2buf.at[slot], j, sem.at[slot]).wait()
        return c

    lax.fori_loop(0, CB_TM, wait, 0, unroll=8)
    meta = meta_ref[...]
    x_new = x_ref[...] + meta[:, 4:5] * y1buf[slot] + meta[:, 5:6] * y2buf[slot]
    o_ref[...] = x_new
    n_ref[...] = _rms(x_new, g_ref[...], RMS_EPS).astype(n_ref.dtype)


def _combine(pos1, pos2, ys, x, meta, g_next, next_dtype):
    t = x.shape[0]
    tile = pl.BlockSpec((CB_TM, D_MODEL), lambda i, p1, p2: (i, 0))
    grid_spec = pltpu.PrefetchScalarGridSpec(
        num_scalar_prefetch=2,
        grid=(t // CB_TM,),
        in_specs=[
            pl.BlockSpec(memory_space=pl.ANY),
            tile,
            pl.BlockSpec((CB_TM, LANES), lambda i, p1, p2: (i, 0)),
            pl.BlockSpec((1, D_MODEL), lambda i, p1, p2: (0, 0)),
        ],
        out_specs=[tile, tile],
        scratch_shapes=[pltpu.VMEM((2, CB_TM, D_MODEL), F32), pltpu.VMEM((2, CB_TM, D_MODEL), F32),
                        pltpu.SemaphoreType.DMA((2,))],
    )
    return pl.pallas_call(
        _combine_kernel,
        grid_spec=grid_spec,
        out_shape=[jax.ShapeDtypeStruct((t, D_MODEL), F32), jax.ShapeDtypeStruct((t, D_MODEL), next_dtype)],
        compiler_params=_params(("arbitrary",)),
        name="moe_combine",
    )(pos1, pos2, ys, x, meta, g_next)


def _dispatch_plan(meta, cnt):
    t = meta.shape[0]
    e1 = meta[:, 0].astype(jnp.int32)
    e2 = meta[:, 1].astype(jnp.int32)
    r1 = meta[:, 2].astype(jnp.int32)
    r2 = meta[:, 3].astype(jnp.int32)
    counts = cnt[0, _E_LANE0:_E_LANE0 + MOE_EXPERTS].astype(jnp.int32)
    ends = jnp.cumsum(counts)
    starts = ends - counts
    pos1 = starts[e1] + r1
    pos2 = starts[e2] + r2
    t_lo = starts // EX_TM
    t_hi = (ends + EX_TM - 1) // EX_TM
    nsteps = jnp.where(counts > 0, t_hi - t_lo, 0)
    step_end = jnp.cumsum(nsteps)
    step_start = step_end - nsteps
    s = jnp.arange(EX_STEPS, dtype=jnp.int32)
    se = jnp.minimum(jnp.sum((s[:, None] >= step_end[None, :]).astype(jnp.int32), axis=1), MOE_EXPERTS - 1)
    valid = s < step_end[-1]
    st = jnp.where(valid, t_lo[se] + s - step_start[se], EX_TILES - 1)
    lo = jnp.where(valid, jnp.clip(starts[se] - st * EX_TM, 0, EX_TM), 0)
    hi = jnp.where(valid, jnp.clip(ends[se] - st * EX_TM, 0, EX_TM), 0)
    first = jnp.concatenate([jnp.ones((1,), jnp.int32), (st[1:] != st[:-1]).astype(jnp.int32)])
    wnew = jnp.concatenate([jnp.ones((1,), jnp.int32), (se[1:] != se[:-1]).astype(jnp.int32)])
    plan = tuple(a.astype(jnp.int32) for a in (se, st, lo, hi, first, wnew))
    return pos1.astype(jnp.int32), pos2.astype(jnp.int32), plan


FN_TM = 512


def _norm_kernel(x_ref, g_ref, o_ref):
    o_ref[...] = _rms(x_ref[...], g_ref[...], RMS_EPS).astype(o_ref.dtype)


def _first_norm(x, g_all):
    t = x.shape[0]
    return pl.pallas_call(
        _norm_kernel,
        grid=(t // FN_TM,),
        in_specs=[pl.BlockSpec((FN_TM, D_MODEL), lambda i: (i, 0)), _vec_spec(D_MODEL, 0, 1)],
        out_specs=pl.BlockSpec((FN_TM, D_MODEL), lambda i: (i, 0)),
        out_shape=jax.ShapeDtypeStruct((t, D_MODEL), BF16),
        compiler_params=_params(("arbitrary",)),
        name="first_norm",
    )(x, g_all)


def kernel(x, mem, positions, norm_mix, w_in, da_lam_q1, da_lam_k1, da_lam_q2, da_lam_k2, da_head_norm, w_da_out, cv_dw_w, cv_dw_b, cv_ln_g, cv_ln_b, w_cv_out, ssm_lam_re, ssm_lam_im, ssm_log_dt, ssm_b_re, ssm_b_im, ssm_c_re, ssm_c_im, ssm_d, w_ssm_glu, b_ssm_glu, w_ssm_out, w_mix_out, norm_xa, norm_mem, w_xa_q, w_xa_kv, w_xa_out, norm_ffn, w_router_group, b_router_group, w_router_expert, b_router_expert, w_exp_gate, w_exp_up, w_exp_down, norm_final):
    bsz, seq, _ = x.shape
    assert bsz == 1 and seq == SEQ
    nl = w_in.shape[0]
    xs = x.reshape(seq, D_MODEL).astype(F32)
    mem2 = mem.reshape(MEM_LEN, D_MODEL).astype(F32)

    inv_freq = ROPE_THETA ** (-jnp.arange(0, DA_HEAD_DIM, 2, dtype=F32) / DA_HEAD_DIM)
    ang = positions.reshape(seq).astype(F32)[:, None] * inv_freq
    cos = jnp.cos(ang)
    sin = jnp.sin(ang)
    cos_t = jnp.concatenate([cos, cos, cos, cos], axis=-1)
    sin_t = jnp.concatenate([-sin, sin, -sin, sin], axis=-1)

    w_da_b = w_da_out.astype(BF16)
    w_cv_b = w_cv_out.astype(BF16)
    w_glu_b = w_ssm_glu.astype(BF16)
    w_so_b = w_ssm_out.astype(BF16)
    w_mix_b = w_mix_out.astype(BF16)
    w_xq_b = w_xa_q.astype(BF16)
    w_xkv_b = w_xa_kv.astype(BF16)
    w_xo_b = w_xa_out.astype(BF16)
    pad = LANES - MOE_GROUPS - MOE_EXPERTS
    w_r = jnp.concatenate([w_router_group, w_router_expert,
                           jnp.zeros((nl, D_MODEL, pad), F32)], axis=-1).astype(BF16)
    b_r = jnp.concatenate([b_router_group, b_router_expert, jnp.zeros((nl, pad), F32)], axis=-1).astype(F32)

    def vec3(a):
        return a.astype(F32).reshape(nl, 1, a.shape[-1])

    norm_mix, da_head_norm, cv_dw_b, cv_ln_g, cv_ln_b, b_ssm_glu, norm_xa, norm_mem, norm_ffn, b_r = map(
        vec3, (norm_mix, da_head_norm, cv_dw_b, cv_ln_g, cv_ln_b, b_ssm_glu, norm_xa, norm_mem, norm_ffn, b_r))
    cv_dw_w = cv_dw_w.astype(F32)

    lam_inits = jnp.asarray([0.8 - 0.6 * math.exp(-0.3 * l) for l in range(nl)], F32)
    lam_pack = jnp.stack([da_lam_q1, da_lam_k1, da_lam_q2, da_lam_k2], axis=1).astype(F32)
    lam_pack = jnp.concatenate(
        [lam_pack, jnp.broadcast_to(lam_inits[:, None, None], (nl, 4, DA_HEAD_DIM))], axis=1)

    ssm_mats = _ssm_matrices(ssm_lam_re, ssm_lam_im, ssm_log_dt, ssm_b_re, ssm_b_im,
                             ssm_c_re, ssm_c_im, ssm_d)

    h = _first_norm(xs, norm_mix)
    for l in range(nl):
        proj = _inproj(h, w_in, l, cos_t, sin_t)
        o_a = _attention(proj, lam_pack, da_head_norm, l)
        z_b = _conv(proj, cv_dw_w, cv_dw_b, cv_ln_g, cv_ln_b, l)
        y_c = _ssm(proj, [m[l] for m in ssm_mats])
        xs = _merge_mix(o_a, z_b, y_c, proj, w_da_b, w_cv_b, w_glu_b, b_ssm_glu, w_so_b, w_mix_b, xs, l)
        kv = _mem_kv(mem2, norm_mem, w_xkv_b, l)
        xs = _xattn(xs, norm_xa, w_xq_b, kv, w_xo_b, l)
        meta, cnt = _router(xs, norm_ffn, w_r, b_r, l)
        pos1, pos2, plan = _dispatch_plan(meta, cnt)
        xs_sorted = _dispatch(pos1, pos2, xs)
        ys = _experts(plan, xs_sorted, norm_ffn, w_exp_gate, w_exp_up, w_exp_down, l)
        if l + 1 < nl:
            xs, h = _combine(pos1, pos2, ys, xs, meta, norm_mix[l + 1], BF16)
        else:
            xs, out = _combine(pos1, pos2, ys, xs, meta, norm_final.astype(F32).reshape(1, D_MODEL), F32)
    return out.reshape(bsz, seq, D_MODEL)
```

```python
import functools
import math

import jax
import jax.numpy as jnp
from jax import lax
from jax.experimental import pallas as pl
from jax.experimental.pallas import tpu as pltpu

F32 = jnp.float32
BF16 = jnp.bfloat16

D_MODEL = 2048
SEQ = 8192
DEPTH = 4
MEM_LEN = 256
DA_HEADS = 8
DA_HEAD_DIM = 64
DA_V_DIM = 128
DA_WIDTH = 1024
ROPE_THETA = 10000.0
CV_WIDTH = 512
CONV_TAPS = 31
SSM_WIDTH = 512
SSM_GROUP = 16
SSM_GROUPS = 32
SSM_STATE = 64
XA_HEADS = 4
XA_HEAD_DIM = 128
XA_WIDTH = 512
MOE_GROUPS = 4
MOE_PER_GROUP = 4
MOE_EXPERTS = 16
MOE_FF = 512
RMS_EPS = 1e-6
HEAD_NORM_EPS = 1e-5
LN_EPS = 1e-5

COL_Q = 0
COL_K = COL_Q + 1024
COL_V = COL_K + 1024
COL_CVA = COL_V + 1024
COL_CVB = COL_CVA + CV_WIDTH
COL_SSM = COL_CVB + CV_WIDTH
COL_GA = COL_SSM + SSM_WIDTH
COL_GB = COL_GA + D_MODEL
COL_GC = COL_GB + D_MODEL
IN_TOTAL = COL_GC + D_MODEL

LANES = 128
VMEM_LIMIT = 56 * 1024 * 1024

SSM_CHUNK = 16
SSM_NCHUNK = SEQ // SSM_CHUNK
SSM_PAIRS = SSM_GROUPS // 2
SSM_PAIR_W = 2 * SSM_CHUNK * SSM_GROUP


def _params(sem, vmem=VMEM_LIMIT):
    return pltpu.CompilerParams(dimension_semantics=sem, vmem_limit_bytes=vmem)


def _vec_spec(width, layer, ngrid):
    if ngrid == 1:
        return pl.BlockSpec((None, 1, width), lambda i: (layer, 0, 0))
    return pl.BlockSpec((None, 1, width), lambda i, j: (layer, 0, 0))


def _rms(xf, g, eps):
    ms = jnp.mean(xf * xf, axis=-1, keepdims=True)
    return xf * lax.rsqrt(ms + eps) * g


INP_TM = 2048
INP_TN = 512
_Q_TILE0 = COL_Q // INP_TN
_K_TILE0 = COL_K // INP_TN
_V_TILE0 = COL_V // INP_TN


def _inproj_kernel(h_ref, w_ref, cos_ref, sin_ref, o_ref):
    j = pl.program_id(1)
    acc = jnp.dot(h_ref[...], w_ref[...].astype(BF16), preferred_element_type=F32)
    is_rope = jnp.logical_and(j >= _Q_TILE0, j < _V_TILE0)

    @pl.when(is_rope)
    def _():
        scale = jnp.where(j < _K_TILE0, math.log2(math.e) * DA_HEAD_DIM ** -0.5, 1.0).astype(F32)
        cos = cos_ref[...] * scale
        sin = sin_ref[...] * scale
        lane = lax.broadcasted_iota(jnp.int32, (INP_TM, LANES), 1)
        first_half = (lane % DA_HEAD_DIM) < (DA_HEAD_DIM // 2)
        for c in range(INP_TN // LANES):
            t = acc[:, c * LANES:(c + 1) * LANES]
            swapped = jnp.where(first_half, pltpu.roll(t, LANES - 32, 1), pltpu.roll(t, 32, 1))
            o_ref[:, c * LANES:(c + 1) * LANES] = (t * cos + swapped * sin).astype(BF16)

    @pl.when(jnp.logical_not(is_rope))
    def _():
        o_ref[...] = acc.astype(BF16)


def _inproj(h, w_all, layer, cos_t, sin_t):
    t = h.shape[0]
    return pl.pallas_call(
        _inproj_kernel,
        grid=(t // INP_TM, IN_TOTAL // INP_TN),
        in_specs=[
            pl.BlockSpec((INP_TM, D_MODEL), lambda i, j: (i, 0)),
            pl.BlockSpec((None, D_MODEL, INP_TN), lambda i, j: (layer, 0, j)),
            pl.BlockSpec((INP_TM, LANES), lambda i, j: (i, 0)),
            pl.BlockSpec((INP_TM, LANES), lambda i, j: (i, 0)),
        ],
        out_specs=pl.BlockSpec((INP_TM, INP_TN), lambda i, j: (i, j)),
        out_shape=jax.ShapeDtypeStruct((t, IN_TOTAL), BF16),
        compiler_params=_params(("arbitrary", "arbitrary")),
        name="inproj",
    )(h, w_all, cos_t, sin_t)


ATT_TQ = 1024
ATT_TK = 1024


ATT_RG = 1024
ATT_DRG = 512


def _attn_kernel(lam_ref, g_ref, q_ref, k_ref, v_ref, o_ref, qs_ref, m_ref, l_ref, acc_ref):
    i = pl.program_id(1)
    tq = ATT_TQ
    q = q_ref[...]
    lane = lax.broadcasted_iota(jnp.int32, q.shape, 1)
    zero = jnp.zeros_like(q)
    qs_ref[0:tq, :] = jnp.where(lane < DA_HEAD_DIM, q, zero)
    qs_ref[tq:2 * tq, :] = jnp.where(lane >= DA_HEAD_DIM, q, zero)
    m_ref[...] = jnp.full(m_ref.shape, -jnp.inf, F32)
    l_ref[...] = jnp.zeros(l_ref.shape, F32)
    acc_ref[...] = jnp.zeros(acc_ref.shape, F32)

    def update_rows(r0, nrows, k, v, mask):
        n = k.shape[0]
        rows = slice(r0, r0 + nrows)
        s = lax.dot_general(qs_ref[rows, :], k, (((1,), (1,)), ((), ())), preferred_element_type=F32)
        if mask is not None:
            s = jnp.where(mask, s, -jnp.inf)
        tiles = [s[:, t * LANES:(t + 1) * LANES] for t in range(n // LANES)]
        mc = functools.reduce(jnp.maximum, tiles)
        m_old = m_ref[rows, :]
        m_new = jnp.maximum(m_old, jnp.max(mc, axis=1, keepdims=True))
        alpha = jnp.exp2(m_old - m_new)
        p_tiles = [jnp.exp2(t - m_new) for t in tiles]
        l_ref[rows, :] = alpha * l_ref[rows, :] + functools.reduce(jnp.add, p_tiles)
        p = jnp.concatenate(p_tiles, axis=1).astype(BF16)
        acc_ref[rows, :] = alpha * acc_ref[rows, :] + jnp.dot(p, v, preferred_element_type=F32)
        m_ref[rows, :] = m_new

    def body(c, carry):
        off = pl.multiple_of(c * ATT_TK, ATT_TK)
        k = k_ref[pl.ds(off, ATT_TK), :]
        v = v_ref[pl.ds(off, ATT_TK), :]
        for r0 in range(0, 2 * tq, ATT_RG):
            update_rows(r0, ATT_RG, k, v, None)
        return carry

    lax.fori_loop(0, i * (tq // ATT_TK), body, 0)

    off = pl.multiple_of(i * tq, tq)
    for r0 in range(0, 2 * tq, ATT_DRG):
        qo = r0 % tq
        n = qo + ATT_DRG
        row = lax.broadcasted_iota(jnp.int32, (ATT_DRG, n), 0)
        col = lax.broadcasted_iota(jnp.int32, (ATT_DRG, n), 1)
        update_rows(r0, ATT_DRG, k_ref[pl.ds(off, n), :], v_ref[pl.ds(off, n), :], col <= row + qo)

    lam_init = lam_ref[4:5, 0:1]
    lam = (jnp.exp(jnp.sum(lam_ref[0:1, :] * lam_ref[1:2, :], axis=1, keepdims=True))
           - jnp.exp(jnp.sum(lam_ref[2:3, :] * lam_ref[3:4, :], axis=1, keepdims=True))
           + lam_init)
    acc = acc_ref[...]
    inv_l = 1.0 / jnp.sum(l_ref[...], axis=1, keepdims=True)
    o = acc[0:tq] * inv_l[0:tq] - lam * (acc[tq:2 * tq] * inv_l[tq:2 * tq])
    o = _rms(o, g_ref[...], HEAD_NORM_EPS) * (1.0 - lam_init)
    o_ref[...] = o.astype(BF16)


def _attention(proj, lam_pack, head_g, layer):
    t = proj.shape[0]
    qb, kb, vb = COL_Q // LANES, COL_K // LANES, COL_V // LANES
    return pl.pallas_call(
        _attn_kernel,
        grid=(DA_HEADS, t // ATT_TQ),
        in_specs=[
            pl.BlockSpec((None, 8, DA_HEAD_DIM), lambda h, i: (layer, 0, 0)),
            _vec_spec(DA_V_DIM, layer, 2),
            pl.BlockSpec((ATT_TQ, LANES), lambda h, i: (i, qb + h)),
            pl.BlockSpec((t, LANES), lambda h, i: (0, kb + h)),
            pl.BlockSpec((t, LANES), lambda h, i: (0, vb + h)),
        ],
        out_specs=pl.BlockSpec((ATT_TQ, LANES), lambda h, i: (i, h)),
        out_shape=jax.ShapeDtypeStruct((t, DA_WIDTH), BF16),
        scratch_shapes=[
            pltpu.VMEM((2 * ATT_TQ, LANES), BF16),
            pltpu.VMEM((2 * ATT_TQ, LANES), F32),
            pltpu.VMEM((2 * ATT_TQ, LANES), F32),
            pltpu.VMEM((2 * ATT_TQ, LANES), F32),
        ],
        compiler_params=_params(("arbitrary", "arbitrary")),
        name="diff_attn",
    )(lam_pack, head_g, proj, proj, proj)


CV_TM = 512
CV_HALO = 32
CV_ROWS = 64
CV_SUB = 8
CV_SHIFT_ROWS = CV_TM + (CONV_TAPS - 1) // CV_SUB * CV_SUB


def _conv_kernel(a_ref, b_ref, w_ref, bias_ref, g_ref, beta_ref, o_ref, z_ref, zs_ref):
    i = pl.program_id(0)

    @pl.when(i == 0)
    def _():
        z_ref[0:CV_HALO, :] = jnp.zeros((CV_HALO, CV_WIDTH), F32)

    @pl.when(i > 0)
    def _():
        z_ref[0:CV_HALO, :] = z_ref[CV_TM:CV_TM + CV_HALO, :]

    a = a_ref[...].astype(F32)
    b = b_ref[...].astype(F32)
    z_ref[CV_HALO:CV_HALO + CV_TM, :] = a * jax.nn.sigmoid(b)

    base = CV_HALO - (CONV_TAPS - 1)
    for res in range(CV_SUB):
        n = CV_TM + (CONV_TAPS - 1 - res) // CV_SUB * CV_SUB
        zs_ref[res, 0:n, :] = z_ref[base + res:base + res + n, :]
    for r in range(0, CV_TM, CV_ROWS):
        acc = jnp.zeros((CV_ROWS, CV_WIDTH), F32) + bias_ref[...]
        for j in range(CONV_TAPS):
            blk, res = divmod(j, CV_SUB)
            lo = r + blk * CV_SUB
            acc = acc + w_ref[j:j + 1, :] * zs_ref[res, lo:lo + CV_ROWS, :]
        mu = jnp.mean(acc, axis=-1, keepdims=True)
        xc = acc - mu
        var = jnp.mean(xc * xc, axis=-1, keepdims=True)
        y = xc * lax.rsqrt(var + LN_EPS) * g_ref[...] + beta_ref[...]
        o_ref[r:r + CV_ROWS, :] = (y * jax.nn.sigmoid(y)).astype(BF16)


def _conv(proj, dw_w, dw_b, ln_g, ln_b, layer):
    t = proj.shape[0]
    ab, bb = COL_CVA // CV_WIDTH, COL_CVB // CV_WIDTH
    vec = _vec_spec(CV_WIDTH, layer, 1)
    return pl.pallas_call(
        _conv_kernel,
        grid=(t // CV_TM,),
        in_specs=[
            pl.BlockSpec((CV_TM, CV_WIDTH), lambda i: (i, ab)),
            pl.BlockSpec((CV_TM, CV_WIDTH), lambda i: (i, bb)),
            pl.BlockSpec((None, CONV_TAPS, CV_WIDTH), lambda i: (layer, 0, 0)),
            vec, vec, vec,
        ],
        out_specs=pl.BlockSpec((CV_TM, CV_WIDTH), lambda i: (i, 0)),
        out_shape=jax.ShapeDtypeStruct((t, CV_WIDTH), BF16),
        scratch_shapes=[pltpu.VMEM((CV_HALO + CV_TM, CV_WIDTH), F32),
                        pltpu.VMEM((CV_SUB, CV_SHIFT_ROWS, CV_WIDTH), F32)],
        compiler_params=_params(("arbitrary",)),
        name="conformer_conv",
    )(proj, proj, dw_w, dw_b, ln_g, ln_b)


SSM_CB = 128
SSM_RB = SSM_CB * SSM_CHUNK
SSM_SLAB = 2 * SSM_GROUP
SSM_SLABS = LANES // SSM_SLAB


def _pick_slabs(pieces, src_slab, lane_slab):
    out = None
    for k, piece in enumerate(pieces):
        shift = ((k - src_slab) * SSM_SLAB) % LANES
        moved = piece if shift == 0 else pltpu.roll(piece, shift, 1)
        out = moved if out is None else jnp.where(lane_slab == k, moved, out)
    return out


def _ssm_kernel(u_ref, t_ref, wre_ref, wim_ref, vre_ref, vim_ref, ar_ref, ai_ref, d_ref, y_ref,
                uf_ref, up_ref, sre_ref, sim_ref, yp_ref, yf_ref, xr_ref, xi_ref):
    npair, cb = SSM_PAIRS, SSM_CB

    @pl.when(pl.program_id(0) == 0)
    def _():
        xr_ref[...] = jnp.zeros(xr_ref.shape, F32)
        xi_ref[...] = jnp.zeros(xi_ref.shape, F32)

    ntile = SSM_WIDTH // LANES
    for j in range(ntile):
        uf_ref[j] = u_ref[:, j * LANES:(j + 1) * LANES].astype(F32)
    lane_slab = lax.shift_right_logical(lax.broadcasted_iota(jnp.int32, (cb, LANES), 1), 5)

    for q in range(SSM_CHUNK // SSM_SLABS):
        for tile in range(ntile):
            src = uf_ref.at[tile]
            pieces = [src[pl.ds(SSM_SLABS * q + k, cb, stride=SSM_CHUNK), :] for k in range(SSM_SLABS)]
            for slab in range(SSM_SLABS):
                up_ref[SSM_SLABS * tile + slab, :, q * LANES:(q + 1) * LANES] = _pick_slabs(
                    pieces, slab, lane_slab).astype(BF16)

    for p in range(npair):
        u = up_ref[p]
        sre_ref[pl.ds(p, cb, stride=npair), :] = jnp.dot(u, wre_ref[p], preferred_element_type=F32)
        sim_ref[pl.ds(p, cb, stride=npair), :] = jnp.dot(u, wim_ref[p], preferred_element_type=F32)

    ar = ar_ref[...]
    ai = ai_ref[...]

    def body(c, carry):
        xr, xi = carry
        off = pl.multiple_of(c * npair, npair)
        sr = sre_ref[pl.ds(off, npair), :]
        si = sim_ref[pl.ds(off, npair), :]
        sre_ref[pl.ds(off, npair), :] = xr
        sim_ref[pl.ds(off, npair), :] = xi
        return ar * xr - ai * xi + sr, ar * xi + ai * xr + si

    xr, xi = lax.fori_loop(0, cb, body, (xr_ref[...], xi_ref[...]), unroll=8)
    xr_ref[...] = xr
    xi_ref[...] = xi

    for p in range(npair):
        u = up_ref[p]
        sr = sre_ref[pl.ds(p, cb, stride=npair), :].astype(BF16)
        si = sim_ref[pl.ds(p, cb, stride=npair), :].astype(BF16)
        nt = (((1,), (1,)), ((), ()))
        y = jnp.dot(u, t_ref[p], preferred_element_type=F32)
        y = y + lax.dot_general(sr, vre_ref[p], nt, preferred_element_type=F32)
        yp_ref[p] = y + lax.dot_general(si, vim_ref[p], nt, preferred_element_type=F32)

    for tile in range(ntile):
        for q in range(SSM_CHUNK // SSM_SLABS):
            pieces = [yp_ref[SSM_SLABS * tile + k, :, q * LANES:(q + 1) * LANES] for k in range(SSM_SLABS)]
            dst = yf_ref.at[tile]
            for slab in range(SSM_SLABS):
                dst[pl.ds(SSM_SLABS * q + slab, cb, stride=SSM_CHUNK), :] = _pick_slabs(pieces, slab, lane_slab)

    for j in range(ntile):
        cols = slice(j * LANES, (j + 1) * LANES)
        y_ref[:, cols] = (yf_ref[j] + uf_ref[j] * d_ref[:, cols]).astype(BF16)


def _ssm(proj, mats):
    t = proj.shape[0]
    vm = pl.BlockSpec(memory_space=pltpu.VMEM)
    return pl.pallas_call(
        _ssm_kernel,
        grid=(t // SSM_RB,),
        in_specs=[pl.BlockSpec((SSM_RB, SSM_WIDTH), lambda i: (i, COL_SSM // SSM_WIDTH))] + [vm] * 8,
        out_specs=pl.BlockSpec((SSM_RB, SSM_WIDTH), lambda i: (i, 0)),
        out_shape=jax.ShapeDtypeStruct((t, SSM_WIDTH), BF16),
        scratch_shapes=[
            pltpu.VMEM((SSM_WIDTH // LANES, SSM_RB, LANES), F32),
            pltpu.VMEM((SSM_PAIRS, SSM_CB, SSM_PAIR_W), BF16),
            pltpu.VMEM((SSM_CB * SSM_PAIRS, LANES), F32),
            pltpu.VMEM((SSM_CB * SSM_PAIRS, LANES), F32),
            pltpu.VMEM((SSM_PAIRS, SSM_CB, SSM_PAIR_W), F32),
            pltpu.VMEM((SSM_WIDTH // LANES, SSM_RB, LANES), F32),
            pltpu.VMEM((SSM_PAIRS, LANES), F32),
            pltpu.VMEM((SSM_PAIRS, LANES), F32),
        ],
        compiler_params=_params(("arbitrary",)),
        name="s5_scan",
    )(proj, *mats)


def _ssm_gen_kernel(bre_ref, bim_ref, cre_ref, cim_ref, pr_ref, pi_ref,
                    t_ref, wre_ref, wim_ref, vre_ref, vim_ref, car_ref, cai_ref):
    bre, bim, cre, cim = bre_ref[...], bim_ref[...], cre_ref[...], cim_ref[...]
    for tau in range(SSM_CHUNK):
        rows = slice(tau * SSM_SLAB, (tau + 1) * SSM_SLAB)
        pr = pr_ref[tau:tau + 1, :]
        pi = pi_ref[tau:tau + 1, :]
        car_ref[rows, :] = cre * pr - cim * pi
        cai_ref[rows, :] = cre * pi + cim * pr
    nt = (((1,), (1,)), ((), ()))
    hp = lax.Precision.HIGHEST
    k = (lax.dot_general(bre, car_ref[...], nt, precision=hp, preferred_element_type=F32)
         - lax.dot_general(bim, cai_ref[...], nt, precision=hp, preferred_element_type=F32))
    lane = lax.broadcasted_iota(jnp.int32, k.shape, 1)
    for s in range(SSM_CHUNK):
        rows = slice(s * SSM_SLAB, (s + 1) * SSM_SLAB)
        moved = k if s == 0 else pltpu.roll(k, s * SSM_SLAB, 1)
        t_ref[rows, :] = jnp.where(lane >= s * SSM_SLAB, moved, 0.0).astype(BF16)
        pr = pr_ref[SSM_CHUNK - 1 - s:SSM_CHUNK - s, :]
        pi = pi_ref[SSM_CHUNK - 1 - s:SSM_CHUNK - s, :]
        wre_ref[rows, :] = (bre * pr - bim * pi).astype(BF16)
        wim_ref[rows, :] = (bre * pi + bim * pr).astype(BF16)
        qr = pr_ref[s + 1:s + 2, :]
        qi = pi_ref[s + 1:s + 2, :]
        vre_ref[rows, :] = (cre * qr - cim * qi).astype(BF16)
        vim_ref[rows, :] = (-(cre * qi + cim * qr)).astype(BF16)


def _ssm_matrices(lam_re, lam_im, log_dt, b_re, b_im, c_re, c_im, d_skip):
    nl = lam_re.shape[0]
    lr = lam_re.astype(F32)
    li = lam_im.astype(F32)
    dt = jnp.exp(log_dt.astype(F32))[..., None]
    mag = jnp.exp(lr * dt)
    ab_re = mag * jnp.cos(li * dt)
    ab_im = mag * jnp.sin(li * dt)
    den = lr * lr + li * li
    f_re = ((ab_re - 1.0) * lr + ab_im * li) / den
    f_im = (ab_im * lr - (ab_re - 1.0) * li) / den
    br = b_re.astype(F32)
    bi = b_im.astype(F32)
    bb_re = f_re[..., None] * br - f_im[..., None] * bi
    bb_im = f_re[..., None] * bi + f_im[..., None] * br
    tau = jnp.arange(SSM_CHUNK + 1, dtype=F32)[None, None, :, None]
    pmag = jnp.exp(tau * (lr * dt)[:, :, None, :])
    pw_re = pmag * jnp.cos(tau * (li * dt)[:, :, None, :])
    pw_im = pmag * jnp.sin(tau * (li * dt)[:, :, None, :])
    cr = c_re.astype(F32)
    ci = c_im.astype(F32)
    eye2 = jnp.eye(2, dtype=F32)

    def paired(a):
        return a.reshape((nl, SSM_PAIRS, 2) + a.shape[2:])

    bbd_re = jnp.einsum('lpenk,ef->lpekfn', paired(bb_re), eye2).reshape(nl, SSM_PAIRS, SSM_SLAB, LANES)
    bbd_im = jnp.einsum('lpenk,ef->lpekfn', paired(bb_im), eye2).reshape(nl, SSM_PAIRS, SSM_SLAB, LANES)
    cbd_re = jnp.einsum('lpehn,ef->lpehfn', paired(cr), eye2).reshape(nl, SSM_PAIRS, SSM_SLAB, LANES)
    cbd_im = jnp.einsum('lpehn,ef->lpehfn', paired(ci), eye2).reshape(nl, SSM_PAIRS, SSM_SLAB, LANES)
    pwp_re = paired(pw_re).transpose(0, 1, 3, 2, 4).reshape(nl, SSM_PAIRS, SSM_CHUNK + 1, LANES)
    pwp_im = paired(pw_im).transpose(0, 1, 3, 2, 4).reshape(nl, SSM_PAIRS, SSM_CHUNK + 1, LANES)

    def spec(r, c):
        return pl.BlockSpec((None, None, r, c), lambda l, p: (l, p, 0, 0))

    wide = jax.ShapeDtypeStruct((nl, SSM_PAIRS, SSM_PAIR_W, SSM_PAIR_W), BF16)
    tall = jax.ShapeDtypeStruct((nl, SSM_PAIRS, SSM_PAIR_W, LANES), BF16)
    tm, w_re, w_im, vt_re, vt_im = pl.pallas_call(
        _ssm_gen_kernel,
        grid=(nl, SSM_PAIRS),
        in_specs=[spec(SSM_SLAB, LANES)] * 4 + [spec(SSM_CHUNK + 1, LANES)] * 2,
        out_specs=[spec(SSM_PAIR_W, SSM_PAIR_W)] + [spec(SSM_PAIR_W, LANES)] * 4,
        out_shape=[wide, tall, tall, tall, tall],
        scratch_shapes=[pltpu.VMEM((SSM_PAIR_W, LANES), F32), pltpu.VMEM((SSM_PAIR_W, LANES), F32)],
        compiler_params=_params(("arbitrary", "arbitrary")),
        name="s5_matrices",
    )(bbd_re, bbd_im, cbd_re, cbd_im, pwp_re, pwp_im)
    a_re = pwp_re[:, :, SSM_CHUNK]
    a_im = pwp_im[:, :, SSM_CHUNK]
    d_t = d_skip.astype(F32).reshape(nl, 1, SSM_WIDTH)
    return tm, w_re, w_im, vt_re, vt_im, a_re, a_im, d_t


MG_TM = 512
MG_TN = 512


def _merge_mix_kernel(oa_ref, zb_ref, yc_ref, ga_ref, gb_ref, gc_ref, wda_ref, wcv_ref, wglu_ref,
                      bglu_ref, wso_ref, wmix_ref, x_ref, o_ref, sc_ref):
    n = pl.program_id(1)

    @pl.when(n == 0)
    def _():
        glu = jnp.dot(yc_ref[...], wglu_ref[...], preferred_element_type=F32) + bglu_ref[...]
        sc_ref[...] = (glu[:, :SSM_WIDTH] * jax.nn.sigmoid(glu[:, SSM_WIDTH:])).astype(BF16)
        o_ref[...] = x_ref[...]

    y_a = jnp.dot(oa_ref[...], wda_ref[...], preferred_element_type=F32)
    y_b = jnp.dot(zb_ref[...], wcv_ref[...], preferred_element_type=F32)
    y_c = jnp.dot(sc_ref[...], wso_ref[...], preferred_element_type=F32)
    merged = (jax.nn.sigmoid(ga_ref[...].astype(F32)) * y_a
              + jax.nn.sigmoid(gb_ref[...].astype(F32)) * y_b
              + jax.nn.sigmoid(gc_ref[...].astype(F32)) * y_c)
    o_ref[...] += jnp.dot(merged.astype(BF16), wmix_ref[...], preferred_element_type=F32)


def _merge_mix(o_a, z_b, y_c, proj, w_da, w_cv, w_glu, b_glu, w_so, w_mix, x, layer):
    t = o_a.shape[0]
    ga, gb, gc = COL_GA // MG_TN, COL_GB // MG_TN, COL_GC // MG_TN

    def wcol(k):
        return pl.BlockSpec((None, k, MG_TN), lambda i, n: (layer, 0, n))

    return pl.pallas_call(
        _merge_mix_kernel,
        grid=(t // MG_TM, D_MODEL // MG_TN),
        in_specs=[
            pl.BlockSpec((MG_TM, DA_WIDTH), lambda i, n: (i, 0)),
            pl.BlockSpec((MG_TM, CV_WIDTH), lambda i, n: (i, 0)),
            pl.BlockSpec((MG_TM, SSM_WIDTH), lambda i, n: (i, 0)),
            pl.BlockSpec((MG_TM, MG_TN), lambda i, n: (i, ga + n)),
            pl.BlockSpec((MG_TM, MG_TN), lambda i, n: (i, gb + n)),
            pl.BlockSpec((MG_TM, MG_TN), lambda i, n: (i, gc + n)),
            wcol(DA_WIDTH), wcol(CV_WIDTH),
            pl.BlockSpec((None, SSM_WIDTH, 2 * SSM_WIDTH), lambda i, n: (layer, 0, 0)),
            _vec_spec(2 * SSM_WIDTH, layer, 2),
            wcol(SSM_WIDTH),
            pl.BlockSpec((None, MG_TN, D_MODEL), lambda i, n: (layer, n, 0)),
            pl.BlockSpec((MG_TM, D_MODEL), lambda i, n: (i, 0)),
        ],
        out_specs=pl.BlockSpec((MG_TM, D_MODEL), lambda i, n: (i, 0)),
        out_shape=jax.ShapeDtypeStruct((t, D_MODEL), F32),
        scratch_shapes=[pltpu.VMEM((MG_TM, SSM_WIDTH), BF16)],
        compiler_params=_params(("arbitrary", "arbitrary")),
        name="merge_mix",
    )(o_a, z_b, y_c, proj, proj, proj, w_da, w_cv, w_glu, b_glu, w_so, w_mix, x)


def _norm_matmul_kernel(x_ref, g_ref, w_ref, o_ref):
    h = _rms(x_ref[...], g_ref[...], RMS_EPS).astype(BF16)
    o_ref[...] = jnp.dot(h, w_ref[...], preferred_element_type=F32).astype(o_ref.dtype)


def _mem_kv(mem, g_all, w_all, layer):
    m = mem.shape[0]
    n = 2 * XA_WIDTH
    return pl.pallas_call(
        _norm_matmul_kernel,
        grid=(1,),
        in_specs=[
            pl.BlockSpec((m, D_MODEL), lambda i: (0, 0)),
            _vec_spec(D_MODEL, layer, 1),
            pl.BlockSpec((None, D_MODEL, n), lambda i: (layer, 0, 0)),
        ],
        out_specs=pl.BlockSpec((m, n), lambda i: (0, 0)),
        out_shape=jax.ShapeDtypeStruct((m, n), BF16),
        compiler_params=_params(("arbitrary",)),
        name="mem_kv",
    )(mem, g_all, w_all)


XA_TM = 512


def _xattn_kernel(x_ref, g_ref, wq_ref, kv_ref, wo_ref, o_ref):
    x = x_ref[...]
    h = _rms(x, g_ref[...], RMS_EPS).astype(BF16)
    q = jnp.dot(h, wq_ref[...], preferred_element_type=F32).astype(BF16)
    heads = []
    for hd in range(XA_HEADS):
        lo = hd * XA_HEAD_DIM
        k = kv_ref[:, lo:lo + XA_HEAD_DIM]
        v = kv_ref[:, XA_WIDTH + lo:XA_WIDTH + lo + XA_HEAD_DIM]
        s = lax.dot_general(q[:, lo:lo + XA_HEAD_DIM], k, (((1,), (1,)), ((), ())),
                            preferred_element_type=F32) * (XA_HEAD_DIM ** -0.5)
        m = jnp.max(s, axis=-1, keepdims=True)
        e = jnp.exp(s - m)
        p = e / jnp.sum(e, axis=-1, keepdims=True)
        heads.append(jnp.dot(p.astype(BF16), v, preferred_element_type=F32).astype(BF16))
    o = jnp.concatenate(heads, axis=-1)
    o_ref[...] = x + jnp.dot(o, wo_ref[...], preferred_element_type=F32)


def _xattn(x, g_all, wq_all, kv, wo_all, layer):
    t = x.shape[0]
    return pl.pallas_call(
        _xattn_kernel,
        grid=(t // XA_TM,),
        in_specs=[
            pl.BlockSpec((XA_TM, D_MODEL), lambda i: (i, 0)),
            _vec_spec(D_MODEL, layer, 1),
            pl.BlockSpec((None, D_MODEL, XA_WIDTH), lambda i: (layer, 0, 0)),
            pl.BlockSpec((MEM_LEN, 2 * XA_WIDTH), lambda i: (0, 0)),
            pl.BlockSpec((None, XA_WIDTH, D_MODEL), lambda i: (layer, 0, 0)),
        ],
        out_specs=pl.BlockSpec((XA_TM, D_MODEL), lambda i: (i, 0)),
        out_shape=jax.ShapeDtypeStruct((t, D_MODEL), F32),
        compiler_params=_params(("arbitrary",)),
        name="mem_xattn",
    )(x, g_all, wq_all, kv, wo_all)


RT_TM = 512
_E_LANE0 = MOE_GROUPS


def _router_kernel(x_ref, g_ref, w_ref, b_ref, meta_ref, cnt_ref, run_ref):
    i = pl.program_id(0)

    @pl.when(i == 0)
    def _():
        run_ref[...] = jnp.zeros(run_ref.shape, F32)

    h = _rms(x_ref[...], g_ref[...], RMS_EPS).astype(BF16)
    logits = jnp.dot(h, w_ref[...], preferred_element_type=F32) + b_ref[...]
    lane = lax.broadcasted_iota(jnp.int32, logits.shape, 1).astype(F32)
    neg = jnp.float32(-jnp.inf)
    big = jnp.float32(LANES)

    def first_argmax(vals):
        top = jnp.max(vals, axis=-1, keepdims=True)
        idx = jnp.min(jnp.where(vals == top, lane, big), axis=-1, keepdims=True)
        return top, idx

    gl = jnp.where(lane < MOE_GROUPS, logits, neg)
    g_top, g_idx = first_argmax(gl)
    g_w = 1.0 / jnp.sum(jnp.exp(gl - g_top), axis=-1, keepdims=True)
    e_lane = lane - _E_LANE0
    in_group = jnp.logical_and(e_lane >= g_idx * MOE_PER_GROUP, e_lane < (g_idx + 1) * MOE_PER_GROUP)
    el = jnp.where(in_group, logits, neg)
    v1, i1 = first_argmax(el)
    el2 = jnp.where(lane == i1, neg, el)
    v2, i2 = first_argmax(el2)
    e2 = jnp.exp(v2 - v1)
    w1 = 1.0 / (1.0 + e2)
    w2 = e2 / (1.0 + e2)
    oh1 = (lane == i1).astype(F32)
    oh2 = (lane == i2).astype(F32)
    both = oh1 + oh2
    row = lax.broadcasted_iota(jnp.int32, (RT_TM, RT_TM), 0)
    col = lax.broadcasted_iota(jnp.int32, (RT_TM, RT_TM), 1)
    earlier = jnp.where(col < row, 1.0, 0.0).astype(BF16)
    before = jnp.dot(earlier, both.astype(BF16), preferred_element_type=F32) + run_ref[...]
    rank1 = jnp.sum(oh1 * before, axis=-1, keepdims=True)
    rank2 = jnp.sum(oh2 * before, axis=-1, keepdims=True)
    run_ref[...] += jnp.sum(both, axis=0, keepdims=True)
    cnt_ref[...] = run_ref[...]
    meta = jnp.where(lane == 0.0, i1 - _E_LANE0, 0.0)
    for k, val in enumerate((i2 - _E_LANE0, rank1, rank2, w1 * g_w, w2 * g_w), start=1):
        meta = jnp.where(lane == float(k), val, meta)
    meta_ref[...] = meta


def _router(x, g_all, w_r, b_r, layer):
    t = x.shape[0]
    return pl.pallas_call(
        _router_kernel,
        grid=(t // RT_TM,),
        in_specs=[
            pl.BlockSpec((RT_TM, D_MODEL), lambda i: (i, 0)),
            _vec_spec(D_MODEL, layer, 1),
            pl.BlockSpec((None, D_MODEL, LANES), lambda i: (layer, 0, 0)),
            _vec_spec(LANES, layer, 1),
        ],
        out_specs=[
            pl.BlockSpec((RT_TM, LANES), lambda i: (i, 0)),
            pl.BlockSpec((1, LANES), lambda i: (0, 0)),
        ],
        out_shape=[jax.ShapeDtypeStruct((t, LANES), F32), jax.ShapeDtypeStruct((1, LANES), F32)],
        scratch_shapes=[pltpu.VMEM((1, LANES), F32)],
        compiler_params=_params(("arbitrary",)),
        name="moe_router",
    )(x, g_all, w_r, b_r)


EX_TM = 256
EX_ROWS = 2 * SEQ
EX_TILES = EX_ROWS // EX_TM
EX_STEPS = EX_TILES + MOE_EXPERTS - 1


def _row_copy(src, src_row, dst, dst_row, sem):
    return pltpu.make_async_copy(src.at[pl.ds(src_row, 1), :], dst.at[pl.ds(dst_row, 1), :], sem)


DP_TM = 256


def _dispatch_kernel(p1_ref, p2_ref, x_ref, xs_hbm, sem):
    base = pl.program_id(0) * DP_TM

    def issue(j, c):
        _row_copy(x_ref, j, xs_hbm, p1_ref[base + j], sem).start(priority=0)
        _row_copy(x_ref, j, xs_hbm, p2_ref[base + j], sem).start(priority=1)
        return c

    lax.fori_loop(0, DP_TM, issue, 0, unroll=True)

    def wait(j, c):
        _row_copy(x_ref, j, xs_hbm, 0, sem).wait()
        _row_copy(x_ref, j, xs_hbm, 0, sem).wait()
        return c

    lax.fori_loop(0, DP_TM, wait, 0, unroll=8)


def _dispatch(pos1, pos2, x):
    t = x.shape[0]
    grid_spec = pltpu.PrefetchScalarGridSpec(
        num_scalar_prefetch=2,
        grid=(t // DP_TM,),
        in_specs=[pl.BlockSpec((DP_TM, D_MODEL), lambda i, p1, p2: (i, 0))],
        out_specs=pl.BlockSpec(memory_space=pl.ANY),
        scratch_shapes=[pltpu.SemaphoreType.DMA(())],
    )
    return pl.pallas_call(
        _dispatch_kernel,
        grid_spec=grid_spec,
        out_shape=jax.ShapeDtypeStruct((EX_ROWS, D_MODEL), F32),
        compiler_params=_params(("arbitrary",)),
        name="moe_dispatch",
    )(pos1, pos2, x)


def _experts_kernel(se_ref, st_ref, lo_ref, hi_ref, first_ref, wnew_ref, x_ref, g_ref, wg_ref, wu_ref,
                    wd_ref, o_ref, wgb_ref, wub_ref, wdb_ref):
    s = pl.program_id(0)
    lo = lo_ref[s]
    hi = hi_ref[s]

    @pl.when(wnew_ref[s] == 1)
    def _():
        wgb_ref[...] = wg_ref[...].astype(BF16)
        wub_ref[...] = wu_ref[...].astype(BF16)
        wdb_ref[...] = wd_ref[...].astype(BF16)

    @pl.when(hi > lo)
    def _():
        h = _rms(x_ref[...], g_ref[...], RMS_EPS).astype(BF16)
        gate = jnp.dot(h, wgb_ref[...], preferred_element_type=F32)
        up = jnp.dot(h, wub_ref[...], preferred_element_type=F32)
        act = (gate * jax.nn.sigmoid(gate) * up).astype(BF16)
        res = jnp.dot(act, wdb_ref[...], preferred_element_type=F32)
        row = lax.broadcasted_iota(jnp.int32, (EX_TM, 1), 0)
        mine = jnp.logical_and(row >= lo, row < hi)

        @pl.when(first_ref[s] == 1)
        def _():
            o_ref[...] = jnp.where(mine, res, 0.0)

        @pl.when(first_ref[s] == 0)
        def _():
            o_ref[...] = jnp.where(mine, res, o_ref[...])


def _experts(plan, xs_sorted, g_all, wg_all, wu_all, wd_all, layer):
    def wspec(k, n):
        return pl.BlockSpec((None, None, k, n), lambda s, se, st, lo, hi, fi, wn: (layer, se[s], 0, 0))

    grid_spec = pltpu.PrefetchScalarGridSpec(
        num_scalar_prefetch=6,
        grid=(EX_STEPS,),
        in_specs=[
            pl.BlockSpec((EX_TM, D_MODEL), lambda s, se, st, lo, hi, fi, wn: (st[s], 0)),
            pl.BlockSpec((None, 1, D_MODEL), lambda s, se, st, lo, hi, fi, wn: (layer, 0, 0)),
            wspec(D_MODEL, MOE_FF), wspec(D_MODEL, MOE_FF), wspec(MOE_FF, D_MODEL),
        ],
        out_specs=pl.BlockSpec((EX_TM, D_MODEL), lambda s, se, st, lo, hi, fi, wn: (st[s], 0)),
        scratch_shapes=[pltpu.VMEM((D_MODEL, MOE_FF), BF16), pltpu.VMEM((D_MODEL, MOE_FF), BF16),
                        pltpu.VMEM((MOE_FF, D_MODEL), BF16)],
    )
    return pl.pallas_call(
        _experts_kernel,
        grid_spec=grid_spec,
        out_shape=jax.ShapeDtypeStruct((EX_ROWS, D_MODEL), F32),
        compiler_params=_params(("arbitrary",)),
        name="moe_experts",
    )(*plan, xs_sorted, g_all, wg_all, wu_all, wd_all)


CB_TM = 256


def _combine_kernel(p1_ref, p2_ref, y_hbm, x_ref, meta_ref, g_ref, o_ref, n_ref, y1buf, y2buf, sem):
    i = pl.program_id(0)
    slot = lax.rem(i, 2)

    def fetch(tile, slot_):
        base = tile * CB_TM

        def issue(j, c):
            _row_copy(y_hbm, p1_ref[base + j], y1buf.at[slot_], j, sem.at[slot_]).start(priority=0)
            _row_copy(y_hbm, p2_ref[base + j], y2buf.at[slot_], j, sem.at[slot_]).start(priority=1)
            return c

        lax.fori_loop(0, CB_TM, issue, 0, unroll=True)

    @pl.when(i == 0)
    def _():
        fetch(0, 0)

    @pl.when(i + 1 < pl.num_programs(0))
    def _():
        fetch(i + 1, 1 - slot)

    def wait(j, c):
        _row_copy(y_hbm, 0, y1buf.at[slot], j, sem.at[slot]).wait()
        _row_copy(y_hbm, 0, y2buf.at[slot], j, sem.at[slot]).wait()
        return c

    lax.fori_loop(0, CB_TM, wait, 0, unroll=8)
    meta = meta_ref[...]
    x_new = x_ref[...] + meta[:, 4:5] * y1buf[slot] + meta[:, 5:6] * y2buf[slot]
    o_ref[...] = x_new
    n_ref[...] = _rms(x_new, g_ref[...], RMS_EPS).astype(n_ref.dtype)


def _combine(pos1, pos2, ys, x, meta, g_next, next_dtype):
    t = x.shape[0]
    tile = pl.BlockSpec((CB_TM, D_MODEL), lambda i, p1, p2: (i, 0))
    grid_spec = pltpu.PrefetchScalarGridSpec(
        num_scalar_prefetch=2,
        grid=(t // CB_TM,),
        in_specs=[
            pl.BlockSpec(memory_space=pl.ANY),
            tile,
            pl.BlockSpec((CB_TM, LANES), lambda i, p1, p2: (i, 0)),
            pl.BlockSpec((1, D_MODEL), lambda i, p1, p2: (0, 0)),
        ],
        out_specs=[tile, tile],
        scratch_shapes=[pltpu.VMEM((2, CB_TM, D_MODEL), F32), pltpu.VMEM((2, CB_TM, D_MODEL), F32),
                        pltpu.SemaphoreType.DMA((2,))],
    )
    return pl.pallas_call(
        _combine_kernel,
        grid_spec=grid_spec,
        out_shape=[jax.ShapeDtypeStruct((t, D_MODEL), F32), jax.ShapeDtypeStruct((t, D_MODEL), next_dtype)],
        compiler_params=_params(("arbitrary",)),
        name="moe_combine",
    )(pos1, pos2, ys, x, meta, g_next)


def _dispatch_plan(meta, cnt):
    t = meta.shape[0]
    e1 = meta[:, 0].astype(jnp.int32)
    e2 = meta[:, 1].astype(jnp.int32)
    r1 = meta[:, 2].astype(jnp.int32)
    r2 = meta[:, 3].astype(jnp.int32)
    counts = cnt[0, _E_LANE0:_E_LANE0 + MOE_EXPERTS].astype(jnp.int32)
    ends = jnp.cumsum(counts)
    starts = ends - counts
    pos1 = starts[e1] + r1
    pos2 = starts[e2] + r2
    t_lo = starts // EX_TM
    t_hi = (ends + EX_TM - 1) // EX_TM
    nsteps = jnp.where(counts > 0, t_hi - t_lo, 0)
    step_end = jnp.cumsum(nsteps)
    step_start = step_end - nsteps
    s = jnp.arange(EX_STEPS, dtype=jnp.int32)
    se = jnp.minimum(jnp.sum((s[:, None] >= step_end[None, :]).astype(jnp.int32), axis=1), MOE_EXPERTS - 1)
    valid = s < step_end[-1]
    st = jnp.where(valid, t_lo[se] + s - step_start[se], EX_TILES - 1)
    lo = jnp.where(valid, jnp.clip(starts[se] - st * EX_TM, 0, EX_TM), 0)
    hi = jnp.where(valid, jnp.clip(ends[se] - st * EX_TM, 0, EX_TM), 0)
    first = jnp.concatenate([jnp.ones((1,), jnp.int32), (st[1:] != st[:-1]).astype(jnp.int32)])
    wnew = jnp.concatenate([jnp.ones((1,), jnp.int32), (se[1:] != se[:-1]).astype(jnp.int32)])
    plan = tuple(a.astype(jnp.int32) for a in (se, st, lo, hi, first, wnew))
    return pos1.astype(jnp.int32), pos2.astype(jnp.int32), plan


FN_TM = 512


def _norm_kernel(x_ref, g_ref, o_ref):
    o_ref[...] = _rms(x_ref[...], g_ref[...], RMS_EPS).astype(o_ref.dtype)


def _first_norm(x, g_all):
    t = x.shape[0]
    return pl.pallas_call(
        _norm_kernel,
        grid=(t // FN_TM,),
        in_specs=[pl.BlockSpec((FN_TM, D_MODEL), lambda i: (i, 0)), _vec_spec(D_MODEL, 0, 1)],
        out_specs=pl.BlockSpec((FN_TM, D_MODEL), lambda i: (i, 0)),
        out_shape=jax.ShapeDtypeStruct((t, D_MODEL), BF16),
        compiler_params=_params(("arbitrary",)),
        name="first_norm",
    )(x, g_all)


def kernel(x, mem, positions, norm_mix, w_in, da_lam_q1, da_lam_k1, da_lam_q2, da_lam_k2, da_head_norm, w_da_out, cv_dw_w, cv_dw_b, cv_ln_g, cv_ln_b, w_cv_out, ssm_lam_re, ssm_lam_im, ssm_log_dt, ssm_b_re, ssm_b_im, ssm_c_re, ssm_c_im, ssm_d, w_ssm_glu, b_ssm_glu, w_ssm_out, w_mix_out, norm_xa, norm_mem, w_xa_q, w_xa_kv, w_xa_out, norm_ffn, w_router_group, b_router_group, w_router_expert, b_router_expert, w_exp_gate, w_exp_up, w_exp_down, norm_final):
    bsz, seq, _ = x.shape
    assert bsz == 1 and seq == SEQ
    nl = w_in.shape[0]
    xs = x.reshape(seq, D_MODEL).astype(F32)
    mem2 = mem.reshape(MEM_LEN, D_MODEL).astype(F32)

    inv_freq = ROPE_THETA ** (-jnp.arange(0, DA_HEAD_DIM, 2, dtype=F32) / DA_HEAD_DIM)
    ang = positions.reshape(seq).astype(F32)[:, None] * inv_freq
    cos = jnp.cos(ang)
    sin = jnp.sin(ang)
    cos_t = jnp.concatenate([cos, cos, cos, cos], axis=-1)
    sin_t = jnp.concatenate([-sin, sin, -sin, sin], axis=-1)

    w_da_b = w_da_out.astype(BF16)
    w_cv_b = w_cv_out.astype(BF16)
    w_glu_b = w_ssm_glu.astype(BF16)
    w_so_b = w_ssm_out.astype(BF16)
    w_mix_b = w_mix_out.astype(BF16)
    w_xq_b = w_xa_q.astype(BF16)
    w_xkv_b = w_xa_kv.astype(BF16)
    w_xo_b = w_xa_out.astype(BF16)
    pad = LANES - MOE_GROUPS - MOE_EXPERTS
    w_r = jnp.concatenate([w_router_group, w_router_expert,
                           jnp.zeros((nl, D_MODEL, pad), F32)], axis=-1).astype(BF16)
    b_r = jnp.concatenate([b_router_group, b_router_expert, jnp.zeros((nl, pad), F32)], axis=-1).astype(F32)

    def vec3(a):
        return a.astype(F32).reshape(nl, 1, a.shape[-1])

    norm_mix, da_head_norm, cv_dw_b, cv_ln_g, cv_ln_b, b_ssm_glu, norm_xa, norm_mem, norm_ffn, b_r = map(
        vec3, (norm_mix, da_head_norm, cv_dw_b, cv_ln_g, cv_ln_b, b_ssm_glu, norm_xa, norm_mem, norm_ffn, b_r))
    cv_dw_w = cv_dw_w.astype(F32)

    lam_inits = jnp.asarray([0.8 - 0.6 * math.exp(-0.3 * l) for l in range(nl)], F32)
    lam_pack = jnp.stack([da_lam_q1, da_lam_k1, da_lam_q2, da_lam_k2], axis=1).astype(F32)
    lam_pack = jnp.concatenate(
        [lam_pack, jnp.broadcast_to(lam_inits[:, None, None], (nl, 4, DA_HEAD_DIM))], axis=1)

    ssm_mats = _ssm_matrices(ssm_lam_re, ssm_lam_im, ssm_log_dt, ssm_b_re, ssm_b_im,
                             ssm_c_re, ssm_c_im, ssm_d)

    h = _first_norm(xs, norm_mix)
    for l in range(nl):
        proj = _inproj(h, w_in, l, cos_t, sin_t)
        o_a = _attention(proj, lam_pack, da_head_norm, l)
        z_b = _conv(proj, cv_dw_w, cv_dw_b, cv_ln_g, cv_ln_b, l)
        y_c = _ssm(proj, [m[l] for m in ssm_mats])
        xs = _merge_mix(o_a, z_b, y_c, proj, w_da_b, w_cv_b, w_glu_b, b_ssm_glu, w_so_b, w_mix_b, xs, l)
        kv = _mem_kv(mem2, norm_mem, w_xkv_b, l)
        xs = _xattn(xs, norm_xa, w_xq_b, kv, w_xo_b, l)
        meta, cnt = _router(xs, norm_ffn, w_r, b_r, l)
        pos1, pos2, plan = _dispatch_plan(meta, cnt)
        xs_sorted = _dispatch(pos1, pos2, xs)
        ys = _experts(plan, xs_sorted, norm_ffn, w_exp_gate, w_exp_up, w_exp_down, l)
        if l + 1 < nl:
            xs, h = _combine(pos1, pos2, ys, xs, meta, norm_mix[l + 1], BF16)
        else:
            xs, out = _combine(pos1, pos2, ys, xs, meta, norm_final.astype(F32).reshape(1, D_MODEL), F32)
    return out.reshape(bsz, seq, D_MODEL)
```

```python
import functools
import math

import jax
import jax.numpy as jnp
from jax import lax
from jax.experimental import pallas as pl
from jax.experimental.pallas import tpu as pltpu

F32 = jnp.float32
BF16 = jnp.bfloat16

D_MODEL = 2048
SEQ = 8192
DEPTH = 4
MEM_LEN = 256
DA_HEADS = 8
DA_HEAD_DIM = 64
DA_V_DIM = 128
DA_WIDTH = 1024
ROPE_THETA = 10000.0
CV_WIDTH = 512
CONV_TAPS = 31
SSM_WIDTH = 512
SSM_GROUP = 16
SSM_GROUPS = 32
SSM_STATE = 64
XA_HEADS = 4
XA_HEAD_DIM = 128
XA_WIDTH = 512
MOE_GROUPS = 4
MOE_PER_GROUP = 4
MOE_EXPERTS = 16
MOE_FF = 512
RMS_EPS = 1e-6
HEAD_NORM_EPS = 1e-5
LN_EPS = 1e-5

COL_Q = 0
COL_K = COL_Q + 1024
COL_V = COL_K + 1024
COL_CVA = COL_V + 1024
COL_CVB = COL_CVA + CV_WIDTH
COL_SSM = COL_CVB + CV_WIDTH
COL_GA = COL_SSM + SSM_WIDTH
COL_GB = COL_GA + D_MODEL
COL_GC = COL_GB + D_MODEL
IN_TOTAL = COL_GC + D_MODEL

LANES = 128
VMEM_LIMIT = 56 * 1024 * 1024

SSM_CHUNK = 16
SSM_NCHUNK = SEQ // SSM_CHUNK
SSM_PAIRS = SSM_GROUPS // 2
SSM_PAIR_W = 2 * SSM_CHUNK * SSM_GROUP


def _params(sem, vmem=VMEM_LIMIT):
    return pltpu.CompilerParams(dimension_semantics=sem, vmem_limit_bytes=vmem)


def _vec_spec(width, layer, ngrid):
    if ngrid == 1:
        return pl.BlockSpec((None, 1, width), lambda i: (layer, 0, 0))
    return pl.BlockSpec((None, 1, width), lambda i, j: (layer, 0, 0))


def _rms(xf, g, eps):
    ms = jnp.mean(xf * xf, axis=-1, keepdims=True)
    return xf * lax.rsqrt(ms + eps) * g


INP_TM = 2048
INP_TN = 512
_Q_TILE0 = COL_Q // INP_TN
_K_TILE0 = COL_K // INP_TN
_V_TILE0 = COL_V // INP_TN


def _inproj_kernel(h_ref, w_ref, cos_ref, sin_ref, o_ref):
    j = pl.program_id(1)
    acc = jnp.dot(h_ref[...], w_ref[...].astype(BF16), preferred_element_type=F32)
    is_rope = jnp.logical_and(j >= _Q_TILE0, j < _V_TILE0)

    @pl.when(is_rope)
    def _():
        scale = jnp.where(j < _K_TILE0, math.log2(math.e) * DA_HEAD_DIM ** -0.5, 1.0).astype(F32)
        cos = cos_ref[...] * scale
        sin = sin_ref[...] * scale
        lane = lax.broadcasted_iota(jnp.int32, (INP_TM, LANES), 1)
        first_half = (lane % DA_HEAD_DIM) < (DA_HEAD_DIM // 2)
        for c in range(INP_TN // LANES):
            t = acc[:, c * LANES:(c + 1) * LANES]
            swapped = jnp.where(first_half, pltpu.roll(t, LANES - 32, 1), pltpu.roll(t, 32, 1))
            o_ref[:, c * LANES:(c + 1) * LANES] = (t * cos + swapped * sin).astype(BF16)

    @pl.when(jnp.logical_not(is_rope))
    def _():
        o_ref[...] = acc.astype(BF16)


def _inproj(h, w_all, layer, cos_t, sin_t):
    t = h.shape[0]
    return pl.pallas_call(
        _inproj_kernel,
        grid=(t // INP_TM, IN_TOTAL // INP_TN),
        in_specs=[
            pl.BlockSpec((INP_TM, D_MODEL), lambda i, j: (i, 0)),
            pl.BlockSpec((None, D_MODEL, INP_TN), lambda i, j: (layer, 0, j)),
            pl.BlockSpec((INP_TM, LANES), lambda i, j: (i, 0)),
            pl.BlockSpec((INP_TM, LANES), lambda i, j: (i, 0)),
        ],
        out_specs=pl.BlockSpec((INP_TM, INP_TN), lambda i, j: (i, j)),
        out_shape=jax.ShapeDtypeStruct((t, IN_TOTAL), BF16),
        compiler_params=_params(("arbitrary", "arbitrary")),
        name="inproj",
    )(h, w_all, cos_t, sin_t)


ATT_TQ = 1024
ATT_TK = 1024


ATT_RG = 1024
ATT_DRG = 512


def _attn_kernel(lam_ref, g_ref, q_ref, k_ref, v_ref, o_ref, qs_ref, m_ref, l_ref, acc_ref):
    i = pl.program_id(1)
    tq = ATT_TQ
    q = q_ref[...]
    lane = lax.broadcasted_iota(jnp.int32, q.shape, 1)
    zero = jnp.zeros_like(q)
    qs_ref[0:tq, :] = jnp.where(lane < DA_HEAD_DIM, q, zero)
    qs_ref[tq:2 * tq, :] = jnp.where(lane >= DA_HEAD_DIM, q, zero)
    m_ref[...] = jnp.full(m_ref.shape, -jnp.inf, F32)
    l_ref[...] = jnp.zeros(l_ref.shape, F32)
    acc_ref[...] = jnp.zeros(acc_ref.shape, F32)

    def update_rows(r0, nrows, k, v, mask):
        n = k.shape[0]
        rows = slice(r0, r0 + nrows)
        s = lax.dot_general(qs_ref[rows, :], k, (((1,), (1,)), ((), ())), preferred_element_type=F32)
        if mask is not None:
            s = jnp.where(mask, s, -jnp.inf)
        tiles = [s[:, t * LANES:(t + 1) * LANES] for t in range(n // LANES)]
        mc = functools.reduce(jnp.maximum, tiles)
        m_old = m_ref[rows, :]
        m_new = jnp.maximum(m_old, jnp.max(mc, axis=1, keepdims=True))
        alpha = jnp.exp2(m_old - m_new)
        p_tiles = [jnp.exp2(t - m_new) for t in tiles]
        l_ref[rows, :] = alpha * l_ref[rows, :] + functools.reduce(jnp.add, p_tiles)
        p = jnp.concatenate(p_tiles, axis=1).astype(BF16)
        acc_ref[rows, :] = alpha * acc_ref[rows, :] + jnp.dot(p, v, preferred_element_type=F32)
        m_ref[rows, :] = m_new

    def body(c, carry):
        off = pl.multiple_of(c * ATT_TK, ATT_TK)
        k = k_ref[pl.ds(off, ATT_TK), :]
        v = v_ref[pl.ds(off, ATT_TK), :]
        for r0 in range(0, 2 * tq, ATT_RG):
            update_rows(r0, ATT_RG, k, v, None)
        return carry

    lax.fori_loop(0, i * (tq // ATT_TK), body, 0)

    off = pl.multiple_of(i * tq, tq)
    for r0 in range(0, 2 * tq, ATT_DRG):
        qo = r0 % tq
        n = qo + ATT_DRG
        row = lax.broadcasted_iota(jnp.int32, (ATT_DRG, n), 0)
        col = lax.broadcasted_iota(jnp.int32, (ATT_DRG, n), 1)
        update_rows(r0, ATT_DRG, k_ref[pl.ds(off, n), :], v_ref[pl.ds(off, n), :], col <= row + qo)

    lam_init = lam_ref[4:5, 0:1]
    lam = (jnp.exp(jnp.sum(lam_ref[0:1, :] * lam_ref[1:2, :], axis=1, keepdims=True))
           - jnp.exp(jnp.sum(lam_ref[2:3, :] * lam_ref[3:4, :], axis=1, keepdims=True))
           + lam_init)
    acc = acc_ref[...]
    inv_l = 1.0 / jnp.sum(l_ref[...], axis=1, keepdims=True)
    o = acc[0:tq] * inv_l[0:tq] - lam * (acc[tq:2 * tq] * inv_l[tq:2 * tq])
    o = _rms(o, g_ref[...], HEAD_NORM_EPS) * (1.0 - lam_init)
    o_ref[...] = o.astype(BF16)


def _attention(proj, lam_pack, head_g, layer):
    t = proj.shape[0]
    qb, kb, vb = COL_Q // LANES, COL_K // LANES, COL_V // LANES
    return pl.pallas_call(
        _attn_kernel,
        grid=(DA_HEADS, t // ATT_TQ),
        in_specs=[
            pl.BlockSpec((None, 8, DA_HEAD_DIM), lambda h, i: (layer, 0, 0)),
            _vec_spec(DA_V_DIM, layer, 2),
            pl.BlockSpec((ATT_TQ, LANES), lambda h, i: (i, qb + h)),
            pl.BlockSpec((t, LANES), lambda h, i: (0, kb + h)),
            pl.BlockSpec((t, LANES), lambda h, i: (0, vb + h)),
        ],
        out_specs=pl.BlockSpec((ATT_TQ, LANES), lambda h, i: (i, h)),
        out_shape=jax.ShapeDtypeStruct((t, DA_WIDTH), BF16),
        scratch_shapes=[
            pltpu.VMEM((2 * ATT_TQ, LANES), BF16),
            pltpu.VMEM((2 * ATT_TQ, LANES), F32),
            pltpu.VMEM((2 * ATT_TQ, LANES), F32),
            pltpu.VMEM((2 * ATT_TQ, LANES), F32),
        ],
        compiler_params=_params(("arbitrary", "arbitrary")),
        name="diff_attn",
    )(lam_pack, head_g, proj, proj, proj)


CV_TM = 512
CV_HALO = 32
CV_ROWS = 64
CV_SUB = 8
CV_SHIFT_ROWS = CV_TM + (CONV_TAPS - 1) // CV_SUB * CV_SUB


def _conv_kernel(a_ref, b_ref, w_ref, bias_ref, g_ref, beta_ref, o_ref, z_ref, zs_ref):
    i = pl.program_id(0)

    @pl.when(i == 0)
    def _():
        z_ref[0:CV_HALO, :] = jnp.zeros((CV_HALO, CV_WIDTH), F32)

    @pl.when(i > 0)
    def _():
        z_ref[0:CV_HALO, :] = z_ref[CV_TM:CV_TM + CV_HALO, :]

    a = a_ref[...].astype(F32)
    b = b_ref[...].astype(F32)
    z_ref[CV_HALO:CV_HALO + CV_TM, :] = a * jax.nn.sigmoid(b)

    base = CV_HALO - (CONV_TAPS - 1)
    for res in range(CV_SUB):
        n = CV_TM + (CONV_TAPS - 1 - res) // CV_SUB * CV_SUB
        zs_ref[res, 0:n, :] = z_ref[base + res:base + res + n, :]
    for r in range(0, CV_TM, CV_ROWS):
        acc = jnp.zeros((CV_ROWS, CV_WIDTH), F32) + bias_ref[...]
        for j in range(CONV_TAPS):
            blk, res = divmod(j, CV_SUB)
            lo = r + blk * CV_SUB
            acc = acc + w_ref[j:j + 1, :] * zs_ref[res, lo:lo + CV_ROWS, :]
        mu = jnp.mean(acc, axis=-1, keepdims=True)
        xc = acc - mu
        var = jnp.mean(xc * xc, axis=-1, keepdims=True)
        y = xc * lax.rsqrt(var + LN_EPS) * g_ref[...] + beta_ref[...]
        o_ref[r:r + CV_ROWS, :] = (y * jax.nn.sigmoid(y)).astype(BF16)


def _conv(proj, dw_w, dw_b, ln_g, ln_b, layer):
    t = proj.shape[0]
    ab, bb = COL_CVA // CV_WIDTH, COL_CVB // CV_WIDTH
    vec = _vec_spec(CV_WIDTH, layer, 1)
    return pl.pallas_call(
        _conv_kernel,
        grid=(t // CV_TM,),
        in_specs=[
            pl.BlockSpec((CV_TM, CV_WIDTH), lambda i: (i, ab)),
            pl.BlockSpec((CV_TM, CV_WIDTH), lambda i: (i, bb)),
            pl.BlockSpec((None, CONV_TAPS, CV_WIDTH), lambda i: (layer, 0, 0)),
            vec, vec, vec,
        ],
        out_specs=pl.BlockSpec((CV_TM, CV_WIDTH), lambda i: (i, 0)),
        out_shape=jax.ShapeDtypeStruct((t, CV_WIDTH), BF16),
        scratch_shapes=[pltpu.VMEM((CV_HALO + CV_TM, CV_WIDTH), F32),
                        pltpu.VMEM((CV_SUB, CV_SHIFT_ROWS, CV_WIDTH), F32)],
        compiler_params=_params(("arbitrary",)),
        name="conformer_conv",
    )(proj, proj, dw_w, dw_b, ln_g, ln_b)


SSM_CB = 128
SSM_RB = SSM_CB * SSM_CHUNK
SSM_SLAB = 2 * SSM_GROUP
SSM_SLABS = LANES // SSM_SLAB


def _pick_slabs(pieces, src_slab, lane_slab):
    out = None
    for k, piece in enumerate(pieces):
        shift = ((k - src_slab) * SSM_SLAB) % LANES
        moved = piece if shift == 0 else pltpu.roll(piece, shift, 1)
        out = moved if out is None else jnp.where(lane_slab == k, moved, out)
    return out


def _ssm_kernel(u_ref, t_ref, wre_ref, wim_ref, vre_ref, vim_ref, ar_ref, ai_ref, d_ref, y_ref,
                uf_ref, up_ref, sre_ref, sim_ref, yp_ref, yf_ref, xr_ref, xi_ref):
    npair, cb = SSM_PAIRS, SSM_CB

    @pl.when(pl.program_id(0) == 0)
    def _():
        xr_ref[...] = jnp.zeros(xr_ref.shape, F32)
        xi_ref[...] = jnp.zeros(xi_ref.shape, F32)

    ntile = SSM_WIDTH // LANES
    for j in range(ntile):
        uf_ref[j] = u_ref[:, j * LANES:(j + 1) * LANES].astype(F32)
    lane_slab = lax.shift_right_logical(lax.broadcasted_iota(jnp.int32, (cb, LANES), 1), 5)

    for q in range(SSM_CHUNK // SSM_SLABS):
        for tile in range(ntile):
            src = uf_ref.at[tile]
            pieces = [src[pl.ds(SSM_SLABS * q + k, cb, stride=SSM_CHUNK), :] for k in range(SSM_SLABS)]
            for slab in range(SSM_SLABS):
                up_ref[SSM_SLABS * tile + slab, :, q * LANES:(q + 1) * LANES] = _pick_slabs(
                    pieces, slab, lane_slab).astype(BF16)

    for p in range(npair):
        u = up_ref[p]
        sre_ref[pl.ds(p, cb, stride=npair), :] = jnp.dot(u, wre_ref[p], preferred_element_type=F32)
        sim_ref[pl.ds(p, cb, stride=npair), :] = jnp.dot(u, wim_ref[p], preferred_element_type=F32)

    ar = ar_ref[...]
    ai = ai_ref[...]

    def body(c, carry):
        xr, xi = carry
        off = pl.multiple_of(c * npair, npair)
        sr = sre_ref[pl.ds(off, npair), :]
        si = sim_ref[pl.ds(off, npair), :]
        sre_ref[pl.ds(off, npair), :] = xr
        sim_ref[pl.ds(off, npair), :] = xi
        return ar * xr - ai * xi + sr, ar * xi + ai * xr + si

    xr, xi = lax.fori_loop(0, cb, body, (xr_ref[...], xi_ref[...]), unroll=8)
    xr_ref[...] = xr
    xi_ref[...] = xi

    for p in range(npair):
        u = up_ref[p]
        sr = sre_ref[pl.ds(p, cb, stride=npair), :].astype(BF16)
        si = sim_ref[pl.ds(p, cb, stride=npair), :].astype(BF16)
        nt = (((1,), (1,)), ((), ()))
        y = jnp.dot(u, t_ref[p], preferred_element_type=F32)
        y = y + lax.dot_general(sr, vre_ref[p], nt, preferred_element_type=F32)
        yp_ref[p] = y + lax.dot_general(si, vim_ref[p], nt, preferred_element_type=F32)

    for tile in range(ntile):
        for q in range(SSM_CHUNK // SSM_SLABS):
            pieces = [yp_ref[SSM_SLABS * tile + k, :, q * LANES:(q + 1) * LANES] for k in range(SSM_SLABS)]
            dst = yf_ref.at[tile]
            for slab in range(SSM_SLABS):
                dst[pl.ds(SSM_SLABS * q + slab, cb, stride=SSM_CHUNK), :] = _pick_slabs(pieces, slab, lane_slab)

    for j in range(ntile):
        cols = slice(j * LANES, (j + 1) * LANES)
        y_ref[:, cols] = (yf_ref[j] + uf_ref[j] * d_ref[:, cols]).astype(BF16)


def _ssm(proj, mats):
    t = proj.shape[0]
    vm = pl.BlockSpec(memory_space=pltpu.VMEM)
    return pl.pallas_call(
        _ssm_kernel,
        grid=(t // SSM_RB,),
        in_specs=[pl.BlockSpec((SSM_RB, SSM_WIDTH), lambda i: (i, COL_SSM // SSM_WIDTH))] + [vm] * 8,
        out_specs=pl.BlockSpec((SSM_RB, SSM_WIDTH), lambda i: (i, 0)),
        out_shape=jax.ShapeDtypeStruct((t, SSM_WIDTH), BF16),
        scratch_shapes=[
            pltpu.VMEM((SSM_WIDTH // LANES, SSM_RB, LANES), F32),
            pltpu.VMEM((SSM_PAIRS, SSM_CB, SSM_PAIR_W), BF16),
            pltpu.VMEM((SSM_CB * SSM_PAIRS, LANES), F32),
            pltpu.VMEM((SSM_CB * SSM_PAIRS, LANES), F32),
            pltpu.VMEM((SSM_PAIRS, SSM_CB, SSM_PAIR_W), F32),
            pltpu.VMEM((SSM_WIDTH // LANES, SSM_RB, LANES), F32),
            pltpu.VMEM((SSM_PAIRS, LANES), F32),
            pltpu.VMEM((SSM_PAIRS, LANES), F32),
        ],
        compiler_params=_params(("arbitrary",)),
        name="s5_scan",
    )(proj, *mats)


def _ssm_gen_kernel(bre_ref, bim_ref, cre_ref, cim_ref, pr_ref, pi_ref,
                    t_ref, wre_ref, wim_ref, vre_ref, vim_ref, car_ref, cai_ref):
    bre, bim, cre, cim = bre_ref[...], bim_ref[...], cre_ref[...], cim_ref[...]
    for tau in range(SSM_CHUNK):
        rows = slice(tau * SSM_SLAB, (tau + 1) * SSM_SLAB)
        pr = pr_ref[tau:tau + 1, :]
        pi = pi_ref[tau:tau + 1, :]
        car_ref[rows, :] = cre * pr - cim * pi
        cai_ref[rows, :] = cre * pi + cim * pr
    nt = (((1,), (1,)), ((), ()))
    hp = lax.Precision.HIGHEST
    k = (lax.dot_general(bre, car_ref[...], nt, precision=hp, preferred_element_type=F32)
         - lax.dot_general(bim, cai_ref[...], nt, precision=hp, preferred_element_type=F32))
    lane = lax.broadcasted_iota(jnp.int32, k.shape, 1)
    for s in range(SSM_CHUNK):
        rows = slice(s * SSM_SLAB, (s + 1) * SSM_SLAB)
        moved = k if s == 0 else pltpu.roll(k, s * SSM_SLAB, 1)
        t_ref[rows, :] = jnp.where(lane >= s * SSM_SLAB, moved, 0.0).astype(BF16)
        pr = pr_ref[SSM_CHUNK - 1 - s:SSM_CHUNK - s, :]
        pi = pi_ref[SSM_CHUNK - 1 - s:SSM_CHUNK - s, :]
        wre_ref[rows, :] = (bre * pr - bim * pi).astype(BF16)
        wim_ref[rows, :] = (bre * pi + bim * pr).astype(BF16)
        qr = pr_ref[s + 1:s + 2, :]
        qi = pi_ref[s + 1:s + 2, :]
        vre_ref[rows, :] = (cre * qr - cim * qi).astype(BF16)
        vim_ref[rows, :] = (-(cre * qi + cim * qr)).astype(BF16)


def _ssm_matrices(lam_re, lam_im, log_dt, b_re, b_im, c_re, c_im, d_skip):
    nl = lam_re.shape[0]
    lr = lam_re.astype(F32)
    li = lam_im.astype(F32)
    dt = jnp.exp(log_dt.astype(F32))[..., None]
    mag = jnp.exp(lr * dt)
    ab_re = mag * jnp.cos(li * dt)
    ab_im = mag * jnp.sin(li * dt)
    den = lr * lr + li * li
    f_re = ((ab_re - 1.0) * lr + ab_im * li) / den
    f_im = (ab_im * lr - (ab_re - 1.0) * li) / den
    br = b_re.astype(F32)
    bi = b_im.astype(F32)
    bb_re = f_re[..., None] * br - f_im[..., None] * bi
    bb_im = f_re[..., None] * bi + f_im[..., None] * br
    tau = jnp.arange(SSM_CHUNK + 1, dtype=F32)[None, None, :, None]
    pmag = jnp.exp(tau * (lr * dt)[:, :, None, :])
    pw_re = pmag * jnp.cos(tau * (li * dt)[:, :, None, :])
    pw_im = pmag * jnp.sin(tau * (li * dt)[:, :, None, :])
    cr = c_re.astype(F32)
    ci = c_im.astype(F32)
    eye2 = jnp.eye(2, dtype=F32)

    def paired(a):
        return a.reshape((nl, SSM_PAIRS, 2) + a.shape[2:])

    bbd_re = jnp.einsum('lpenk,ef->lpekfn', paired(bb_re), eye2).reshape(nl, SSM_PAIRS, SSM_SLAB, LANES)
    bbd_im = jnp.einsum('lpenk,ef->lpekfn', paired(bb_im), eye2).reshape(nl, SSM_PAIRS, SSM_SLAB, LANES)
    cbd_re = jnp.einsum('lpehn,ef->lpehfn', paired(cr), eye2).reshape(nl, SSM_PAIRS, SSM_SLAB, LANES)
    cbd_im = jnp.einsum('lpehn,ef->lpehfn', paired(ci), eye2).reshape(nl, SSM_PAIRS, SSM_SLAB, LANES)
    pwp_re = paired(pw_re).transpose(0, 1, 3, 2, 4).reshape(nl, SSM_PAIRS, SSM_CHUNK + 1, LANES)
    pwp_im = paired(pw_im).transpose(0, 1, 3, 2, 4).reshape(nl, SSM_PAIRS, SSM_CHUNK + 1, LANES)

    def spec(r, c):
        return pl.BlockSpec((None, None, r, c), lambda l, p: (l, p, 0, 0))

    wide = jax.ShapeDtypeStruct((nl, SSM_PAIRS, SSM_PAIR_W, SSM_PAIR_W), BF16)
    tall = jax.ShapeDtypeStruct((nl, SSM_PAIRS, SSM_PAIR_W, LANES), BF16)
    tm, w_re, w_im, vt_re, vt_im = pl.pallas_call(
        _ssm_gen_kernel,
        grid=(nl, SSM_PAIRS),
        in_specs=[spec(SSM_SLAB, LANES)] * 4 + [spec(SSM_CHUNK + 1, LANES)] * 2,
        out_specs=[spec(SSM_PAIR_W, SSM_PAIR_W)] + [spec(SSM_PAIR_W, LANES)] * 4,
        out_shape=[wide, tall, tall, tall, tall],
        scratch_shapes=[pltpu.VMEM((SSM_PAIR_W, LANES), F32), pltpu.VMEM((SSM_PAIR_W, LANES), F32)],
        compiler_params=_params(("arbitrary", "arbitrary")),
        name="s5_matrices",
    )(bbd_re, bbd_im, cbd_re, cbd_im, pwp_re, pwp_im)
    a_re = pwp_re[:, :, SSM_CHUNK]
    a_im = pwp_im[:, :, SSM_CHUNK]
    d_t = d_skip.astype(F32).reshape(nl, 1, SSM_WIDTH)
    return tm, w_re, w_im, vt_re, vt_im, a_re, a_im, d_t


MG_TM = 512
MG_TN = 1024
MG_GW = MG_TN // 2


def _merge_mix_kernel(oa_ref, zb_ref, yc_ref, ga0_ref, ga1_ref, gb0_ref, gb1_ref, gc0_ref, gc1_ref,
                      wda_ref, wcv_ref, wglu_ref, bglu_ref, wso_ref, wmix_ref, x_ref, o_ref, sc_ref):
    n = pl.program_id(1)

    def gate(lo_ref, hi_ref):
        return jax.nn.sigmoid(jnp.concatenate([lo_ref[...], hi_ref[...]], axis=1).astype(F32))

    @pl.when(n == 0)
    def _():
        glu = jnp.dot(yc_ref[...], wglu_ref[...], preferred_element_type=F32) + bglu_ref[...]
        sc_ref[...] = (glu[:, :SSM_WIDTH] * jax.nn.sigmoid(glu[:, SSM_WIDTH:])).astype(BF16)
        o_ref[...] = x_ref[...]

    y_a = jnp.dot(oa_ref[...], wda_ref[...], preferred_element_type=F32)
    y_b = jnp.dot(zb_ref[...], wcv_ref[...], preferred_element_type=F32)
    y_c = jnp.dot(sc_ref[...], wso_ref[...], preferred_element_type=F32)
    merged = gate(ga0_ref, ga1_ref) * y_a + gate(gb0_ref, gb1_ref) * y_b + gate(gc0_ref, gc1_ref) * y_c
    o_ref[...] += jnp.dot(merged.astype(BF16), wmix_ref[...], preferred_element_type=F32)


def _merge_mix(o_a, z_b, y_c, proj, w_da, w_cv, w_glu, b_glu, w_so, w_mix, x, layer):
    t = o_a.shape[0]

    def wcol(k):
        return pl.BlockSpec((None, k, MG_TN), lambda i, n: (layer, 0, n))

    def gate_halves(col0):
        first = col0 // MG_GW
        return [pl.BlockSpec((MG_TM, MG_GW), lambda i, n, h=h: (i, first + 2 * n + h)) for h in range(2)]

    return pl.pallas_call(
        _merge_mix_kernel,
        grid=(t // MG_TM, D_MODEL // MG_TN),
        in_specs=[
            pl.BlockSpec((MG_TM, DA_WIDTH), lambda i, n: (i, 0)),
            pl.BlockSpec((MG_TM, CV_WIDTH), lambda i, n: (i, 0)),
            pl.BlockSpec((MG_TM, SSM_WIDTH), lambda i, n: (i, 0)),
            *gate_halves(COL_GA), *gate_halves(COL_GB), *gate_halves(COL_GC),
            wcol(DA_WIDTH), wcol(CV_WIDTH),
            pl.BlockSpec((None, SSM_WIDTH, 2 * SSM_WIDTH), lambda i, n: (layer, 0, 0)),
            _vec_spec(2 * SSM_WIDTH, layer, 2),
            wcol(SSM_WIDTH),
            pl.BlockSpec((None, MG_TN, D_MODEL), lambda i, n: (layer, n, 0)),
            pl.BlockSpec((MG_TM, D_MODEL), lambda i, n: (i, 0)),
        ],
        out_specs=pl.BlockSpec((MG_TM, D_MODEL), lambda i, n: (i, 0)),
        out_shape=jax.ShapeDtypeStruct((t, D_MODEL), F32),
        scratch_shapes=[pltpu.VMEM((MG_TM, SSM_WIDTH), BF16)],
        compiler_params=_params(("arbitrary", "arbitrary")),
        name="merge_mix",
    )(o_a, z_b, y_c, *([proj] * 6), w_da, w_cv, w_glu, b_glu, w_so, w_mix, x)


def _norm_matmul_kernel(x_ref, g_ref, w_ref, o_ref):
    h = _rms(x_ref[...], g_ref[...], RMS_EPS).astype(BF16)
    o_ref[...] = jnp.dot(h, w_ref[...], preferred_element_type=F32).astype(o_ref.dtype)


def _mem_kv(mem, g_all, w_all, layer):
    m = mem.shape[0]
    n = 2 * XA_WIDTH
    return pl.pallas_call(
        _norm_matmul_kernel,
        grid=(1,),
        in_specs=[
            pl.BlockSpec((m, D_MODEL), lambda i: (0, 0)),
            _vec_spec(D_MODEL, layer, 1),
            pl.BlockSpec((None, D_MODEL, n), lambda i: (layer, 0, 0)),
        ],
        out_specs=pl.BlockSpec((m, n), lambda i: (0, 0)),
        out_shape=jax.ShapeDtypeStruct((m, n), BF16),
        compiler_params=_params(("arbitrary",)),
        name="mem_kv",
    )(mem, g_all, w_all)


XA_TM = 512


def _xattn_kernel(x_ref, g_ref, wq_ref, kv_ref, wo_ref, o_ref):
    x = x_ref[...]
    h = _rms(x, g_ref[...], RMS_EPS).astype(BF16)
    q = jnp.dot(h, wq_ref[...], preferred_element_type=F32).astype(BF16)
    heads = []
    for hd in range(XA_HEADS):
        lo = hd * XA_HEAD_DIM
        k = kv_ref[:, lo:lo + XA_HEAD_DIM]
        v = kv_ref[:, XA_WIDTH + lo:XA_WIDTH + lo + XA_HEAD_DIM]
        s = lax.dot_general(q[:, lo:lo + XA_HEAD_DIM], k, (((1,), (1,)), ((), ())),
                            preferred_element_type=F32) * (XA_HEAD_DIM ** -0.5)
        m = jnp.max(s, axis=-1, keepdims=True)
        e = jnp.exp(s - m)
        p = e / jnp.sum(e, axis=-1, keepdims=True)
        heads.append(jnp.dot(p.astype(BF16), v, preferred_element_type=F32).astype(BF16))
    o = jnp.concatenate(heads, axis=-1)
    o_ref[...] = x + jnp.dot(o, wo_ref[...], preferred_element_type=F32)


def _xattn(x, g_all, wq_all, kv, wo_all, layer):
    t = x.shape[0]
    return pl.pallas_call(
        _xattn_kernel,
        grid=(t // XA_TM,),
        in_specs=[
            pl.BlockSpec((XA_TM, D_MODEL), lambda i: (i, 0)),
            _vec_spec(D_MODEL, layer, 1),
            pl.BlockSpec((None, D_MODEL, XA_WIDTH), lambda i: (layer, 0, 0)),
            pl.BlockSpec((MEM_LEN, 2 * XA_WIDTH), lambda i: (0, 0)),
            pl.BlockSpec((None, XA_WIDTH, D_MODEL), lambda i: (layer, 0, 0)),
        ],
        out_specs=pl.BlockSpec((XA_TM, D_MODEL), lambda i: (i, 0)),
        out_shape=jax.ShapeDtypeStruct((t, D_MODEL), F32),
        compiler_params=_params(("arbitrary",)),
        name="mem_xattn",
    )(x, g_all, wq_all, kv, wo_all)


RT_TM = 512
_E_LANE0 = MOE_GROUPS


def _router_kernel(x_ref, g_ref, w_ref, b_ref, meta_ref, cnt_ref, run_ref):
    i = pl.program_id(0)

    @pl.when(i == 0)
    def _():
        run_ref[...] = jnp.zeros(run_ref.shape, F32)

    h = _rms(x_ref[...], g_ref[...], RMS_EPS).astype(BF16)
    logits = jnp.dot(h, w_ref[...], preferred_element_type=F32) + b_ref[...]
    lane = lax.broadcasted_iota(jnp.int32, logits.shape, 1).astype(F32)
    neg = jnp.float32(-jnp.inf)
    big = jnp.float32(LANES)

    def first_argmax(vals):
        top = jnp.max(vals, axis=-1, keepdims=True)
        idx = jnp.min(jnp.where(vals == top, lane, big), axis=-1, keepdims=True)
        return top, idx

    gl = jnp.where(lane < MOE_GROUPS, logits, neg)
    g_top, g_idx = first_argmax(gl)
    g_w = 1.0 / jnp.sum(jnp.exp(gl - g_top), axis=-1, keepdims=True)
    e_lane = lane - _E_LANE0
    in_group = jnp.logical_and(e_lane >= g_idx * MOE_PER_GROUP, e_lane < (g_idx + 1) * MOE_PER_GROUP)
    el = jnp.where(in_group, logits, neg)
    v1, i1 = first_argmax(el)
    el2 = jnp.where(lane == i1, neg, el)
    v2, i2 = first_argmax(el2)
    e2 = jnp.exp(v2 - v1)
    w1 = 1.0 / (1.0 + e2)
    w2 = e2 / (1.0 + e2)
    oh1 = (lane == i1).astype(F32)
    oh2 = (lane == i2).astype(F32)
    both = oh1 + oh2
    row = lax.broadcasted_iota(jnp.int32, (RT_TM, RT_TM), 0)
    col = lax.broadcasted_iota(jnp.int32, (RT_TM, RT_TM), 1)
    earlier = jnp.where(col < row, 1.0, 0.0).astype(BF16)
    before = jnp.dot(earlier, both.astype(BF16), preferred_element_type=F32) + run_ref[...]
    rank1 = jnp.sum(oh1 * before, axis=-1, keepdims=True)
    rank2 = jnp.sum(oh2 * before, axis=-1, keepdims=True)
    run_ref[...] += jnp.sum(both, axis=0, keepdims=True)
    cnt_ref[...] = run_ref[...]
    meta = jnp.where(lane == 0.0, i1 - _E_LANE0, 0.0)
    for k, val in enumerate((i2 - _E_LANE0, rank1, rank2, w1 * g_w, w2 * g_w), start=1):
        meta = jnp.where(lane == float(k), val, meta)
    meta_ref[...] = meta


def _router(x, g_all, w_r, b_r, layer):
    t = x.shape[0]
    return pl.pallas_call(
        _router_kernel,
        grid=(t // RT_TM,),
        in_specs=[
            pl.BlockSpec((RT_TM, D_MODEL), lambda i: (i, 0)),
            _vec_spec(D_MODEL, layer, 1),
            pl.BlockSpec((None, D_MODEL, LANES), lambda i: (layer, 0, 0)),
            _vec_spec(LANES, layer, 1),
        ],
        out_specs=[
            pl.BlockSpec((RT_TM, LANES), lambda i: (i, 0)),
            pl.BlockSpec((1, LANES), lambda i: (0, 0)),
        ],
        out_shape=[jax.ShapeDtypeStruct((t, LANES), F32), jax.ShapeDtypeStruct((1, LANES), F32)],
        scratch_shapes=[pltpu.VMEM((1, LANES), F32)],
        compiler_params=_params(("arbitrary",)),
        name="moe_router",
    )(x, g_all, w_r, b_r)


EX_TM = 256
EX_ROWS = 2 * SEQ
EX_TILES = EX_ROWS // EX_TM
EX_STEPS = EX_TILES + MOE_EXPERTS - 1


def _row_copy(src, src_row, dst, dst_row, sem):
    return pltpu.make_async_copy(src.at[pl.ds(src_row, 1), :], dst.at[pl.ds(dst_row, 1), :], sem)


DP_TM = 256


def _dispatch_kernel(p1_ref, p2_ref, x_ref, xs_hbm, sem):
    base = pl.program_id(0) * DP_TM

    def issue(j, c):
        _row_copy(x_ref, j, xs_hbm, p1_ref[base + j], sem).start(priority=0)
        _row_copy(x_ref, j, xs_hbm, p2_ref[base + j], sem).start(priority=1)
        return c

    lax.fori_loop(0, DP_TM, issue, 0, unroll=True)

    def wait(j, c):
        _row_copy(x_ref, j, xs_hbm, 0, sem).wait()
        _row_copy(x_ref, j, xs_hbm, 0, sem).wait()
        return c

    lax.fori_loop(0, DP_TM, wait, 0, unroll=8)


def _dispatch(pos1, pos2, x):
    t = x.shape[0]
    grid_spec = pltpu.PrefetchScalarGridSpec(
        num_scalar_prefetch=2,
        grid=(t // DP_TM,),
        in_specs=[pl.BlockSpec((DP_TM, D_MODEL), lambda i, p1, p2: (i, 0))],
        out_specs=pl.BlockSpec(memory_space=pl.ANY),
        scratch_shapes=[pltpu.SemaphoreType.DMA(())],
    )
    return pl.pallas_call(
        _dispatch_kernel,
        grid_spec=grid_spec,
        out_shape=jax.ShapeDtypeStruct((EX_ROWS, D_MODEL), F32),
        compiler_params=_params(("arbitrary",)),
        name="moe_dispatch",
    )(pos1, pos2, x)


def _experts_kernel(se_ref, st_ref, lo_ref, hi_ref, first_ref, wnew_ref, x_ref, g_ref, wg_ref, wu_ref,
                    wd_ref, o_ref, wgb_ref, wub_ref, wdb_ref):
    s = pl.program_id(0)
    lo = lo_ref[s]
    hi = hi_ref[s]

    @pl.when(wnew_ref[s] == 1)
    def _():
        wgb_ref[...] = wg_ref[...].astype(BF16)
        wub_ref[...] = wu_ref[...].astype(BF16)
        wdb_ref[...] = wd_ref[...].astype(BF16)

    @pl.when(hi > lo)
    def _():
        h = _rms(x_ref[...], g_ref[...], RMS_EPS).astype(BF16)
        gate = jnp.dot(h, wgb_ref[...], preferred_element_type=F32)
        up = jnp.dot(h, wub_ref[...], preferred_element_type=F32)
        act = (gate * jax.nn.sigmoid(gate) * up).astype(BF16)
        res = jnp.dot(act, wdb_ref[...], preferred_element_type=F32)
        row = lax.broadcasted_iota(jnp.int32, (EX_TM, 1), 0)
        mine = jnp.logical_and(row >= lo, row < hi)

        @pl.when(first_ref[s] == 1)
        def _():
            o_ref[...] = jnp.where(mine, res, 0.0)

        @pl.when(first_ref[s] == 0)
        def _():
            o_ref[...] = jnp.where(mine, res, o_ref[...])


def _experts(plan, xs_sorted, g_all, wg_all, wu_all, wd_all, layer):
    def wspec(k, n):
        return pl.BlockSpec((None, None, k, n), lambda s, se, st, lo, hi, fi, wn: (layer, se[s], 0, 0))

    grid_spec = pltpu.PrefetchScalarGridSpec(
        num_scalar_prefetch=6,
        grid=(EX_STEPS,),
        in_specs=[
            pl.BlockSpec((EX_TM, D_MODEL), lambda s, se, st, lo, hi, fi, wn: (st[s], 0)),
            pl.BlockSpec((None, 1, D_MODEL), lambda s, se, st, lo, hi, fi, wn: (layer, 0, 0)),
            wspec(D_MODEL, MOE_FF), wspec(D_MODEL, MOE_FF), wspec(MOE_FF, D_MODEL),
        ],
        out_specs=pl.BlockSpec((EX_TM, D_MODEL), lambda s, se, st, lo, hi, fi, wn: (st[s], 0)),
        scratch_shapes=[pltpu.VMEM((D_MODEL, MOE_FF), BF16), pltpu.VMEM((D_MODEL, MOE_FF), BF16),
                        pltpu.VMEM((MOE_FF, D_MODEL), BF16)],
    )
    return pl.pallas_call(
        _experts_kernel,
        grid_spec=grid_spec,
        out_shape=jax.ShapeDtypeStruct((EX_ROWS, D_MODEL), F32),
        compiler_params=_params(("arbitrary",)),
        name="moe_experts",
    )(*plan, xs_sorted, g_all, wg_all, wu_all, wd_all)


CB_TM = 256


def _combine_kernel(p1_ref, p2_ref, y_hbm, x_ref, meta_ref, g_ref, o_ref, n_ref, y1buf, y2buf, sem):
    i = pl.program_id(0)
    slot = lax.rem(i, 2)

    def fetch(tile, slot_):
        base = tile * CB_TM

        def issue(j, c):
            _row_copy(y_hbm, p1_ref[base + j], y1buf.at[slot_], j, sem.at[slot_]).start(priority=0)
            _row_copy(y_hbm, p2_ref[base + j], y2buf.at[slot_], j, sem.at[slot_]).start(priority=1)
            return c

        lax.fori_loop(0, CB_TM, issue, 0, unroll=True)

    @pl.when(i == 0)
    def _():
        fetch(0, 0)

    @pl.when(i + 1 < pl.num_programs(0))
    def _():
        fetch(i + 1, 1 - slot)

    def wait(j, c):
        _row_copy(y_hbm, 0, y1buf.at[slot], j, sem.at[slot]).wait()
        _row_copy(y_hbm, 0, y2buf.at[slot], j, sem.at[slot]).wait()
        return c

    lax.fori_loop(0, CB_TM, wait, 0, unroll=8)
    meta = meta_ref[...]
    x_new = x_ref[...] + meta[:, 4:5] * y1buf[slot] + meta[:, 5:6] * y2buf[slot]
    o_ref[...] = x_new
    n_ref[...] = _rms(x_new, g_ref[...], RMS_EPS).astype(n_ref.dtype)


def _combine(pos1, pos2, ys, x, meta, g_next, next_dtype):
    t = x.shape[0]
    tile = pl.BlockSpec((CB_TM, D_MODEL), lambda i, p1, p2: (i, 0))
    grid_spec = pltpu.PrefetchScalarGridSpec(
        num_scalar_prefetch=2,
        grid=(t // CB_TM,),
        in_specs=[
            pl.BlockSpec(memory_space=pl.ANY),
            tile,
            pl.BlockSpec((CB_TM, LANES), lambda i, p1, p2: (i, 0)),
            pl.BlockSpec((1, D_MODEL), lambda i, p1, p2: (0, 0)),
        ],
        out_specs=[tile, tile],
        scratch_shapes=[pltpu.VMEM((2, CB_TM, D_MODEL), F32), pltpu.VMEM((2, CB_TM, D_MODEL), F32),
                        pltpu.SemaphoreType.DMA((2,))],
    )
    return pl.pallas_call(
        _combine_kernel,
        grid_spec=grid_spec,
        out_shape=[jax.ShapeDtypeStruct((t, D_MODEL), F32), jax.ShapeDtypeStruct((t, D_MODEL), next_dtype)],
        compiler_params=_params(("arbitrary",)),
        name="moe_combine",
    )(pos1, pos2, ys, x, meta, g_next)


def _dispatch_plan(meta, cnt):
    t = meta.shape[0]
    e1 = meta[:, 0].astype(jnp.int32)
    e2 = meta[:, 1].astype(jnp.int32)
    r1 = meta[:, 2].astype(jnp.int32)
    r2 = meta[:, 3].astype(jnp.int32)
    counts = cnt[0, _E_LANE0:_E_LANE0 + MOE_EXPERTS].astype(jnp.int32)
    ends = jnp.cumsum(counts)
    starts = ends - counts
    pos1 = starts[e1] + r1
    pos2 = starts[e2] + r2
    t_lo = starts // EX_TM
    t_hi = (ends + EX_TM - 1) // EX_TM
    nsteps = jnp.where(counts > 0, t_hi - t_lo, 0)
    step_end = jnp.cumsum(nsteps)
    step_start = step_end - nsteps
    s = jnp.arange(EX_STEPS, dtype=jnp.int32)
    se = jnp.minimum(jnp.sum((s[:, None] >= step_end[None, :]).astype(jnp.int32), axis=1), MOE_EXPERTS - 1)
    valid = s < step_end[-1]
    st = jnp.where(valid, t_lo[se] + s - step_start[se], EX_TILES - 1)
    lo = jnp.where(valid, jnp.clip(starts[se] - st * EX_TM, 0, EX_TM), 0)
    hi = jnp.where(valid, jnp.clip(ends[se] - st * EX_TM, 0, EX_TM), 0)
    first = jnp.concatenate([jnp.ones((1,), jnp.int32), (st[1:] != st[:-1]).astype(jnp.int32)])
    wnew = jnp.concatenate([jnp.ones((1,), jnp.int32), (se[1:] != se[:-1]).astype(jnp.int32)])
    plan = tuple(a.astype(jnp.int32) for a in (se, st, lo, hi, first, wnew))
    return pos1.astype(jnp.int32), pos2.astype(jnp.int32), plan


FN_TM = 512


def _norm_kernel(x_ref, g_ref, o_ref):
    o_ref[...] = _rms(x_ref[...], g_ref[...], RMS_EPS).astype(o_ref.dtype)


def _first_norm(x, g_all):
    t = x.shape[0]
    return pl.pallas_call(
        _norm_kernel,
        grid=(t // FN_TM,),
        in_specs=[pl.BlockSpec((FN_TM, D_MODEL), lambda i: (i, 0)), _vec_spec(D_MODEL, 0, 1)],
        out_specs=pl.BlockSpec((FN_TM, D_MODEL), lambda i: (i, 0)),
        out_shape=jax.ShapeDtypeStruct((t, D_MODEL), BF16),
        compiler_params=_params(("arbitrary",)),
        name="first_norm",
    )(x, g_all)


def kernel(x, mem, positions, norm_mix, w_in, da_lam_q1, da_lam_k1, da_lam_q2, da_lam_k2, da_head_norm, w_da_out, cv_dw_w, cv_dw_b, cv_ln_g, cv_ln_b, w_cv_out, ssm_lam_re, ssm_lam_im, ssm_log_dt, ssm_b_re, ssm_b_im, ssm_c_re, ssm_c_im, ssm_d, w_ssm_glu, b_ssm_glu, w_ssm_out, w_mix_out, norm_xa, norm_mem, w_xa_q, w_xa_kv, w_xa_out, norm_ffn, w_router_group, b_router_group, w_router_expert, b_router_expert, w_exp_gate, w_exp_up, w_exp_down, norm_final):
    bsz, seq, _ = x.shape
    assert bsz == 1 and seq == SEQ
    nl = w_in.shape[0]
    xs = x.reshape(seq, D_MODEL).astype(F32)
    mem2 = mem.reshape(MEM_LEN, D_MODEL).astype(F32)

    inv_freq = ROPE_THETA ** (-jnp.arange(0, DA_HEAD_DIM, 2, dtype=F32) / DA_HEAD_DIM)
    ang = positions.reshape(seq).astype(F32)[:, None] * inv_freq
    cos = jnp.cos(ang)
    sin = jnp.sin(ang)
    cos_t = jnp.concatenate([cos, cos, cos, cos], axis=-1)
    sin_t = jnp.concatenate([-sin, sin, -sin, sin], axis=-1)

    w_da_b = w_da_out.astype(BF16)
    w_cv_b = w_cv_out.astype(BF16)
    w_glu_b = w_ssm_glu.astype(BF16)
    w_so_b = w_ssm_out.astype(BF16)
    w_mix_b = w_mix_out.astype(BF16)
    w_xq_b = w_xa_q.astype(BF16)
    w_xkv_b = w_xa_kv.astype(BF16)
    w_xo_b = w_xa_out.astype(BF16)
    pad = LANES - MOE_GROUPS - MOE_EXPERTS
    w_r = jnp.concatenate([w_router_group, w_router_expert,
                           jnp.zeros((nl, D_MODEL, pad), F32)], axis=-1).astype(BF16)
    b_r = jnp.concatenate([b_router_group, b_router_expert, jnp.zeros((nl, pad), F32)], axis=-1).astype(F32)

    def vec3(a):
        return a.astype(F32).reshape(nl, 1, a.shape[-1])

    norm_mix, da_head_norm, cv_dw_b, cv_ln_g, cv_ln_b, b_ssm_glu, norm_xa, norm_mem, norm_ffn, b_r = map(
        vec3, (norm_mix, da_head_norm, cv_dw_b, cv_ln_g, cv_ln_b, b_ssm_glu, norm_xa, norm_mem, norm_ffn, b_r))
    cv_dw_w = cv_dw_w.astype(F32)

    lam_inits = jnp.asarray([0.8 - 0.6 * math.exp(-0.3 * l) for l in range(nl)], F32)
    lam_pack = jnp.stack([da_lam_q1, da_lam_k1, da_lam_q2, da_lam_k2], axis=1).astype(F32)
    lam_pack = jnp.concatenate(
        [lam_pack, jnp.broadcast_to(lam_inits[:, None, None], (nl, 4, DA_HEAD_DIM))], axis=1)

    ssm_mats = _ssm_matrices(ssm_lam_re, ssm_lam_im, ssm_log_dt, ssm_b_re, ssm_b_im,
                             ssm_c_re, ssm_c_im, ssm_d)

    h = _first_norm(xs, norm_mix)
    for l in range(nl):
        proj = _inproj(h, w_in, l, cos_t, sin_t)
        o_a = _attention(proj, lam_pack, da_head_norm, l)
        z_b = _conv(proj, cv_dw_w, cv_dw_b, cv_ln_g, cv_ln_b, l)
        y_c = _ssm(proj, [m[l] for m in ssm_mats])
        xs = _merge_mix(o_a, z_b, y_c, proj, w_da_b, w_cv_b, w_glu_b, b_ssm_glu, w_so_b, w_mix_b, xs, l)
        kv = _mem_kv(mem2, norm_mem, w_xkv_b, l)
        xs = _xattn(xs, norm_xa, w_xq_b, kv, w_xo_b, l)
        meta, cnt = _router(xs, norm_ffn, w_r, b_r, l)
        pos1, pos2, plan = _dispatch_plan(meta, cnt)
        xs_sorted = _dispatch(pos1, pos2, xs)
        ys = _experts(plan, xs_sorted, norm_ffn, w_exp_gate, w_exp_up, w_exp_down, l)
        if l + 1 < nl:
            xs, h = _combine(pos1, pos2, ys, xs, meta, norm_mix[l + 1], BF16)
        else:
            xs, out = _combine(pos1, pos2, ys, xs, meta, norm_final.astype(F32).reshape(1, D_MODEL), F32)
    return out.reshape(bsz, seq, D_MODEL)
```

```python
import functools
import math

import jax
import jax.numpy as jnp
from jax import lax
from jax.experimental import pallas as pl
from jax.experimental.pallas import tpu as pltpu

F32 = jnp.float32
BF16 = jnp.bfloat16

D_MODEL = 2048
SEQ = 8192
DEPTH = 4
MEM_LEN = 256
DA_HEADS = 8
DA_HEAD_DIM = 64
DA_V_DIM = 128
DA_WIDTH = 1024
ROPE_THETA = 10000.0
CV_WIDTH = 512
CONV_TAPS = 31
SSM_WIDTH = 512
SSM_GROUP = 16
SSM_GROUPS = 32
SSM_STATE = 64
XA_HEADS = 4
XA_HEAD_DIM = 128
XA_WIDTH = 512
MOE_GROUPS = 4
MOE_PER_GROUP = 4
MOE_EXPERTS = 16
MOE_FF = 512
RMS_EPS = 1e-6
HEAD_NORM_EPS = 1e-5
LN_EPS = 1e-5

COL_Q = 0
COL_K = COL_Q + 1024
COL_V = COL_K + 1024
COL_CVA = COL_V + 1024
COL_CVB = COL_CVA + CV_WIDTH
COL_SSM = COL_CVB + CV_WIDTH
COL_GA = COL_SSM + SSM_WIDTH
COL_GB = COL_GA + D_MODEL
COL_GC = COL_GB + D_MODEL
IN_TOTAL = COL_GC + D_MODEL

LANES = 128
VMEM_LIMIT = 56 * 1024 * 1024

SSM_CHUNK = 16
SSM_NCHUNK = SEQ // SSM_CHUNK
SSM_PAIRS = SSM_GROUPS // 2
SSM_PAIR_W = 2 * SSM_CHUNK * SSM_GROUP


def _params(sem, vmem=VMEM_LIMIT):
    return pltpu.CompilerParams(dimension_semantics=sem, vmem_limit_bytes=vmem)


def _vec_spec(width, layer, ngrid):
    if ngrid == 1:
        return pl.BlockSpec((None, 1, width), lambda i: (layer, 0, 0))
    return pl.BlockSpec((None, 1, width), lambda i, j: (layer, 0, 0))


def _rms(xf, g, eps):
    ms = jnp.mean(xf * xf, axis=-1, keepdims=True)
    return xf * lax.rsqrt(ms + eps) * g


INP_TM = 2048
INP_TN = 512
_Q_TILE0 = COL_Q // INP_TN
_K_TILE0 = COL_K // INP_TN
_V_TILE0 = COL_V // INP_TN


def _inproj_kernel(h_ref, w_ref, cos_ref, sin_ref, o_ref):
    j = pl.program_id(1)
    acc = jnp.dot(h_ref[...], w_ref[...].astype(BF16), preferred_element_type=F32)
    is_rope = jnp.logical_and(j >= _Q_TILE0, j < _V_TILE0)

    @pl.when(is_rope)
    def _():
        scale = jnp.where(j < _K_TILE0, math.log2(math.e) * DA_HEAD_DIM ** -0.5, 1.0).astype(F32)
        cos = cos_ref[...] * scale
        sin = sin_ref[...] * scale
        lane = lax.broadcasted_iota(jnp.int32, (INP_TM, LANES), 1)
        first_half = (lane % DA_HEAD_DIM) < (DA_HEAD_DIM // 2)
        for c in range(INP_TN // LANES):
            t = acc[:, c * LANES:(c + 1) * LANES]
            swapped = jnp.where(first_half, pltpu.roll(t, LANES - 32, 1), pltpu.roll(t, 32, 1))
            o_ref[:, c * LANES:(c + 1) * LANES] = (t * cos + swapped * sin).astype(BF16)

    @pl.when(jnp.logical_not(is_rope))
    def _():
        o_ref[...] = acc.astype(BF16)


def _inproj(h, w_all, layer, cos_t, sin_t):
    t = h.shape[0]
    return pl.pallas_call(
        _inproj_kernel,
        grid=(t // INP_TM, IN_TOTAL // INP_TN),
        in_specs=[
            pl.BlockSpec((INP_TM, D_MODEL), lambda i, j: (i, 0)),
            pl.BlockSpec((None, D_MODEL, INP_TN), lambda i, j: (layer, 0, j)),
            pl.BlockSpec((INP_TM, LANES), lambda i, j: (i, 0)),
            pl.BlockSpec((INP_TM, LANES), lambda i, j: (i, 0)),
        ],
        out_specs=pl.BlockSpec((INP_TM, INP_TN), lambda i, j: (i, j)),
        out_shape=jax.ShapeDtypeStruct((t, IN_TOTAL), BF16),
        compiler_params=_params(("arbitrary", "arbitrary")),
        name="inproj",
    )(h, w_all, cos_t, sin_t)


ATT_TQ = 1024
ATT_TK = 1024


ATT_RG = 1024
ATT_DRG = 512


def _attn_kernel(lam_ref, g_ref, q_ref, k_ref, v_ref, o_ref, qs_ref, m_ref, l_ref, acc_ref):
    i = pl.program_id(1)
    tq = ATT_TQ
    q = q_ref[...]
    lane = lax.broadcasted_iota(jnp.int32, q.shape, 1)
    zero = jnp.zeros_like(q)
    qs_ref[0:tq, :] = jnp.where(lane < DA_HEAD_DIM, q, zero)
    qs_ref[tq:2 * tq, :] = jnp.where(lane >= DA_HEAD_DIM, q, zero)
    m_ref[...] = jnp.full(m_ref.shape, -jnp.inf, F32)
    l_ref[...] = jnp.zeros(l_ref.shape, F32)
    acc_ref[...] = jnp.zeros(acc_ref.shape, F32)

    def update_rows(r0, nrows, k, v, mask):
        n = k.shape[0]
        rows = slice(r0, r0 + nrows)
        s = lax.dot_general(qs_ref[rows, :], k, (((1,), (1,)), ((), ())), preferred_element_type=F32)
        if mask is not None:
            s = jnp.where(mask, s, -jnp.inf)
        tiles = [s[:, t * LANES:(t + 1) * LANES] for t in range(n // LANES)]
        mc = functools.reduce(jnp.maximum, tiles)
        m_old = m_ref[rows, :]
        m_new = jnp.maximum(m_old, jnp.max(mc, axis=1, keepdims=True))
        alpha = jnp.exp2(m_old - m_new)
        p_tiles = [jnp.exp2(t - m_new) for t in tiles]
        l_ref[rows, :] = alpha * l_ref[rows, :] + functools.reduce(jnp.add, p_tiles)
        p = jnp.concatenate(p_tiles, axis=1).astype(BF16)
        acc_ref[rows, :] = alpha * acc_ref[rows, :] + jnp.dot(p, v, preferred_element_type=F32)
        m_ref[rows, :] = m_new

    def body(c, carry):
        off = pl.multiple_of(c * ATT_TK, ATT_TK)
        k = k_ref[pl.ds(off, ATT_TK), :]
        v = v_ref[pl.ds(off, ATT_TK), :]
        for r0 in range(0, 2 * tq, ATT_RG):
            update_rows(r0, ATT_RG, k, v, None)
        return carry

    lax.fori_loop(0, i * (tq // ATT_TK), body, 0)

    off = pl.multiple_of(i * tq, tq)
    for r0 in range(0, 2 * tq, ATT_DRG):
        qo = r0 % tq
        n = qo + ATT_DRG
        row = lax.broadcasted_iota(jnp.int32, (ATT_DRG, n), 0)
        col = lax.broadcasted_iota(jnp.int32, (ATT_DRG, n), 1)
        update_rows(r0, ATT_DRG, k_ref[pl.ds(off, n), :], v_ref[pl.ds(off, n), :], col <= row + qo)

    lam_init = lam_ref[4:5, 0:1]
    lam = (jnp.exp(jnp.sum(lam_ref[0:1, :] * lam_ref[1:2, :], axis=1, keepdims=True))
           - jnp.exp(jnp.sum(lam_ref[2:3, :] * lam_ref[3:4, :], axis=1, keepdims=True))
           + lam_init)
    acc = acc_ref[...]
    inv_l = 1.0 / jnp.sum(l_ref[...], axis=1, keepdims=True)
    o = acc[0:tq] * inv_l[0:tq] - lam * (acc[tq:2 * tq] * inv_l[tq:2 * tq])
    o = _rms(o, g_ref[...], HEAD_NORM_EPS) * (1.0 - lam_init)
    o_ref[...] = o.astype(BF16)


def _attention(proj, lam_pack, head_g, layer):
    t = proj.shape[0]
    qb, kb, vb = COL_Q // LANES, COL_K // LANES, COL_V // LANES
    return pl.pallas_call(
        _attn_kernel,
        grid=(DA_HEADS, t // ATT_TQ),
        in_specs=[
            pl.BlockSpec((None, 8, DA_HEAD_DIM), lambda h, i: (layer, 0, 0)),
            _vec_spec(DA_V_DIM, layer, 2),
            pl.BlockSpec((ATT_TQ, LANES), lambda h, i: (i, qb + h)),
            pl.BlockSpec((t, LANES), lambda h, i: (0, kb + h)),
            pl.BlockSpec((t, LANES), lambda h, i: (0, vb + h)),
        ],
        out_specs=pl.BlockSpec((ATT_TQ, LANES), lambda h, i: (i, h)),
        out_shape=jax.ShapeDtypeStruct((t, DA_WIDTH), BF16),
        scratch_shapes=[
            pltpu.VMEM((2 * ATT_TQ, LANES), BF16),
            pltpu.VMEM((2 * ATT_TQ, LANES), F32),
            pltpu.VMEM((2 * ATT_TQ, LANES), F32),
            pltpu.VMEM((2 * ATT_TQ, LANES), F32),
        ],
        compiler_params=_params(("arbitrary", "arbitrary")),
        name="diff_attn",
    )(lam_pack, head_g, proj, proj, proj)


CV_TM = 512
CV_HALO = 32
CV_ROWS = 128
CV_SUB = 8
CV_SHIFT_ROWS = CV_TM + (CONV_TAPS - 1) // CV_SUB * CV_SUB


def _conv_kernel(a_ref, b_ref, w_ref, bias_ref, g_ref, beta_ref, o_ref, z_ref, zs_ref):
    i = pl.program_id(0)

    @pl.when(i == 0)
    def _():
        z_ref[0:CV_HALO, :] = jnp.zeros((CV_HALO, CV_WIDTH), F32)

    @pl.when(i > 0)
    def _():
        z_ref[0:CV_HALO, :] = z_ref[CV_TM:CV_TM + CV_HALO, :]

    a = a_ref[...].astype(F32)
    b = b_ref[...].astype(F32)
    z_ref[CV_HALO:CV_HALO + CV_TM, :] = a * jax.nn.sigmoid(b)

    base = CV_HALO - (CONV_TAPS - 1)
    for res in range(CV_SUB):
        n = CV_TM + (CONV_TAPS - 1 - res) // CV_SUB * CV_SUB
        zs_ref[res, 0:n, :] = z_ref[base + res:base + res + n, :]
    for r in range(0, CV_TM, CV_ROWS):
        acc = jnp.zeros((CV_ROWS, CV_WIDTH), F32) + bias_ref[...]
        for j in range(CONV_TAPS):
            blk, res = divmod(j, CV_SUB)
            lo = r + blk * CV_SUB
            acc = acc + w_ref[j:j + 1, :] * zs_ref[res, lo:lo + CV_ROWS, :]
        mu = jnp.mean(acc, axis=-1, keepdims=True)
        xc = acc - mu
        var = jnp.mean(xc * xc, axis=-1, keepdims=True)
        y = xc * lax.rsqrt(var + LN_EPS) * g_ref[...] + beta_ref[...]
        o_ref[r:r + CV_ROWS, :] = (y * jax.nn.sigmoid(y)).astype(BF16)


def _conv(proj, dw_w, dw_b, ln_g, ln_b, layer):
    t = proj.shape[0]
    ab, bb = COL_CVA // CV_WIDTH, COL_CVB // CV_WIDTH
    vec = _vec_spec(CV_WIDTH, layer, 1)
    return pl.pallas_call(
        _conv_kernel,
        grid=(t // CV_TM,),
        in_specs=[
            pl.BlockSpec((CV_TM, CV_WIDTH), lambda i: (i, ab)),
            pl.BlockSpec((CV_TM, CV_WIDTH), lambda i: (i, bb)),
            pl.BlockSpec((None, CONV_TAPS, CV_WIDTH), lambda i: (layer, 0, 0)),
            vec, vec, vec,
        ],
        out_specs=pl.BlockSpec((CV_TM, CV_WIDTH), lambda i: (i, 0)),
        out_shape=jax.ShapeDtypeStruct((t, CV_WIDTH), BF16),
        scratch_shapes=[pltpu.VMEM((CV_HALO + CV_TM, CV_WIDTH), F32),
                        pltpu.VMEM((CV_SUB, CV_SHIFT_ROWS, CV_WIDTH), F32)],
        compiler_params=_params(("arbitrary",)),
        name="conformer_conv",
    )(proj, proj, dw_w, dw_b, ln_g, ln_b)


SSM_CB = 128
SSM_RB = SSM_CB * SSM_CHUNK
SSM_SLAB = 2 * SSM_GROUP
SSM_SLABS = LANES // SSM_SLAB


def _pick_slabs(pieces, src_slab, lane_slab):
    out = None
    for k, piece in enumerate(pieces):
        shift = ((k - src_slab) * SSM_SLAB) % LANES
        moved = piece if shift == 0 else pltpu.roll(piece, shift, 1)
        out = moved if out is None else jnp.where(lane_slab == k, moved, out)
    return out


def _ssm_kernel(u_ref, t_ref, wre_ref, wim_ref, vre_ref, vim_ref, ar_ref, ai_ref, d_ref, y_ref,
                uf_ref, up_ref, sre_ref, sim_ref, yp_ref, yf_ref, xr_ref, xi_ref):
    npair, cb = SSM_PAIRS, SSM_CB

    @pl.when(pl.program_id(0) == 0)
    def _():
        xr_ref[...] = jnp.zeros(xr_ref.shape, F32)
        xi_ref[...] = jnp.zeros(xi_ref.shape, F32)

    ntile = SSM_WIDTH // LANES
    for j in range(ntile):
        uf_ref[j] = u_ref[:, j * LANES:(j + 1) * LANES].astype(F32)
    lane_slab = lax.shift_right_logical(lax.broadcasted_iota(jnp.int32, (cb, LANES), 1), 5)

    for q in range(SSM_CHUNK // SSM_SLABS):
        for tile in range(ntile):
            src = uf_ref.at[tile]
            pieces = [src[pl.ds(SSM_SLABS * q + k, cb, stride=SSM_CHUNK), :] for k in range(SSM_SLABS)]
            for slab in range(SSM_SLABS):
                up_ref[SSM_SLABS * tile + slab, :, q * LANES:(q + 1) * LANES] = _pick_slabs(
                    pieces, slab, lane_slab).astype(BF16)

    for p in range(npair):
        u = up_ref[p]
        sre_ref[pl.ds(p, cb, stride=npair), :] = jnp.dot(u, wre_ref[p], preferred_element_type=F32)
        sim_ref[pl.ds(p, cb, stride=npair), :] = jnp.dot(u, wim_ref[p], preferred_element_type=F32)

    ar = ar_ref[...]
    ai = ai_ref[...]

    def body(c, carry):
        xr, xi = carry
        off = pl.multiple_of(c * npair, npair)
        sr = sre_ref[pl.ds(off, npair), :]
        si = sim_ref[pl.ds(off, npair), :]
        sre_ref[pl.ds(off, npair), :] = xr
        sim_ref[pl.ds(off, npair), :] = xi
        return ar * xr - ai * xi + sr, ar * xi + ai * xr + si

    xr, xi = lax.fori_loop(0, cb, body, (xr_ref[...], xi_ref[...]), unroll=8)
    xr_ref[...] = xr
    xi_ref[...] = xi

    for p in range(npair):
        u = up_ref[p]
        sr = sre_ref[pl.ds(p, cb, stride=npair), :].astype(BF16)
        si = sim_ref[pl.ds(p, cb, stride=npair), :].astype(BF16)
        nt = (((1,), (1,)), ((), ()))
        y = jnp.dot(u, t_ref[p], preferred_element_type=F32)
        y = y + lax.dot_general(sr, vre_ref[p], nt, preferred_element_type=F32)
        yp_ref[p] = y + lax.dot_general(si, vim_ref[p], nt, preferred_element_type=F32)

    for tile in range(ntile):
        for q in range(SSM_CHUNK // SSM_SLABS):
            pieces = [yp_ref[SSM_SLABS * tile + k, :, q * LANES:(q + 1) * LANES] for k in range(SSM_SLABS)]
            dst = yf_ref.at[tile]
            for slab in range(SSM_SLABS):
                dst[pl.ds(SSM_SLABS * q + slab, cb, stride=SSM_CHUNK), :] = _pick_slabs(pieces, slab, lane_slab)

    for j in range(ntile):
        cols = slice(j * LANES, (j + 1) * LANES)
        y_ref[:, cols] = (yf_ref[j] + uf_ref[j] * d_ref[:, cols]).astype(BF16)


def _ssm(proj, mats):
    t = proj.shape[0]
    vm = pl.BlockSpec(memory_space=pltpu.VMEM)
    return pl.pallas_call(
        _ssm_kernel,
        grid=(t // SSM_RB,),
        in_specs=[pl.BlockSpec((SSM_RB, SSM_WIDTH), lambda i: (i, COL_SSM // SSM_WIDTH))] + [vm] * 8,
        out_specs=pl.BlockSpec((SSM_RB, SSM_WIDTH), lambda i: (i, 0)),
        out_shape=jax.ShapeDtypeStruct((t, SSM_WIDTH), BF16),
        scratch_shapes=[
            pltpu.VMEM((SSM_WIDTH // LANES, SSM_RB, LANES), F32),
            pltpu.VMEM((SSM_PAIRS, SSM_CB, SSM_PAIR_W), BF16),
            pltpu.VMEM((SSM_CB * SSM_PAIRS, LANES), F32),
            pltpu.VMEM((SSM_CB * SSM_PAIRS, LANES), F32),
            pltpu.VMEM((SSM_PAIRS, SSM_CB, SSM_PAIR_W), F32),
            pltpu.VMEM((SSM_WIDTH // LANES, SSM_RB, LANES), F32),
            pltpu.VMEM((SSM_PAIRS, LANES), F32),
            pltpu.VMEM((SSM_PAIRS, LANES), F32),
        ],
        compiler_params=_params(("arbitrary",)),
        name="s5_scan",
    )(proj, *mats)


def _ssm_gen_kernel(bre_ref, bim_ref, cre_ref, cim_ref, pr_ref, pi_ref,
                    t_ref, wre_ref, wim_ref, vre_ref, vim_ref, car_ref, cai_ref):
    bre, bim, cre, cim = bre_ref[...], bim_ref[...], cre_ref[...], cim_ref[...]
    for tau in range(SSM_CHUNK):
        rows = slice(tau * SSM_SLAB, (tau + 1) * SSM_SLAB)
        pr = pr_ref[tau:tau + 1, :]
        pi = pi_ref[tau:tau + 1, :]
        car_ref[rows, :] = cre * pr - cim * pi
        cai_ref[rows, :] = cre * pi + cim * pr
    nt = (((1,), (1,)), ((), ()))
    hp = lax.Precision.HIGHEST
    k = (lax.dot_general(bre, car_ref[...], nt, precision=hp, preferred_element_type=F32)
         - lax.dot_general(bim, cai_ref[...], nt, precision=hp, preferred_element_type=F32))
    lane = lax.broadcasted_iota(jnp.int32, k.shape, 1)
    for s in range(SSM_CHUNK):
        rows = slice(s * SSM_SLAB, (s + 1) * SSM_SLAB)
        moved = k if s == 0 else pltpu.roll(k, s * SSM_SLAB, 1)
        t_ref[rows, :] = jnp.where(lane >= s * SSM_SLAB, moved, 0.0).astype(BF16)
        pr = pr_ref[SSM_CHUNK - 1 - s:SSM_CHUNK - s, :]
        pi = pi_ref[SSM_CHUNK - 1 - s:SSM_CHUNK - s, :]
        wre_ref[rows, :] = (bre * pr - bim * pi).astype(BF16)
        wim_ref[rows, :] = (bre * pi + bim * pr).astype(BF16)
        qr = pr_ref[s + 1:s + 2, :]
        qi = pi_ref[s + 1:s + 2, :]
        vre_ref[rows, :] = (cre * qr - cim * qi).astype(BF16)
        vim_ref[rows, :] = (-(cre * qi + cim * qr)).astype(BF16)


def _ssm_matrices(lam_re, lam_im, log_dt, b_re, b_im, c_re, c_im, d_skip):
    nl = lam_re.shape[0]
    lr = lam_re.astype(F32)
    li = lam_im.astype(F32)
    dt = jnp.exp(log_dt.astype(F32))[..., None]
    mag = jnp.exp(lr * dt)
    ab_re = mag * jnp.cos(li * dt)
    ab_im = mag * jnp.sin(li * dt)
    den = lr * lr + li * li
    f_re = ((ab_re - 1.0) * lr + ab_im * li) / den
    f_im = (ab_im * lr - (ab_re - 1.0) * li) / den
    br = b_re.astype(F32)
    bi = b_im.astype(F32)
    bb_re = f_re[..., None] * br - f_im[..., None] * bi
    bb_im = f_re[..., None] * bi + f_im[..., None] * br
    tau = jnp.arange(SSM_CHUNK + 1, dtype=F32)[None, None, :, None]
    pmag = jnp.exp(tau * (lr * dt)[:, :, None, :])
    pw_re = pmag * jnp.cos(tau * (li * dt)[:, :, None, :])
    pw_im = pmag * jnp.sin(tau * (li * dt)[:, :, None, :])
    cr = c_re.astype(F32)
    ci = c_im.astype(F32)
    eye2 = jnp.eye(2, dtype=F32)

    def paired(a):
        return a.reshape((nl, SSM_PAIRS, 2) + a.shape[2:])

    bbd_re = jnp.einsum('lpenk,ef->lpekfn', paired(bb_re), eye2).reshape(nl, SSM_PAIRS, SSM_SLAB, LANES)
    bbd_im = jnp.einsum('lpenk,ef->lpekfn', paired(bb_im), eye2).reshape(nl, SSM_PAIRS, SSM_SLAB, LANES)
    cbd_re = jnp.einsum('lpehn,ef->lpehfn', paired(cr), eye2).reshape(nl, SSM_PAIRS, SSM_SLAB, LANES)
    cbd_im = jnp.einsum('lpehn,ef->lpehfn', paired(ci), eye2).reshape(nl, SSM_PAIRS, SSM_SLAB, LANES)
    pwp_re = paired(pw_re).transpose(0, 1, 3, 2, 4).reshape(nl, SSM_PAIRS, SSM_CHUNK + 1, LANES)
    pwp_im = paired(pw_im).transpose(0, 1, 3, 2, 4).reshape(nl, SSM_PAIRS, SSM_CHUNK + 1, LANES)

    def spec(r, c):
        return pl.BlockSpec((None, None, r, c), lambda l, p: (l, p, 0, 0))

    wide = jax.ShapeDtypeStruct((nl, SSM_PAIRS, SSM_PAIR_W, SSM_PAIR_W), BF16)
    tall = jax.ShapeDtypeStruct((nl, SSM_PAIRS, SSM_PAIR_W, LANES), BF16)
    tm, w_re, w_im, vt_re, vt_im = pl.pallas_call(
        _ssm_gen_kernel,
        grid=(nl, SSM_PAIRS),
        in_specs=[spec(SSM_SLAB, LANES)] * 4 + [spec(SSM_CHUNK + 1, LANES)] * 2,
        out_specs=[spec(SSM_PAIR_W, SSM_PAIR_W)] + [spec(SSM_PAIR_W, LANES)] * 4,
        out_shape=[wide, tall, tall, tall, tall],
        scratch_shapes=[pltpu.VMEM((SSM_PAIR_W, LANES), F32), pltpu.VMEM((SSM_PAIR_W, LANES), F32)],
        compiler_params=_params(("arbitrary", "arbitrary")),
        name="s5_matrices",
    )(bbd_re, bbd_im, cbd_re, cbd_im, pwp_re, pwp_im)
    a_re = pwp_re[:, :, SSM_CHUNK]
    a_im = pwp_im[:, :, SSM_CHUNK]
    d_t = d_skip.astype(F32).reshape(nl, 1, SSM_WIDTH)
    return tm, w_re, w_im, vt_re, vt_im, a_re, a_im, d_t


MG_TM = 512
MG_TN = 1024
MG_GW = MG_TN // 2


def _merge_mix_kernel(oa_ref, zb_ref, yc_ref, ga0_ref, ga1_ref, gb0_ref, gb1_ref, gc0_ref, gc1_ref,
                      wda_ref, wcv_ref, wglu_ref, bglu_ref, wso_ref, wmix_ref, x_ref, o_ref, sc_ref):
    n = pl.program_id(1)

    def gate(lo_ref, hi_ref):
        return jax.nn.sigmoid(jnp.concatenate([lo_ref[...], hi_ref[...]], axis=1).astype(F32))

    @pl.when(n == 0)
    def _():
        glu = jnp.dot(yc_ref[...], wglu_ref[...], preferred_element_type=F32) + bglu_ref[...]
        sc_ref[...] = (glu[:, :SSM_WIDTH] * jax.nn.sigmoid(glu[:, SSM_WIDTH:])).astype(BF16)
        o_ref[...] = x_ref[...]

    y_a = jnp.dot(oa_ref[...], wda_ref[...], preferred_element_type=F32)
    y_b = jnp.dot(zb_ref[...], wcv_ref[...], preferred_element_type=F32)
    y_c = jnp.dot(sc_ref[...], wso_ref[...], preferred_element_type=F32)
    merged = gate(ga0_ref, ga1_ref) * y_a + gate(gb0_ref, gb1_ref) * y_b + gate(gc0_ref, gc1_ref) * y_c
    o_ref[...] += jnp.dot(merged.astype(BF16), wmix_ref[...], preferred_element_type=F32)


def _merge_mix(o_a, z_b, y_c, proj, w_da, w_cv, w_glu, b_glu, w_so, w_mix, x, layer):
    t = o_a.shape[0]

    def wcol(k):
        return pl.BlockSpec((None, k, MG_TN), lambda i, n: (layer, 0, n))

    def gate_halves(col0):
        first = col0 // MG_GW
        return [pl.BlockSpec((MG_TM, MG_GW), lambda i, n, h=h: (i, first + 2 * n + h)) for h in range(2)]

    return pl.pallas_call(
        _merge_mix_kernel,
        grid=(t // MG_TM, D_MODEL // MG_TN),
        in_specs=[
            pl.BlockSpec((MG_TM, DA_WIDTH), lambda i, n: (i, 0)),
            pl.BlockSpec((MG_TM, CV_WIDTH), lambda i, n: (i, 0)),
            pl.BlockSpec((MG_TM, SSM_WIDTH), lambda i, n: (i, 0)),
            *gate_halves(COL_GA), *gate_halves(COL_GB), *gate_halves(COL_GC),
            wcol(DA_WIDTH), wcol(CV_WIDTH),
            pl.BlockSpec((None, SSM_WIDTH, 2 * SSM_WIDTH), lambda i, n: (layer, 0, 0)),
            _vec_spec(2 * SSM_WIDTH, layer, 2),
            wcol(SSM_WIDTH),
            pl.BlockSpec((None, MG_TN, D_MODEL), lambda i, n: (layer, n, 0)),
            pl.BlockSpec((MG_TM, D_MODEL), lambda i, n: (i, 0)),
        ],
        out_specs=pl.BlockSpec((MG_TM, D_MODEL), lambda i, n: (i, 0)),
        out_shape=jax.ShapeDtypeStruct((t, D_MODEL), F32),
        scratch_shapes=[pltpu.VMEM((MG_TM, SSM_WIDTH), BF16)],
        compiler_params=_params(("arbitrary", "arbitrary")),
        name="merge_mix",
    )(o_a, z_b, y_c, *([proj] * 6), w_da, w_cv, w_glu, b_glu, w_so, w_mix, x)


def _norm_matmul_kernel(x_ref, g_ref, w_ref, o_ref):
    h = _rms(x_ref[...], g_ref[...], RMS_EPS).astype(BF16)
    o_ref[...] = jnp.dot(h, w_ref[...], preferred_element_type=F32).astype(o_ref.dtype)


def _mem_kv(mem, g_all, w_all, layer):
    m = mem.shape[0]
    n = 2 * XA_WIDTH
    return pl.pallas_call(
        _norm_matmul_kernel,
        grid=(1,),
        in_specs=[
            pl.BlockSpec((m, D_MODEL), lambda i: (0, 0)),
            _vec_spec(D_MODEL, layer, 1),
            pl.BlockSpec((None, D_MODEL, n), lambda i: (layer, 0, 0)),
        ],
        out_specs=pl.BlockSpec((m, n), lambda i: (0, 0)),
        out_shape=jax.ShapeDtypeStruct((m, n), BF16),
        compiler_params=_params(("arbitrary",)),
        name="mem_kv",
    )(mem, g_all, w_all)


XA_TM = 512


def _xattn_kernel(x_ref, g_ref, wq_ref, kv_ref, wo_ref, o_ref):
    x = x_ref[...]
    h = _rms(x, g_ref[...], RMS_EPS).astype(BF16)
    q = jnp.dot(h, wq_ref[...], preferred_element_type=F32).astype(BF16)
    heads = []
    for hd in range(XA_HEADS):
        lo = hd * XA_HEAD_DIM
        k = kv_ref[:, lo:lo + XA_HEAD_DIM]
        v = kv_ref[:, XA_WIDTH + lo:XA_WIDTH + lo + XA_HEAD_DIM]
        s = lax.dot_general(q[:, lo:lo + XA_HEAD_DIM], k, (((1,), (1,)), ((), ())),
                            preferred_element_type=F32) * (XA_HEAD_DIM ** -0.5)
        m = jnp.max(s, axis=-1, keepdims=True)
        e = jnp.exp(s - m)
        p = e / jnp.sum(e, axis=-1, keepdims=True)
        heads.append(jnp.dot(p.astype(BF16), v, preferred_element_type=F32).astype(BF16))
    o = jnp.concatenate(heads, axis=-1)
    o_ref[...] = x + jnp.dot(o, wo_ref[...], preferred_element_type=F32)


def _xattn(x, g_all, wq_all, kv, wo_all, layer):
    t = x.shape[0]
    return pl.pallas_call(
        _xattn_kernel,
        grid=(t // XA_TM,),
        in_specs=[
            pl.BlockSpec((XA_TM, D_MODEL), lambda i: (i, 0)),
            _vec_spec(D_MODEL, layer, 1),
            pl.BlockSpec((None, D_MODEL, XA_WIDTH), lambda i: (layer, 0, 0)),
            pl.BlockSpec((MEM_LEN, 2 * XA_WIDTH), lambda i: (0, 0)),
            pl.BlockSpec((None, XA_WIDTH, D_MODEL), lambda i: (layer, 0, 0)),
        ],
        out_specs=pl.BlockSpec((XA_TM, D_MODEL), lambda i: (i, 0)),
        out_shape=jax.ShapeDtypeStruct((t, D_MODEL), F32),
        compiler_params=_params(("arbitrary",)),
        name="mem_xattn",
    )(x, g_all, wq_all, kv, wo_all)


RT_TM = 512
_E_LANE0 = MOE_GROUPS


def _router_kernel(x_ref, g_ref, w_ref, b_ref, meta_ref, cnt_ref, run_ref):
    i = pl.program_id(0)

    @pl.when(i == 0)
    def _():
        run_ref[...] = jnp.zeros(run_ref.shape, F32)

    h = _rms(x_ref[...], g_ref[...], RMS_EPS).astype(BF16)
    logits = jnp.dot(h, w_ref[...], preferred_element_type=F32) + b_ref[...]
    lane = lax.broadcasted_iota(jnp.int32, logits.shape, 1).astype(F32)
    neg = jnp.float32(-jnp.inf)
    big = jnp.float32(LANES)

    def first_argmax(vals):
        top = jnp.max(vals, axis=-1, keepdims=True)
        idx = jnp.min(jnp.where(vals == top, lane, big), axis=-1, keepdims=True)
        return top, idx

    gl = jnp.where(lane < MOE_GROUPS, logits, neg)
    g_top, g_idx = first_argmax(gl)
    g_w = 1.0 / jnp.sum(jnp.exp(gl - g_top), axis=-1, keepdims=True)
    e_lane = lane - _E_LANE0
    in_group = jnp.logical_and(e_lane >= g_idx * MOE_PER_GROUP, e_lane < (g_idx + 1) * MOE_PER_GROUP)
    el = jnp.where(in_group, logits, neg)
    v1, i1 = first_argmax(el)
    el2 = jnp.where(lane == i1, neg, el)
    v2, i2 = first_argmax(el2)
    e2 = jnp.exp(v2 - v1)
    w1 = 1.0 / (1.0 + e2)
    w2 = e2 / (1.0 + e2)
    oh1 = (lane == i1).astype(F32)
    oh2 = (lane == i2).astype(F32)
    both = oh1 + oh2
    row = lax.broadcasted_iota(jnp.int32, (RT_TM, RT_TM), 0)
    col = lax.broadcasted_iota(jnp.int32, (RT_TM, RT_TM), 1)
    earlier = jnp.where(col < row, 1.0, 0.0).astype(BF16)
    before = jnp.dot(earlier, both.astype(BF16), preferred_element_type=F32) + run_ref[...]
    rank1 = jnp.sum(oh1 * before, axis=-1, keepdims=True)
    rank2 = jnp.sum(oh2 * before, axis=-1, keepdims=True)
    run_ref[...] += jnp.sum(both, axis=0, keepdims=True)
    cnt_ref[...] = run_ref[...]
    meta = jnp.where(lane == 0.0, i1 - _E_LANE0, 0.0)
    for k, val in enumerate((i2 - _E_LANE0, rank1, rank2, w1 * g_w, w2 * g_w), start=1):
        meta = jnp.where(lane == float(k), val, meta)
    meta_ref[...] = meta


def _router(x, g_all, w_r, b_r, layer):
    t = x.shape[0]
    return pl.pallas_call(
        _router_kernel,
        grid=(t // RT_TM,),
        in_specs=[
            pl.BlockSpec((RT_TM, D_MODEL), lambda i: (i, 0)),
            _vec_spec(D_MODEL, layer, 1),
            pl.BlockSpec((None, D_MODEL, LANES), lambda i: (layer, 0, 0)),
            _vec_spec(LANES, layer, 1),
        ],
        out_specs=[
            pl.BlockSpec((RT_TM, LANES), lambda i: (i, 0)),
            pl.BlockSpec((1, LANES), lambda i: (0, 0)),
        ],
        out_shape=[jax.ShapeDtypeStruct((t, LANES), F32), jax.ShapeDtypeStruct((1, LANES), F32)],
        scratch_shapes=[pltpu.VMEM((1, LANES), F32)],
        compiler_params=_params(("arbitrary",)),
        name="moe_router",
    )(x, g_all, w_r, b_r)


EX_TM = 256
EX_ROWS = 2 * SEQ
EX_TILES = EX_ROWS // EX_TM
EX_STEPS = EX_TILES + MOE_EXPERTS - 1


def _row_copy(src, src_row, dst, dst_row, sem):
    return pltpu.make_async_copy(src.at[pl.ds(src_row, 1), :], dst.at[pl.ds(dst_row, 1), :], sem)


DP_TM = 256


def _dispatch_kernel(p1_ref, p2_ref, x_ref, xs_hbm, sem):
    base = pl.program_id(0) * DP_TM

    def issue(j, c):
        _row_copy(x_ref, j, xs_hbm, p1_ref[base + j], sem).start(priority=0)
        _row_copy(x_ref, j, xs_hbm, p2_ref[base + j], sem).start(priority=1)
        return c

    lax.fori_loop(0, DP_TM, issue, 0, unroll=True)

    def wait(j, c):
        _row_copy(x_ref, j, xs_hbm, 0, sem).wait()
        _row_copy(x_ref, j, xs_hbm, 0, sem).wait()
        return c

    lax.fori_loop(0, DP_TM, wait, 0, unroll=8)


def _dispatch(pos1, pos2, x):
    t = x.shape[0]
    grid_spec = pltpu.PrefetchScalarGridSpec(
        num_scalar_prefetch=2,
        grid=(t // DP_TM,),
        in_specs=[pl.BlockSpec((DP_TM, D_MODEL), lambda i, p1, p2: (i, 0))],
        out_specs=pl.BlockSpec(memory_space=pl.ANY),
        scratch_shapes=[pltpu.SemaphoreType.DMA(())],
    )
    return pl.pallas_call(
        _dispatch_kernel,
        grid_spec=grid_spec,
        out_shape=jax.ShapeDtypeStruct((EX_ROWS, D_MODEL), F32),
        compiler_params=_params(("arbitrary",)),
        name="moe_dispatch",
    )(pos1, pos2, x)


def _experts_kernel(se_ref, st_ref, lo_ref, hi_ref, first_ref, wnew_ref, x_ref, g_ref, wg_ref, wu_ref,
                    wd_ref, o_ref, wgb_ref, wub_ref, wdb_ref):
    s = pl.program_id(0)
    lo = lo_ref[s]
    hi = hi_ref[s]

    @pl.when(wnew_ref[s] == 1)
    def _():
        wgb_ref[...] = wg_ref[...].astype(BF16)
        wub_ref[...] = wu_ref[...].astype(BF16)
        wdb_ref[...] = wd_ref[...].astype(BF16)

    @pl.when(hi > lo)
    def _():
        h = _rms(x_ref[...], g_ref[...], RMS_EPS).astype(BF16)
        gate = jnp.dot(h, wgb_ref[...], preferred_element_type=F32)
        up = jnp.dot(h, wub_ref[...], preferred_element_type=F32)
        act = (gate * jax.nn.sigmoid(gate) * up).astype(BF16)
        res = jnp.dot(act, wdb_ref[...], preferred_element_type=F32)
        row = lax.broadcasted_iota(jnp.int32, (EX_TM, 1), 0)
        mine = jnp.logical_and(row >= lo, row < hi)

        @pl.when(first_ref[s] == 1)
        def _():
            o_ref[...] = jnp.where(mine, res, 0.0)

        @pl.when(first_ref[s] == 0)
        def _():
            o_ref[...] = jnp.where(mine, res, o_ref[...])


def _experts(plan, xs_sorted, g_all, wg_all, wu_all, wd_all, layer):
    def wspec(k, n):
        return pl.BlockSpec((None, None, k, n), lambda s, se, st, lo, hi, fi, wn: (layer, se[s], 0, 0))

    grid_spec = pltpu.PrefetchScalarGridSpec(
        num_scalar_prefetch=6,
        grid=(EX_STEPS,),
        in_specs=[
            pl.BlockSpec((EX_TM, D_MODEL), lambda s, se, st, lo, hi, fi, wn: (st[s], 0)),
            pl.BlockSpec((None, 1, D_MODEL), lambda s, se, st, lo, hi, fi, wn: (layer, 0, 0)),
            wspec(D_MODEL, MOE_FF), wspec(D_MODEL, MOE_FF), wspec(MOE_FF, D_MODEL),
        ],
        out_specs=pl.BlockSpec((EX_TM, D_MODEL), lambda s, se, st, lo, hi, fi, wn: (st[s], 0)),
        scratch_shapes=[pltpu.VMEM((D_MODEL, MOE_FF), BF16), pltpu.VMEM((D_MODEL, MOE_FF), BF16),
                        pltpu.VMEM((MOE_FF, D_MODEL), BF16)],
    )
    return pl.pallas_call(
        _experts_kernel,
        grid_spec=grid_spec,
        out_shape=jax.ShapeDtypeStruct((EX_ROWS, D_MODEL), F32),
        compiler_params=_params(("arbitrary",)),
        name="moe_experts",
    )(*plan, xs_sorted, g_all, wg_all, wu_all, wd_all)


CB_TM = 256


def _combine_kernel(p1_ref, p2_ref, y_hbm, x_ref, meta_ref, g_ref, o_ref, n_ref, y1buf, y2buf, sem):
    i = pl.program_id(0)
    slot = lax.rem(i, 2)

    def fetch(tile, slot_):
        base = tile * CB_TM

        def issue(j, c):
            _row_copy(y_hbm, p1_ref[base + j], y1buf.at[slot_], j, sem.at[slot_]).start(priority=0)
            _row_copy(y_hbm, p2_ref[base + j], y2buf.at[slot_], j, sem.at[slot_]).start(priority=1)
            return c

        lax.fori_loop(0, CB_TM, issue, 0, unroll=True)

    @pl.when(i == 0)
    def _():
        fetch(0, 0)

    @pl.when(i + 1 < pl.num_programs(0))
    def _():
        fetch(i + 1, 1 - slot)

    def wait(j, c):
        _row_copy(y_hbm, 0, y1buf.at[slot], j, sem.at[slot]).wait()
        _row_copy(y_hbm, 0, y2buf.at[slot], j, sem.at[slot]).wait()
        return c

    lax.fori_loop(0, CB_TM, wait, 0, unroll=8)
    meta = meta_ref[...]
    x_new = x_ref[...] + meta[:, 4:5] * y1buf[slot] + meta[:, 5:6] * y2buf[slot]
    o_ref[...] = x_new
    n_ref[...] = _rms(x_new, g_ref[...], RMS_EPS).astype(n_ref.dtype)


def _combine(pos1, pos2, ys, x, meta, g_next, next_dtype):
    t = x.shape[0]
    tile = pl.BlockSpec((CB_TM, D_MODEL), lambda i, p1, p2: (i, 0))
    grid_spec = pltpu.PrefetchScalarGridSpec(
        num_scalar_prefetch=2,
        grid=(t // CB_TM,),
        in_specs=[
            pl.BlockSpec(memory_space=pl.ANY),
            tile,
            pl.BlockSpec((CB_TM, LANES), lambda i, p1, p2: (i, 0)),
            pl.BlockSpec((1, D_MODEL), lambda i, p1, p2: (0, 0)),
        ],
        out_specs=[tile, tile],
        scratch_shapes=[pltpu.VMEM((2, CB_TM, D_MODEL), F32), pltpu.VMEM((2, CB_TM, D_MODEL), F32),
                        pltpu.SemaphoreType.DMA((2,))],
    )
    return pl.pallas_call(
        _combine_kernel,
        grid_spec=grid_spec,
        out_shape=[jax.ShapeDtypeStruct((t, D_MODEL), F32), jax.ShapeDtypeStruct((t, D_MODEL), next_dtype)],
        compiler_params=_params(("arbitrary",)),
        name="moe_combine",
    )(pos1, pos2, ys, x, meta, g_next)


def _dispatch_plan(meta, cnt):
    t = meta.shape[0]
    e1 = meta[:, 0].astype(jnp.int32)
    e2 = meta[:, 1].astype(jnp.int32)
    r1 = meta[:, 2].astype(jnp.int32)
    r2 = meta[:, 3].astype(jnp.int32)
    counts = cnt[0, _E_LANE0:_E_LANE0 + MOE_EXPERTS].astype(jnp.int32)
    ends = jnp.cumsum(counts)
    starts = ends - counts
    pos1 = starts[e1] + r1
    pos2 = starts[e2] + r2
    t_lo = starts // EX_TM
    t_hi = (ends + EX_TM - 1) // EX_TM
    nsteps = jnp.where(counts > 0, t_hi - t_lo, 0)
    step_end = jnp.cumsum(nsteps)
    step_start = step_end - nsteps
    s = jnp.arange(EX_STEPS, dtype=jnp.int32)
    se = jnp.minimum(jnp.sum((s[:, None] >= step_end[None, :]).astype(jnp.int32), axis=1), MOE_EXPERTS - 1)
    valid = s < step_end[-1]
    st = jnp.where(valid, t_lo[se] + s - step_start[se], EX_TILES - 1)
    lo = jnp.where(valid, jnp.clip(starts[se] - st * EX_TM, 0, EX_TM), 0)
    hi = jnp.where(valid, jnp.clip(ends[se] - st * EX_TM, 0, EX_TM), 0)
    first = jnp.concatenate([jnp.ones((1,), jnp.int32), (st[1:] != st[:-1]).astype(jnp.int32)])
    wnew = jnp.concatenate([jnp.ones((1,), jnp.int32), (se[1:] != se[:-1]).astype(jnp.int32)])
    plan = tuple(a.astype(jnp.int32) for a in (se, st, lo, hi, first, wnew))
    return pos1.astype(jnp.int32), pos2.astype(jnp.int32), plan


FN_TM = 512


def _norm_kernel(x_ref, g_ref, o_ref):
    o_ref[...] = _rms(x_ref[...], g_ref[...], RMS_EPS).astype(o_ref.dtype)


def _first_norm(x, g_all):
    t = x.shape[0]
    return pl.pallas_call(
        _norm_kernel,
        grid=(t // FN_TM,),
        in_specs=[pl.BlockSpec((FN_TM, D_MODEL), lambda i: (i, 0)), _vec_spec(D_MODEL, 0, 1)],
        out_specs=pl.BlockSpec((FN_TM, D_MODEL), lambda i: (i, 0)),
        out_shape=jax.ShapeDtypeStruct((t, D_MODEL), BF16),
        compiler_params=_params(("arbitrary",)),
        name="first_norm",
    )(x, g_all)


def kernel(x, mem, positions, norm_mix, w_in, da_lam_q1, da_lam_k1, da_lam_q2, da_lam_k2, da_head_norm, w_da_out, cv_dw_w, cv_dw_b, cv_ln_g, cv_ln_b, w_cv_out, ssm_lam_re, ssm_lam_im, ssm_log_dt, ssm_b_re, ssm_b_im, ssm_c_re, ssm_c_im, ssm_d, w_ssm_glu, b_ssm_glu, w_ssm_out, w_mix_out, norm_xa, norm_mem, w_xa_q, w_xa_kv, w_xa_out, norm_ffn, w_router_group, b_router_group, w_router_expert, b_router_expert, w_exp_gate, w_exp_up, w_exp_down, norm_final):
    bsz, seq, _ = x.shape
    assert bsz == 1 and seq == SEQ
    nl = w_in.shape[0]
    xs = x.reshape(seq, D_MODEL).astype(F32)
    mem2 = mem.reshape(MEM_LEN, D_MODEL).astype(F32)

    inv_freq = ROPE_THETA ** (-jnp.arange(0, DA_HEAD_DIM, 2, dtype=F32) / DA_HEAD_DIM)
    ang = positions.reshape(seq).astype(F32)[:, None] * inv_freq
    cos = jnp.cos(ang)
    sin = jnp.sin(ang)
    cos_t = jnp.concatenate([cos, cos, cos, cos], axis=-1)
    sin_t = jnp.concatenate([-sin, sin, -sin, sin], axis=-1)

    w_da_b = w_da_out.astype(BF16)
    w_cv_b = w_cv_out.astype(BF16)
    w_glu_b = w_ssm_glu.astype(BF16)
    w_so_b = w_ssm_out.astype(BF16)
    w_mix_b = w_mix_out.astype(BF16)
    w_xq_b = w_xa_q.astype(BF16)
    w_xkv_b = w_xa_kv.astype(BF16)
    w_xo_b = w_xa_out.astype(BF16)
    pad = LANES - MOE_GROUPS - MOE_EXPERTS
    w_r = jnp.concatenate([w_router_group, w_router_expert,
                           jnp.zeros((nl, D_MODEL, pad), F32)], axis=-1).astype(BF16)
    b_r = jnp.concatenate([b_router_group, b_router_expert, jnp.zeros((nl, pad), F32)], axis=-1).astype(F32)

    def vec3(a):
        return a.astype(F32).reshape(nl, 1, a.shape[-1])

    norm_mix, da_head_norm, cv_dw_b, cv_ln_g, cv_ln_b, b_ssm_glu, norm_xa, norm_mem, norm_ffn, b_r = map(
        vec3, (norm_mix, da_head_norm, cv_dw_b, cv_ln_g, cv_ln_b, b_ssm_glu, norm_xa, norm_mem, norm_ffn, b_r))
    cv_dw_w = cv_dw_w.astype(F32)

    lam_inits = jnp.asarray([0.8 - 0.6 * math.exp(-0.3 * l) for l in range(nl)], F32)
    lam_pack = jnp.stack([da_lam_q1, da_lam_k1, da_lam_q2, da_lam_k2], axis=1).astype(F32)
    lam_pack = jnp.concatenate(
        [lam_pack, jnp.broadcast_to(lam_inits[:, None, None], (nl, 4, DA_HEAD_DIM))], axis=1)

    ssm_mats = _ssm_matrices(ssm_lam_re, ssm_lam_im, ssm_log_dt, ssm_b_re, ssm_b_im,
                             ssm_c_re, ssm_c_im, ssm_d)

    h = _first_norm(xs, norm_mix)
    for l in range(nl):
        proj = _inproj(h, w_in, l, cos_t, sin_t)
        o_a = _attention(proj, lam_pack, da_head_norm, l)
        z_b = _conv(proj, cv_dw_w, cv_dw_b, cv_ln_g, cv_ln_b, l)
        y_c = _ssm(proj, [m[l] for m in ssm_mats])
        xs = _merge_mix(o_a, z_b, y_c, proj, w_da_b, w_cv_b, w_glu_b, b_ssm_glu, w_so_b, w_mix_b, xs, l)
        kv = _mem_kv(mem2, norm_mem, w_xkv_b, l)
        xs = _xattn(xs, norm_xa, w_xq_b, kv, w_xo_b, l)
        meta, cnt = _router(xs, norm_ffn, w_r, b_r, l)
        pos1, pos2, plan = _dispatch_plan(meta, cnt)
        xs_sorted = _dispatch(pos1, pos2, xs)
        ys = _experts(plan, xs_sorted, norm_ffn, w_exp_gate, w_exp_up, w_exp_down, l)
        if l + 1 < nl:
            xs, h = _combine(pos1, pos2, ys, xs, meta, norm_mix[l + 1], BF16)
        else:
            xs, out = _combine(pos1, pos2, ys, xs, meta, norm_final.astype(F32).reshape(1, D_MODEL), F32)
    return out.reshape(bsz, seq, D_MODEL)
```

```python
import functools
import math

import jax
import jax.numpy as jnp
from jax import lax
from jax.experimental import pallas as pl
from jax.experimental.pallas import tpu as pltpu

F32 = jnp.float32
BF16 = jnp.bfloat16

D_MODEL = 2048
SEQ = 8192
DEPTH = 4
MEM_LEN = 256
DA_HEADS = 8
DA_HEAD_DIM = 64
DA_V_DIM = 128
DA_WIDTH = 1024
ROPE_THETA = 10000.0
CV_WIDTH = 512
CONV_TAPS = 31
SSM_WIDTH = 512
SSM_GROUP = 16
SSM_GROUPS = 32
SSM_STATE = 64
XA_HEADS = 4
XA_HEAD_DIM = 128
XA_WIDTH = 512
MOE_GROUPS = 4
MOE_PER_GROUP = 4
MOE_EXPERTS = 16
MOE_FF = 512
RMS_EPS = 1e-6
HEAD_NORM_EPS = 1e-5
LN_EPS = 1e-5

COL_Q = 0
COL_K = COL_Q + 1024
COL_V = COL_K + 1024
COL_CVA = COL_V + 1024
COL_CVB = COL_CVA + CV_WIDTH
COL_SSM = COL_CVB + CV_WIDTH
COL_GA = COL_SSM + SSM_WIDTH
COL_GB = COL_GA + D_MODEL
COL_GC = COL_GB + D_MODEL
IN_TOTAL = COL_GC + D_MODEL

LANES = 128
VMEM_LIMIT = 56 * 1024 * 1024

SSM_CHUNK = 16
SSM_NCHUNK = SEQ // SSM_CHUNK
SSM_PAIRS = SSM_GROUPS // 2
SSM_PAIR_W = 2 * SSM_CHUNK * SSM_GROUP


def _params(sem, vmem=VMEM_LIMIT):
    return pltpu.CompilerParams(dimension_semantics=sem, vmem_limit_bytes=vmem)


def _vec_spec(width, layer, ngrid):
    if ngrid == 1:
        return pl.BlockSpec((None, 1, width), lambda i: (layer, 0, 0))
    return pl.BlockSpec((None, 1, width), lambda i, j: (layer, 0, 0))


def _rms(xf, g, eps):
    ms = jnp.mean(xf * xf, axis=-1, keepdims=True)
    return xf * lax.rsqrt(ms + eps) * g


INP_TM = 2048
INP_TN = 512
_Q_TILE0 = COL_Q // INP_TN
_K_TILE0 = COL_K // INP_TN
_V_TILE0 = COL_V // INP_TN


def _inproj_kernel(h_ref, w_ref, cos_ref, sin_ref, o_ref):
    j = pl.program_id(1)
    acc = jnp.dot(h_ref[...], w_ref[...].astype(BF16), preferred_element_type=F32)
    is_rope = jnp.logical_and(j >= _Q_TILE0, j < _V_TILE0)

    @pl.when(is_rope)
    def _():
        scale = jnp.where(j < _K_TILE0, math.log2(math.e) * DA_HEAD_DIM ** -0.5, 1.0).astype(F32)
        cos = cos_ref[...] * scale
        sin = sin_ref[...] * scale
        lane = lax.broadcasted_iota(jnp.int32, (INP_TM, LANES), 1)
        first_half = (lane % DA_HEAD_DIM) < (DA_HEAD_DIM // 2)
        for c in range(INP_TN // LANES):
            t = acc[:, c * LANES:(c + 1) * LANES]
            swapped = jnp.where(first_half, pltpu.roll(t, LANES - 32, 1), pltpu.roll(t, 32, 1))
            o_ref[:, c * LANES:(c + 1) * LANES] = (t * cos + swapped * sin).astype(BF16)

    @pl.when(jnp.logical_not(is_rope))
    def _():
        o_ref[...] = acc.astype(BF16)


def _inproj(h, w_all, layer, cos_t, sin_t):
    t = h.shape[0]
    return pl.pallas_call(
        _inproj_kernel,
        grid=(t // INP_TM, IN_TOTAL // INP_TN),
        in_specs=[
            pl.BlockSpec((INP_TM, D_MODEL), lambda i, j: (i, 0)),
            pl.BlockSpec((None, D_MODEL, INP_TN), lambda i, j: (layer, 0, j)),
            pl.BlockSpec((INP_TM, LANES), lambda i, j: (i, 0)),
            pl.BlockSpec((INP_TM, LANES), lambda i, j: (i, 0)),
        ],
        out_specs=pl.BlockSpec((INP_TM, INP_TN), lambda i, j: (i, j)),
        out_shape=jax.ShapeDtypeStruct((t, IN_TOTAL), BF16),
        compiler_params=_params(("arbitrary", "arbitrary")),
        name="inproj",
    )(h, w_all, cos_t, sin_t)


ATT_TQ = 1024
ATT_TK = 1024


ATT_RG = 1024
ATT_DRG = 512


def _attn_kernel(lam_ref, g_ref, q_ref, k_ref, v_ref, o_ref, qs_ref, m_ref, l_ref, acc_ref):
    i = pl.program_id(1)
    tq = ATT_TQ
    q = q_ref[...]
    lane = lax.broadcasted_iota(jnp.int32, q.shape, 1)
    zero = jnp.zeros_like(q)
    qs_ref[0:tq, :] = jnp.where(lane < DA_HEAD_DIM, q, zero)
    qs_ref[tq:2 * tq, :] = jnp.where(lane >= DA_HEAD_DIM, q, zero)
    m_ref[...] = jnp.full(m_ref.shape, -jnp.inf, F32)
    l_ref[...] = jnp.zeros(l_ref.shape, F32)
    acc_ref[...] = jnp.zeros(acc_ref.shape, F32)

    def update_rows(r0, nrows, k, v, mask):
        n = k.shape[0]
        rows = slice(r0, r0 + nrows)
        s = lax.dot_general(qs_ref[rows, :], k, (((1,), (1,)), ((), ())), preferred_element_type=F32)
        if mask is not None:
            s = jnp.where(mask, s, -jnp.inf)
        tiles = [s[:, t * LANES:(t + 1) * LANES] for t in range(n // LANES)]
        mc = functools.reduce(jnp.maximum, tiles)
        m_old = m_ref[rows, :]
        m_new = jnp.maximum(m_old, jnp.max(mc, axis=1, keepdims=True))
        alpha = jnp.exp2(m_old - m_new)
        p_tiles = [jnp.exp2(t - m_new) for t in tiles]
        l_ref[rows, :] = alpha * l_ref[rows, :] + functools.reduce(jnp.add, p_tiles)
        p = jnp.concatenate(p_tiles, axis=1).astype(BF16)
        acc_ref[rows, :] = alpha * acc_ref[rows, :] + jnp.dot(p, v, preferred_element_type=F32)
        m_ref[rows, :] = m_new

    def body(c, carry):
        off = pl.multiple_of(c * ATT_TK, ATT_TK)
        k = k_ref[pl.ds(off, ATT_TK), :]
        v = v_ref[pl.ds(off, ATT_TK), :]
        for r0 in range(0, 2 * tq, ATT_RG):
            update_rows(r0, ATT_RG, k, v, None)
        return carry

    lax.fori_loop(0, i * (tq // ATT_TK), body, 0)

    off = pl.multiple_of(i * tq, tq)
    for r0 in range(0, 2 * tq, ATT_DRG):
        qo = r0 % tq
        n = qo + ATT_DRG
        row = lax.broadcasted_iota(jnp.int32, (ATT_DRG, n), 0)
        col = lax.broadcasted_iota(jnp.int32, (ATT_DRG, n), 1)
        update_rows(r0, ATT_DRG, k_ref[pl.ds(off, n), :], v_ref[pl.ds(off, n), :], col <= row + qo)

    lam_init = lam_ref[4:5, 0:1]
    lam = (jnp.exp(jnp.sum(lam_ref[0:1, :] * lam_ref[1:2, :], axis=1, keepdims=True))
           - jnp.exp(jnp.sum(lam_ref[2:3, :] * lam_ref[3:4, :], axis=1, keepdims=True))
           + lam_init)
    acc = acc_ref[...]
    inv_l = 1.0 / jnp.sum(l_ref[...], axis=1, keepdims=True)
    o = acc[0:tq] * inv_l[0:tq] - lam * (acc[tq:2 * tq] * inv_l[tq:2 * tq])
    o = _rms(o, g_ref[...], HEAD_NORM_EPS) * (1.0 - lam_init)
    o_ref[...] = o.astype(BF16)


def _attention(proj, lam_pack, head_g, layer):
    t = proj.shape[0]
    qb, kb, vb = COL_Q // LANES, COL_K // LANES, COL_V // LANES
    return pl.pallas_call(
        _attn_kernel,
        grid=(DA_HEADS, t // ATT_TQ),
        in_specs=[
            pl.BlockSpec((None, 8, DA_HEAD_DIM), lambda h, i: (layer, 0, 0)),
            _vec_spec(DA_V_DIM, layer, 2),
            pl.BlockSpec((ATT_TQ, LANES), lambda h, i: (i, qb + h)),
            pl.BlockSpec((t, LANES), lambda h, i: (0, kb + h)),
            pl.BlockSpec((t, LANES), lambda h, i: (0, vb + h)),
        ],
        out_specs=pl.BlockSpec((ATT_TQ, LANES), lambda h, i: (i, h)),
        out_shape=jax.ShapeDtypeStruct((t, DA_WIDTH), BF16),
        scratch_shapes=[
            pltpu.VMEM((2 * ATT_TQ, LANES), BF16),
            pltpu.VMEM((2 * ATT_TQ, LANES), F32),
            pltpu.VMEM((2 * ATT_TQ, LANES), F32),
            pltpu.VMEM((2 * ATT_TQ, LANES), F32),
        ],
        compiler_params=_params(("arbitrary", "arbitrary")),
        name="diff_attn",
    )(lam_pack, head_g, proj, proj, proj)


CV_TM = 512
CV_HALO = 32
CV_ROWS = 128
CV_SUB = 8
CV_SHIFT_ROWS = CV_TM + (CONV_TAPS - 1) // CV_SUB * CV_SUB


def _conv_kernel(a_ref, b_ref, w_ref, bias_ref, g_ref, beta_ref, o_ref, z_ref, zs_ref):
    i = pl.program_id(0)

    @pl.when(i == 0)
    def _():
        z_ref[0:CV_HALO, :] = jnp.zeros((CV_HALO, CV_WIDTH), F32)

    @pl.when(i > 0)
    def _():
        z_ref[0:CV_HALO, :] = z_ref[CV_TM:CV_TM + CV_HALO, :]

    a = a_ref[...].astype(F32)
    b = b_ref[...].astype(F32)
    z_ref[CV_HALO:CV_HALO + CV_TM, :] = a * jax.nn.sigmoid(b)

    base = CV_HALO - (CONV_TAPS - 1)
    for res in range(CV_SUB):
        n = CV_TM + (CONV_TAPS - 1 - res) // CV_SUB * CV_SUB
        zs_ref[res, 0:n, :] = z_ref[base + res:base + res + n, :]
    for r in range(0, CV_TM, CV_ROWS):
        acc = jnp.zeros((CV_ROWS, CV_WIDTH), F32) + bias_ref[...]
        for j in range(CONV_TAPS):
            blk, res = divmod(j, CV_SUB)
            lo = r + blk * CV_SUB
            acc = acc + w_ref[j:j + 1, :] * zs_ref[res, lo:lo + CV_ROWS, :]
        mu = jnp.mean(acc, axis=-1, keepdims=True)
        xc = acc - mu
        var = jnp.mean(xc * xc, axis=-1, keepdims=True)
        y = xc * lax.rsqrt(var + LN_EPS) * g_ref[...] + beta_ref[...]
        o_ref[r:r + CV_ROWS, :] = (y * jax.nn.sigmoid(y)).astype(BF16)


def _conv(proj, dw_w, dw_b, ln_g, ln_b, layer):
    t = proj.shape[0]
    ab, bb = COL_CVA // CV_WIDTH, COL_CVB // CV_WIDTH
    vec = _vec_spec(CV_WIDTH, layer, 1)
    return pl.pallas_call(
        _conv_kernel,
        grid=(t // CV_TM,),
        in_specs=[
            pl.BlockSpec((CV_TM, CV_WIDTH), lambda i: (i, ab)),
            pl.BlockSpec((CV_TM, CV_WIDTH), lambda i: (i, bb)),
            pl.BlockSpec((None, CONV_TAPS, CV_WIDTH), lambda i: (layer, 0, 0)),
            vec, vec, vec,
        ],
        out_specs=pl.BlockSpec((CV_TM, CV_WIDTH), lambda i: (i, 0)),
        out_shape=jax.ShapeDtypeStruct((t, CV_WIDTH), BF16),
        scratch_shapes=[pltpu.VMEM((CV_HALO + CV_TM, CV_WIDTH), F32),
                        pltpu.VMEM((CV_SUB, CV_SHIFT_ROWS, CV_WIDTH), F32)],
        compiler_params=_params(("arbitrary",)),
        name="conformer_conv",
    )(proj, proj, dw_w, dw_b, ln_g, ln_b)


SSM_CB = 128
SSM_RB = SSM_CB * SSM_CHUNK
SSM_SLAB = 2 * SSM_GROUP
SSM_SLABS = LANES // SSM_SLAB


def _pick_slabs(pieces, src_slab, lane_slab):
    out = None
    for k, piece in enumerate(pieces):
        shift = ((k - src_slab) * SSM_SLAB) % LANES
        moved = piece if shift == 0 else pltpu.roll(piece, shift, 1)
        out = moved if out is None else jnp.where(lane_slab == k, moved, out)
    return out


def _ssm_kernel(u_ref, t_ref, wre_ref, wim_ref, vre_ref, vim_ref, ar_ref, ai_ref, d_ref, y_ref,
                uf_ref, up_ref, sre_ref, sim_ref, yp_ref, yf_ref, xr_ref, xi_ref):
    npair, cb = SSM_PAIRS, SSM_CB

    @pl.when(pl.program_id(0) == 0)
    def _():
        xr_ref[...] = jnp.zeros(xr_ref.shape, F32)
        xi_ref[...] = jnp.zeros(xi_ref.shape, F32)

    ntile = SSM_WIDTH // LANES
    for j in range(ntile):
        uf_ref[j] = u_ref[:, j * LANES:(j + 1) * LANES].astype(F32)
    lane_slab = lax.shift_right_logical(lax.broadcasted_iota(jnp.int32, (cb, LANES), 1), 5)

    for q in range(SSM_CHUNK // SSM_SLABS):
        for tile in range(ntile):
            src = uf_ref.at[tile]
            pieces = [src[pl.ds(SSM_SLABS * q + k, cb, stride=SSM_CHUNK), :] for k in range(SSM_SLABS)]
            for slab in range(SSM_SLABS):
                up_ref[SSM_SLABS * tile + slab, :, q * LANES:(q + 1) * LANES] = _pick_slabs(
                    pieces, slab, lane_slab).astype(BF16)

    for p in range(npair):
        u = up_ref[p]
        sre_ref[pl.ds(p, cb, stride=npair), :] = jnp.dot(u, wre_ref[p], preferred_element_type=F32)
        sim_ref[pl.ds(p, cb, stride=npair), :] = jnp.dot(u, wim_ref[p], preferred_element_type=F32)

    ar = ar_ref[...]
    ai = ai_ref[...]

    def body(c, carry):
        xr, xi = carry
        off = pl.multiple_of(c * npair, npair)
        sr = sre_ref[pl.ds(off, npair), :]
        si = sim_ref[pl.ds(off, npair), :]
        sre_ref[pl.ds(off, npair), :] = xr
        sim_ref[pl.ds(off, npair), :] = xi
        return ar * xr - ai * xi + sr, ar * xi + ai * xr + si

    xr, xi = lax.fori_loop(0, cb, body, (xr_ref[...], xi_ref[...]), unroll=8)
    xr_ref[...] = xr
    xi_ref[...] = xi

    for p in range(npair):
        u = up_ref[p]
        sr = sre_ref[pl.ds(p, cb, stride=npair), :].astype(BF16)
        si = sim_ref[pl.ds(p, cb, stride=npair), :].astype(BF16)
        nt = (((1,), (1,)), ((), ()))
        y = jnp.dot(u, t_ref[p], preferred_element_type=F32)
        y = y + lax.dot_general(sr, vre_ref[p], nt, preferred_element_type=F32)
        yp_ref[p] = y + lax.dot_general(si, vim_ref[p], nt, preferred_element_type=F32)

    for tile in range(ntile):
        for q in range(SSM_CHUNK // SSM_SLABS):
            pieces = [yp_ref[SSM_SLABS * tile + k, :, q * LANES:(q + 1) * LANES] for k in range(SSM_SLABS)]
            dst = yf_ref.at[tile]
            for slab in range(SSM_SLABS):
                dst[pl.ds(SSM_SLABS * q + slab, cb, stride=SSM_CHUNK), :] = _pick_slabs(pieces, slab, lane_slab)

    for j in range(ntile):
        cols = slice(j * LANES, (j + 1) * LANES)
        y_ref[:, cols] = (yf_ref[j] + uf_ref[j] * d_ref[:, cols]).astype(BF16)


def _ssm(proj, mats):
    t = proj.shape[0]
    vm = pl.BlockSpec(memory_space=pltpu.VMEM)
    return pl.pallas_call(
        _ssm_kernel,
        grid=(t // SSM_RB,),
        in_specs=[pl.BlockSpec((SSM_RB, SSM_WIDTH), lambda i: (i, COL_SSM // SSM_WIDTH))] + [vm] * 8,
        out_specs=pl.BlockSpec((SSM_RB, SSM_WIDTH), lambda i: (i, 0)),
        out_shape=jax.ShapeDtypeStruct((t, SSM_WIDTH), BF16),
        scratch_shapes=[
            pltpu.VMEM((SSM_WIDTH // LANES, SSM_RB, LANES), F32),
            pltpu.VMEM((SSM_PAIRS, SSM_CB, SSM_PAIR_W), BF16),
            pltpu.VMEM((SSM_CB * SSM_PAIRS, LANES), F32),
            pltpu.VMEM((SSM_CB * SSM_PAIRS, LANES), F32),
            pltpu.VMEM((SSM_PAIRS, SSM_CB, SSM_PAIR_W), F32),
            pltpu.VMEM((SSM_WIDTH // LANES, SSM_RB, LANES), F32),
            pltpu.VMEM((SSM_PAIRS, LANES), F32),
            pltpu.VMEM((SSM_PAIRS, LANES), F32),
        ],
        compiler_params=_params(("arbitrary",)),
        name="s5_scan",
    )(proj, *mats)


def _ssm_gen_kernel(bre_ref, bim_ref, cre_ref, cim_ref, pr_ref, pi_ref,
                    t_ref, wre_ref, wim_ref, vre_ref, vim_ref, car_ref, cai_ref):
    bre, bim, cre, cim = bre_ref[...], bim_ref[...], cre_ref[...], cim_ref[...]
    for tau in range(SSM_CHUNK):
        rows = slice(tau * SSM_SLAB, (tau + 1) * SSM_SLAB)
        pr = pr_ref[tau:tau + 1, :]
        pi = pi_ref[tau:tau + 1, :]
        car_ref[rows, :] = cre * pr - cim * pi
        cai_ref[rows, :] = cre * pi + cim * pr
    nt = (((1,), (1,)), ((), ()))
    hp = lax.Precision.HIGHEST
    k = (lax.dot_general(bre, car_ref[...], nt, precision=hp, preferred_element_type=F32)
         - lax.dot_general(bim, cai_ref[...], nt, precision=hp, preferred_element_type=F32))
    lane = lax.broadcasted_iota(jnp.int32, k.shape, 1)
    for s in range(SSM_CHUNK):
        rows = slice(s * SSM_SLAB, (s + 1) * SSM_SLAB)
        moved = k if s == 0 else pltpu.roll(k, s * SSM_SLAB, 1)
        t_ref[rows, :] = jnp.where(lane >= s * SSM_SLAB, moved, 0.0).astype(BF16)
        pr = pr_ref[SSM_CHUNK - 1 - s:SSM_CHUNK - s, :]
        pi = pi_ref[SSM_CHUNK - 1 - s:SSM_CHUNK - s, :]
        wre_ref[rows, :] = (bre * pr - bim * pi).astype(BF16)
        wim_ref[rows, :] = (bre * pi + bim * pr).astype(BF16)
        qr = pr_ref[s + 1:s + 2, :]
        qi = pi_ref[s + 1:s + 2, :]
        vre_ref[rows, :] = (cre * qr - cim * qi).astype(BF16)
        vim_ref[rows, :] = (-(cre * qi + cim * qr)).astype(BF16)


def _ssm_matrices(lam_re, lam_im, log_dt, b_re, b_im, c_re, c_im, d_skip):
    nl = lam_re.shape[0]
    lr = lam_re.astype(F32)
    li = lam_im.astype(F32)
    dt = jnp.exp(log_dt.astype(F32))[..., None]
    mag = jnp.exp(lr * dt)
    ab_re = mag * jnp.cos(li * dt)
    ab_im = mag * jnp.sin(li * dt)
    den = lr * lr + li * li
    f_re = ((ab_re - 1.0) * lr + ab_im * li) / den
    f_im = (ab_im * lr - (ab_re - 1.0) * li) / den
    br = b_re.astype(F32)
    bi = b_im.astype(F32)
    bb_re = f_re[..., None] * br - f_im[..., None] * bi
    bb_im = f_re[..., None] * bi + f_im[..., None] * br
    tau = jnp.arange(SSM_CHUNK + 1, dtype=F32)[None, None, :, None]
    pmag = jnp.exp(tau * (lr * dt)[:, :, None, :])
    pw_re = pmag * jnp.cos(tau * (li * dt)[:, :, None, :])
    pw_im = pmag * jnp.sin(tau * (li * dt)[:, :, None, :])
    cr = c_re.astype(F32)
    ci = c_im.astype(F32)
    eye2 = jnp.eye(2, dtype=F32)

    def paired(a):
        return a.reshape((nl, SSM_PAIRS, 2) + a.shape[2:])

    bbd_re = jnp.einsum('lpenk,ef->lpekfn', paired(bb_re), eye2).reshape(nl, SSM_PAIRS, SSM_SLAB, LANES)
    bbd_im = jnp.einsum('lpenk,ef->lpekfn', paired(bb_im), eye2).reshape(nl, SSM_PAIRS, SSM_SLAB, LANES)
    cbd_re = jnp.einsum('lpehn,ef->lpehfn', paired(cr), eye2).reshape(nl, SSM_PAIRS, SSM_SLAB, LANES)
    cbd_im = jnp.einsum('lpehn,ef->lpehfn', paired(ci), eye2).reshape(nl, SSM_PAIRS, SSM_SLAB, LANES)
    pwp_re = paired(pw_re).transpose(0, 1, 3, 2, 4).reshape(nl, SSM_PAIRS, SSM_CHUNK + 1, LANES)
    pwp_im = paired(pw_im).transpose(0, 1, 3, 2, 4).reshape(nl, SSM_PAIRS, SSM_CHUNK + 1, LANES)

    def spec(r, c):
        return pl.BlockSpec((None, None, r, c), lambda l, p: (l, p, 0, 0))

    wide = jax.ShapeDtypeStruct((nl, SSM_PAIRS, SSM_PAIR_W, SSM_PAIR_W), BF16)
    tall = jax.ShapeDtypeStruct((nl, SSM_PAIRS, SSM_PAIR_W, LANES), BF16)
    tm, w_re, w_im, vt_re, vt_im = pl.pallas_call(
        _ssm_gen_kernel,
        grid=(nl, SSM_PAIRS),
        in_specs=[spec(SSM_SLAB, LANES)] * 4 + [spec(SSM_CHUNK + 1, LANES)] * 2,
        out_specs=[spec(SSM_PAIR_W, SSM_PAIR_W)] + [spec(SSM_PAIR_W, LANES)] * 4,
        out_shape=[wide, tall, tall, tall, tall],
        scratch_shapes=[pltpu.VMEM((SSM_PAIR_W, LANES), F32), pltpu.VMEM((SSM_PAIR_W, LANES), F32)],
        compiler_params=_params(("arbitrary", "arbitrary")),
        name="s5_matrices",
    )(bbd_re, bbd_im, cbd_re, cbd_im, pwp_re, pwp_im)
    a_re = pwp_re[:, :, SSM_CHUNK]
    a_im = pwp_im[:, :, SSM_CHUNK]
    d_t = d_skip.astype(F32).reshape(nl, 1, SSM_WIDTH)
    return tm, w_re, w_im, vt_re, vt_im, a_re, a_im, d_t


MG_TM = 512
MG_TN = 1024
MG_GW = MG_TN // 2


def _merge_mix_kernel(oa_ref, zb_ref, yc_ref, ga0_ref, ga1_ref, gb0_ref, gb1_ref, gc0_ref, gc1_ref,
                      wda_ref, wcv_ref, wglu_ref, bglu_ref, wso_ref, wmix_ref, x_ref, o_ref, sc_ref):
    n = pl.program_id(1)

    def gate(lo_ref, hi_ref):
        return jax.nn.sigmoid(jnp.concatenate([lo_ref[...], hi_ref[...]], axis=1).astype(F32))

    @pl.when(n == 0)
    def _():
        glu = jnp.dot(yc_ref[...], wglu_ref[...], preferred_element_type=F32) + bglu_ref[...]
        sc_ref[...] = (glu[:, :SSM_WIDTH] * jax.nn.sigmoid(glu[:, SSM_WIDTH:])).astype(BF16)
        o_ref[...] = x_ref[...]

    y_a = jnp.dot(oa_ref[...], wda_ref[...], preferred_element_type=F32)
    y_b = jnp.dot(zb_ref[...], wcv_ref[...], preferred_element_type=F32)
    y_c = jnp.dot(sc_ref[...], wso_ref[...], preferred_element_type=F32)
    merged = gate(ga0_ref, ga1_ref) * y_a + gate(gb0_ref, gb1_ref) * y_b + gate(gc0_ref, gc1_ref) * y_c
    o_ref[...] += jnp.dot(merged.astype(BF16), wmix_ref[...], preferred_element_type=F32)


def _merge_mix(o_a, z_b, y_c, proj, w_da, w_cv, w_glu, b_glu, w_so, w_mix, x, layer):
    t = o_a.shape[0]

    def wcol(k):
        return pl.BlockSpec((None, k, MG_TN), lambda i, n: (layer, 0, n))

    def gate_halves(col0):
        first = col0 // MG_GW
        return [pl.BlockSpec((MG_TM, MG_GW), lambda i, n, h=h: (i, first + 2 * n + h)) for h in range(2)]

    return pl.pallas_call(
        _merge_mix_kernel,
        grid=(t // MG_TM, D_MODEL // MG_TN),
        in_specs=[
            pl.BlockSpec((MG_TM, DA_WIDTH), lambda i, n: (i, 0)),
            pl.BlockSpec((MG_TM, CV_WIDTH), lambda i, n: (i, 0)),
            pl.BlockSpec((MG_TM, SSM_WIDTH), lambda i, n: (i, 0)),
            *gate_halves(COL_GA), *gate_halves(COL_GB), *gate_halves(COL_GC),
            wcol(DA_WIDTH), wcol(CV_WIDTH),
            pl.BlockSpec((None, SSM_WIDTH, 2 * SSM_WIDTH), lambda i, n: (layer, 0, 0)),
            _vec_spec(2 * SSM_WIDTH, layer, 2),
            wcol(SSM_WIDTH),
            pl.BlockSpec((None, MG_TN, D_MODEL), lambda i, n: (layer, n, 0)),
            pl.BlockSpec((MG_TM, D_MODEL), lambda i, n: (i, 0)),
        ],
        out_specs=pl.BlockSpec((MG_TM, D_MODEL), lambda i, n: (i, 0)),
        out_shape=jax.ShapeDtypeStruct((t, D_MODEL), F32),
        scratch_shapes=[pltpu.VMEM((MG_TM, SSM_WIDTH), BF16)],
        compiler_params=_params(("arbitrary", "arbitrary")),
        name="merge_mix",
    )(o_a, z_b, y_c, *([proj] * 6), w_da, w_cv, w_glu, b_glu, w_so, w_mix, x)


def _norm_matmul_kernel(x_ref, g_ref, w_ref, o_ref):
    h = _rms(x_ref[...], g_ref[...], RMS_EPS).astype(BF16)
    o_ref[...] = jnp.dot(h, w_ref[...], preferred_element_type=F32).astype(o_ref.dtype)


def _mem_kv(mem, g_all, w_all, layer):
    m = mem.shape[0]
    n = 2 * XA_WIDTH
    return pl.pallas_call(
        _norm_matmul_kernel,
        grid=(1,),
        in_specs=[
            pl.BlockSpec((m, D_MODEL), lambda i: (0, 0)),
            _vec_spec(D_MODEL, layer, 1),
            pl.BlockSpec((None, D_MODEL, n), lambda i: (layer, 0, 0)),
        ],
        out_specs=pl.BlockSpec((m, n), lambda i: (0, 0)),
        out_shape=jax.ShapeDtypeStruct((m, n), BF16),
        compiler_params=_params(("arbitrary",)),
        name="mem_kv",
    )(mem, g_all, w_all)


XA_TM = 1024


def _xattn_kernel(x_ref, g_ref, wq_ref, kv_ref, wo_ref, o_ref):
    x = x_ref[...]
    h = _rms(x, g_ref[...], RMS_EPS).astype(BF16)
    q = jnp.dot(h, wq_ref[...], preferred_element_type=F32).astype(BF16)
    heads = []
    for hd in range(XA_HEADS):
        lo = hd * XA_HEAD_DIM
        k = kv_ref[:, lo:lo + XA_HEAD_DIM]
        v = kv_ref[:, XA_WIDTH + lo:XA_WIDTH + lo + XA_HEAD_DIM]
        s = lax.dot_general(q[:, lo:lo + XA_HEAD_DIM], k, (((1,), (1,)), ((), ())),
                            preferred_element_type=F32) * (XA_HEAD_DIM ** -0.5)
        m = jnp.max(s, axis=-1, keepdims=True)
        e = jnp.exp(s - m)
        p = e / jnp.sum(e, axis=-1, keepdims=True)
        heads.append(jnp.dot(p.astype(BF16), v, preferred_element_type=F32).astype(BF16))
    o = jnp.concatenate(heads, axis=-1)
    o_ref[...] = x + jnp.dot(o, wo_ref[...], preferred_element_type=F32)


def _xattn(x, g_all, wq_all, kv, wo_all, layer):
    t = x.shape[0]
    return pl.pallas_call(
        _xattn_kernel,
        grid=(t // XA_TM,),
        in_specs=[
            pl.BlockSpec((XA_TM, D_MODEL), lambda i: (i, 0)),
            _vec_spec(D_MODEL, layer, 1),
            pl.BlockSpec((None, D_MODEL, XA_WIDTH), lambda i: (layer, 0, 0), pipeline_mode=pl.Buffered(1)),
            pl.BlockSpec((MEM_LEN, 2 * XA_WIDTH), lambda i: (0, 0), pipeline_mode=pl.Buffered(1)),
            pl.BlockSpec((None, XA_WIDTH, D_MODEL), lambda i: (layer, 0, 0), pipeline_mode=pl.Buffered(1)),
        ],
        out_specs=pl.BlockSpec((XA_TM, D_MODEL), lambda i: (i, 0)),
        out_shape=jax.ShapeDtypeStruct((t, D_MODEL), F32),
        compiler_params=_params(("arbitrary",)),
        name="mem_xattn",
    )(x, g_all, wq_all, kv, wo_all)


RT_TM = 512
_E_LANE0 = MOE_GROUPS


def _router_kernel(x_ref, g_ref, w_ref, b_ref, meta_ref, cnt_ref, run_ref):
    i = pl.program_id(0)

    @pl.when(i == 0)
    def _():
        run_ref[...] = jnp.zeros(run_ref.shape, F32)

    h = _rms(x_ref[...], g_ref[...], RMS_EPS).astype(BF16)
    logits = jnp.dot(h, w_ref[...], preferred_element_type=F32) + b_ref[...]
    lane = lax.broadcasted_iota(jnp.int32, logits.shape, 1).astype(F32)
    neg = jnp.float32(-jnp.inf)
    big = jnp.float32(LANES)

    def first_argmax(vals):
        top = jnp.max(vals, axis=-1, keepdims=True)
        idx = jnp.min(jnp.where(vals == top, lane, big), axis=-1, keepdims=True)
        return top, idx

    gl = jnp.where(lane < MOE_GROUPS, logits, neg)
    g_top, g_idx = first_argmax(gl)
    g_w = 1.0 / jnp.sum(jnp.exp(gl - g_top), axis=-1, keepdims=True)
    e_lane = lane - _E_LANE0
    in_group = jnp.logical_and(e_lane >= g_idx * MOE_PER_GROUP, e_lane < (g_idx + 1) * MOE_PER_GROUP)
    el = jnp.where(in_group, logits, neg)
    v1, i1 = first_argmax(el)
    el2 = jnp.where(lane == i1, neg, el)
    v2, i2 = first_argmax(el2)
    e2 = jnp.exp(v2 - v1)
    w1 = 1.0 / (1.0 + e2)
    w2 = e2 / (1.0 + e2)
    oh1 = (lane == i1).astype(F32)
    oh2 = (lane == i2).astype(F32)
    both = oh1 + oh2
    row = lax.broadcasted_iota(jnp.int32, (RT_TM, RT_TM), 0)
    col = lax.broadcasted_iota(jnp.int32, (RT_TM, RT_TM), 1)
    earlier = jnp.where(col < row, 1.0, 0.0).astype(BF16)
    before = jnp.dot(earlier, both.astype(BF16), preferred_element_type=F32) + run_ref[...]
    rank1 = jnp.sum(oh1 * before, axis=-1, keepdims=True)
    rank2 = jnp.sum(oh2 * before, axis=-1, keepdims=True)
    run_ref[...] += jnp.sum(both, axis=0, keepdims=True)
    cnt_ref[...] = run_ref[...]
    meta = jnp.where(lane == 0.0, i1 - _E_LANE0, 0.0)
    for k, val in enumerate((i2 - _E_LANE0, rank1, rank2, w1 * g_w, w2 * g_w), start=1):
        meta = jnp.where(lane == float(k), val, meta)
    meta_ref[...] = meta


def _router(x, g_all, w_r, b_r, layer):
    t = x.shape[0]
    return pl.pallas_call(
        _router_kernel,
        grid=(t // RT_TM,),
        in_specs=[
            pl.BlockSpec((RT_TM, D_MODEL), lambda i: (i, 0)),
            _vec_spec(D_MODEL, layer, 1),
            pl.BlockSpec((None, D_MODEL, LANES), lambda i: (layer, 0, 0)),
            _vec_spec(LANES, layer, 1),
        ],
        out_specs=[
            pl.BlockSpec((RT_TM, LANES), lambda i: (i, 0)),
            pl.BlockSpec((1, LANES), lambda i: (0, 0)),
        ],
        out_shape=[jax.ShapeDtypeStruct((t, LANES), F32), jax.ShapeDtypeStruct((1, LANES), F32)],
        scratch_shapes=[pltpu.VMEM((1, LANES), F32)],
        compiler_params=_params(("arbitrary",)),
        name="moe_router",
    )(x, g_all, w_r, b_r)


EX_TM = 256
EX_ROWS = 2 * SEQ
EX_TILES = EX_ROWS // EX_TM
EX_STEPS = EX_TILES + MOE_EXPERTS - 1


def _row_copy(src, src_row, dst, dst_row, sem):
    return pltpu.make_async_copy(src.at[pl.ds(src_row, 1), :], dst.at[pl.ds(dst_row, 1), :], sem)


DP_TM = 256


def _dispatch_kernel(p1_ref, p2_ref, x_ref, xs_hbm, sem):
    base = pl.program_id(0) * DP_TM

    def issue(j, c):
        _row_copy(x_ref, j, xs_hbm, p1_ref[base + j], sem).start(priority=0)
        _row_copy(x_ref, j, xs_hbm, p2_ref[base + j], sem).start(priority=1)
        return c

    lax.fori_loop(0, DP_TM, issue, 0, unroll=True)

    def wait(j, c):
        _row_copy(x_ref, j, xs_hbm, 0, sem).wait()
        _row_copy(x_ref, j, xs_hbm, 0, sem).wait()
        return c

    lax.fori_loop(0, DP_TM, wait, 0, unroll=8)


def _dispatch(pos1, pos2, x):
    t = x.shape[0]
    grid_spec = pltpu.PrefetchScalarGridSpec(
        num_scalar_prefetch=2,
        grid=(t // DP_TM,),
        in_specs=[pl.BlockSpec((DP_TM, D_MODEL), lambda i, p1, p2: (i, 0))],
        out_specs=pl.BlockSpec(memory_space=pl.ANY),
        scratch_shapes=[pltpu.SemaphoreType.DMA(())],
    )
    return pl.pallas_call(
        _dispatch_kernel,
        grid_spec=grid_spec,
        out_shape=jax.ShapeDtypeStruct((EX_ROWS, D_MODEL), F32),
        compiler_params=_params(("arbitrary",)),
        name="moe_dispatch",
    )(pos1, pos2, x)


def _experts_kernel(se_ref, st_ref, lo_ref, hi_ref, first_ref, wnew_ref, x_ref, g_ref, wg_ref, wu_ref,
                    wd_ref, o_ref, wgb_ref, wub_ref, wdb_ref):
    s = pl.program_id(0)
    lo = lo_ref[s]
    hi = hi_ref[s]

    @pl.when(wnew_ref[s] == 1)
    def _():
        wgb_ref[...] = wg_ref[...].astype(BF16)
        wub_ref[...] = wu_ref[...].astype(BF16)
        wdb_ref[...] = wd_ref[...].astype(BF16)

    @pl.when(hi > lo)
    def _():
        h = _rms(x_ref[...], g_ref[...], RMS_EPS).astype(BF16)
        gate = jnp.dot(h, wgb_ref[...], preferred_element_type=F32)
        up = jnp.dot(h, wub_ref[...], preferred_element_type=F32)
        act = (gate * jax.nn.sigmoid(gate) * up).astype(BF16)
        res = jnp.dot(act, wdb_ref[...], preferred_element_type=F32)
        row = lax.broadcasted_iota(jnp.int32, (EX_TM, 1), 0)
        mine = jnp.logical_and(row >= lo, row < hi)

        @pl.when(first_ref[s] == 1)
        def _():
            o_ref[...] = jnp.where(mine, res, 0.0)

        @pl.when(first_ref[s] == 0)
        def _():
            o_ref[...] = jnp.where(mine, res, o_ref[...])


def _experts(plan, xs_sorted, g_all, wg_all, wu_all, wd_all, layer):
    def wspec(k, n):
        return pl.BlockSpec((None, None, k, n), lambda s, se, st, lo, hi, fi, wn: (layer, se[s], 0, 0))

    grid_spec = pltpu.PrefetchScalarGridSpec(
        num_scalar_prefetch=6,
        grid=(EX_STEPS,),
        in_specs=[
            pl.BlockSpec((EX_TM, D_MODEL), lambda s, se, st, lo, hi, fi, wn: (st[s], 0)),
            pl.BlockSpec((None, 1, D_MODEL), lambda s, se, st, lo, hi, fi, wn: (layer, 0, 0)),
            wspec(D_MODEL, MOE_FF), wspec(D_MODEL, MOE_FF), wspec(MOE_FF, D_MODEL),
        ],
        out_specs=pl.BlockSpec((EX_TM, D_MODEL), lambda s, se, st, lo, hi, fi, wn: (st[s], 0)),
        scratch_shapes=[pltpu.VMEM((D_MODEL, MOE_FF), BF16), pltpu.VMEM((D_MODEL, MOE_FF), BF16),
                        pltpu.VMEM((MOE_FF, D_MODEL), BF16)],
    )
    return pl.pallas_call(
        _experts_kernel,
        grid_spec=grid_spec,
        out_shape=jax.ShapeDtypeStruct((EX_ROWS, D_MODEL), F32),
        compiler_params=_params(("arbitrary",)),
        name="moe_experts",
    )(*plan, xs_sorted, g_all, wg_all, wu_all, wd_all)


CB_TM = 256


def _combine_kernel(p1_ref, p2_ref, y_hbm, x_ref, meta_ref, g_ref, o_ref, n_ref, y1buf, y2buf, sem):
    i = pl.program_id(0)
    slot = lax.rem(i, 2)

    def fetch(tile, slot_):
        base = tile * CB_TM

        def issue(j, c):
            _row_copy(y_hbm, p1_ref[base + j], y1buf.at[slot_], j, sem.at[slot_]).start(priority=0)
            _row_copy(y_hbm, p2_ref[base + j], y2buf.at[slot_], j, sem.at[slot_]).start(priority=1)
            return c

        lax.fori_loop(0, CB_TM, issue, 0, unroll=True)

    @pl.when(i == 0)
    def _():
        fetch(0, 0)

    @pl.when(i + 1 < pl.num_programs(0))
    def _():
        fetch(i + 1, 1 - slot)

    def wait(j, c):
        _row_copy(y_hbm, 0, y1buf.at[slot], j, sem.at[slot]).wait()
        _row_copy(y_hbm, 0, y2buf.at[slot], j, sem.at[slot]).wait()
        return c

    lax.fori_loop(0, CB_TM, wait, 0, unroll=8)
    meta = meta_ref[...]
    x_new = x_ref[...] + meta[:, 4:5] * y1buf[slot] + meta[:, 5:6] * y2buf[slot]
    o_ref[...] = x_new
    n_ref[...] = _rms(x_new, g_ref[...], RMS_EPS).astype(n_ref.dtype)


def _combine(pos1, pos2, ys, x, meta, g_next, next_dtype):
    t = x.shape[0]
    tile = pl.BlockSpec((CB_TM, D_MODEL), lambda i, p1, p2: (i, 0))
    grid_spec = pltpu.PrefetchScalarGridSpec(
        num_scalar_prefetch=2,
        grid=(t // CB_TM,),
        in_specs=[
            pl.BlockSpec(memory_space=pl.ANY),
            tile,
            pl.BlockSpec((CB_TM, LANES), lambda i, p1, p2: (i, 0)),
            pl.BlockSpec((1, D_MODEL), lambda i, p1, p2: (0, 0)),
        ],
        out_specs=[tile, tile],
        scratch_shapes=[pltpu.VMEM((2, CB_TM, D_MODEL), F32), pltpu.VMEM((2, CB_TM, D_MODEL), F32),
                        pltpu.SemaphoreType.DMA((2,))],
    )
    return pl.pallas_call(
        _combine_kernel,
        grid_spec=grid_spec,
        out_shape=[jax.ShapeDtypeStruct((t, D_MODEL), F32), jax.ShapeDtypeStruct((t, D_MODEL), next_dtype)],
        compiler_params=_params(("arbitrary",)),
        name="moe_combine",
    )(pos1, pos2, ys, x, meta, g_next)


def _dispatch_plan(meta, cnt):
    t = meta.shape[0]
    e1 = meta[:, 0].astype(jnp.int32)
    e2 = meta[:, 1].astype(jnp.int32)
    r1 = meta[:, 2].astype(jnp.int32)
    r2 = meta[:, 3].astype(jnp.int32)
    counts = cnt[0, _E_LANE0:_E_LANE0 + MOE_EXPERTS].astype(jnp.int32)
    ends = jnp.cumsum(counts)
    starts = ends - counts
    pos1 = starts[e1] + r1
    pos2 = starts[e2] + r2
    t_lo = starts // EX_TM
    t_hi = (ends + EX_TM - 1) // EX_TM
    nsteps = jnp.where(counts > 0, t_hi - t_lo, 0)
    step_end = jnp.cumsum(nsteps)
    step_start = step_end - nsteps
    s = jnp.arange(EX_STEPS, dtype=jnp.int32)
    se = jnp.minimum(jnp.sum((s[:, None] >= step_end[None, :]).astype(jnp.int32), axis=1), MOE_EXPERTS - 1)
    valid = s < step_end[-1]
    st = jnp.where(valid, t_lo[se] + s - step_start[se], EX_TILES - 1)
    lo = jnp.where(valid, jnp.clip(starts[se] - st * EX_TM, 0, EX_TM), 0)
    hi = jnp.where(valid, jnp.clip(ends[se] - st * EX_TM, 0, EX_TM), 0)
    first = jnp.concatenate([jnp.ones((1,), jnp.int32), (st[1:] != st[:-1]).astype(jnp.int32)])
    wnew = jnp.concatenate([jnp.ones((1,), jnp.int32), (se[1:] != se[:-1]).astype(jnp.int32)])
    plan = tuple(a.astype(jnp.int32) for a in (se, st, lo, hi, first, wnew))
    return pos1.astype(jnp.int32), pos2.astype(jnp.int32), plan


FN_TM = 512


def _norm_kernel(x_ref, g_ref, o_ref):
    o_ref[...] = _rms(x_ref[...], g_ref[...], RMS_EPS).astype(o_ref.dtype)


def _first_norm(x, g_all):
    t = x.shape[0]
    return pl.pallas_call(
        _norm_kernel,
        grid=(t // FN_TM,),
        in_specs=[pl.BlockSpec((FN_TM, D_MODEL), lambda i: (i, 0)), _vec_spec(D_MODEL, 0, 1)],
        out_specs=pl.BlockSpec((FN_TM, D_MODEL), lambda i: (i, 0)),
        out_shape=jax.ShapeDtypeStruct((t, D_MODEL), BF16),
        compiler_params=_params(("arbitrary",)),
        name="first_norm",
    )(x, g_all)


def kernel(x, mem, positions, norm_mix, w_in, da_lam_q1, da_lam_k1, da_lam_q2, da_lam_k2, da_head_norm, w_da_out, cv_dw_w, cv_dw_b, cv_ln_g, cv_ln_b, w_cv_out, ssm_lam_re, ssm_lam_im, ssm_log_dt, ssm_b_re, ssm_b_im, ssm_c_re, ssm_c_im, ssm_d, w_ssm_glu, b_ssm_glu, w_ssm_out, w_mix_out, norm_xa, norm_mem, w_xa_q, w_xa_kv, w_xa_out, norm_ffn, w_router_group, b_router_group, w_router_expert, b_router_expert, w_exp_gate, w_exp_up, w_exp_down, norm_final):
    bsz, seq, _ = x.shape
    assert bsz == 1 and seq == SEQ
    nl = w_in.shape[0]
    xs = x.reshape(seq, D_MODEL).astype(F32)
    mem2 = mem.reshape(MEM_LEN, D_MODEL).astype(F32)

    inv_freq = ROPE_THETA ** (-jnp.arange(0, DA_HEAD_DIM, 2, dtype=F32) / DA_HEAD_DIM)
    ang = positions.reshape(seq).astype(F32)[:, None] * inv_freq
    cos = jnp.cos(ang)
    sin = jnp.sin(ang)
    cos_t = jnp.concatenate([cos, cos, cos, cos], axis=-1)
    sin_t = jnp.concatenate([-sin, sin, -sin, sin], axis=-1)

    w_da_b = w_da_out.astype(BF16)
    w_cv_b = w_cv_out.astype(BF16)
    w_glu_b = w_ssm_glu.astype(BF16)
    w_so_b = w_ssm_out.astype(BF16)
    w_mix_b = w_mix_out.astype(BF16)
    w_xq_b = w_xa_q.astype(BF16)
    w_xkv_b = w_xa_kv.astype(BF16)
    w_xo_b = w_xa_out.astype(BF16)
    pad = LANES - MOE_GROUPS - MOE_EXPERTS
    w_r = jnp.concatenate([w_router_group, w_router_expert,
                           jnp.zeros((nl, D_MODEL, pad), F32)], axis=-1).astype(BF16)
    b_r = jnp.concatenate([b_router_group, b_router_expert, jnp.zeros((nl, pad), F32)], axis=-1).astype(F32)

    def vec3(a):
        return a.astype(F32).reshape(nl, 1, a.shape[-1])

    norm_mix, da_head_norm, cv_dw_b, cv_ln_g, cv_ln_b, b_ssm_glu, norm_xa, norm_mem, norm_ffn, b_r = map(
        vec3, (norm_mix, da_head_norm, cv_dw_b, cv_ln_g, cv_ln_b, b_ssm_glu, norm_xa, norm_mem, norm_ffn, b_r))
    cv_dw_w = cv_dw_w.astype(F32)

    lam_inits = jnp.asarray([0.8 - 0.6 * math.exp(-0.3 * l) for l in range(nl)], F32)
    lam_pack = jnp.stack([da_lam_q1, da_lam_k1, da_lam_q2, da_lam_k2], axis=1).astype(F32)
    lam_pack = jnp.concatenate(
        [lam_pack, jnp.broadcast_to(lam_inits[:, None, None], (nl, 4, DA_HEAD_DIM))], axis=1)

    ssm_mats = _ssm_matrices(ssm_lam_re, ssm_lam_im, ssm_log_dt, ssm_b_re, ssm_b_im,
                             ssm_c_re, ssm_c_im, ssm_d)

    h = _first_norm(xs, norm_mix)
    for l in range(nl):
        proj = _inproj(h, w_in, l, cos_t, sin_t)
        o_a = _attention(proj, lam_pack, da_head_norm, l)
        z_b = _conv(proj, cv_dw_w, cv_dw_b, cv_ln_g, cv_ln_b, l)
        y_c = _ssm(proj, [m[l] for m in ssm_mats])
        xs = _merge_mix(o_a, z_b, y_c, proj, w_da_b, w_cv_b, w_glu_b, b_ssm_glu, w_so_b, w_mix_b, xs, l)
        kv = _mem_kv(mem2, norm_mem, w_xkv_b, l)
        xs = _xattn(xs, norm_xa, w_xq_b, kv, w_xo_b, l)
        meta, cnt = _router(xs, norm_ffn, w_r, b_r, l)
        pos1, pos2, plan = _dispatch_plan(meta, cnt)
        xs_sorted = _dispatch(pos1, pos2, xs)
        ys = _experts(plan, xs_sorted, norm_ffn, w_exp_gate, w_exp_up, w_exp_down, l)
        if l + 1 < nl:
            xs, h = _combine(pos1, pos2, ys, xs, meta, norm_mix[l + 1], BF16)
        else:
            xs, out = _combine(pos1, pos2, ys, xs, meta, norm_final.astype(F32).reshape(1, D_MODEL), F32)
    return out.reshape(bsz, seq, D_MODEL)
```
